```python
import math
import jax, jax.numpy as jnp
from jax import lax
import numpy as np

D_MODEL = 1024
BATCH = 4
SEQ = 4096
DEPTH = 2
DEC_BATCH = 128
DEC_SEQ = 8
PAST_LEN = 16384
PAGE_SIZE = 128

N_MIXERS = 2
N_CONV_LAYERS = (DEPTH + 1) // 2
N_ATTN_LAYERS = DEPTH // 2
CONV_DIM = D_MODEL
CONV_WIDTH = 31
HEAD_DIM = 64
N_HEADS = D_MODEL // HEAD_DIM
N_KV_HEADS = N_HEADS * 8 // 64
GROUP = N_HEADS // N_KV_HEADS
WINDOW = 128
BLOCK = WINDOW
N_BUCKETS = 32
MAX_DISTANCE = 128
N_GROUPS = 4
EXPERTS_PER_GROUP = 8
N_EXPERTS = N_GROUPS * EXPERTS_PER_GROUP
TOP_K_INNER = 2
D_EXPERT = 256
RMS_EPS = 1e-6
LN_EPS = 1e-5
NEG_INF = -1e30

kernel_name = 'hybrid_conformer_swa_hmoe_step'


def rms_norm(x, g):
    xf = x.astype(jnp.float32)
    y = xf * lax.rsqrt(jnp.mean(xf * xf, axis=-1, keepdims=True) + RMS_EPS)
    return (y * g.astype(jnp.float32)).astype(x.dtype)


def layer_norm(x, g, b):
    xf = x.astype(jnp.float32)
    mu = jnp.mean(xf, axis=-1, keepdims=True)
    xc = xf - mu
    y = xc * lax.rsqrt(jnp.mean(xc * xc, axis=-1, keepdims=True) + LN_EPS)
    return (y * g.astype(jnp.float32) + b.astype(jnp.float32)).astype(x.dtype)


def t5_bucket(dist):
    n = jnp.maximum(dist, 0)
    max_exact = N_BUCKETS // 2
    large = max_exact + (jnp.log(jnp.maximum(n, 1).astype(jnp.float32) / max_exact)
                         / math.log(MAX_DISTANCE / max_exact) * (N_BUCKETS - max_exact)).astype(jnp.int32)
    large = jnp.minimum(large, N_BUCKETS - 1)
    return jnp.where(n < max_exact, n, large)


def rel_bias_for(dist, rel_bias):
    b = rel_bias[t5_bucket(dist)].astype(jnp.float32)
    b = jnp.moveaxis(b, -1, 0)
    return b.reshape(N_KV_HEADS, GROUP, dist.shape[0], dist.shape[1])


def sink_softmax(logits, sinks):
    s = sinks.astype(jnp.float32).reshape(N_KV_HEADS, GROUP)[:, :, None, None]
    m = jnp.maximum(jnp.max(logits, axis=-1, keepdims=True), s)
    e = jnp.exp(logits - m)
    return e / (jnp.sum(e, axis=-1, keepdims=True) + jnp.exp(s - m))


def conv_mixer(h, past, w_in, dw_w, dw_b, ln_g, ln_b, w_out):
    a, gate = jnp.split(h @ w_in, 2, axis=-1)
    u = a * jax.nn.sigmoid(gate)
    up = jnp.concatenate([past.astype(u.dtype), u], axis=1)
    y = lax.conv_general_dilated(up, dw_w[:, None, :], (1,), 'VALID',
                                 dimension_numbers=('NWC', 'WIO', 'NWC'),
                                 feature_group_count=CONV_DIM) + dw_b
    y = jax.nn.silu(layer_norm(y, ln_g, ln_b))
    return y @ w_out, up[:, -(CONV_WIDTH - 1):]


def qkv_heads(h, w_qkv, q_g, k_g):
    lead = h.shape[:-1]
    q, k, v = jnp.split(h @ w_qkv, [N_HEADS * HEAD_DIM, (N_HEADS + N_KV_HEADS) * HEAD_DIM], axis=-1)
    q = rms_norm(q.reshape(*lead, N_HEADS, HEAD_DIM), q_g)
    k = rms_norm(k.reshape(*lead, N_KV_HEADS, HEAD_DIM), k_g)
    v = v.reshape(*lead, N_KV_HEADS, HEAD_DIM)
    return q, k, v


def swa_prompt(h, w_qkv, q_g, k_g, sinks, w_o, rel_bias):
    b, s, _ = h.shape
    nb = s // BLOCK
    q, k, v = qkv_heads(h, w_qkv, q_g, k_g)
    qb = q.reshape(b, nb, BLOCK, N_KV_HEADS, GROUP, HEAD_DIM)

    def with_prev(t):
        tb = t.reshape(b, nb, BLOCK, N_KV_HEADS, HEAD_DIM)
        prev = jnp.pad(tb, ((0, 0), (1, 0), (0, 0), (0, 0), (0, 0)))[:, :-1]
        return jnp.concatenate([prev, tb], axis=2)

    kk, vv = with_prev(k), with_prev(v)
    scores = jnp.einsum('bnqkgd,bnskd->bnkgqs', qb, kk,
                        preferred_element_type=jnp.float32) * (HEAD_DIM ** -0.5)
    q_off = jnp.arange(BLOCK)
    k_off = jnp.arange(2 * BLOCK)
    dist = q_off[:, None] + BLOCK - k_off[None, :]
    key_pos = (jnp.arange(nb)[:, None] - 1) * BLOCK + k_off[None, :]
    mask = ((dist >= 0) & (dist <= WINDOW))[None] & (key_pos >= 0)[:, None, :]
    logits = jnp.where(mask[None, :, None, None], scores + rel_bias_for(dist, rel_bias), NEG_INF)
    p = sink_softmax(logits, sinks)
    out = jnp.einsum('bnkgqs,bnskd->bnqkgd', p.astype(vv.dtype), vv)
    out = out.reshape(b, s, N_HEADS * HEAD_DIM)
    return out @ w_o, k[:, -WINDOW:], v[:, -WINDOW:]


def swa_sample(h, cache_k, cache_v, w_qkv, q_g, k_g, sinks, w_o, rel_bias):
    b, t, _ = h.shape
    q, k, v = qkv_heads(h, w_qkv, q_g, k_g)
    kk = jnp.concatenate([cache_k.astype(k.dtype), k], axis=1)
    vv = jnp.concatenate([cache_v.astype(v.dtype), v], axis=1)
    qg = q.reshape(b, t, N_KV_HEADS, GROUP, HEAD_DIM)
    scores = jnp.einsum('btkgd,bskd->bkgts', qg, kk,
                        preferred_element_type=jnp.float32) * (HEAD_DIM ** -0.5)
    t_idx = jnp.arange(t)
    s_idx = jnp.arange(WINDOW + t)
    dist = t_idx[:, None] + WINDOW - s_idx[None, :]
    key_pos = PAST_LEN - WINDOW + s_idx
    mask = (dist >= 0) & (dist <= WINDOW) & (key_pos >= 0)[None, :]
    logits = jnp.where(mask, scores + rel_bias_for(dist, rel_bias), NEG_INF)
    p = sink_softmax(logits, sinks)
    out = jnp.einsum('bkgts,bskd->btkgd', p.astype(vv.dtype), vv)
    out = out.reshape(b, t, N_HEADS * HEAD_DIM)
    return out @ w_o, kk[:, -WINDOW:], vv[:, -WINDOW:]


def hier_moe(h, wg, bg, we, be, w_in, w_out):
    lead = h.shape[:-1]
    x = h.reshape(-1, D_MODEL)
    g_logits = jnp.matmul(x, wg, preferred_element_type=jnp.float32) + bg.astype(jnp.float32)
    p_group = jax.nn.softmax(g_logits, axis=-1)
    g_sel = jnp.argmax(g_logits, axis=-1)
    gate_g = jnp.take_along_axis(p_group, g_sel[:, None], axis=-1)
    e_logits = jnp.einsum('nd,gde->nge', x, we, preferred_element_type=jnp.float32) + be.astype(jnp.float32)
    e_logits = jnp.take_along_axis(e_logits, g_sel[:, None, None], axis=1)[:, 0]
    top_v, top_i = lax.top_k(e_logits, TOP_K_INNER)
    w_top = jax.nn.softmax(top_v, axis=-1) * gate_g
    expert_id = g_sel[:, None] * EXPERTS_PER_GROUP + top_i
    combine = jnp.sum(jax.nn.one_hot(expert_id, N_EXPERTS, dtype=jnp.float32) * w_top[..., None], axis=1)
    hid = jnp.einsum('nd,edf->nef', x, w_in)
    a, u = jnp.split(hid, 2, axis=-1)
    act = jax.nn.silu(a) * u * combine[..., None].astype(x.dtype)
    y = jnp.einsum('nef,efd->nd', act, w_out)
    return y.reshape(*lead, D_MODEL)


def setup_inputs(seed: int = 0) -> dict:
    key = jax.random.key(seed)
    ks = jax.random.split(key, 32)
    f32 = jnp.float32

    def nrm(k, shape, scale):
        return jax.random.normal(k, shape, f32) * scale

    qkv_out = (N_HEADS + 2 * N_KV_HEADS) * HEAD_DIM
    return {
        'x_prompt': nrm(ks[0], (BATCH, SEQ, D_MODEL), 1.0),
        'x_sample': nrm(ks[1], (DEC_BATCH, DEC_SEQ, D_MODEL), 1.0),
        'state_conv': nrm(ks[2], (N_CONV_LAYERS, DEC_BATCH, CONV_WIDTH - 1, CONV_DIM), 0.5),
        'cache_swa_k': nrm(ks[3], (N_ATTN_LAYERS, DEC_BATCH, WINDOW, N_KV_HEADS, HEAD_DIM), 1.0),
        'cache_swa_v': nrm(ks[4], (N_ATTN_LAYERS, DEC_BATCH, WINDOW, N_KV_HEADS, HEAD_DIM), 1.0),
        'rms_mix_g': 1.0 + nrm(ks[5], (DEPTH, D_MODEL), 0.02),
        'rms_ffn_g': 1.0 + nrm(ks[6], (DEPTH, D_MODEL), 0.02),
        'conv_w_in': nrm(ks[7], (N_CONV_LAYERS, D_MODEL, 2 * CONV_DIM), D_MODEL ** -0.5),
        'conv_dw_w': nrm(ks[8], (N_CONV_LAYERS, CONV_WIDTH, CONV_DIM), CONV_WIDTH ** -0.5),
        'conv_dw_b': nrm(ks[9], (N_CONV_LAYERS, CONV_DIM), 0.02),
        'conv_ln_g': 1.0 + nrm(ks[10], (N_CONV_LAYERS, CONV_DIM), 0.02),
        'conv_ln_b': nrm(ks[11], (N_CONV_LAYERS, CONV_DIM), 0.02),
        'conv_w_out': nrm(ks[12], (N_CONV_LAYERS, CONV_DIM, D_MODEL), CONV_DIM ** -0.5),
        'attn_w_qkv': nrm(ks[13], (N_ATTN_LAYERS, D_MODEL, qkv_out), D_MODEL ** -0.5),
        'attn_q_norm_g': 1.0 + nrm(ks[14], (N_ATTN_LAYERS, HEAD_DIM), 0.02),
        'attn_k_norm_g': 1.0 + nrm(ks[15], (N_ATTN_LAYERS, HEAD_DIM), 0.02),
        'attn_sinks': nrm(ks[16], (N_ATTN_LAYERS, N_HEADS), 0.5),
        'attn_w_o': nrm(ks[17], (N_ATTN_LAYERS, N_HEADS * HEAD_DIM, D_MODEL), (N_HEADS * HEAD_DIM) ** -0.5),
        'rel_bias': nrm(ks[18], (N_BUCKETS, N_HEADS), 0.5),
        'router_group_w': nrm(ks[19], (DEPTH, D_MODEL, N_GROUPS), D_MODEL ** -0.5),
        'router_group_b': nrm(ks[20], (DEPTH, N_GROUPS), 0.01),
        'router_expert_w': nrm(ks[21], (DEPTH, N_GROUPS, D_MODEL, EXPERTS_PER_GROUP), D_MODEL ** -0.5),
        'router_expert_b': nrm(ks[22], (DEPTH, N_GROUPS, EXPERTS_PER_GROUP), 0.01),
        'expert_w_in': nrm(ks[23], (DEPTH, N_EXPERTS, D_MODEL, 2 * D_EXPERT), D_MODEL ** -0.5),
        'expert_w_out': nrm(ks[24], (DEPTH, N_EXPERTS, D_EXPERT, D_MODEL), D_EXPERT ** -0.5),
    }


def reference(x_prompt, x_sample, state_conv, cache_swa_k, cache_swa_v,
              rms_mix_g, rms_ffn_g,
              conv_w_in, conv_dw_w, conv_dw_b, conv_ln_g, conv_ln_b, conv_w_out,
              attn_w_qkv, attn_q_norm_g, attn_k_norm_g, attn_sinks, attn_w_o, rel_bias,
              router_group_w, router_group_b, router_expert_w, router_expert_b,
              expert_w_in, expert_w_out):
    xp, xs = x_prompt, x_sample
    conv_p, conv_s, k_p, v_p, k_s, v_s = [], [], [], [], [], []
    for i in range(DEPTH):
        j = i // N_MIXERS
        hp = rms_norm(xp, rms_mix_g[i])
        hs = rms_norm(xs, rms_mix_g[i])
        if i % N_MIXERS == 0:
            cw = (conv_w_in[j], conv_dw_w[j], conv_dw_b[j], conv_ln_g[j], conv_ln_b[j], conv_w_out[j])
            zero_past = jnp.zeros((xp.shape[0], CONV_WIDTH - 1, CONV_DIM), xp.dtype)
            yp, sp = conv_mixer(hp, zero_past, *cw)
            ys, ss = conv_mixer(hs, state_conv[j], *cw)
            conv_p.append(sp)
            conv_s.append(ss)
        else:
            aw = (attn_w_qkv[j], attn_q_norm_g[j], attn_k_norm_g[j], attn_sinks[j], attn_w_o[j], rel_bias)
            yp, kpn, vpn = swa_prompt(hp, *aw)
            ys, ksn, vsn = swa_sample(hs, cache_swa_k[j], cache_swa_v[j], *aw)
            k_p.append(kpn)
            v_p.append(vpn)
            k_s.append(ksn)
            v_s.append(vsn)
        xp = xp + yp
        xs = xs + ys
        mw = (router_group_w[i], router_group_b[i], router_expert_w[i], router_expert_b[i],
              expert_w_in[i], expert_w_out[i])
        xp = xp + hier_moe(rms_norm(xp, rms_ffn_g[i]), *mw)
        xs = xs + hier_moe(rms_norm(xs, rms_ffn_g[i]), *mw)
    return (xp, xs, jnp.stack(conv_p), jnp.stack(conv_s),
            jnp.stack(k_p), jnp.stack(v_p), jnp.stack(k_s), jnp.stack(v_s))
```

```python
import functools
import math

import numpy as np
import jax
import jax.numpy as jnp
from jax import lax
from jax.experimental import pallas as pl
from jax.experimental.pallas import tpu as pltpu

D_MODEL = 1024
DEPTH = 2
CONV_WIDTH = 31
PAST = CONV_WIDTH - 1
HEAD_DIM = 64
N_HEADS = 16
N_KV = 2
GROUP = 8
WINDOW = 128
N_BUCKETS = 32
MAX_DISTANCE = 128
N_GROUPS = 4
EPG = 8
N_EXPERTS = 32
D_EXPERT = 256
RMS_EPS = 1e-6
LN_EPS = 1e-5
NEG_INF = -1e30

F32 = jnp.float32
BF16 = jnp.bfloat16
LANES = 128
VMEM_LIMIT = 48 * 1024 * 1024


def _params(*sem):
    return pltpu.CompilerParams(dimension_semantics=sem, vmem_limit_bytes=VMEM_LIMIT)


def _rms(x, g):
    return x * lax.rsqrt(jnp.mean(x * x, axis=-1, keepdims=True) + RMS_EPS) * g


def _sigmoid(x):
    return 1.0 / (1.0 + jnp.exp(-x))


def _full(shape):
    return pl.BlockSpec(shape, lambda *_: (0,) * len(shape))


def _glu_kernel(x_ref, g_ref, wa_ref, wg_ref, u_ref):
    h = _rms(x_ref[...], g_ref[...]).astype(BF16)
    a = jnp.dot(h, wa_ref[...], preferred_element_type=F32)
    gate = jnp.dot(h, wg_ref[...], preferred_element_type=F32)
    u_ref[...] = a * _sigmoid(gate)


def glu_proj(x, g, w_in, tm):
    n, d = x.shape
    c = w_in.shape[1] // 2
    return pl.pallas_call(
        _glu_kernel,
        grid=(n // tm,),
        in_specs=[
            pl.BlockSpec((tm, d), lambda i: (i, 0)),
            _full((1, d)),
            pl.BlockSpec((d, c), lambda i: (0, 0)),
            pl.BlockSpec((d, c), lambda i: (0, 1)),
        ],
        out_specs=pl.BlockSpec((tm, c), lambda i: (i, 0)),
        out_shape=jax.ShapeDtypeStruct((n, c), F32),
        compiler_params=_params("parallel"),
        name="glu_proj",
    )(x, g, w_in, w_in)


def _ln_silu_out(y, lng, lnb, wout_ref, x):
    mu = jnp.mean(y, axis=-1, keepdims=True)
    yc = y - mu
    z = yc * lax.rsqrt(jnp.mean(yc * yc, axis=-1, keepdims=True) + LN_EPS) * lng + lnb
    z = z * _sigmoid(z)
    return x + jnp.dot(z.astype(BF16), wout_ref[...], preferred_element_type=F32)


HALO = 32
CONV_RC = 32
CONV_CC = 256


def _conv_prompt_kernel(ucur_ref, uprev_ref, x_ref, dww_ref, dwb_ref, lng_ref, lnb_ref,
                        wout_ref, o_ref, up_ref, y_ref):
    t = pl.program_id(1)
    tt, c = ucur_ref.shape
    keep = (t > 0).astype(F32)
    up_ref[0:HALO, :] = uprev_ref[...] * keep
    up_ref[HALO:, :] = ucur_ref[...]
    off = HALO - PAST
    for r0 in range(0, tt, CONV_RC):
        for c0 in range(0, c, CONV_CC):
            acc = jnp.zeros((CONV_RC, CONV_CC), F32) + dwb_ref[:, c0:c0 + CONV_CC]
            for k in range(CONV_WIDTH):
                s = r0 + off + k
                acc = acc + up_ref[s:s + CONV_RC, c0:c0 + CONV_CC] * dww_ref[k:k + 1, c0:c0 + CONV_CC]
            y_ref[r0:r0 + CONV_RC, c0:c0 + CONV_CC] = acc
    o_ref[...] = _ln_silu_out(y_ref[...], lng_ref[...], lnb_ref[...], wout_ref, x_ref[...])


def conv_prompt(u, x, dww, dwb, lng, lnb, wout, batch, seq, tt):
    n, c = u.shape
    d = x.shape[1]
    nt = seq // tt
    hb = tt // HALO
    return pl.pallas_call(
        _conv_prompt_kernel,
        grid=(batch, nt),
        in_specs=[
            pl.BlockSpec((tt, c), lambda b, t: (b * nt + t, 0)),
            pl.BlockSpec((HALO, c), lambda b, t: (jnp.maximum((b * nt + t) * hb - 1, 0), 0)),
            pl.BlockSpec((tt, d), lambda b, t: (b * nt + t, 0)),
            _full((HALO, c)), _full((1, c)), _full((1, c)), _full((1, c)),
            _full((c, d)),
        ],
        out_specs=pl.BlockSpec((tt, d), lambda b, t: (b * nt + t, 0)),
        out_shape=jax.ShapeDtypeStruct((n, d), F32),
        scratch_shapes=[pltpu.VMEM((tt + HALO, c), F32), pltpu.VMEM((tt, c), F32)],
        compiler_params=_params("parallel", "parallel"),
        name="conv_prompt",
    )(u, u, x, dww, dwb, lng, lnb, wout)


CONV_SB = 4


def _conv_sample_kernel(u_ref, st_ref, x_ref, dww_ref, dwb_ref, lng_ref, lnb_ref,
                        wout_ref, o_ref, up_ref, y_ref):
    bs, ts, c = u_ref.shape
    up_ref[:, 0:PAST, :] = st_ref[...]
    up_ref[:, PAST:PAST + ts, :] = u_ref[...]
    for b0 in range(0, bs, CONV_SB):
        acc = jnp.zeros((CONV_SB, ts, c), F32) + dwb_ref[...][None]
        for k in range(CONV_WIDTH):
            acc = acc + up_ref[b0:b0 + CONV_SB, k:k + ts, :] * dww_ref[k:k + 1, :][None]
        y_ref[b0 * ts:(b0 + CONV_SB) * ts, :] = acc.reshape(CONV_SB * ts, c)
    o_ref[...] = _ln_silu_out(y_ref[...], lng_ref[...], lnb_ref[...], wout_ref, x_ref[...])


def conv_sample(u3, state, x, dww, dwb, lng, lnb, wout, bs):
    nb, ts, c = u3.shape
    d = x.shape[1]
    return pl.pallas_call(
        _conv_sample_kernel,
        grid=(nb // bs,),
        in_specs=[
            pl.BlockSpec((bs, ts, c), lambda i: (i, 0, 0)),
            pl.BlockSpec((bs, PAST, c), lambda i: (i, 0, 0)),
            pl.BlockSpec((bs * ts, d), lambda i: (i, 0)),
            _full((HALO, c)), _full((1, c)), _full((1, c)), _full((1, c)),
            _full((c, d)),
        ],
        out_specs=pl.BlockSpec((bs * ts, d), lambda i: (i, 0)),
        out_shape=jax.ShapeDtypeStruct((nb * ts, d), F32),
        scratch_shapes=[pltpu.VMEM((bs, PAST + ts, c), F32), pltpu.VMEM((bs * ts, c), F32)],
        compiler_params=_params("parallel"),
        name="conv_sample",
    )(u3, state, x, dww, dwb, lng, lnb, wout)


def _router_kernel(x_ref, g_ref, wr_ref, br_ref, h_ref, comb_ref):
    h = _rms(x_ref[...], g_ref[...])
    h_ref[...] = h.astype(BF16)
    logits = jnp.dot(h, wr_ref[...], preferred_element_type=F32,
                     precision=lax.Precision.HIGHEST) + br_ref[...]
    lane = lax.broadcasted_iota(jnp.int32, logits.shape, 1).astype(F32)
    big = jnp.float32(LANES)
    is_g = (lane >= N_EXPERTS) & (lane < N_EXPERTS + N_GROUPS)
    gl = jnp.where(is_g, logits, NEG_INF)
    gm = jnp.max(gl, axis=-1, keepdims=True)
    g_sel = jnp.min(jnp.where(gl == gm, lane, big), axis=-1, keepdims=True) - N_EXPERTS
    gate_g = 1.0 / jnp.sum(jnp.where(is_g, jnp.exp(gl - gm), 0.0), axis=-1, keepdims=True)
    lo = g_sel * EPG
    in_grp = (lane >= lo) & (lane < lo + EPG)
    el = jnp.where(in_grp, logits, NEG_INF)
    v1 = jnp.max(el, axis=-1, keepdims=True)
    i1 = jnp.min(jnp.where(el == v1, lane, big), axis=-1, keepdims=True)
    el2 = jnp.where(lane == i1, NEG_INF, el)
    v2 = jnp.max(el2, axis=-1, keepdims=True)
    i2 = jnp.min(jnp.where(el2 == v2, lane, big), axis=-1, keepdims=True)
    e2 = jnp.exp(v2 - v1)
    w1 = gate_g / (1.0 + e2)
    w2 = gate_g * e2 / (1.0 + e2)
    comb_ref[...] = jnp.where(lane == i1, w1, 0.0) + jnp.where(lane == i2, w2, 0.0)


def router(x, g, wr, br, tm):
    n, d = x.shape
    return pl.pallas_call(
        _router_kernel,
        grid=(n // tm,),
        in_specs=[pl.BlockSpec((tm, d), lambda i: (i, 0)), _full((1, d)),
                  _full((d, LANES)), _full((1, LANES))],
        out_specs=[pl.BlockSpec((tm, d), lambda i: (i, 0)),
                   pl.BlockSpec((tm, LANES), lambda i: (i, 0))],
        out_shape=[jax.ShapeDtypeStruct((n, d), BF16),
                   jax.ShapeDtypeStruct((n, LANES), F32)],
        compiler_params=_params("parallel"),
        name="router",
    )(x, g, wr, br)


def _moe_dense_kernel(h_ref, comb_ref, x_ref, win_ref, wout_ref, o_ref):
    e = pl.program_id(1)

    @pl.when(e == 0)
    def _():
        o_ref[...] = x_ref[...]

    comb = comb_ref[...]
    lane = lax.broadcasted_iota(jnp.int32, comb.shape, 1)
    cw = jnp.sum(jnp.where(lane == e, comb, 0.0), axis=-1, keepdims=True)
    hid = jnp.dot(h_ref[...], win_ref[0], preferred_element_type=F32)
    a = hid[:, :D_EXPERT]
    u = hid[:, D_EXPERT:]
    act = a * _sigmoid(a) * u * cw
    o_ref[...] += jnp.dot(act.astype(BF16), wout_ref[0], preferred_element_type=F32)


def moe_dense(h, comb, x, w_in, w_out, tm):
    n, d = x.shape
    ne, _, f2 = w_in.shape
    return pl.pallas_call(
        _moe_dense_kernel,
        grid=(n // tm, ne),
        in_specs=[
            pl.BlockSpec((tm, d), lambda i, e: (i, 0)),
            pl.BlockSpec((tm, LANES), lambda i, e: (i, 0)),
            pl.BlockSpec((tm, d), lambda i, e: (i, 0)),
            pl.BlockSpec((1, d, f2), lambda i, e: (e, 0, 0)),
            pl.BlockSpec((1, f2 // 2, d), lambda i, e: (e, 0, 0)),
        ],
        out_specs=pl.BlockSpec((tm, d), lambda i, e: (i, 0)),
        out_shape=jax.ShapeDtypeStruct((n, d), F32),
        compiler_params=_params("parallel", "arbitrary"),
        name="moe_dense",
    )(h, comb, x, w_in, w_out)


def _qkv_kernel(x_ref, g_ref, w_ref, qg_ref, kg_ref, seg_ref, q_ref, k_ref, v_ref):
    h = _rms(x_ref[...], g_ref[...]).astype(BF16)
    qkv = jnp.dot(h, w_ref[...], preferred_element_type=F32)
    nq = N_HEADS * HEAD_DIM
    nk = N_KV * HEAD_DIM
    q = qkv[:, :nq]
    k = qkv[:, nq:nq + nk]
    v_ref[...] = qkv[:, nq + nk:]

    def seg_mean_sq(z, seg):
        zz = z * z
        hi = zz.astype(BF16)
        lo = (zz - hi.astype(F32)).astype(BF16)
        return (jnp.dot(hi, seg, preferred_element_type=F32)
                + jnp.dot(lo, seg, preferred_element_type=F32))

    seg = seg_ref[...]
    qn = q * lax.rsqrt(seg_mean_sq(q, seg) + RMS_EPS) * qg_ref[...]
    q_ref[...] = (qn * (HEAD_DIM ** -0.5)).astype(BF16)
    kn = k * lax.rsqrt(seg_mean_sq(k, seg[:nk, :nk]) + RMS_EPS) * kg_ref[...]
    k_ref[...] = kn


def qkv_proj(x, g, w, qg, kg, seg, tm):
    n, d = x.shape
    nq = N_HEADS * HEAD_DIM
    nk = N_KV * HEAD_DIM
    return pl.pallas_call(
        _qkv_kernel,
        grid=(n // tm,),
        in_specs=[pl.BlockSpec((tm, d), lambda i: (i, 0)), _full((1, d)),
                  _full((d, nq + 2 * nk)), _full((1, nq)), _full((1, nk)), _full((nq, nq))],
        out_specs=[pl.BlockSpec((tm, nq), lambda i: (i, 0)),
                   pl.BlockSpec((tm, nk), lambda i: (i, 0)),
                   pl.BlockSpec((tm, nk), lambda i: (i, 0))],
        out_shape=[jax.ShapeDtypeStruct((n, nq), BF16),
                   jax.ShapeDtypeStruct((n, nk), F32),
                   jax.ShapeDtypeStruct((n, nk), F32)],
        compiler_params=_params("parallel"),
        name="qkv_proj",
    )(x, g, w, qg, kg, seg)


def _attn_prompt_kernel(q_ref, kc_ref, kp_ref, vc_ref, vp_ref, tbl_ref, sink_ref, x_ref,
                        wo_ref, o_ref, cat_ref):
    n = pl.program_id(1)
    first = jnp.where(n == 0, NEG_INF, 0.0).astype(F32)
    kk = jnp.concatenate([kp_ref[...], kc_ref[...]], axis=0).astype(BF16)
    vv = jnp.concatenate([vp_ref[...], vc_ref[...]], axis=0).astype(BF16)
    col = lax.broadcasted_iota(jnp.int32, (WINDOW, 2 * WINDOW), 1)
    prev_mask = jnp.where(col < WINDOW, first, 0.0)
    for h in range(N_HEADS):
        kh = h // GROUP
        qh = q_ref[:, h * HEAD_DIM:(h + 1) * HEAD_DIM]
        kslice = kk[:, kh * HEAD_DIM:(kh + 1) * HEAD_DIM]
        vslice = vv[:, kh * HEAD_DIM:(kh + 1) * HEAD_DIM]
        s = lax.dot_general(qh, kslice, (((1,), (1,)), ((), ())), preferred_element_type=F32)
        logits = s + tbl_ref[h] + prev_mask
        sink = sink_ref[h]
        m = jnp.maximum(jnp.max(logits, axis=-1, keepdims=True), sink)
        e = jnp.exp(logits - m)
        denom = jnp.sum(e, axis=-1, keepdims=True) + jnp.exp(sink - m)
        p = e * (1.0 / denom)
        cat_ref[:, h * HEAD_DIM:(h + 1) * HEAD_DIM] = jnp.dot(
            p.astype(BF16), vslice, preferred_element_type=F32).astype(BF16)
    o_ref[...] = x_ref[...] + jnp.dot(cat_ref[...], wo_ref[...], preferred_element_type=F32)


def attn_prompt(q, k, v, tbl, sinks, x, wo, batch, seq):
    n, d = x.shape
    nb = seq // WINDOW
    nk = N_KV * HEAD_DIM
    cur = lambda b, i: (b * nb + i, 0)
    prev = lambda b, i: (b * nb + jnp.maximum(i - 1, 0), 0)
    return pl.pallas_call(
        _attn_prompt_kernel,
        grid=(batch, nb),
        in_specs=[
            pl.BlockSpec((WINDOW, d), cur),
            pl.BlockSpec((WINDOW, nk), cur), pl.BlockSpec((WINDOW, nk), prev),
            pl.BlockSpec((WINDOW, nk), cur), pl.BlockSpec((WINDOW, nk), prev),
            _full((N_HEADS, WINDOW, 2 * WINDOW)),
            pl.BlockSpec(memory_space=pltpu.SMEM),
            pl.BlockSpec((WINDOW, d), cur),
            _full((d, d)),
        ],
        out_specs=pl.BlockSpec((WINDOW, d), cur),
        out_shape=jax.ShapeDtypeStruct((n, d), F32),
        scratch_shapes=[pltpu.VMEM((WINDOW, d), BF16)],
        compiler_params=_params("parallel", "parallel"),
        name="attn_prompt",
    )(q, k, k, v, v, tbl, sinks, x, wo)


def _attn_sample_kernel(q_ref, kn_ref, vn_ref, ck_ref, cv_ref, tbl_ref, sink_ref, o_ref,
                        kf_ref, vf_ref):
    bs = q_ref.shape[0]
    ts = kn_ref.shape[1]
    for b in range(bs):
        kf_ref[0:WINDOW, :] = ck_ref[b]
        kf_ref[WINDOW:WINDOW + ts, :] = kn_ref[b]
        vf_ref[0:WINDOW, :] = cv_ref[b]
        vf_ref[WINDOW:WINDOW + ts, :] = vn_ref[b]
        for kh in range(N_KV):
            kslice = kf_ref[:, kh * HEAD_DIM:(kh + 1) * HEAD_DIM].astype(BF16)
            vslice = vf_ref[:, kh * HEAD_DIM:(kh + 1) * HEAD_DIM].astype(BF16)
            qt = q_ref[b, kh]
            s = lax.dot_general(qt, kslice, (((1,), (1,)), ((), ())),
                                preferred_element_type=F32)
            logits = s + tbl_ref[kh]
            sink = sink_ref[kh]
            m = jnp.maximum(jnp.max(logits, axis=-1, keepdims=True), sink)
            e = jnp.exp(logits - m)
            denom = jnp.sum(e, axis=-1, keepdims=True) + jnp.exp(sink - m)
            p = (e * (1.0 / denom)).astype(BF16)
            o_ref[b, kh] = jnp.dot(p, vslice, preferred_element_type=F32).astype(BF16)


def attn_sample(q4, kn, vn, ck, cv, tbl, sink, bs):
    nb = q4.shape[0]
    ts = kn.shape[1]
    nk = N_KV * HEAD_DIM
    tg = ts * GROUP
    kt = WINDOW + ts
    return pl.pallas_call(
        _attn_sample_kernel,
        grid=(nb // bs,),
        in_specs=[
            pl.BlockSpec((bs, N_KV, tg, HEAD_DIM), lambda i: (i, 0, 0, 0)),
            pl.BlockSpec((bs, ts, nk), lambda i: (i, 0, 0)),
            pl.BlockSpec((bs, ts, nk), lambda i: (i, 0, 0)),
            pl.BlockSpec((bs, WINDOW, nk), lambda i: (i, 0, 0)),
            pl.BlockSpec((bs, WINDOW, nk), lambda i: (i, 0, 0)),
            _full((N_KV, tg, kt)),
            _full((N_KV, tg, 1)),
        ],
        out_specs=pl.BlockSpec((bs, N_KV, tg, HEAD_DIM), lambda i: (i, 0, 0, 0)),
        out_shape=jax.ShapeDtypeStruct((nb, N_KV, tg, HEAD_DIM), BF16),
        scratch_shapes=[pltpu.VMEM((kt, nk), F32), pltpu.VMEM((kt, nk), F32)],
        compiler_params=_params("parallel"),
        name="attn_sample",
    )(q4, kn, vn, ck, cv, tbl, sink)


def _proj_res_kernel(a_ref, w_ref, x_ref, o_ref):
    o_ref[...] = x_ref[...] + jnp.dot(a_ref[...], w_ref[...], preferred_element_type=F32)


def proj_residual(a, w, x, tm):
    n, d = x.shape
    kdim = a.shape[1]
    return pl.pallas_call(
        _proj_res_kernel,
        grid=(n // tm,),
        in_specs=[pl.BlockSpec((tm, kdim), lambda i: (i, 0)), _full((kdim, d)),
                  pl.BlockSpec((tm, d), lambda i: (i, 0))],
        out_specs=pl.BlockSpec((tm, d), lambda i: (i, 0)),
        out_shape=jax.ShapeDtypeStruct((n, d), F32),
        compiler_params=_params("parallel"),
        name="proj_residual",
    )(a, w, x)


def _t5_bucket_np(dist):
    n = np.maximum(dist, 0)
    max_exact = N_BUCKETS // 2
    large = max_exact + (np.log(np.maximum(n, 1).astype(np.float32) / max_exact)
                         / math.log(MAX_DISTANCE / max_exact) * (N_BUCKETS - max_exact)).astype(np.int32)
    large = np.minimum(large, N_BUCKETS - 1)
    return np.where(n < max_exact, n, large)


def _bias_table(rel_bias, dist):
    valid = (dist >= 0) & (dist <= WINDOW)
    onehot = (np.asarray(_t5_bucket_np(dist))[..., None] == np.arange(N_BUCKETS)).astype(np.float32)
    b = jnp.einsum("qkb,bh->hqk", jnp.asarray(onehot), rel_bias.astype(F32),
                   precision=lax.Precision.HIGHEST)
    return jnp.where(jnp.asarray(valid)[None], b, NEG_INF)


def _moe_layer(x, g, wr, br, w_in, w_out, tm_r, tm_e):
    h, comb = router(x, g, wr, br, tm_r)
    return moe_dense(h, comb, x, w_in, w_out, tm_e)


def kernel(x_prompt, x_sample, state_conv, cache_swa_k, cache_swa_v, rms_mix_g, rms_ffn_g, conv_w_in, conv_dw_w, conv_dw_b, conv_ln_g, conv_ln_b, conv_w_out, attn_w_qkv, attn_q_norm_g, attn_k_norm_g, attn_sinks, attn_w_o, rel_bias, router_group_w, router_group_b, router_expert_w, router_expert_b, expert_w_in, expert_w_out):
    batch, seq, d = x_prompt.shape
    nsb, ts, _ = x_sample.shape
    xp = x_prompt.reshape(batch * seq, d)
    xs = x_sample.reshape(nsb * ts, d)
    row = lambda a: a.reshape(1, -1).astype(F32)

    def router_w(i):
        we = jnp.transpose(router_expert_w[i], (1, 0, 2)).reshape(d, N_EXPERTS)
        wr = jnp.concatenate([we, router_group_w[i]], axis=1)
        wr = jnp.pad(wr, ((0, 0), (0, LANES - wr.shape[1])))
        br = jnp.concatenate([router_expert_b[i].reshape(-1), router_group_b[i]])
        br = jnp.pad(br, (0, LANES - br.shape[0])).reshape(1, LANES)
        return wr.astype(F32), br.astype(F32)

    g0 = row(rms_mix_g[0])
    w_in = conv_w_in[0].astype(BF16)
    dww = jnp.pad(conv_dw_w[0].astype(F32), ((0, HALO - CONV_WIDTH), (0, 0)))
    dwb, lng, lnb = row(conv_dw_b[0]), row(conv_ln_g[0]), row(conv_ln_b[0])
    w_out = conv_w_out[0].astype(BF16)
    up = glu_proj(xp, g0, w_in, 512)
    us = glu_proj(xs, g0, w_in, 512)
    xp = conv_prompt(up, xp, dww, dwb, lng, lnb, w_out, batch, seq, 256)
    us3 = us.reshape(nsb, ts, -1)
    xs = conv_sample(us3, state_conv[0], xs, dww, dwb, lng, lnb, w_out, 32)
    conv_p = up.reshape(batch, seq, -1)[:, seq - PAST:]
    conv_s = jnp.concatenate([state_conv[0], us3], axis=1)[:, ts:]

    wr0, br0 = router_w(0)
    ew_in0, ew_out0 = expert_w_in[0].astype(BF16), expert_w_out[0].astype(BF16)
    xp = _moe_layer(xp, row(rms_ffn_g[0]), wr0, br0, ew_in0, ew_out0, 512, 1024)
    xs = _moe_layer(xs, row(rms_ffn_g[0]), wr0, br0, ew_in0, ew_out0, 512, 1024)

    g1 = row(rms_mix_g[1])
    w_qkv = attn_w_qkv[0].astype(BF16)
    qg = jnp.tile(attn_q_norm_g[0].astype(F32), N_HEADS).reshape(1, -1)
    kg = jnp.tile(attn_k_norm_g[0].astype(F32), N_KV).reshape(1, -1)
    nq = N_HEADS * HEAD_DIM
    seg = jnp.asarray(np.kron(np.eye(N_HEADS), np.ones((HEAD_DIM, HEAD_DIM))) / HEAD_DIM, BF16)
    w_o = attn_w_o[0].astype(BF16)
    sinks = attn_sinks[0].astype(F32)

    qp, kp, vp = qkv_proj(xp, g1, w_qkv, qg, kg, seg, 512)
    qs, ks, vs = qkv_proj(xs, g1, w_qkv, qg, kg, seg, 512)

    q_off = np.arange(WINDOW)[:, None]
    dist_p = q_off + WINDOW - np.arange(2 * WINDOW)[None, :]
    tbl_p = _bias_table(rel_bias, dist_p)
    xp = attn_prompt(qp, kp, vp, tbl_p, sinks, xp, w_o, batch, seq)

    kt = WINDOW + ts
    dist_s = np.arange(ts)[:, None] + WINDOW - np.arange(kt)[None, :]
    tbl_s = _bias_table(rel_bias, dist_s)
    tbl_s = jnp.transpose(tbl_s.reshape(N_KV, GROUP, ts, kt), (0, 2, 1, 3)).reshape(N_KV, ts * GROUP, kt)
    sink_s = jnp.tile(sinks.reshape(N_KV, 1, GROUP), (1, ts, 1)).reshape(N_KV, ts * GROUP, 1)
    q4 = jnp.transpose(qs.reshape(nsb, ts, N_KV, GROUP, HEAD_DIM), (0, 2, 1, 3, 4))
    q4 = q4.reshape(nsb, N_KV, ts * GROUP, HEAD_DIM)
    nk = N_KV * HEAD_DIM
    ks3, vs3 = ks.reshape(nsb, ts, nk), vs.reshape(nsb, ts, nk)
    ck = cache_swa_k[0].reshape(nsb, WINDOW, nk)
    cv = cache_swa_v[0].reshape(nsb, WINDOW, nk)
    o4 = attn_sample(q4, ks3, vs3, ck, cv, tbl_s, sink_s, 16)
    os_ = jnp.transpose(o4.reshape(nsb, N_KV, ts, GROUP, HEAD_DIM), (0, 2, 1, 3, 4)).reshape(nsb * ts, nq)
    xs = proj_residual(os_, w_o, xs, 512)

    k_p = kp.reshape(batch, seq, N_KV, HEAD_DIM)[:, seq - WINDOW:]
    v_p = vp.reshape(batch, seq, N_KV, HEAD_DIM)[:, seq - WINDOW:]
    k_s = jnp.concatenate([cache_swa_k[0], ks.reshape(nsb, ts, N_KV, HEAD_DIM)], axis=1)[:, ts:]
    v_s = jnp.concatenate([cache_swa_v[0], vs.reshape(nsb, ts, N_KV, HEAD_DIM)], axis=1)[:, ts:]

    wr1, br1 = router_w(1)
    ew_in1, ew_out1 = expert_w_in[1].astype(BF16), expert_w_out[1].astype(BF16)
    xp = _moe_layer(xp, row(rms_ffn_g[1]), wr1, br1, ew_in1, ew_out1, 512, 1024)
    xs = _moe_layer(xs, row(rms_ffn_g[1]), wr1, br1, ew_in1, ew_out1, 512, 1024)

    return (xp.reshape(batch, seq, d), xs.reshape(nsb, ts, d),
            conv_p[None], conv_s[None], k_p[None], v_p[None], k_s[None], v_s[None])
```

```python
import functools
import math

import numpy as np
import jax
import jax.numpy as jnp
from jax import lax
from jax.experimental import pallas as pl
from jax.experimental.pallas import tpu as pltpu

D_MODEL = 1024
DEPTH = 2
CONV_WIDTH = 31
PAST = CONV_WIDTH - 1
HEAD_DIM = 64
N_HEADS = 16
N_KV = 2
GROUP = 8
WINDOW = 128
N_BUCKETS = 32
MAX_DISTANCE = 128
N_GROUPS = 4
EPG = 8
N_EXPERTS = 32
D_EXPERT = 256
RMS_EPS = 1e-6
LN_EPS = 1e-5
NEG_INF = -1e30

F32 = jnp.float32
BF16 = jnp.bfloat16
LANES = 128
ROW_TILE = 8
MOE_TM = 256
MOE_TMG = 256
VMEM_LIMIT = 48 * 1024 * 1024


def _params(*sem):
    return pltpu.CompilerParams(dimension_semantics=sem, vmem_limit_bytes=VMEM_LIMIT)


def _rms(x, g):
    return x * lax.rsqrt(jnp.mean(x * x, axis=-1, keepdims=True) + RMS_EPS) * g


def _sigmoid(x):
    return 1.0 / (1.0 + jnp.exp(-x))


def _full(shape):
    return pl.BlockSpec(shape, lambda *_: (0,) * len(shape))


def _glu_kernel(x_ref, g_ref, wa_ref, wg_ref, u_ref):
    h = _rms(x_ref[...], g_ref[...]).astype(BF16)
    a = jnp.dot(h, wa_ref[...], preferred_element_type=F32)
    gate = jnp.dot(h, wg_ref[...], preferred_element_type=F32)
    u_ref[...] = a * _sigmoid(gate)


def glu_proj(x, g, w_in, tm):
    n, d = x.shape
    c = w_in.shape[1] // 2
    return pl.pallas_call(
        _glu_kernel,
        grid=(n // tm,),
        in_specs=[
            pl.BlockSpec((tm, d), lambda i: (i, 0)),
            _full((1, d)),
            pl.BlockSpec((d, c), lambda i: (0, 0)),
            pl.BlockSpec((d, c), lambda i: (0, 1)),
        ],
        out_specs=pl.BlockSpec((tm, c), lambda i: (i, 0)),
        out_shape=jax.ShapeDtypeStruct((n, c), F32),
        compiler_params=_params("parallel"),
        name="glu_proj",
    )(x, g, w_in, w_in)


def _ln_silu_out(y, lng, lnb, wout_ref, x):
    mu = jnp.mean(y, axis=-1, keepdims=True)
    yc = y - mu
    z = yc * lax.rsqrt(jnp.mean(yc * yc, axis=-1, keepdims=True) + LN_EPS) * lng + lnb
    z = z * _sigmoid(z)
    return x + jnp.dot(z.astype(BF16), wout_ref[...], preferred_element_type=F32)


HALO = 32
CONV_RC = 32
CONV_CC = 256


def _conv_prompt_kernel(ucur_ref, uprev_ref, x_ref, dww_ref, dwb_ref, lng_ref, lnb_ref,
                        wout_ref, o_ref, up_ref, y_ref):
    t = pl.program_id(1)
    tt, c = ucur_ref.shape
    keep = (t > 0).astype(F32)
    up_ref[0:HALO, :] = uprev_ref[...] * keep
    up_ref[HALO:, :] = ucur_ref[...]
    off = HALO - PAST
    for r0 in range(0, tt, CONV_RC):
        for c0 in range(0, c, CONV_CC):
            acc = jnp.zeros((CONV_RC, CONV_CC), F32) + dwb_ref[:, c0:c0 + CONV_CC]
            for k in range(CONV_WIDTH):
                s = r0 + off + k
                acc = acc + up_ref[s:s + CONV_RC, c0:c0 + CONV_CC] * dww_ref[k:k + 1, c0:c0 + CONV_CC]
            y_ref[r0:r0 + CONV_RC, c0:c0 + CONV_CC] = acc
    o_ref[...] = _ln_silu_out(y_ref[...], lng_ref[...], lnb_ref[...], wout_ref, x_ref[...])


def conv_prompt(u, x, dww, dwb, lng, lnb, wout, batch, seq, tt):
    n, c = u.shape
    d = x.shape[1]
    nt = seq // tt
    hb = tt // HALO
    return pl.pallas_call(
        _conv_prompt_kernel,
        grid=(batch, nt),
        in_specs=[
            pl.BlockSpec((tt, c), lambda b, t: (b * nt + t, 0)),
            pl.BlockSpec((HALO, c), lambda b, t: (jnp.maximum((b * nt + t) * hb - 1, 0), 0)),
            pl.BlockSpec((tt, d), lambda b, t: (b * nt + t, 0)),
            _full((HALO, c)), _full((1, c)), _full((1, c)), _full((1, c)),
            _full((c, d)),
        ],
        out_specs=pl.BlockSpec((tt, d), lambda b, t: (b * nt + t, 0)),
        out_shape=jax.ShapeDtypeStruct((n, d), F32),
        scratch_shapes=[pltpu.VMEM((tt + HALO, c), F32), pltpu.VMEM((tt, c), F32)],
        compiler_params=_params("parallel", "parallel"),
        name="conv_prompt",
    )(u, u, x, dww, dwb, lng, lnb, wout)


CONV_SB = 4


def _conv_sample_kernel(u_ref, st_ref, x_ref, dww_ref, dwb_ref, lng_ref, lnb_ref,
                        wout_ref, o_ref, up_ref, y_ref):
    bs, ts, c = u_ref.shape
    up_ref[:, 0:PAST, :] = st_ref[...]
    up_ref[:, PAST:PAST + ts, :] = u_ref[...]
    for b0 in range(0, bs, CONV_SB):
        acc = jnp.zeros((CONV_SB, ts, c), F32) + dwb_ref[...][None]
        for k in range(CONV_WIDTH):
            acc = acc + up_ref[b0:b0 + CONV_SB, k:k + ts, :] * dww_ref[k:k + 1, :][None]
        y_ref[b0 * ts:(b0 + CONV_SB) * ts, :] = acc.reshape(CONV_SB * ts, c)
    o_ref[...] = _ln_silu_out(y_ref[...], lng_ref[...], lnb_ref[...], wout_ref, x_ref[...])


def conv_sample(u3, state, x, dww, dwb, lng, lnb, wout, bs):
    nb, ts, c = u3.shape
    d = x.shape[1]
    return pl.pallas_call(
        _conv_sample_kernel,
        grid=(nb // bs,),
        in_specs=[
            pl.BlockSpec((bs, ts, c), lambda i: (i, 0, 0)),
            pl.BlockSpec((bs, PAST, c), lambda i: (i, 0, 0)),
            pl.BlockSpec((bs * ts, d), lambda i: (i, 0)),
            _full((HALO, c)), _full((1, c)), _full((1, c)), _full((1, c)),
            _full((c, d)),
        ],
        out_specs=pl.BlockSpec((bs * ts, d), lambda i: (i, 0)),
        out_shape=jax.ShapeDtypeStruct((nb * ts, d), F32),
        scratch_shapes=[pltpu.VMEM((bs, PAST + ts, c), F32), pltpu.VMEM((bs * ts, c), F32)],
        compiler_params=_params("parallel"),
        name="conv_sample",
    )(u3, state, x, dww, dwb, lng, lnb, wout)


def _router_kernel(x_ref, g_ref, wr_ref, br_ref, hs_ref, route_ref):
    h = _rms(x_ref[...], g_ref[...])
    tm, d = h.shape
    for j in range(d // LANES):
        hs_ref[pl.ds(j, tm, stride=ROW_TILE), :] = h[:, j * LANES:(j + 1) * LANES]
    logits = jnp.dot(h, wr_ref[...], preferred_element_type=F32,
                     precision=lax.Precision.HIGHEST) + br_ref[...]
    lane = lax.broadcasted_iota(jnp.int32, logits.shape, 1).astype(F32)
    big = jnp.float32(LANES)
    is_g = (lane >= N_EXPERTS) & (lane < N_EXPERTS + N_GROUPS)
    gl = jnp.where(is_g, logits, NEG_INF)
    gm = jnp.max(gl, axis=-1, keepdims=True)
    g_sel = jnp.min(jnp.where(gl == gm, lane, big), axis=-1, keepdims=True) - N_EXPERTS
    gate_g = 1.0 / jnp.sum(jnp.where(is_g, jnp.exp(gl - gm), 0.0), axis=-1, keepdims=True)
    lo = g_sel * EPG
    in_grp = (lane >= lo) & (lane < lo + EPG)
    el = jnp.where(in_grp, logits, NEG_INF)
    v1 = jnp.max(el, axis=-1, keepdims=True)
    i1 = jnp.min(jnp.where(el == v1, lane, big), axis=-1, keepdims=True)
    el2 = jnp.where(lane == i1, NEG_INF, el)
    v2 = jnp.max(el2, axis=-1, keepdims=True)
    i2 = jnp.min(jnp.where(el2 == v2, lane, big), axis=-1, keepdims=True)
    e2 = jnp.exp(v2 - v1)
    w1 = gate_g / (1.0 + e2)
    w2 = gate_g * e2 / (1.0 + e2)
    route_ref[...] = jnp.where(lane == 0.0, i1, jnp.where(lane == 1.0, i2, jnp.where(
        lane == 2.0, w1, jnp.where(lane == 3.0, w2, 0.0))))


def router(x, g, wr, br, tm):
    n, d = x.shape
    return pl.pallas_call(
        _router_kernel,
        grid=(n // tm,),
        in_specs=[pl.BlockSpec((tm, d), lambda i: (i, 0)), _full((1, d)),
                  _full((d, LANES)), _full((1, LANES))],
        out_specs=[pl.BlockSpec((tm * ROW_TILE, LANES), lambda i: (i, 0)),
                   pl.BlockSpec((tm, LANES), lambda i: (i, 0))],
        out_shape=[jax.ShapeDtypeStruct((n * ROW_TILE, LANES), F32),
                   jax.ShapeDtypeStruct((n, LANES), F32)],
        compiler_params=_params("parallel"),
        name="router",
    )(x, g, wr, br)


def _plan_kernel(route_ref, tri_ref, rank_ref, cnt_ref, carry_ref):
    i = pl.program_id(0)

    @pl.when(i == 0)
    def _():
        carry_ref[...] = jnp.zeros_like(carry_ref)

    route = route_ref[...]
    lane = lax.broadcasted_iota(jnp.int32, route.shape, 1).astype(F32)
    i1 = route[:, 0:1]
    i2 = route[:, 1:2]
    hit1 = lane == i1
    hit2 = lane == i2
    onehot = jnp.where(hit1 | hit2, 1.0, 0.0)
    before = carry_ref[...] + jnp.dot(tri_ref[...], onehot.astype(BF16), preferred_element_type=F32)
    r1 = jnp.sum(jnp.where(hit1, before, 0.0), axis=-1, keepdims=True)
    r2 = jnp.sum(jnp.where(hit2, before, 0.0), axis=-1, keepdims=True)
    rank_ref[...] = jnp.where(lane == 0.0, r1, jnp.where(lane == 1.0, r2, 0.0))
    carry_ref[...] += jnp.sum(onehot, axis=0, keepdims=True)
    cnt_ref[...] = jnp.broadcast_to(carry_ref[...], cnt_ref.shape)


def plan(route, tm):
    n = route.shape[0]
    tri = jnp.asarray(np.tril(np.ones((tm, tm), np.float32), -1), BF16)
    return pl.pallas_call(
        _plan_kernel,
        grid=(n // tm,),
        in_specs=[pl.BlockSpec((tm, LANES), lambda i: (i, 0)), _full((tm, tm))],
        out_specs=[pl.BlockSpec((tm, LANES), lambda i: (i, 0)), _full((ROW_TILE, LANES))],
        out_shape=[jax.ShapeDtypeStruct((n, LANES), F32),
                   jax.ShapeDtypeStruct((ROW_TILE, LANES), F32)],
        scratch_shapes=[pltpu.VMEM((1, LANES), F32)],
        compiler_params=_params("arbitrary"),
        name="moe_plan",
    )(route, tri)


def _gather_rows(idx_ref, src_hbm, dst, sem, n_rows):
    unroll = 8

    def body(c, carry):
        for u in range(unroll):
            m = c * unroll + u
            tok = idx_ref[m]
            pltpu.make_async_copy(
                src_hbm.at[pl.ds(pl.multiple_of(tok * ROW_TILE, ROW_TILE), ROW_TILE), :],
                dst.at[pl.ds(pl.multiple_of(m * ROW_TILE, ROW_TILE), ROW_TILE), :],
                sem).start()
        return carry

    lax.fori_loop(0, n_rows // unroll, body, 0)


def _wait_rows(src_hbm, dst, sem):
    pltpu.make_async_copy(src_hbm.at[pl.ds(0, dst.shape[0]), :], dst, sem).wait()


def _from_token_tiles(buf, start, n_rows, stride):
    return jnp.concatenate(
        [buf[pl.ds(start + j, n_rows, stride=stride), :] for j in range(ROW_TILE)], axis=-1)


def _experts_kernel(te_ref, nv_ref, idx_cur_ref, idx_nxt_ref, hs_hbm, win_ref, wout_ref,
                    ys_ref, gbuf, sems, winb, woutb):
    j = pl.program_id(0)
    nv = nv_ref[0]
    tmg = idx_cur_ref.shape[0]
    slot = j % 2

    @pl.when(j == 0)
    def _():
        _gather_rows(idx_cur_ref, hs_hbm, gbuf.at[0], sems.at[0], tmg)

    @pl.when(j + 1 < nv)
    def _():
        _gather_rows(idx_nxt_ref, hs_hbm, gbuf.at[1 - slot], sems.at[1 - slot], tmg)

    changed = jnp.logical_or(j == 0, te_ref[j] != te_ref[jnp.maximum(j - 1, 0)])

    @pl.when(jnp.logical_and(changed, j < nv))
    def _():
        winb[...] = win_ref[0].astype(BF16)
        woutb[...] = wout_ref[0].astype(BF16)

    @pl.when(j < nv)
    def _():
        _wait_rows(hs_hbm, gbuf.at[slot], sems.at[slot])
        x = _from_token_tiles(gbuf.at[slot], 0, tmg, ROW_TILE).astype(BF16)
        hid = jnp.dot(x, winb[...], preferred_element_type=F32)
        a = hid[:, :D_EXPERT]
        u = hid[:, D_EXPERT:]
        act = (a * _sigmoid(a) * u).astype(BF16)
        y = jnp.dot(act, woutb[...], preferred_element_type=F32)
        for c in range(ROW_TILE):
            ys_ref[pl.ds(c, tmg, stride=ROW_TILE), :] = y[:, c * LANES:(c + 1) * LANES]

    @pl.when(j >= nv)
    def _():
        ys_ref[...] = jnp.zeros_like(ys_ref)


def experts(hs, src_tok, tile_expert, n_valid, w_in, w_out, tmg):
    ne, d, f2 = w_in.shape
    p_rows = src_tok.shape[0]
    nt = p_rows // tmg
    grid_spec = pltpu.PrefetchScalarGridSpec(
        num_scalar_prefetch=2,
        grid=(nt,),
        in_specs=[
            pl.BlockSpec((tmg,), lambda j, te, nv: (j,), memory_space=pltpu.SMEM),
            pl.BlockSpec((tmg,), lambda j, te, nv: (jnp.minimum(j + 1, nt - 1),),
                         memory_space=pltpu.SMEM),
            pl.BlockSpec(memory_space=pl.ANY),
            pl.BlockSpec((1, d, f2), lambda j, te, nv: (te[j], 0, 0)),
            pl.BlockSpec((1, f2 // 2, d), lambda j, te, nv: (te[j], 0, 0)),
        ],
        out_specs=pl.BlockSpec((tmg * ROW_TILE, LANES), lambda j, te, nv: (j, 0)),
        scratch_shapes=[
            pltpu.VMEM((2, tmg * ROW_TILE, LANES), F32),
            pltpu.SemaphoreType.DMA((2,)),
            pltpu.VMEM((d, f2), BF16),
            pltpu.VMEM((f2 // 2, d), BF16),
        ],
    )
    return pl.pallas_call(
        _experts_kernel,
        grid_spec=grid_spec,
        out_shape=jax.ShapeDtypeStruct((p_rows * ROW_TILE, LANES), F32),
        compiler_params=_params("arbitrary"),
        name="moe_experts",
    )(tile_expert, n_valid, src_tok, src_tok, hs, w_in, w_out)


def _combine_kernel(pos_cur_ref, pos_nxt_ref, ys_hbm, route_ref, x_ref, o_ref, ybuf, sems):
    i = pl.program_id(0)
    n_steps = pl.num_programs(0)
    tm = x_ref.shape[0]
    slot = i % 2

    @pl.when(i == 0)
    def _():
        _gather_rows(pos_cur_ref, ys_hbm, ybuf.at[0], sems.at[0], 2 * tm)

    @pl.when(i + 1 < n_steps)
    def _():
        _gather_rows(pos_nxt_ref, ys_hbm, ybuf.at[1 - slot], sems.at[1 - slot], 2 * tm)

    _wait_rows(ys_hbm, ybuf.at[slot], sems.at[slot])
    route = route_ref[...]
    y1 = _from_token_tiles(ybuf.at[slot], 0, tm, 2 * ROW_TILE)
    y2 = _from_token_tiles(ybuf.at[slot], ROW_TILE, tm, 2 * ROW_TILE)
    o_ref[...] = x_ref[...] + route[:, 2:3] * y1 + route[:, 3:4] * y2


def combine(ys, pos, route, x, tm):
    n, d = x.shape
    nsteps = n // tm
    return pl.pallas_call(
        _combine_kernel,
        grid=(nsteps,),
        in_specs=[
            pl.BlockSpec((2 * tm,), lambda i: (i,), memory_space=pltpu.SMEM),
            pl.BlockSpec((2 * tm,), lambda i: (jnp.minimum(i + 1, nsteps - 1),),
                         memory_space=pltpu.SMEM),
            pl.BlockSpec(memory_space=pl.ANY),
            pl.BlockSpec((tm, LANES), lambda i: (i, 0)),
            pl.BlockSpec((tm, d), lambda i: (i, 0)),
        ],
        out_specs=pl.BlockSpec((tm, d), lambda i: (i, 0)),
        out_shape=jax.ShapeDtypeStruct((n, d), F32),
        scratch_shapes=[pltpu.VMEM((2, 2 * tm * ROW_TILE, LANES), F32),
                        pltpu.SemaphoreType.DMA((2,))],
        compiler_params=_params("arbitrary"),
        name="moe_combine",
    )(pos, pos, ys, route, x)


def moe_layer(x, g, wr, br, w_in, w_out):
    n = x.shape[0]
    hs, route = router(x, g, wr, br, MOE_TM)
    rank, counts = plan(route, MOE_TM)
    ids = route[:, 0:2].astype(jnp.int32)
    cnt = counts[0, :N_EXPERTS].astype(jnp.int32)
    padded = (cnt + MOE_TMG - 1) // MOE_TMG * MOE_TMG
    ends = jnp.cumsum(padded)
    starts = ends - padded
    pos = starts[ids] + rank[:, 0:2].astype(jnp.int32)
    p_rows = (2 * n + N_EXPERTS * (MOE_TMG - 1)) // MOE_TMG * MOE_TMG
    nt = p_rows // MOE_TMG
    tok = jnp.broadcast_to(jnp.arange(n, dtype=jnp.int32)[:, None], (n, 2))
    src_tok = jnp.zeros((p_rows,), jnp.int32).at[pos.reshape(-1)].set(tok.reshape(-1))
    tile_expert = jnp.minimum(
        jnp.searchsorted(ends, jnp.arange(nt, dtype=jnp.int32) * MOE_TMG, side="right"),
        N_EXPERTS - 1).astype(jnp.int32)
    n_valid = (ends[-1:] // MOE_TMG).astype(jnp.int32)
    ys = experts(hs, src_tok, tile_expert, n_valid, w_in, w_out, MOE_TMG)
    return combine(ys, pos.reshape(-1), route, x, MOE_TM)


def _qkv_kernel(x_ref, g_ref, w_ref, qg_ref, kg_ref, seg_ref, q_ref, k_ref, v_ref):
    h = _rms(x_ref[...], g_ref[...]).astype(BF16)
    qkv = jnp.dot(h, w_ref[...], preferred_element_type=F32)
    nq = N_HEADS * HEAD_DIM
    nk = N_KV * HEAD_DIM
    q = qkv[:, :nq]
    k = qkv[:, nq:nq + nk]
    v_ref[...] = qkv[:, nq + nk:]

    def seg_mean_sq(z, seg):
        zz = z * z
        hi = zz.astype(BF16)
        lo = (zz - hi.astype(F32)).astype(BF16)
        return (jnp.dot(hi, seg, preferred_element_type=F32)
                + jnp.dot(lo, seg, preferred_element_type=F32))

    seg = seg_ref[...]
    qn = q * lax.rsqrt(seg_mean_sq(q, seg) + RMS_EPS) * qg_ref[...]
    q_ref[...] = (qn * (HEAD_DIM ** -0.5)).astype(BF16)
    kn = k * lax.rsqrt(seg_mean_sq(k, seg[:nk, :nk]) + RMS_EPS) * kg_ref[...]
    k_ref[...] = kn


def qkv_proj(x, g, w, qg, kg, seg, tm):
    n, d = x.shape
    nq = N_HEADS * HEAD_DIM
    nk = N_KV * HEAD_DIM
    return pl.pallas_call(
        _qkv_kernel,
        grid=(n // tm,),
        in_specs=[pl.BlockSpec((tm, d), lambda i: (i, 0)), _full((1, d)),
                  _full((d, nq + 2 * nk)), _full((1, nq)), _full((1, nk)), _full((nq, nq))],
        out_specs=[pl.BlockSpec((tm, nq), lambda i: (i, 0)),
                   pl.BlockSpec((tm, nk), lambda i: (i, 0)),
                   pl.BlockSpec((tm, nk), lambda i: (i, 0))],
        out_shape=[jax.ShapeDtypeStruct((n, nq), BF16),
                   jax.ShapeDtypeStruct((n, nk), F32),
                   jax.ShapeDtypeStruct((n, nk), F32)],
        compiler_params=_params("parallel"),
        name="qkv_proj",
    )(x, g, w, qg, kg, seg)


def _attn_prompt_kernel(q_ref, kc_ref, kp_ref, vc_ref, vp_ref, tbl_ref, sink_ref, x_ref,
                        wo_ref, o_ref, cat_ref):
    n = pl.program_id(1)
    first = jnp.where(n == 0, NEG_INF, 0.0).astype(F32)
    kk = jnp.concatenate([kp_ref[...], kc_ref[...]], axis=0).astype(BF16)
    vv = jnp.concatenate([vp_ref[...], vc_ref[...]], axis=0).astype(BF16)
    col = lax.broadcasted_iota(jnp.int32, (WINDOW, 2 * WINDOW), 1)
    prev_mask = jnp.where(col < WINDOW, first, 0.0)
    for h in range(N_HEADS):
        kh = h // GROUP
        qh = q_ref[:, h * HEAD_DIM:(h + 1) * HEAD_DIM]
        kslice = kk[:, kh * HEAD_DIM:(kh + 1) * HEAD_DIM]
        vslice = vv[:, kh * HEAD_DIM:(kh + 1) * HEAD_DIM]
        s = lax.dot_general(qh, kslice, (((1,), (1,)), ((), ())), preferred_element_type=F32)
        logits = s + tbl_ref[h] + prev_mask
        sink = sink_ref[h]
        m = jnp.maximum(jnp.max(logits, axis=-1, keepdims=True), sink)
        e = jnp.exp(logits - m)
        denom = jnp.sum(e, axis=-1, keepdims=True) + jnp.exp(sink - m)
        p = e * (1.0 / denom)
        cat_ref[:, h * HEAD_DIM:(h + 1) * HEAD_DIM] = jnp.dot(
            p.astype(BF16), vslice, preferred_element_type=F32).astype(BF16)
    o_ref[...] = x_ref[...] + jnp.dot(cat_ref[...], wo_ref[...], preferred_element_type=F32)


def attn_prompt(q, k, v, tbl, sinks, x, wo, batch, seq):
    n, d = x.shape
    nb = seq // WINDOW
    nk = N_KV * HEAD_DIM
    cur = lambda b, i: (b * nb + i, 0)
    prev = lambda b, i: (b * nb + jnp.maximum(i - 1, 0), 0)
    return pl.pallas_call(
        _attn_prompt_kernel,
        grid=(batch, nb),
        in_specs=[
            pl.BlockSpec((WINDOW, d), cur),
            pl.BlockSpec((WINDOW, nk), cur), pl.BlockSpec((WINDOW, nk), prev),
            pl.BlockSpec((WINDOW, nk), cur), pl.BlockSpec((WINDOW, nk), prev),
            _full((N_HEADS, WINDOW, 2 * WINDOW)),
            pl.BlockSpec(memory_space=pltpu.SMEM),
            pl.BlockSpec((WINDOW, d), cur),
            _full((d, d)),
        ],
        out_specs=pl.BlockSpec((WINDOW, d), cur),
        out_shape=jax.ShapeDtypeStruct((n, d), F32),
        scratch_shapes=[pltpu.VMEM((WINDOW, d), BF16)],
        compiler_params=_params("parallel", "parallel"),
        name="attn_prompt",
    )(q, k, k, v, v, tbl, sinks, x, wo)


def _attn_sample_kernel(q_ref, kn_ref, vn_ref, ck_ref, cv_ref, tbl_ref, sink_ref, o_ref,
                        kf_ref, vf_ref):
    bs = q_ref.shape[0]
    ts = kn_ref.shape[1]
    for b in range(bs):
        kf_ref[0:WINDOW, :] = ck_ref[b]
        kf_ref[WINDOW:WINDOW + ts, :] = kn_ref[b]
        vf_ref[0:WINDOW, :] = cv_ref[b]
        vf_ref[WINDOW:WINDOW + ts, :] = vn_ref[b]
        for kh in range(N_KV):
            kslice = kf_ref[:, kh * HEAD_DIM:(kh + 1) * HEAD_DIM].astype(BF16)
            vslice = vf_ref[:, kh * HEAD_DIM:(kh + 1) * HEAD_DIM].astype(BF16)
            qt = q_ref[b, kh]
            s = lax.dot_general(qt, kslice, (((1,), (1,)), ((), ())),
                                preferred_element_type=F32)
            logits = s + tbl_ref[kh]
            sink = sink_ref[kh]
            m = jnp.maximum(jnp.max(logits, axis=-1, keepdims=True), sink)
            e = jnp.exp(logits - m)
            denom = jnp.sum(e, axis=-1, keepdims=True) + jnp.exp(sink - m)
            p = (e * (1.0 / denom)).astype(BF16)
            o_ref[b, kh] = jnp.dot(p, vslice, preferred_element_type=F32).astype(BF16)


def attn_sample(q4, kn, vn, ck, cv, tbl, sink, bs):
    nb = q4.shape[0]
    ts = kn.shape[1]
    nk = N_KV * HEAD_DIM
    tg = ts * GROUP
    kt = WINDOW + ts
    return pl.pallas_call(
        _attn_sample_kernel,
        grid=(nb // bs,),
        in_specs=[
            pl.BlockSpec((bs, N_KV, tg, HEAD_DIM), lambda i: (i, 0, 0, 0)),
            pl.BlockSpec((bs, ts, nk), lambda i: (i, 0, 0)),
            pl.BlockSpec((bs, ts, nk), lambda i: (i, 0, 0)),
            pl.BlockSpec((bs, WINDOW, nk), lambda i: (i, 0, 0)),
            pl.BlockSpec((bs, WINDOW, nk), lambda i: (i, 0, 0)),
            _full((N_KV, tg, kt)),
            _full((N_KV, tg, 1)),
        ],
        out_specs=pl.BlockSpec((bs, N_KV, tg, HEAD_DIM), lambda i: (i, 0, 0, 0)),
        out_shape=jax.ShapeDtypeStruct((nb, N_KV, tg, HEAD_DIM), BF16),
        scratch_shapes=[pltpu.VMEM((kt, nk), F32), pltpu.VMEM((kt, nk), F32)],
        compiler_params=_params("parallel"),
        name="attn_sample",
    )(q4, kn, vn, ck, cv, tbl, sink)


def _proj_res_kernel(a_ref, w_ref, x_ref, o_ref):
    o_ref[...] = x_ref[...] + jnp.dot(a_ref[...], w_ref[...], preferred_element_type=F32)


def proj_residual(a, w, x, tm):
    n, d = x.shape
    kdim = a.shape[1]
    return pl.pallas_call(
        _proj_res_kernel,
        grid=(n // tm,),
        in_specs=[pl.BlockSpec((tm, kdim), lambda i: (i, 0)), _full((kdim, d)),
                  pl.BlockSpec((tm, d), lambda i: (i, 0))],
        out_specs=pl.BlockSpec((tm, d), lambda i: (i, 0)),
        out_shape=jax.ShapeDtypeStruct((n, d), F32),
        compiler_params=_params("parallel"),
        name="proj_residual",
    )(a, w, x)


def _t5_bucket_np(dist):
    n = np.maximum(dist, 0)
    max_exact = N_BUCKETS // 2
    large = max_exact + (np.log(np.maximum(n, 1).astype(np.float32) / max_exact)
                         / math.log(MAX_DISTANCE / max_exact) * (N_BUCKETS - max_exact)).astype(np.int32)
    large = np.minimum(large, N_BUCKETS - 1)
    return np.where(n < max_exact, n, large)


def _bias_table(rel_bias, dist):
    valid = (dist >= 0) & (dist <= WINDOW)
    onehot = (np.asarray(_t5_bucket_np(dist))[..., None] == np.arange(N_BUCKETS)).astype(np.float32)
    b = jnp.einsum("qkb,bh->hqk", jnp.asarray(onehot), rel_bias.astype(F32),
                   precision=lax.Precision.HIGHEST)
    return jnp.where(jnp.asarray(valid)[None], b, NEG_INF)


def kernel(x_prompt, x_sample, state_conv, cache_swa_k, cache_swa_v, rms_mix_g, rms_ffn_g, conv_w_in, conv_dw_w, conv_dw_b, conv_ln_g, conv_ln_b, conv_w_out, attn_w_qkv, attn_q_norm_g, attn_k_norm_g, attn_sinks, attn_w_o, rel_bias, router_group_w, router_group_b, router_expert_w, router_expert_b, expert_w_in, expert_w_out):
    batch, seq, d = x_prompt.shape
    nsb, ts, _ = x_sample.shape
    xp = x_prompt.reshape(batch * seq, d)
    xs = x_sample.reshape(nsb * ts, d)
    row = lambda a: a.reshape(1, -1).astype(F32)

    def router_w(i):
        we = jnp.transpose(router_expert_w[i], (1, 0, 2)).reshape(d, N_EXPERTS)
        wr = jnp.concatenate([we, router_group_w[i]], axis=1)
        wr = jnp.pad(wr, ((0, 0), (0, LANES - wr.shape[1])))
        br = jnp.concatenate([router_expert_b[i].reshape(-1), router_group_b[i]])
        br = jnp.pad(br, (0, LANES - br.shape[0])).reshape(1, LANES)
        return wr.astype(F32), br.astype(F32)

    g0 = row(rms_mix_g[0])
    w_in = conv_w_in[0].astype(BF16)
    dww = jnp.pad(conv_dw_w[0].astype(F32), ((0, HALO - CONV_WIDTH), (0, 0)))
    dwb, lng, lnb = row(conv_dw_b[0]), row(conv_ln_g[0]), row(conv_ln_b[0])
    w_out = conv_w_out[0].astype(BF16)
    up = glu_proj(xp, g0, w_in, 512)
    us = glu_proj(xs, g0, w_in, 512)
    xp = conv_prompt(up, xp, dww, dwb, lng, lnb, w_out, batch, seq, 256)
    us3 = us.reshape(nsb, ts, -1)
    xs = conv_sample(us3, state_conv[0], xs, dww, dwb, lng, lnb, w_out, 32)
    conv_p = up.reshape(batch, seq, -1)[:, seq - PAST:]
    conv_s = jnp.concatenate([state_conv[0], us3], axis=1)[:, ts:]

    wr0, br0 = router_w(0)
    xp = moe_layer(xp, row(rms_ffn_g[0]), wr0, br0, expert_w_in[0], expert_w_out[0])
    xs = moe_layer(xs, row(rms_ffn_g[0]), wr0, br0, expert_w_in[0], expert_w_out[0])

    g1 = row(rms_mix_g[1])
    w_qkv = attn_w_qkv[0].astype(BF16)
    qg = jnp.tile(attn_q_norm_g[0].astype(F32), N_HEADS).reshape(1, -1)
    kg = jnp.tile(attn_k_norm_g[0].astype(F32), N_KV).reshape(1, -1)
    nq = N_HEADS * HEAD_DIM
    seg = jnp.asarray(np.kron(np.eye(N_HEADS), np.ones((HEAD_DIM, HEAD_DIM))) / HEAD_DIM, BF16)
    w_o = attn_w_o[0].astype(BF16)
    sinks = attn_sinks[0].astype(F32)

    qp, kp, vp = qkv_proj(xp, g1, w_qkv, qg, kg, seg, 512)
    qs, ks, vs = qkv_proj(xs, g1, w_qkv, qg, kg, seg, 512)

    q_off = np.arange(WINDOW)[:, None]
    dist_p = q_off + WINDOW - np.arange(2 * WINDOW)[None, :]
    tbl_p = _bias_table(rel_bias, dist_p)
    xp = attn_prompt(qp, kp, vp, tbl_p, sinks, xp, w_o, batch, seq)

    kt = WINDOW + ts
    dist_s = np.arange(ts)[:, None] + WINDOW - np.arange(kt)[None, :]
    tbl_s = _bias_table(rel_bias, dist_s)
    tbl_s = jnp.transpose(tbl_s.reshape(N_KV, GROUP, ts, kt), (0, 2, 1, 3)).reshape(N_KV, ts * GROUP, kt)
    sink_s = jnp.tile(sinks.reshape(N_KV, 1, GROUP), (1, ts, 1)).reshape(N_KV, ts * GROUP, 1)
    q4 = jnp.transpose(qs.reshape(nsb, ts, N_KV, GROUP, HEAD_DIM), (0, 2, 1, 3, 4))
    q4 = q4.reshape(nsb, N_KV, ts * GROUP, HEAD_DIM)
    nk = N_KV * HEAD_DIM
    ks3, vs3 = ks.reshape(nsb, ts, nk), vs.reshape(nsb, ts, nk)
    ck = cache_swa_k[0].reshape(nsb, WINDOW, nk)
    cv = cache_swa_v[0].reshape(nsb, WINDOW, nk)
    o4 = attn_sample(q4, ks3, vs3, ck, cv, tbl_s, sink_s, 16)
    os_ = jnp.transpose(o4.reshape(nsb, N_KV, ts, GROUP, HEAD_DIM), (0, 2, 1, 3, 4)).reshape(nsb * ts, nq)
    xs = proj_residual(os_, w_o, xs, 512)

    k_p = kp.reshape(batch, seq, N_KV, HEAD_DIM)[:, seq - WINDOW:]
    v_p = vp.reshape(batch, seq, N_KV, HEAD_DIM)[:, seq - WINDOW:]
    k_s = jnp.concatenate([cache_swa_k[0], ks.reshape(nsb, ts, N_KV, HEAD_DIM)], axis=1)[:, ts:]
    v_s = jnp.concatenate([cache_swa_v[0], vs.reshape(nsb, ts, N_KV, HEAD_DIM)], axis=1)[:, ts:]

    wr1, br1 = router_w(1)
    xp = moe_layer(xp, row(rms_ffn_g[1]), wr1, br1, expert_w_in[1], expert_w_out[1])
    xs = moe_layer(xs, row(rms_ffn_g[1]), wr1, br1, expert_w_in[1], expert_w_out[1])

    return (xp.reshape(batch, seq, d), xs.reshape(nsb, ts, d),
            conv_p[None], conv_s[None], k_p[None], v_p[None], k_s[None], v_s[None])
```

```python
import functools
import math

import numpy as np
import jax
import jax.numpy as jnp
from jax import lax
from jax.experimental import pallas as pl
from jax.experimental.pallas import tpu as pltpu

D_MODEL = 1024
DEPTH = 2
CONV_WIDTH = 31
PAST = CONV_WIDTH - 1
HEAD_DIM = 64
N_HEADS = 16
N_KV = 2
GROUP = 8
WINDOW = 128
N_BUCKETS = 32
MAX_DISTANCE = 128
N_GROUPS = 4
EPG = 8
N_EXPERTS = 32
D_EXPERT = 256
RMS_EPS = 1e-6
LN_EPS = 1e-5
NEG_INF = -1e30

F32 = jnp.float32
BF16 = jnp.bfloat16
LANES = 128
ROW_TILE = 8
MOE_TM = 256
MOE_TMG = 256
MOE_CHUNK = 4096
VMEM_LIMIT = 48 * 1024 * 1024


def _params(*sem):
    return pltpu.CompilerParams(dimension_semantics=sem, vmem_limit_bytes=VMEM_LIMIT)


def _rms(x, g):
    return x * lax.rsqrt(jnp.mean(x * x, axis=-1, keepdims=True) + RMS_EPS) * g


def _sigmoid(x):
    return 1.0 / (1.0 + jnp.exp(-x))


def _full(shape):
    return pl.BlockSpec(shape, lambda *_: (0,) * len(shape))


def _glu_kernel(x_ref, g_ref, wa_ref, wg_ref, u_ref):
    h = _rms(x_ref[...], g_ref[...]).astype(BF16)
    a = jnp.dot(h, wa_ref[...], preferred_element_type=F32)
    gate = jnp.dot(h, wg_ref[...], preferred_element_type=F32)
    u_ref[...] = a * _sigmoid(gate)


def glu_proj(x, g, w_in, tm):
    n, d = x.shape
    c = w_in.shape[1] // 2
    return pl.pallas_call(
        _glu_kernel,
        grid=(n // tm,),
        in_specs=[
            pl.BlockSpec((tm, d), lambda i: (i, 0)),
            _full((1, d)),
            pl.BlockSpec((d, c), lambda i: (0, 0)),
            pl.BlockSpec((d, c), lambda i: (0, 1)),
        ],
        out_specs=pl.BlockSpec((tm, c), lambda i: (i, 0)),
        out_shape=jax.ShapeDtypeStruct((n, c), F32),
        compiler_params=_params("parallel"),
        name="glu_proj",
    )(x, g, w_in, w_in)


def _ln_silu_out(y, lng, lnb, wout_ref, x):
    mu = jnp.mean(y, axis=-1, keepdims=True)
    yc = y - mu
    z = yc * lax.rsqrt(jnp.mean(yc * yc, axis=-1, keepdims=True) + LN_EPS) * lng + lnb
    z = z * _sigmoid(z)
    return x + jnp.dot(z.astype(BF16), wout_ref[...], preferred_element_type=F32)


HALO = 32
CONV_RC = 32
CONV_CC = 256


def _conv_prompt_kernel(ucur_ref, uprev_ref, x_ref, dww_ref, dwb_ref, lng_ref, lnb_ref,
                        wout_ref, o_ref, up_ref, y_ref):
    t = pl.program_id(1)
    tt, c = ucur_ref.shape
    keep = (t > 0).astype(F32)
    up_ref[0:HALO, :] = uprev_ref[...] * keep
    up_ref[HALO:, :] = ucur_ref[...]
    off = HALO - PAST
    for r0 in range(0, tt, CONV_RC):
        for c0 in range(0, c, CONV_CC):
            acc = jnp.zeros((CONV_RC, CONV_CC), F32) + dwb_ref[:, c0:c0 + CONV_CC]
            for k in range(CONV_WIDTH):
                s = r0 + off + k
                acc = acc + up_ref[s:s + CONV_RC, c0:c0 + CONV_CC] * dww_ref[k:k + 1, c0:c0 + CONV_CC]
            y_ref[r0:r0 + CONV_RC, c0:c0 + CONV_CC] = acc
    o_ref[...] = _ln_silu_out(y_ref[...], lng_ref[...], lnb_ref[...], wout_ref, x_ref[...])


def conv_prompt(u, x, dww, dwb, lng, lnb, wout, batch, seq, tt):
    n, c = u.shape
    d = x.shape[1]
    nt = seq // tt
    hb = tt // HALO
    return pl.pallas_call(
        _conv_prompt_kernel,
        grid=(batch, nt),
        in_specs=[
            pl.BlockSpec((tt, c), lambda b, t: (b * nt + t, 0)),
            pl.BlockSpec((HALO, c), lambda b, t: (jnp.maximum((b * nt + t) * hb - 1, 0), 0)),
            pl.BlockSpec((tt, d), lambda b, t: (b * nt + t, 0)),
            _full((HALO, c)), _full((1, c)), _full((1, c)), _full((1, c)),
            _full((c, d)),
        ],
        out_specs=pl.BlockSpec((tt, d), lambda b, t: (b * nt + t, 0)),
        out_shape=jax.ShapeDtypeStruct((n, d), F32),
        scratch_shapes=[pltpu.VMEM((tt + HALO, c), F32), pltpu.VMEM((tt, c), F32)],
        compiler_params=_params("parallel", "parallel"),
        name="conv_prompt",
    )(u, u, x, dww, dwb, lng, lnb, wout)


CONV_SB = 4


def _conv_sample_kernel(u_ref, st_ref, x_ref, dww_ref, dwb_ref, lng_ref, lnb_ref,
                        wout_ref, o_ref, up_ref, y_ref):
    bs, ts, c = u_ref.shape
    up_ref[:, 0:PAST, :] = st_ref[...]
    up_ref[:, PAST:PAST + ts, :] = u_ref[...]
    for b0 in range(0, bs, CONV_SB):
        acc = jnp.zeros((CONV_SB, ts, c), F32) + dwb_ref[...][None]
        for k in range(CONV_WIDTH):
            acc = acc + up_ref[b0:b0 + CONV_SB, k:k + ts, :] * dww_ref[k:k + 1, :][None]
        y_ref[b0 * ts:(b0 + CONV_SB) * ts, :] = acc.reshape(CONV_SB * ts, c)
    o_ref[...] = _ln_silu_out(y_ref[...], lng_ref[...], lnb_ref[...], wout_ref, x_ref[...])


def conv_sample(u3, state, x, dww, dwb, lng, lnb, wout, bs):
    nb, ts, c = u3.shape
    d = x.shape[1]
    return pl.pallas_call(
        _conv_sample_kernel,
        grid=(nb // bs,),
        in_specs=[
            pl.BlockSpec((bs, ts, c), lambda i: (i, 0, 0)),
            pl.BlockSpec((bs, PAST, c), lambda i: (i, 0, 0)),
            pl.BlockSpec((bs * ts, d), lambda i: (i, 0)),
            _full((HALO, c)), _full((1, c)), _full((1, c)), _full((1, c)),
            _full((c, d)),
        ],
        out_specs=pl.BlockSpec((bs * ts, d), lambda i: (i, 0)),
        out_shape=jax.ShapeDtypeStruct((nb * ts, d), F32),
        scratch_shapes=[pltpu.VMEM((bs, PAST + ts, c), F32), pltpu.VMEM((bs * ts, c), F32)],
        compiler_params=_params("parallel"),
        name="conv_sample",
    )(u3, state, x, dww, dwb, lng, lnb, wout)


def _router_kernel(x_ref, g_ref, wr_ref, br_ref, hs_ref, route_ref):
    h = _rms(x_ref[...], g_ref[...])
    tm, d = h.shape
    for j in range(d // LANES):
        hs_ref[pl.ds(j, tm, stride=ROW_TILE), :] = h[:, j * LANES:(j + 1) * LANES]
    logits = jnp.dot(h, wr_ref[...], preferred_element_type=F32,
                     precision=lax.Precision.HIGHEST) + br_ref[...]
    lane = lax.broadcasted_iota(jnp.int32, logits.shape, 1).astype(F32)
    big = jnp.float32(LANES)
    is_g = (lane >= N_EXPERTS) & (lane < N_EXPERTS + N_GROUPS)
    gl = jnp.where(is_g, logits, NEG_INF)
    gm = jnp.max(gl, axis=-1, keepdims=True)
    g_sel = jnp.min(jnp.where(gl == gm, lane, big), axis=-1, keepdims=True) - N_EXPERTS
    gate_g = 1.0 / jnp.sum(jnp.where(is_g, jnp.exp(gl - gm), 0.0), axis=-1, keepdims=True)
    lo = g_sel * EPG
    in_grp = (lane >= lo) & (lane < lo + EPG)
    el = jnp.where(in_grp, logits, NEG_INF)
    v1 = jnp.max(el, axis=-1, keepdims=True)
    i1 = jnp.min(jnp.where(el == v1, lane, big), axis=-1, keepdims=True)
    el2 = jnp.where(lane == i1, NEG_INF, el)
    v2 = jnp.max(el2, axis=-1, keepdims=True)
    i2 = jnp.min(jnp.where(el2 == v2, lane, big), axis=-1, keepdims=True)
    e2 = jnp.exp(v2 - v1)
    w1 = gate_g / (1.0 + e2)
    w2 = gate_g * e2 / (1.0 + e2)
    route_ref[...] = jnp.where(lane == 0.0, i1, jnp.where(lane == 1.0, i2, jnp.where(
        lane == 2.0, w1, jnp.where(lane == 3.0, w2, 0.0))))


def router(x, g, wr, br, tm):
    n, d = x.shape
    return pl.pallas_call(
        _router_kernel,
        grid=(n // tm,),
        in_specs=[pl.BlockSpec((tm, d), lambda i: (i, 0)), _full((1, d)),
                  _full((d, LANES)), _full((1, LANES))],
        out_specs=[pl.BlockSpec((tm * ROW_TILE, LANES), lambda i: (i, 0)),
                   pl.BlockSpec((tm, LANES), lambda i: (i, 0))],
        out_shape=[jax.ShapeDtypeStruct((n * ROW_TILE, LANES), F32),
                   jax.ShapeDtypeStruct((n, LANES), F32)],
        compiler_params=_params("parallel"),
        name="router",
    )(x, g, wr, br)


def _plan_kernel(tiles_per_chunk, route_ref, tri_ref, rank_ref, cnt_ref, carry_ref):
    i = pl.program_id(0)

    @pl.when(i % tiles_per_chunk == 0)
    def _():
        carry_ref[...] = jnp.zeros_like(carry_ref)

    route = route_ref[...]
    lane = lax.broadcasted_iota(jnp.int32, route.shape, 1).astype(F32)
    i1 = route[:, 0:1]
    i2 = route[:, 1:2]
    hit1 = lane == i1
    hit2 = lane == i2
    onehot = jnp.where(hit1 | hit2, 1.0, 0.0)
    before = carry_ref[...] + jnp.dot(tri_ref[...], onehot.astype(BF16), preferred_element_type=F32)
    r1 = jnp.sum(jnp.where(hit1, before, 0.0), axis=-1, keepdims=True)
    r2 = jnp.sum(jnp.where(hit2, before, 0.0), axis=-1, keepdims=True)
    rank_ref[...] = jnp.where(lane == 0.0, r1, jnp.where(lane == 1.0, r2, 0.0))
    carry_ref[...] += jnp.sum(onehot, axis=0, keepdims=True)
    cnt_ref[...] = jnp.broadcast_to(carry_ref[...], cnt_ref.shape)


def plan(route, tm, chunk):
    n = route.shape[0]
    tpc = chunk // tm
    tri = jnp.asarray(np.tril(np.ones((tm, tm), np.float32), -1), BF16)
    return pl.pallas_call(
        functools.partial(_plan_kernel, tpc),
        grid=(n // tm,),
        in_specs=[pl.BlockSpec((tm, LANES), lambda i: (i, 0)), _full((tm, tm))],
        out_specs=[pl.BlockSpec((tm, LANES), lambda i: (i, 0)),
                   pl.BlockSpec((ROW_TILE, LANES), lambda i: (i // tpc, 0))],
        out_shape=[jax.ShapeDtypeStruct((n, LANES), F32),
                   jax.ShapeDtypeStruct((n // chunk * ROW_TILE, LANES), F32)],
        scratch_shapes=[pltpu.VMEM((1, LANES), F32)],
        compiler_params=_params("arbitrary"),
        name="moe_plan",
    )(route, tri)


def _gather_rows(idx_ref, src_hbm, dst, sem, n_rows):
    unroll = 8

    def body(c, carry):
        for u in range(unroll):
            m = c * unroll + u
            tok = idx_ref[m]
            pltpu.make_async_copy(
                src_hbm.at[pl.ds(pl.multiple_of(tok * ROW_TILE, ROW_TILE), ROW_TILE), :],
                dst.at[pl.ds(pl.multiple_of(m * ROW_TILE, ROW_TILE), ROW_TILE), :],
                sem).start()
        return carry

    lax.fori_loop(0, n_rows // unroll, body, 0)


def _wait_rows(src_hbm, dst, sem):
    pltpu.make_async_copy(src_hbm.at[pl.ds(0, dst.shape[0]), :], dst, sem).wait()


def _from_token_tiles(buf, start, n_rows, stride):
    return jnp.concatenate(
        [buf[pl.ds(start + j, n_rows, stride=stride), :] for j in range(ROW_TILE)], axis=-1)


def _experts_kernel(tiles_per_chunk, tmg, te_ref, nv_ref, pos_ref, fill_hbm, hs_hbm,
                    win_ref, wout_ref, ys_ref, hsv, gbuf, src_ref, sems, winb, woutb):
    c = pl.program_id(0)
    j = pl.program_id(1)
    t = c * tiles_per_chunk + j
    nv = nv_ref[c]
    chunk = pos_ref.shape[0] // 2
    unroll = 8

    @pl.when(j == 0)
    def _():
        load = pltpu.make_async_copy(
            hs_hbm.at[pl.ds(pl.multiple_of(c * chunk * ROW_TILE, ROW_TILE), chunk * ROW_TILE), :],
            hsv.at[pl.ds(0, chunk * ROW_TILE), :], sems.at[0])
        load.start()
        fill = pltpu.make_async_copy(fill_hbm, src_ref, sems.at[1])
        fill.start()
        hsv[pl.ds(chunk * ROW_TILE, ROW_TILE), :] = jnp.zeros((ROW_TILE, LANES), F32)
        fill.wait()

        def scatter(i, carry):
            for u in range(unroll):
                tok = i * unroll + u
                src_ref[pos_ref[2 * tok]] = tok
                src_ref[pos_ref[2 * tok + 1]] = tok
            return carry

        lax.fori_loop(0, chunk // unroll, scatter, 0)
        load.wait()

    changed = jnp.logical_or(j == 0, te_ref[t] != te_ref[jnp.maximum(t - 1, 0)])

    @pl.when(jnp.logical_and(changed, j < nv))
    def _():
        winb[...] = win_ref[0].astype(BF16)
        woutb[...] = wout_ref[0].astype(BF16)

    @pl.when(j < nv)
    def _():
        base = j * tmg

        def gather(i, carry):
            for u in range(unroll):
                m = i * unroll + u
                tok = src_ref[base + m]
                gbuf[pl.ds(pl.multiple_of(m * ROW_TILE, ROW_TILE), ROW_TILE), :] = hsv[
                    pl.ds(pl.multiple_of(tok * ROW_TILE, ROW_TILE), ROW_TILE), :]
            return carry

        lax.fori_loop(0, tmg // unroll, gather, 0)
        x = _from_token_tiles(gbuf, 0, tmg, ROW_TILE).astype(BF16)
        hid = jnp.dot(x, winb[...], preferred_element_type=F32)
        a = hid[:, :D_EXPERT]
        u = hid[:, D_EXPERT:]
        act = (a * _sigmoid(a) * u).astype(BF16)
        y = jnp.dot(act, woutb[...], preferred_element_type=F32)
        for k in range(ROW_TILE):
            ys_ref[pl.ds(k, tmg, stride=ROW_TILE), :] = y[:, k * LANES:(k + 1) * LANES]

    @pl.when(j >= nv)
    def _():
        ys_ref[...] = jnp.zeros_like(ys_ref)


def experts(hs, pos_local, tile_expert, n_valid, w_in, w_out, tmg, chunk, tiles_per_chunk):
    ne, d, f2 = w_in.shape
    nc = hs.shape[0] // (chunk * ROW_TILE)
    rows_per_chunk = tiles_per_chunk * tmg
    fill = jnp.full((rows_per_chunk,), chunk, jnp.int32)
    grid_spec = pltpu.PrefetchScalarGridSpec(
        num_scalar_prefetch=2,
        grid=(nc, tiles_per_chunk),
        in_specs=[
            pl.BlockSpec((2 * chunk,), lambda c, j, te, nv: (c,), memory_space=pltpu.SMEM),
            pl.BlockSpec(memory_space=pl.ANY),
            pl.BlockSpec(memory_space=pl.ANY),
            pl.BlockSpec((1, d, f2), lambda c, j, te, nv: (te[c * tiles_per_chunk + j], 0, 0)),
            pl.BlockSpec((1, f2 // 2, d), lambda c, j, te, nv: (te[c * tiles_per_chunk + j], 0, 0)),
        ],
        out_specs=pl.BlockSpec((tmg * ROW_TILE, LANES),
                               lambda c, j, te, nv: (c * tiles_per_chunk + j, 0)),
        scratch_shapes=[
            pltpu.VMEM(((chunk + 1) * ROW_TILE, LANES), F32),
            pltpu.VMEM((tmg * ROW_TILE, LANES), F32),
            pltpu.SMEM((rows_per_chunk,), jnp.int32),
            pltpu.SemaphoreType.DMA((2,)),
            pltpu.VMEM((d, f2), BF16),
            pltpu.VMEM((f2 // 2, d), BF16),
        ],
    )
    return pl.pallas_call(
        functools.partial(_experts_kernel, tiles_per_chunk, tmg),
        grid_spec=grid_spec,
        out_shape=jax.ShapeDtypeStruct((nc * rows_per_chunk * ROW_TILE, LANES), F32),
        compiler_params=_params("arbitrary", "arbitrary"),
        name="moe_experts",
    )(tile_expert, n_valid, pos_local, fill, hs, w_in, w_out)


def _combine_kernel(pos_cur_ref, pos_nxt_ref, ys_hbm, route_ref, x_ref, o_ref, ybuf, sems):
    i = pl.program_id(0)
    n_steps = pl.num_programs(0)
    tm = x_ref.shape[0]
    slot = i % 2

    @pl.when(i == 0)
    def _():
        _gather_rows(pos_cur_ref, ys_hbm, ybuf.at[0], sems.at[0], 2 * tm)

    @pl.when(i + 1 < n_steps)
    def _():
        _gather_rows(pos_nxt_ref, ys_hbm, ybuf.at[1 - slot], sems.at[1 - slot], 2 * tm)

    _wait_rows(ys_hbm, ybuf.at[slot], sems.at[slot])
    route = route_ref[...]
    y1 = _from_token_tiles(ybuf.at[slot], 0, tm, 2 * ROW_TILE)
    y2 = _from_token_tiles(ybuf.at[slot], ROW_TILE, tm, 2 * ROW_TILE)
    o_ref[...] = x_ref[...] + route[:, 2:3] * y1 + route[:, 3:4] * y2


def combine(ys, pos, route, x, tm):
    n, d = x.shape
    nsteps = n // tm
    return pl.pallas_call(
        _combine_kernel,
        grid=(nsteps,),
        in_specs=[
            pl.BlockSpec((2 * tm,), lambda i: (i,), memory_space=pltpu.SMEM),
            pl.BlockSpec((2 * tm,), lambda i: (jnp.minimum(i + 1, nsteps - 1),),
                         memory_space=pltpu.SMEM),
            pl.BlockSpec(memory_space=pl.ANY),
            pl.BlockSpec((tm, LANES), lambda i: (i, 0)),
            pl.BlockSpec((tm, d), lambda i: (i, 0)),
        ],
        out_specs=pl.BlockSpec((tm, d), lambda i: (i, 0)),
        out_shape=jax.ShapeDtypeStruct((n, d), F32),
        scratch_shapes=[pltpu.VMEM((2, 2 * tm * ROW_TILE, LANES), F32),
                        pltpu.SemaphoreType.DMA((2,))],
        compiler_params=_params("arbitrary"),
        name="moe_combine",
    )(pos, pos, ys, route, x)


def moe_layer(x, g, wr, br, w_in, w_out):
    n = x.shape[0]
    chunk = min(n, MOE_CHUNK)
    nc = n // chunk
    tpc = (2 * chunk + N_EXPERTS * (MOE_TMG - 1)) // MOE_TMG
    hs, route = router(x, g, wr, br, MOE_TM)
    rank, counts = plan(route, MOE_TM, chunk)
    ids = route[:, 0:2].astype(jnp.int32).reshape(nc, chunk, 2)
    cnt = counts.reshape(nc, ROW_TILE, LANES)[:, 0, :N_EXPERTS].astype(jnp.int32)
    padded = (cnt + MOE_TMG - 1) // MOE_TMG * MOE_TMG
    ends = jnp.cumsum(padded, axis=1)
    starts = ends - padded
    start_of = jnp.take_along_axis(starts, ids.reshape(nc, chunk * 2), axis=1).reshape(nc, chunk, 2)
    pos_local = start_of + rank[:, 0:2].astype(jnp.int32).reshape(nc, chunk, 2)
    tile_row = jnp.arange(tpc, dtype=jnp.int32) * MOE_TMG
    tile_expert = jnp.minimum(jnp.sum(ends[:, None, :] <= tile_row[None, :, None], axis=-1),
                              N_EXPERTS - 1).astype(jnp.int32)
    n_valid = (ends[:, -1] // MOE_TMG).astype(jnp.int32)
    ys = experts(hs, pos_local.reshape(-1), tile_expert.reshape(-1), n_valid, w_in, w_out,
                 MOE_TMG, chunk, tpc)
    pos = pos_local + (jnp.arange(nc, dtype=jnp.int32) * (tpc * MOE_TMG))[:, None, None]
    return combine(ys, pos.reshape(-1), route, x, MOE_TM)


def _qkv_kernel(x_ref, g_ref, w_ref, qg_ref, kg_ref, seg_ref, q_ref, k_ref, v_ref):
    h = _rms(x_ref[...], g_ref[...]).astype(BF16)
    qkv = jnp.dot(h, w_ref[...], preferred_element_type=F32)
    nq = N_HEADS * HEAD_DIM
    nk = N_KV * HEAD_DIM
    q = qkv[:, :nq]
    k = qkv[:, nq:nq + nk]
    v_ref[...] = qkv[:, nq + nk:]

    def seg_mean_sq(z, seg):
        zz = z * z
        hi = zz.astype(BF16)
        lo = (zz - hi.astype(F32)).astype(BF16)
        return (jnp.dot(hi, seg, preferred_element_type=F32)
                + jnp.dot(lo, seg, preferred_element_type=F32))

    seg = seg_ref[...]
    qn = q * lax.rsqrt(seg_mean_sq(q, seg) + RMS_EPS) * qg_ref[...]
    q_ref[...] = (qn * (HEAD_DIM ** -0.5)).astype(BF16)
    kn = k * lax.rsqrt(seg_mean_sq(k, seg[:nk, :nk]) + RMS_EPS) * kg_ref[...]
    k_ref[...] = kn


def qkv_proj(x, g, w, qg, kg, seg, tm):
    n, d = x.shape
    nq = N_HEADS * HEAD_DIM
    nk = N_KV * HEAD_DIM
    return pl.pallas_call(
        _qkv_kernel,
        grid=(n // tm,),
        in_specs=[pl.BlockSpec((tm, d), lambda i: (i, 0)), _full((1, d)),
                  _full((d, nq + 2 * nk)), _full((1, nq)), _full((1, nk)), _full((nq, nq))],
        out_specs=[pl.BlockSpec((tm, nq), lambda i: (i, 0)),
                   pl.BlockSpec((tm, nk), lambda i: (i, 0)),
                   pl.BlockSpec((tm, nk), lambda i: (i, 0))],
        out_shape=[jax.ShapeDtypeStruct((n, nq), BF16),
                   jax.ShapeDtypeStruct((n, nk), F32),
                   jax.ShapeDtypeStruct((n, nk), F32)],
        compiler_params=_params("parallel"),
        name="qkv_proj",
    )(x, g, w, qg, kg, seg)


def _attn_prompt_kernel(q_ref, kc_ref, kp_ref, vc_ref, vp_ref, tbl_ref, sink_ref, x_ref,
                        wo_ref, o_ref, cat_ref):
    n = pl.program_id(1)
    first = jnp.where(n == 0, NEG_INF, 0.0).astype(F32)
    kk = jnp.concatenate([kp_ref[...], kc_ref[...]], axis=0).astype(BF16)
    vv = jnp.concatenate([vp_ref[...], vc_ref[...]], axis=0).astype(BF16)
    col = lax.broadcasted_iota(jnp.int32, (WINDOW, 2 * WINDOW), 1)
    prev_mask = jnp.where(col < WINDOW, first, 0.0)
    for h in range(N_HEADS):
        kh = h // GROUP
        qh = q_ref[:, h * HEAD_DIM:(h + 1) * HEAD_DIM]
        kslice = kk[:, kh * HEAD_DIM:(kh + 1) * HEAD_DIM]
        vslice = vv[:, kh * HEAD_DIM:(kh + 1) * HEAD_DIM]
        s = lax.dot_general(qh, kslice, (((1,), (1,)), ((), ())), preferred_element_type=F32)
        logits = s + tbl_ref[h] + prev_mask
        sink = sink_ref[h]
        m = jnp.maximum(jnp.max(logits, axis=-1, keepdims=True), sink)
        e = jnp.exp(logits - m)
        denom = jnp.sum(e, axis=-1, keepdims=True) + jnp.exp(sink - m)
        p = e * (1.0 / denom)
        cat_ref[:, h * HEAD_DIM:(h + 1) * HEAD_DIM] = jnp.dot(
            p.astype(BF16), vslice, preferred_element_type=F32).astype(BF16)
    o_ref[...] = x_ref[...] + jnp.dot(cat_ref[...], wo_ref[...], preferred_element_type=F32)


def attn_prompt(q, k, v, tbl, sinks, x, wo, batch, seq):
    n, d = x.shape
    nb = seq // WINDOW
    nk = N_KV * HEAD_DIM
    cur = lambda b, i: (b * nb + i, 0)
    prev = lambda b, i: (b * nb + jnp.maximum(i - 1, 0), 0)
    return pl.pallas_call(
        _attn_prompt_kernel,
        grid=(batch, nb),
        in_specs=[
            pl.BlockSpec((WINDOW, d), cur),
            pl.BlockSpec((WINDOW, nk), cur), pl.BlockSpec((WINDOW, nk), prev),
            pl.BlockSpec((WINDOW, nk), cur), pl.BlockSpec((WINDOW, nk), prev),
            _full((N_HEADS, WINDOW, 2 * WINDOW)),
            pl.BlockSpec(memory_space=pltpu.SMEM),
            pl.BlockSpec((WINDOW, d), cur),
            _full((d, d)),
        ],
        out_specs=pl.BlockSpec((WINDOW, d), cur),
        out_shape=jax.ShapeDtypeStruct((n, d), F32),
        scratch_shapes=[pltpu.VMEM((WINDOW, d), BF16)],
        compiler_params=_params("parallel", "parallel"),
        name="attn_prompt",
    )(q, k, k, v, v, tbl, sinks, x, wo)


def _attn_sample_kernel(q_ref, kn_ref, vn_ref, ck_ref, cv_ref, tbl_ref, sink_ref, o_ref,
                        kf_ref, vf_ref):
    bs = q_ref.shape[0]
    ts = kn_ref.shape[1]
    for b in range(bs):
        kf_ref[0:WINDOW, :] = ck_ref[b]
        kf_ref[WINDOW:WINDOW + ts, :] = kn_ref[b]
        vf_ref[0:WINDOW, :] = cv_ref[b]
        vf_ref[WINDOW:WINDOW + ts, :] = vn_ref[b]
        for kh in range(N_KV):
            kslice = kf_ref[:, kh * HEAD_DIM:(kh + 1) * HEAD_DIM].astype(BF16)
            vslice = vf_ref[:, kh * HEAD_DIM:(kh + 1) * HEAD_DIM].astype(BF16)
            qt = q_ref[b, kh]
            s = lax.dot_general(qt, kslice, (((1,), (1,)), ((), ())),
                                preferred_element_type=F32)
            logits = s + tbl_ref[kh]
            sink = sink_ref[kh]
            m = jnp.maximum(jnp.max(logits, axis=-1, keepdims=True), sink)
            e = jnp.exp(logits - m)
            denom = jnp.sum(e, axis=-1, keepdims=True) + jnp.exp(sink - m)
            p = (e * (1.0 / denom)).astype(BF16)
            o_ref[b, kh] = jnp.dot(p, vslice, preferred_element_type=F32).astype(BF16)


def attn_sample(q4, kn, vn, ck, cv, tbl, sink, bs):
    nb = q4.shape[0]
    ts = kn.shape[1]
    nk = N_KV * HEAD_DIM
    tg = ts * GROUP
    kt = WINDOW + ts
    return pl.pallas_call(
        _attn_sample_kernel,
        grid=(nb // bs,),
        in_specs=[
            pl.BlockSpec((bs, N_KV, tg, HEAD_DIM), lambda i: (i, 0, 0, 0)),
            pl.BlockSpec((bs, ts, nk), lambda i: (i, 0, 0)),
            pl.BlockSpec((bs, ts, nk), lambda i: (i, 0, 0)),
            pl.BlockSpec((bs, WINDOW, nk), lambda i: (i, 0, 0)),
            pl.BlockSpec((bs, WINDOW, nk), lambda i: (i, 0, 0)),
            _full((N_KV, tg, kt)),
            _full((N_KV, tg, 1)),
        ],
        out_specs=pl.BlockSpec((bs, N_KV, tg, HEAD_DIM), lambda i: (i, 0, 0, 0)),
        out_shape=jax.ShapeDtypeStruct((nb, N_KV, tg, HEAD_DIM), BF16),
        scratch_shapes=[pltpu.VMEM((kt, nk), F32), pltpu.VMEM((kt, nk), F32)],
        compiler_params=_params("parallel"),
        name="attn_sample",
    )(q4, kn, vn, ck, cv, tbl, sink)


def _proj_res_kernel(a_ref, w_ref, x_ref, o_ref):
    o_ref[...] = x_ref[...] + jnp.dot(a_ref[...], w_ref[...], preferred_element_type=F32)


def proj_residual(a, w, x, tm):
    n, d = x.shape
    kdim = a.shape[1]
    return pl.pallas_call(
        _proj_res_kernel,
        grid=(n // tm,),
        in_specs=[pl.BlockSpec((tm, kdim), lambda i: (i, 0)), _full((kdim, d)),
                  pl.BlockSpec((tm, d), lambda i: (i, 0))],
        out_specs=pl.BlockSpec((tm, d), lambda i: (i, 0)),
        out_shape=jax.ShapeDtypeStruct((n, d), F32),
        compiler_params=_params("parallel"),
        name="proj_residual",
    )(a, w, x)


def _t5_bucket_np(dist):
    n = np.maximum(dist, 0)
    max_exact = N_BUCKETS // 2
    large = max_exact + (np.log(np.maximum(n, 1).astype(np.float32) / max_exact)
                         / math.log(MAX_DISTANCE / max_exact) * (N_BUCKETS - max_exact)).astype(np.int32)
    large = np.minimum(large, N_BUCKETS - 1)
    return np.where(n < max_exact, n, large)


def _bias_table(rel_bias, dist):
    valid = (dist >= 0) & (dist <= WINDOW)
    onehot = (np.asarray(_t5_bucket_np(dist))[..., None] == np.arange(N_BUCKETS)).astype(np.float32)
    b = jnp.einsum("qkb,bh->hqk", jnp.asarray(onehot), rel_bias.astype(F32),
                   precision=lax.Precision.HIGHEST)
    return jnp.where(jnp.asarray(valid)[None], b, NEG_INF)


def kernel(x_prompt, x_sample, state_conv, cache_swa_k, cache_swa_v, rms_mix_g, rms_ffn_g, conv_w_in, conv_dw_w, conv_dw_b, conv_ln_g, conv_ln_b, conv_w_out, attn_w_qkv, attn_q_norm_g, attn_k_norm_g, attn_sinks, attn_w_o, rel_bias, router_group_w, router_group_b, router_expert_w, router_expert_b, expert_w_in, expert_w_out):
    batch, seq, d = x_prompt.shape
    nsb, ts, _ = x_sample.shape
    xp = x_prompt.reshape(batch * seq, d)
    xs = x_sample.reshape(nsb * ts, d)
    row = lambda a: a.reshape(1, -1).astype(F32)

    def router_w(i):
        we = jnp.transpose(router_expert_w[i], (1, 0, 2)).reshape(d, N_EXPERTS)
        wr = jnp.concatenate([we, router_group_w[i]], axis=1)
        wr = jnp.pad(wr, ((0, 0), (0, LANES - wr.shape[1])))
        br = jnp.concatenate([router_expert_b[i].reshape(-1), router_group_b[i]])
        br = jnp.pad(br, (0, LANES - br.shape[0])).reshape(1, LANES)
        return wr.astype(F32), br.astype(F32)

    g0 = row(rms_mix_g[0])
    w_in = conv_w_in[0].astype(BF16)
    dww = jnp.pad(conv_dw_w[0].astype(F32), ((0, HALO - CONV_WIDTH), (0, 0)))
    dwb, lng, lnb = row(conv_dw_b[0]), row(conv_ln_g[0]), row(conv_ln_b[0])
    w_out = conv_w_out[0].astype(BF16)
    up = glu_proj(xp, g0, w_in, 512)
    us = glu_proj(xs, g0, w_in, 512)
    xp = conv_prompt(up, xp, dww, dwb, lng, lnb, w_out, batch, seq, 256)
    us3 = us.reshape(nsb, ts, -1)
    xs = conv_sample(us3, state_conv[0], xs, dww, dwb, lng, lnb, w_out, 32)
    conv_p = up.reshape(batch, seq, -1)[:, seq - PAST:]
    conv_s = jnp.concatenate([state_conv[0], us3], axis=1)[:, ts:]

    wr0, br0 = router_w(0)
    xp = moe_layer(xp, row(rms_ffn_g[0]), wr0, br0, expert_w_in[0], expert_w_out[0])
    xs = moe_layer(xs, row(rms_ffn_g[0]), wr0, br0, expert_w_in[0], expert_w_out[0])

    g1 = row(rms_mix_g[1])
    w_qkv = attn_w_qkv[0].astype(BF16)
    qg = jnp.tile(attn_q_norm_g[0].astype(F32), N_HEADS).reshape(1, -1)
    kg = jnp.tile(attn_k_norm_g[0].astype(F32), N_KV).reshape(1, -1)
    nq = N_HEADS * HEAD_DIM
    seg = jnp.asarray(np.kron(np.eye(N_HEADS), np.ones((HEAD_DIM, HEAD_DIM))) / HEAD_DIM, BF16)
    w_o = attn_w_o[0].astype(BF16)
    sinks = attn_sinks[0].astype(F32)

    qp, kp, vp = qkv_proj(xp, g1, w_qkv, qg, kg, seg, 512)
    qs, ks, vs = qkv_proj(xs, g1, w_qkv, qg, kg, seg, 512)

    q_off = np.arange(WINDOW)[:, None]
    dist_p = q_off + WINDOW - np.arange(2 * WINDOW)[None, :]
    tbl_p = _bias_table(rel_bias, dist_p)
    xp = attn_prompt(qp, kp, vp, tbl_p, sinks, xp, w_o, batch, seq)

    kt = WINDOW + ts
    dist_s = np.arange(ts)[:, None] + WINDOW - np.arange(kt)[None, :]
    tbl_s = _bias_table(rel_bias, dist_s)
    tbl_s = jnp.transpose(tbl_s.reshape(N_KV, GROUP, ts, kt), (0, 2, 1, 3)).reshape(N_KV, ts * GROUP, kt)
    sink_s = jnp.tile(sinks.reshape(N_KV, 1, GROUP), (1, ts, 1)).reshape(N_KV, ts * GROUP, 1)
    q4 = jnp.transpose(qs.reshape(nsb, ts, N_KV, GROUP, HEAD_DIM), (0, 2, 1, 3, 4))
    q4 = q4.reshape(nsb, N_KV, ts * GROUP, HEAD_DIM)
    nk = N_KV * HEAD_DIM
    ks3, vs3 = ks.reshape(nsb, ts, nk), vs.reshape(nsb, ts, nk)
    ck = cache_swa_k[0].reshape(nsb, WINDOW, nk)
    cv = cache_swa_v[0].reshape(nsb, WINDOW, nk)
    o4 = attn_sample(q4, ks3, vs3, ck, cv, tbl_s, sink_s, 16)
    os_ = jnp.transpose(o4.reshape(nsb, N_KV, ts, GROUP, HEAD_DIM), (0, 2, 1, 3, 4)).reshape(nsb * ts, nq)
    xs = proj_residual(os_, w_o, xs, 512)

    k_p = kp.reshape(batch, seq, N_KV, HEAD_DIM)[:, seq - WINDOW:]
    v_p = vp.reshape(batch, seq, N_KV, HEAD_DIM)[:, seq - WINDOW:]
    k_s = jnp.concatenate([cache_swa_k[0], ks.reshape(nsb, ts, N_KV, HEAD_DIM)], axis=1)[:, ts:]
    v_s = jnp.concatenate([cache_swa_v[0], vs.reshape(nsb, ts, N_KV, HEAD_DIM)], axis=1)[:, ts:]

    wr1, br1 = router_w(1)
    xp = moe_layer(xp, row(rms_ffn_g[1]), wr1, br1, expert_w_in[1], expert_w_out[1])
    xs = moe_layer(xs, row(rms_ffn_g[1]), wr1, br1, expert_w_in[1], expert_w_out[1])

    return (xp.reshape(batch, seq, d), xs.reshape(nsb, ts, d),
            conv_p[None], conv_s[None], k_p[None], v_p[None], k_s[None], v_s[None])
```

```python
import functools
import math

import numpy as np
import jax
import jax.numpy as jnp
from jax import lax
from jax.experimental import pallas as pl
from jax.experimental.pallas import tpu as pltpu

D_MODEL = 1024
DEPTH = 2
CONV_WIDTH = 31
PAST = CONV_WIDTH - 1
HEAD_DIM = 64
N_HEADS = 16
N_KV = 2
GROUP = 8
WINDOW = 128
N_BUCKETS = 32
MAX_DISTANCE = 128
N_GROUPS = 4
EPG = 8
N_EXPERTS = 32
D_EXPERT = 256
RMS_EPS = 1e-6
LN_EPS = 1e-5
NEG_INF = -1e30

F32 = jnp.float32
BF16 = jnp.bfloat16
LANES = 128
ROW_TILE = 8
MOE_TM = 256
MOE_TMG = 256
MOE_CHUNK = 4096
VMEM_LIMIT = 48 * 1024 * 1024


def _params(*sem):
    return pltpu.CompilerParams(dimension_semantics=sem, vmem_limit_bytes=VMEM_LIMIT)


def _rms(x, g):
    return x * lax.rsqrt(jnp.mean(x * x, axis=-1, keepdims=True) + RMS_EPS) * g


def _sigmoid(x):
    return 1.0 / (1.0 + jnp.exp(-x))


def _full(shape):
    return pl.BlockSpec(shape, lambda *_: (0,) * len(shape))


def _glu_kernel(x_ref, g_ref, wa_ref, wg_ref, u_ref):
    h = _rms(x_ref[...], g_ref[...]).astype(BF16)
    a = jnp.dot(h, wa_ref[...], preferred_element_type=F32)
    gate = jnp.dot(h, wg_ref[...], preferred_element_type=F32)
    u_ref[...] = a * _sigmoid(gate)


def glu_proj(x, g, w_in, tm):
    n, d = x.shape
    c = w_in.shape[1] // 2
    return pl.pallas_call(
        _glu_kernel,
        grid=(n // tm,),
        in_specs=[
            pl.BlockSpec((tm, d), lambda i: (i, 0)),
            _full((1, d)),
            pl.BlockSpec((d, c), lambda i: (0, 0)),
            pl.BlockSpec((d, c), lambda i: (0, 1)),
        ],
        out_specs=pl.BlockSpec((tm, c), lambda i: (i, 0)),
        out_shape=jax.ShapeDtypeStruct((n, c), F32),
        compiler_params=_params("parallel"),
        name="glu_proj",
    )(x, g, w_in, w_in)


def _ln_silu_out(y, lng, lnb, wout_ref, x):
    mu = jnp.mean(y, axis=-1, keepdims=True)
    yc = y - mu
    z = yc * lax.rsqrt(jnp.mean(yc * yc, axis=-1, keepdims=True) + LN_EPS) * lng + lnb
    z = z * _sigmoid(z)
    return x + jnp.dot(z.astype(BF16), wout_ref[...], preferred_element_type=F32)


HALO = 32
CONV_RC = 64
CONV_CC = 128


def _conv_prompt_kernel(ucur_ref, uprev_ref, x_ref, dww_ref, dwb_ref, lng_ref, lnb_ref,
                        wout_ref, o_ref, up_ref, y_ref):
    t = pl.program_id(1)
    tt, c = ucur_ref.shape
    keep = (t > 0).astype(F32)
    up_ref[0:HALO, :] = uprev_ref[...] * keep
    up_ref[HALO:HALO + tt, :] = ucur_ref[...]
    up_ref[HALO + tt:, :] = jnp.zeros((ROW_TILE, c), F32)
    off = HALO - PAST
    for r0 in range(0, tt, CONV_RC):
        for c0 in range(0, c, CONV_CC):
            y = jnp.zeros((CONV_RC, CONV_CC), F32) + dwb_ref[:, c0:c0 + CONV_CC]
            for s in range(ROW_TILE):
                v = None
                for q in range((off + CONV_WIDTH - 1) // ROW_TILE + 1):
                    k = ROW_TILE * q + s - off
                    if k < 0 or k >= CONV_WIDTH:
                        continue
                    lo = r0 + ROW_TILE * q
                    term = (up_ref[lo:lo + CONV_RC + ROW_TILE, c0:c0 + CONV_CC]
                            * dww_ref[k:k + 1, c0:c0 + CONV_CC])
                    v = term if v is None else v + term
                y = y + v[s:s + CONV_RC]
            y_ref[r0:r0 + CONV_RC, c0:c0 + CONV_CC] = y
    o_ref[...] = _ln_silu_out(y_ref[...], lng_ref[...], lnb_ref[...], wout_ref, x_ref[...])


def conv_prompt(u, x, dww, dwb, lng, lnb, wout, batch, seq, tt):
    n, c = u.shape
    d = x.shape[1]
    nt = seq // tt
    hb = tt // HALO
    return pl.pallas_call(
        _conv_prompt_kernel,
        grid=(batch, nt),
        in_specs=[
            pl.BlockSpec((tt, c), lambda b, t: (b * nt + t, 0)),
            pl.BlockSpec((HALO, c), lambda b, t: (jnp.maximum((b * nt + t) * hb - 1, 0), 0)),
            pl.BlockSpec((tt, d), lambda b, t: (b * nt + t, 0)),
            _full((HALO, c)), _full((1, c)), _full((1, c)), _full((1, c)),
            _full((c, d)),
        ],
        out_specs=pl.BlockSpec((tt, d), lambda b, t: (b * nt + t, 0)),
        out_shape=jax.ShapeDtypeStruct((n, d), F32),
        scratch_shapes=[pltpu.VMEM((tt + HALO + ROW_TILE, c), F32), pltpu.VMEM((tt, c), F32)],
        compiler_params=_params("parallel", "parallel"),
        name="conv_prompt",
    )(u, u, x, dww, dwb, lng, lnb, wout)


CONV_SB = 4


def _conv_sample_kernel(u_ref, st_ref, x_ref, dww_ref, dwb_ref, lng_ref, lnb_ref,
                        wout_ref, o_ref, up_ref, y_ref):
    bs, ts, c = u_ref.shape
    up_ref[:, 0:PAST, :] = st_ref[...]
    up_ref[:, PAST:PAST + ts, :] = u_ref[...]
    for b0 in range(0, bs, CONV_SB):
        acc = jnp.zeros((CONV_SB, ts, c), F32) + dwb_ref[...][None]
        for k in range(CONV_WIDTH):
            acc = acc + up_ref[b0:b0 + CONV_SB, k:k + ts, :] * dww_ref[k:k + 1, :][None]
        y_ref[b0 * ts:(b0 + CONV_SB) * ts, :] = acc.reshape(CONV_SB * ts, c)
    o_ref[...] = _ln_silu_out(y_ref[...], lng_ref[...], lnb_ref[...], wout_ref, x_ref[...])


def conv_sample(u3, state, x, dww, dwb, lng, lnb, wout, bs):
    nb, ts, c = u3.shape
    d = x.shape[1]
    return pl.pallas_call(
        _conv_sample_kernel,
        grid=(nb // bs,),
        in_specs=[
            pl.BlockSpec((bs, ts, c), lambda i: (i, 0, 0)),
            pl.BlockSpec((bs, PAST, c), lambda i: (i, 0, 0)),
            pl.BlockSpec((bs * ts, d), lambda i: (i, 0)),
            _full((HALO, c)), _full((1, c)), _full((1, c)), _full((1, c)),
            _full((c, d)),
        ],
        out_specs=pl.BlockSpec((bs * ts, d), lambda i: (i, 0)),
        out_shape=jax.ShapeDtypeStruct((nb * ts, d), F32),
        scratch_shapes=[pltpu.VMEM((bs, PAST + ts, c), F32), pltpu.VMEM((bs * ts, c), F32)],
        compiler_params=_params("parallel"),
        name="conv_sample",
    )(u3, state, x, dww, dwb, lng, lnb, wout)


def _router_kernel(x_ref, g_ref, wr_ref, br_ref, hs_ref, route_ref):
    h = _rms(x_ref[...], g_ref[...])
    tm, d = h.shape
    for j in range(d // LANES):
        hs_ref[pl.ds(j, tm, stride=ROW_TILE), :] = h[:, j * LANES:(j + 1) * LANES]
    logits = jnp.dot(h, wr_ref[...], preferred_element_type=F32,
                     precision=lax.Precision.HIGHEST) + br_ref[...]
    lane = lax.broadcasted_iota(jnp.int32, logits.shape, 1).astype(F32)
    big = jnp.float32(LANES)
    is_g = (lane >= N_EXPERTS) & (lane < N_EXPERTS + N_GROUPS)
    gl = jnp.where(is_g, logits, NEG_INF)
    gm = jnp.max(gl, axis=-1, keepdims=True)
    g_sel = jnp.min(jnp.where(gl == gm, lane, big), axis=-1, keepdims=True) - N_EXPERTS
    gate_g = 1.0 / jnp.sum(jnp.where(is_g, jnp.exp(gl - gm), 0.0), axis=-1, keepdims=True)
    lo = g_sel * EPG
    in_grp = (lane >= lo) & (lane < lo + EPG)
    el = jnp.where(in_grp, logits, NEG_INF)
    v1 = jnp.max(el, axis=-1, keepdims=True)
    i1 = jnp.min(jnp.where(el == v1, lane, big), axis=-1, keepdims=True)
    el2 = jnp.where(lane == i1, NEG_INF, el)
    v2 = jnp.max(el2, axis=-1, keepdims=True)
    i2 = jnp.min(jnp.where(el2 == v2, lane, big), axis=-1, keepdims=True)
    e2 = jnp.exp(v2 - v1)
    w1 = gate_g / (1.0 + e2)
    w2 = gate_g * e2 / (1.0 + e2)
    route_ref[...] = jnp.where(lane == 0.0, i1, jnp.where(lane == 1.0, i2, jnp.where(
        lane == 2.0, w1, jnp.where(lane == 3.0, w2, 0.0))))


def router(x, g, wr, br, tm):
    n, d = x.shape
    return pl.pallas_call(
        _router_kernel,
        grid=(n // tm,),
        in_specs=[pl.BlockSpec((tm, d), lambda i: (i, 0)), _full((1, d)),
                  _full((d, LANES)), _full((1, LANES))],
        out_specs=[pl.BlockSpec((tm * ROW_TILE, LANES), lambda i: (i, 0)),
                   pl.BlockSpec((tm, LANES), lambda i: (i, 0))],
        out_shape=[jax.ShapeDtypeStruct((n * ROW_TILE, LANES), F32),
                   jax.ShapeDtypeStruct((n, LANES), F32)],
        compiler_params=_params("parallel"),
        name="router",
    )(x, g, wr, br)


def _plan_kernel(tiles_per_chunk, route_ref, tri_ref, rank_ref, cnt_ref, carry_ref):
    i = pl.program_id(0)

    @pl.when(i % tiles_per_chunk == 0)
    def _():
        carry_ref[...] = jnp.zeros_like(carry_ref)

    route = route_ref[...]
    lane = lax.broadcasted_iota(jnp.int32, route.shape, 1).astype(F32)
    i1 = route[:, 0:1]
    i2 = route[:, 1:2]
    hit1 = lane == i1
    hit2 = lane == i2
    onehot = jnp.where(hit1 | hit2, 1.0, 0.0)
    before = carry_ref[...] + jnp.dot(tri_ref[...], onehot.astype(BF16), preferred_element_type=F32)
    r1 = jnp.sum(jnp.where(hit1, before, 0.0), axis=-1, keepdims=True)
    r2 = jnp.sum(jnp.where(hit2, before, 0.0), axis=-1, keepdims=True)
    rank_ref[...] = jnp.where(lane == 0.0, r1, jnp.where(lane == 1.0, r2, 0.0))
    carry_ref[...] += jnp.sum(onehot, axis=0, keepdims=True)
    cnt_ref[...] = jnp.broadcast_to(carry_ref[...], cnt_ref.shape)


def plan(route, tm, chunk):
    n = route.shape[0]
    tpc = chunk // tm
    tri = jnp.asarray(np.tril(np.ones((tm, tm), np.float32), -1), BF16)
    return pl.pallas_call(
        functools.partial(_plan_kernel, tpc),
        grid=(n // tm,),
        in_specs=[pl.BlockSpec((tm, LANES), lambda i: (i, 0)), _full((tm, tm))],
        out_specs=[pl.BlockSpec((tm, LANES), lambda i: (i, 0)),
                   pl.BlockSpec((ROW_TILE, LANES), lambda i: (i // tpc, 0))],
        out_shape=[jax.ShapeDtypeStruct((n, LANES), F32),
                   jax.ShapeDtypeStruct((n // chunk * ROW_TILE, LANES), F32)],
        scratch_shapes=[pltpu.VMEM((1, LANES), F32)],
        compiler_params=_params("arbitrary"),
        name="moe_plan",
    )(route, tri)


def _gather_rows(idx_ref, src_hbm, dst, sem, n_rows):
    unroll = 8

    def body(c, carry):
        for u in range(unroll):
            m = c * unroll + u
            tok = idx_ref[m]
            pltpu.make_async_copy(
                src_hbm.at[pl.ds(pl.multiple_of(tok * ROW_TILE, ROW_TILE), ROW_TILE), :],
                dst.at[pl.ds(pl.multiple_of(m * ROW_TILE, ROW_TILE), ROW_TILE), :],
                sem).start()
        return carry

    lax.fori_loop(0, n_rows // unroll, body, 0)


def _wait_rows(src_hbm, dst, sem):
    pltpu.make_async_copy(src_hbm.at[pl.ds(0, dst.shape[0]), :], dst, sem).wait()


def _from_token_tiles(buf, start, n_rows, stride):
    return jnp.concatenate(
        [buf[pl.ds(start + j, n_rows, stride=stride), :] for j in range(ROW_TILE)], axis=-1)


def _experts_kernel(tiles_per_chunk, tmg, te_ref, nv_ref, pos_ref, fill_hbm, hs_hbm,
                    win_ref, wout_ref, ys_ref, hsv, gbuf, src_ref, sems, winb, woutb):
    c = pl.program_id(0)
    j = pl.program_id(1)
    t = c * tiles_per_chunk + j
    nv = nv_ref[c]
    chunk = pos_ref.shape[0] // 2
    unroll = 8

    @pl.when(j == 0)
    def _():
        load = pltpu.make_async_copy(
            hs_hbm.at[pl.ds(pl.multiple_of(c * chunk * ROW_TILE, ROW_TILE), chunk * ROW_TILE), :],
            hsv.at[pl.ds(0, chunk * ROW_TILE), :], sems.at[0])
        load.start()
        fill = pltpu.make_async_copy(fill_hbm, src_ref, sems.at[1])
        fill.start()
        hsv[pl.ds(chunk * ROW_TILE, ROW_TILE), :] = jnp.zeros((ROW_TILE, LANES), F32)
        fill.wait()

        def scatter(i, carry):
            for u in range(unroll):
                tok = i * unroll + u
                src_ref[pos_ref[2 * tok]] = tok
                src_ref[pos_ref[2 * tok + 1]] = tok
            return carry

        lax.fori_loop(0, chunk // unroll, scatter, 0)
        load.wait()

    changed = jnp.logical_or(j == 0, te_ref[t] != te_ref[jnp.maximum(t - 1, 0)])

    @pl.when(jnp.logical_and(changed, j < nv))
    def _():
        winb[...] = win_ref[0].astype(BF16)
        woutb[...] = wout_ref[0].astype(BF16)

    @pl.when(j < nv)
    def _():
        base = j * tmg

        def gather(i, carry):
            for u in range(unroll):
                m = i * unroll + u
                tok = src_ref[base + m]
                gbuf[pl.ds(pl.multiple_of(m * ROW_TILE, ROW_TILE), ROW_TILE), :] = hsv[
                    pl.ds(pl.multiple_of(tok * ROW_TILE, ROW_TILE), ROW_TILE), :]
            return carry

        lax.fori_loop(0, tmg // unroll, gather, 0)
        x = _from_token_tiles(gbuf, 0, tmg, ROW_TILE).astype(BF16)
        hid = jnp.dot(x, winb[...], preferred_element_type=F32)
        a = hid[:, :D_EXPERT]
        u = hid[:, D_EXPERT:]
        act = (a * _sigmoid(a) * u).astype(BF16)
        y = jnp.dot(act, woutb[...], preferred_element_type=F32)
        for k in range(ROW_TILE):
            ys_ref[pl.ds(k, tmg, stride=ROW_TILE), :] = y[:, k * LANES:(k + 1) * LANES]

    @pl.when(j >= nv)
    def _():
        ys_ref[...] = jnp.zeros_like(ys_ref)


def experts(hs, pos_local, tile_expert, n_valid, w_in, w_out, tmg, chunk, tiles_per_chunk):
    ne, d, f2 = w_in.shape
    nc = hs.shape[0] // (chunk * ROW_TILE)
    rows_per_chunk = tiles_per_chunk * tmg
    fill = jnp.full((rows_per_chunk,), chunk, jnp.int32)
    grid_spec = pltpu.PrefetchScalarGridSpec(
        num_scalar_prefetch=2,
        grid=(nc, tiles_per_chunk),
        in_specs=[
            pl.BlockSpec((2 * chunk,), lambda c, j, te, nv: (c,), memory_space=pltpu.SMEM),
            pl.BlockSpec(memory_space=pl.ANY),
            pl.BlockSpec(memory_space=pl.ANY),
            pl.BlockSpec((1, d, f2), lambda c, j, te, nv: (te[c * tiles_per_chunk + j], 0, 0)),
            pl.BlockSpec((1, f2 // 2, d), lambda c, j, te, nv: (te[c * tiles_per_chunk + j], 0, 0)),
        ],
        out_specs=pl.BlockSpec((tmg * ROW_TILE, LANES),
                               lambda c, j, te, nv: (c * tiles_per_chunk + j, 0)),
        scratch_shapes=[
            pltpu.VMEM(((chunk + 1) * ROW_TILE, LANES), F32),
            pltpu.VMEM((tmg * ROW_TILE, LANES), F32),
            pltpu.SMEM((rows_per_chunk,), jnp.int32),
            pltpu.SemaphoreType.DMA((2,)),
            pltpu.VMEM((d, f2), BF16),
            pltpu.VMEM((f2 // 2, d), BF16),
        ],
    )
    return pl.pallas_call(
        functools.partial(_experts_kernel, tiles_per_chunk, tmg),
        grid_spec=grid_spec,
        out_shape=jax.ShapeDtypeStruct((nc * rows_per_chunk * ROW_TILE, LANES), F32),
        compiler_params=_params("arbitrary", "arbitrary"),
        name="moe_experts",
    )(tile_expert, n_valid, pos_local, fill, hs, w_in, w_out)


def _combine_kernel(pos_cur_ref, pos_nxt_ref, ys_hbm, route_ref, x_ref, o_ref, ybuf, sems):
    i = pl.program_id(0)
    n_steps = pl.num_programs(0)
    tm = x_ref.shape[0]
    slot = i % 2

    @pl.when(i == 0)
    def _():
        _gather_rows(pos_cur_ref, ys_hbm, ybuf.at[0], sems.at[0], 2 * tm)

    @pl.when(i + 1 < n_steps)
    def _():
        _gather_rows(pos_nxt_ref, ys_hbm, ybuf.at[1 - slot], sems.at[1 - slot], 2 * tm)

    _wait_rows(ys_hbm, ybuf.at[slot], sems.at[slot])
    route = route_ref[...]
    y1 = _from_token_tiles(ybuf.at[slot], 0, tm, 2 * ROW_TILE)
    y2 = _from_token_tiles(ybuf.at[slot], ROW_TILE, tm, 2 * ROW_TILE)
    o_ref[...] = x_ref[...] + route[:, 2:3] * y1 + route[:, 3:4] * y2


def combine(ys, pos, route, x, tm):
    n, d = x.shape
    nsteps = n // tm
    return pl.pallas_call(
        _combine_kernel,
        grid=(nsteps,),
        in_specs=[
            pl.BlockSpec((2 * tm,), lambda i: (i,), memory_space=pltpu.SMEM),
            pl.BlockSpec((2 * tm,), lambda i: (jnp.minimum(i + 1, nsteps - 1),),
                         memory_space=pltpu.SMEM),
            pl.BlockSpec(memory_space=pl.ANY),
            pl.BlockSpec((tm, LANES), lambda i: (i, 0)),
            pl.BlockSpec((tm, d), lambda i: (i, 0)),
        ],
        out_specs=pl.BlockSpec((tm, d), lambda i: (i, 0)),
        out_shape=jax.ShapeDtypeStruct((n, d), F32),
        scratch_shapes=[pltpu.VMEM((2, 2 * tm * ROW_TILE, LANES), F32),
                        pltpu.SemaphoreType.DMA((2,))],
        compiler_params=_params("arbitrary"),
        name="moe_combine",
    )(pos, pos, ys, route, x)


def moe_layer(x, g, wr, br, w_in, w_out):
    n = x.shape[0]
    chunk = min(n, MOE_CHUNK)
    nc = n // chunk
    tpc = (2 * chunk + N_EXPERTS * (MOE_TMG - 1)) // MOE_TMG
    hs, route = router(x, g, wr, br, MOE_TM)
    rank, counts = plan(route, MOE_TM, chunk)
    ids = route[:, 0:2].astype(jnp.int32).reshape(nc, chunk, 2)
    cnt = counts.reshape(nc, ROW_TILE, LANES)[:, 0, :N_EXPERTS].astype(jnp.int32)
    padded = (cnt + MOE_TMG - 1) // MOE_TMG * MOE_TMG
    ends = jnp.cumsum(padded, axis=1)
    starts = ends - padded
    sel = ids[..., None] == jnp.arange(N_EXPERTS, dtype=jnp.int32)
    start_of = jnp.sum(jnp.where(sel, starts[:, None, None, :], 0), axis=-1)
    pos_local = start_of + rank[:, 0:2].astype(jnp.int32).reshape(nc, chunk, 2)
    tile_row = jnp.arange(tpc, dtype=jnp.int32) * MOE_TMG
    tile_expert = jnp.minimum(jnp.sum(ends[:, None, :] <= tile_row[None, :, None], axis=-1),
                              N_EXPERTS - 1).astype(jnp.int32)
    n_valid = (ends[:, -1] // MOE_TMG).astype(jnp.int32)
    ys = experts(hs, pos_local.reshape(-1), tile_expert.reshape(-1), n_valid, w_in, w_out,
                 MOE_TMG, chunk, tpc)
    pos = pos_local + (jnp.arange(nc, dtype=jnp.int32) * (tpc * MOE_TMG))[:, None, None]
    return combine(ys, pos.reshape(-1), route, x, MOE_TM)


def _qkv_kernel(x_ref, g_ref, w_ref, qg_ref, kg_ref, seg_ref, q_ref, k_ref, v_ref):
    h = _rms(x_ref[...], g_ref[...]).astype(BF16)
    qkv = jnp.dot(h, w_ref[...], preferred_element_type=F32)
    nq = N_HEADS * HEAD_DIM
    nk = N_KV * HEAD_DIM
    q = qkv[:, :nq]
    k = qkv[:, nq:nq + nk]
    v_ref[...] = qkv[:, nq + nk:]

    def seg_mean_sq(z, seg):
        zz = z * z
        hi = zz.astype(BF16)
        lo = (zz - hi.astype(F32)).astype(BF16)
        return (jnp.dot(hi, seg, preferred_element_type=F32)
                + jnp.dot(lo, seg, preferred_element_type=F32))

    seg = seg_ref[...]
    qn = q * lax.rsqrt(seg_mean_sq(q, seg) + RMS_EPS) * qg_ref[...]
    q_ref[...] = (qn * (HEAD_DIM ** -0.5)).astype(BF16)
    kn = k * lax.rsqrt(seg_mean_sq(k, seg[:nk, :nk]) + RMS_EPS) * kg_ref[...]
    k_ref[...] = kn


def qkv_proj(x, g, w, qg, kg, seg, tm):
    n, d = x.shape
    nq = N_HEADS * HEAD_DIM
    nk = N_KV * HEAD_DIM
    return pl.pallas_call(
        _qkv_kernel,
        grid=(n // tm,),
        in_specs=[pl.BlockSpec((tm, d), lambda i: (i, 0)), _full((1, d)),
                  _full((d, nq + 2 * nk)), _full((1, nq)), _full((1, nk)), _full((nq, nq))],
        out_specs=[pl.BlockSpec((tm, nq), lambda i: (i, 0)),
                   pl.BlockSpec((tm, nk), lambda i: (i, 0)),
                   pl.BlockSpec((tm, nk), lambda i: (i, 0))],
        out_shape=[jax.ShapeDtypeStruct((n, nq), BF16),
                   jax.ShapeDtypeStruct((n, nk), F32),
                   jax.ShapeDtypeStruct((n, nk), F32)],
        compiler_params=_params("parallel"),
        name="qkv_proj",
    )(x, g, w, qg, kg, seg)


def _attn_prompt_kernel(q_ref, kc_ref, kp_ref, vc_ref, vp_ref, tbl_ref, sink_ref, x_ref,
                        wo_ref, o_ref, cat_ref):
    n = pl.program_id(1)
    first = jnp.where(n == 0, NEG_INF, 0.0).astype(F32)
    kk = jnp.concatenate([kp_ref[...], kc_ref[...]], axis=0).astype(BF16)
    vv = jnp.concatenate([vp_ref[...], vc_ref[...]], axis=0).astype(BF16)
    col = lax.broadcasted_iota(jnp.int32, (WINDOW, 2 * WINDOW), 1)
    prev_mask = jnp.where(col < WINDOW, first, 0.0)
    for h in range(N_HEADS):
        kh = h // GROUP
        qh = q_ref[:, h * HEAD_DIM:(h + 1) * HEAD_DIM]
        kslice = kk[:, kh * HEAD_DIM:(kh + 1) * HEAD_DIM]
        vslice = vv[:, kh * HEAD_DIM:(kh + 1) * HEAD_DIM]
        s = lax.dot_general(qh, kslice, (((1,), (1,)), ((), ())), preferred_element_type=F32)
        logits = s + tbl_ref[h] + prev_mask
        sink = sink_ref[h]
        m = jnp.maximum(jnp.max(logits, axis=-1, keepdims=True), sink)
        e = jnp.exp(logits - m)
        denom = jnp.sum(e, axis=-1, keepdims=True) + jnp.exp(sink - m)
        p = e * (1.0 / denom)
        cat_ref[:, h * HEAD_DIM:(h + 1) * HEAD_DIM] = jnp.dot(
            p.astype(BF16), vslice, preferred_element_type=F32).astype(BF16)
    o_ref[...] = x_ref[...] + jnp.dot(cat_ref[...], wo_ref[...], preferred_element_type=F32)


def attn_prompt(q, k, v, tbl, sinks, x, wo, batch, seq):
    n, d = x.shape
    nb = seq // WINDOW
    nk = N_KV * HEAD_DIM
    cur = lambda b, i: (b * nb + i, 0)
    prev = lambda b, i: (b * nb + jnp.maximum(i - 1, 0), 0)
    return pl.pallas_call(
        _attn_prompt_kernel,
        grid=(batch, nb),
        in_specs=[
            pl.BlockSpec((WINDOW, d), cur),
            pl.BlockSpec((WINDOW, nk), cur), pl.BlockSpec((WINDOW, nk), prev),
            pl.BlockSpec((WINDOW, nk), cur), pl.BlockSpec((WINDOW, nk), prev),
            _full((N_HEADS, WINDOW, 2 * WINDOW)),
            pl.BlockSpec(memory_space=pltpu.SMEM),
            pl.BlockSpec((WINDOW, d), cur),
            _full((d, d)),
        ],
        out_specs=pl.BlockSpec((WINDOW, d), cur),
        out_shape=jax.ShapeDtypeStruct((n, d), F32),
        scratch_shapes=[pltpu.VMEM((WINDOW, d), BF16)],
        compiler_params=_params("parallel", "parallel"),
        name="attn_prompt",
    )(q, k, k, v, v, tbl, sinks, x, wo)


def _attn_sample_kernel(q_ref, kn_ref, vn_ref, ck_ref, cv_ref, tbl_ref, sink_ref, o_ref,
                        kf_ref, vf_ref):
    bs = q_ref.shape[0]
    ts = kn_ref.shape[1]
    for b in range(bs):
        kf_ref[0:WINDOW, :] = ck_ref[b]
        kf_ref[WINDOW:WINDOW + ts, :] = kn_ref[b]
        vf_ref[0:WINDOW, :] = cv_ref[b]
        vf_ref[WINDOW:WINDOW + ts, :] = vn_ref[b]
        for kh in range(N_KV):
            kslice = kf_ref[:, kh * HEAD_DIM:(kh + 1) * HEAD_DIM].astype(BF16)
            vslice = vf_ref[:, kh * HEAD_DIM:(kh + 1) * HEAD_DIM].astype(BF16)
            qt = q_ref[b, kh]
            s = lax.dot_general(qt, kslice, (((1,), (1,)), ((), ())),
                                preferred_element_type=F32)
            logits = s + tbl_ref[kh]
            sink = sink_ref[kh]
            m = jnp.maximum(jnp.max(logits, axis=-1, keepdims=True), sink)
            e = jnp.exp(logits - m)
            denom = jnp.sum(e, axis=-1, keepdims=True) + jnp.exp(sink - m)
            p = (e * (1.0 / denom)).astype(BF16)
            o_ref[b, kh] = jnp.dot(p, vslice, preferred_element_type=F32).astype(BF16)


def attn_sample(q4, kn, vn, ck, cv, tbl, sink, bs):
    nb = q4.shape[0]
    ts = kn.shape[1]
    nk = N_KV * HEAD_DIM
    tg = ts * GROUP
    kt = WINDOW + ts
    return pl.pallas_call(
        _attn_sample_kernel,
        grid=(nb // bs,),
        in_specs=[
            pl.BlockSpec((bs, N_KV, tg, HEAD_DIM), lambda i: (i, 0, 0, 0)),
            pl.BlockSpec((bs, ts, nk), lambda i: (i, 0, 0)),
            pl.BlockSpec((bs, ts, nk), lambda i: (i, 0, 0)),
            pl.BlockSpec((bs, WINDOW, nk), lambda i: (i, 0, 0)),
            pl.BlockSpec((bs, WINDOW, nk), lambda i: (i, 0, 0)),
            _full((N_KV, tg, kt)),
            _full((N_KV, tg, 1)),
        ],
        out_specs=pl.BlockSpec((bs, N_KV, tg, HEAD_DIM), lambda i: (i, 0, 0, 0)),
        out_shape=jax.ShapeDtypeStruct((nb, N_KV, tg, HEAD_DIM), BF16),
        scratch_shapes=[pltpu.VMEM((kt, nk), F32), pltpu.VMEM((kt, nk), F32)],
        compiler_params=_params("parallel"),
        name="attn_sample",
    )(q4, kn, vn, ck, cv, tbl, sink)


def _proj_res_kernel(a_ref, w_ref, x_ref, o_ref):
    o_ref[...] = x_ref[...] + jnp.dot(a_ref[...], w_ref[...], preferred_element_type=F32)


def proj_residual(a, w, x, tm):
    n, d = x.shape
    kdim = a.shape[1]
    return pl.pallas_call(
        _proj_res_kernel,
        grid=(n // tm,),
        in_specs=[pl.BlockSpec((tm, kdim), lambda i: (i, 0)), _full((kdim, d)),
                  pl.BlockSpec((tm, d), lambda i: (i, 0))],
        out_specs=pl.BlockSpec((tm, d), lambda i: (i, 0)),
        out_shape=jax.ShapeDtypeStruct((n, d), F32),
        compiler_params=_params("parallel"),
        name="proj_residual",
    )(a, w, x)


def _t5_bucket_np(dist):
    n = np.maximum(dist, 0)
    max_exact = N_BUCKETS // 2
    large = max_exact + (np.log(np.maximum(n, 1).astype(np.float32) / max_exact)
                         / math.log(MAX_DISTANCE / max_exact) * (N_BUCKETS - max_exact)).astype(np.int32)
    large = np.minimum(large, N_BUCKETS - 1)
    return np.where(n < max_exact, n, large)


def _bias_table(rel_bias, dist):
    valid = (dist >= 0) & (dist <= WINDOW)
    onehot = (np.asarray(_t5_bucket_np(dist))[..., None] == np.arange(N_BUCKETS)).astype(np.float32)
    b = jnp.einsum("qkb,bh->hqk", jnp.asarray(onehot), rel_bias.astype(F32),
                   precision=lax.Precision.HIGHEST)
    return jnp.where(jnp.asarray(valid)[None], b, NEG_INF)


def kernel(x_prompt, x_sample, state_conv, cache_swa_k, cache_swa_v, rms_mix_g, rms_ffn_g, conv_w_in, conv_dw_w, conv_dw_b, conv_ln_g, conv_ln_b, conv_w_out, attn_w_qkv, attn_q_norm_g, attn_k_norm_g, attn_sinks, attn_w_o, rel_bias, router_group_w, router_group_b, router_expert_w, router_expert_b, expert_w_in, expert_w_out):
    batch, seq, d = x_prompt.shape
    nsb, ts, _ = x_sample.shape
    xp = x_prompt.reshape(batch * seq, d)
    xs = x_sample.reshape(nsb * ts, d)
    row = lambda a: a.reshape(1, -1).astype(F32)

    def router_w(i):
        we = jnp.transpose(router_expert_w[i], (1, 0, 2)).reshape(d, N_EXPERTS)
        wr = jnp.concatenate([we, router_group_w[i]], axis=1)
        wr = jnp.pad(wr, ((0, 0), (0, LANES - wr.shape[1])))
        br = jnp.concatenate([router_expert_b[i].reshape(-1), router_group_b[i]])
        br = jnp.pad(br, (0, LANES - br.shape[0])).reshape(1, LANES)
        return wr.astype(F32), br.astype(F32)

    g0 = row(rms_mix_g[0])
    w_in = conv_w_in[0].astype(BF16)
    dww = jnp.pad(conv_dw_w[0].astype(F32), ((0, HALO - CONV_WIDTH), (0, 0)))
    dwb, lng, lnb = row(conv_dw_b[0]), row(conv_ln_g[0]), row(conv_ln_b[0])
    w_out = conv_w_out[0].astype(BF16)
    up = glu_proj(xp, g0, w_in, 512)
    us = glu_proj(xs, g0, w_in, 512)
    xp = conv_prompt(up, xp, dww, dwb, lng, lnb, w_out, batch, seq, 256)
    us3 = us.reshape(nsb, ts, -1)
    xs = conv_sample(us3, state_conv[0], xs, dww, dwb, lng, lnb, w_out, 32)
    conv_p = up.reshape(batch, seq, -1)[:, seq - PAST:]
    conv_s = jnp.concatenate([state_conv[0], us3], axis=1)[:, ts:]

    wr0, br0 = router_w(0)
    xp = moe_layer(xp, row(rms_ffn_g[0]), wr0, br0, expert_w_in[0], expert_w_out[0])
    xs = moe_layer(xs, row(rms_ffn_g[0]), wr0, br0, expert_w_in[0], expert_w_out[0])

    g1 = row(rms_mix_g[1])
    w_qkv = attn_w_qkv[0].astype(BF16)
    qg = jnp.tile(attn_q_norm_g[0].astype(F32), N_HEADS).reshape(1, -1)
    kg = jnp.tile(attn_k_norm_g[0].astype(F32), N_KV).reshape(1, -1)
    nq = N_HEADS * HEAD_DIM
    seg = jnp.asarray(np.kron(np.eye(N_HEADS), np.ones((HEAD_DIM, HEAD_DIM))) / HEAD_DIM, BF16)
    w_o = attn_w_o[0].astype(BF16)
    sinks = attn_sinks[0].astype(F32)

    qp, kp, vp = qkv_proj(xp, g1, w_qkv, qg, kg, seg, 512)
    qs, ks, vs = qkv_proj(xs, g1, w_qkv, qg, kg, seg, 512)

    q_off = np.arange(WINDOW)[:, None]
    dist_p = q_off + WINDOW - np.arange(2 * WINDOW)[None, :]
    tbl_p = _bias_table(rel_bias, dist_p)
    xp = attn_prompt(qp, kp, vp, tbl_p, sinks, xp, w_o, batch, seq)

    kt = WINDOW + ts
    dist_s = np.arange(ts)[:, None] + WINDOW - np.arange(kt)[None, :]
    tbl_s = _bias_table(rel_bias, dist_s)
    tbl_s = jnp.transpose(tbl_s.reshape(N_KV, GROUP, ts, kt), (0, 2, 1, 3)).reshape(N_KV, ts * GROUP, kt)
    sink_s = jnp.tile(sinks.reshape(N_KV, 1, GROUP), (1, ts, 1)).reshape(N_KV, ts * GROUP, 1)
    q4 = jnp.transpose(qs.reshape(nsb, ts, N_KV, GROUP, HEAD_DIM), (0, 2, 1, 3, 4))
    q4 = q4.reshape(nsb, N_KV, ts * GROUP, HEAD_DIM)
    nk = N_KV * HEAD_DIM
    ks3, vs3 = ks.reshape(nsb, ts, nk), vs.reshape(nsb, ts, nk)
    ck = cache_swa_k[0].reshape(nsb, WINDOW, nk)
    cv = cache_swa_v[0].reshape(nsb, WINDOW, nk)
    o4 = attn_sample(q4, ks3, vs3, ck, cv, tbl_s, sink_s, 16)
    os_ = jnp.transpose(o4.reshape(nsb, N_KV, ts, GROUP, HEAD_DIM), (0, 2, 1, 3, 4)).reshape(nsb * ts, nq)
    xs = proj_residual(os_, w_o, xs, 512)

    k_p = kp.reshape(batch, seq, N_KV, HEAD_DIM)[:, seq - WINDOW:]
    v_p = vp.reshape(batch, seq, N_KV, HEAD_DIM)[:, seq - WINDOW:]
    k_s = jnp.concatenate([cache_swa_k[0], ks.reshape(nsb, ts, N_KV, HEAD_DIM)], axis=1)[:, ts:]
    v_s = jnp.concatenate([cache_swa_v[0], vs.reshape(nsb, ts, N_KV, HEAD_DIM)], axis=1)[:, ts:]

    wr1, br1 = router_w(1)
    xp = moe_layer(xp, row(rms_ffn_g[1]), wr1, br1, expert_w_in[1], expert_w_out[1])
    xs = moe_layer(xs, row(rms_ffn_g[1]), wr1, br1, expert_w_in[1], expert_w_out[1])

    return (xp.reshape(batch, seq, d), xs.reshape(nsb, ts, d),
            conv_p[None], conv_s[None], k_p[None], v_p[None], k_s[None], v_s[None])
```

```python
import functools
import math

import numpy as np
import jax
import jax.numpy as jnp
from jax import lax
from jax.experimental import pallas as pl
from jax.experimental.pallas import tpu as pltpu

D_MODEL = 1024
DEPTH = 2
CONV_WIDTH = 31
PAST = CONV_WIDTH - 1
HEAD_DIM = 64
N_HEADS = 16
N_KV = 2
GROUP = 8
WINDOW = 128
N_BUCKETS = 32
MAX_DISTANCE = 128
N_GROUPS = 4
EPG = 8
N_EXPERTS = 32
D_EXPERT = 256
RMS_EPS = 1e-6
LN_EPS = 1e-5
NEG_INF = -1e30

F32 = jnp.float32
BF16 = jnp.bfloat16
LANES = 128
ROW_TILE = 8
MOE_TM = 256
MOE_TMG = 256
MOE_CHUNK = 4096
MOE_TS = 128
MOE_VMEM_LIMIT = 56 * 1024 * 1024
VMEM_LIMIT = 48 * 1024 * 1024


def _params(*sem):
    return pltpu.CompilerParams(dimension_semantics=sem, vmem_limit_bytes=VMEM_LIMIT)


def _rms(x, g):
    return x * lax.rsqrt(jnp.mean(x * x, axis=-1, keepdims=True) + RMS_EPS) * g


def _sigmoid(x):
    return 1.0 / (1.0 + jnp.exp(-x))


def _full(shape):
    return pl.BlockSpec(shape, lambda *_: (0,) * len(shape))


def _glu_kernel(x_ref, g_ref, wa_ref, wg_ref, u_ref):
    h = _rms(x_ref[...], g_ref[...]).astype(BF16)
    a = jnp.dot(h, wa_ref[...], preferred_element_type=F32)
    gate = jnp.dot(h, wg_ref[...], preferred_element_type=F32)
    u_ref[...] = a * _sigmoid(gate)


def glu_proj(x, g, w_in, tm):
    n, d = x.shape
    c = w_in.shape[1] // 2
    return pl.pallas_call(
        _glu_kernel,
        grid=(n // tm,),
        in_specs=[
            pl.BlockSpec((tm, d), lambda i: (i, 0)),
            _full((1, d)),
            pl.BlockSpec((d, c), lambda i: (0, 0)),
            pl.BlockSpec((d, c), lambda i: (0, 1)),
        ],
        out_specs=pl.BlockSpec((tm, c), lambda i: (i, 0)),
        out_shape=jax.ShapeDtypeStruct((n, c), F32),
        compiler_params=_params("parallel"),
        name="glu_proj",
    )(x, g, w_in, w_in)


def _ln_silu_out(y, lng, lnb, wout_ref, x):
    mu = jnp.mean(y, axis=-1, keepdims=True)
    yc = y - mu
    z = yc * lax.rsqrt(jnp.mean(yc * yc, axis=-1, keepdims=True) + LN_EPS) * lng + lnb
    z = z * _sigmoid(z)
    return x + jnp.dot(z.astype(BF16), wout_ref[...], preferred_element_type=F32)


HALO = 32
CONV_RC = 64
CONV_CC = 128


def _conv_prompt_kernel(ucur_ref, uprev_ref, x_ref, dww_ref, dwb_ref, lng_ref, lnb_ref,
                        wout_ref, o_ref, up_ref, y_ref):
    t = pl.program_id(1)
    tt, c = ucur_ref.shape
    keep = (t > 0).astype(F32)
    up_ref[0:HALO, :] = uprev_ref[...] * keep
    up_ref[HALO:HALO + tt, :] = ucur_ref[...]
    up_ref[HALO + tt:, :] = jnp.zeros((ROW_TILE, c), F32)
    off = HALO - PAST
    for r0 in range(0, tt, CONV_RC):
        for c0 in range(0, c, CONV_CC):
            y = jnp.zeros((CONV_RC, CONV_CC), F32) + dwb_ref[:, c0:c0 + CONV_CC]
            for s in range(ROW_TILE):
                v = None
                for q in range((off + CONV_WIDTH - 1) // ROW_TILE + 1):
                    k = ROW_TILE * q + s - off
                    if k < 0 or k >= CONV_WIDTH:
                        continue
                    lo = r0 + ROW_TILE * q
                    term = (up_ref[lo:lo + CONV_RC + ROW_TILE, c0:c0 + CONV_CC]
                            * dww_ref[k:k + 1, c0:c0 + CONV_CC])
                    v = term if v is None else v + term
                y = y + v[s:s + CONV_RC]
            y_ref[r0:r0 + CONV_RC, c0:c0 + CONV_CC] = y
    o_ref[...] = _ln_silu_out(y_ref[...], lng_ref[...], lnb_ref[...], wout_ref, x_ref[...])


def conv_prompt(u, x, dww, dwb, lng, lnb, wout, batch, seq, tt):
    n, c = u.shape
    d = x.shape[1]
    nt = seq // tt
    hb = tt // HALO
    return pl.pallas_call(
        _conv_prompt_kernel,
        grid=(batch, nt),
        in_specs=[
            pl.BlockSpec((tt, c), lambda b, t: (b * nt + t, 0)),
            pl.BlockSpec((HALO, c), lambda b, t: (jnp.maximum((b * nt + t) * hb - 1, 0), 0)),
            pl.BlockSpec((tt, d), lambda b, t: (b * nt + t, 0)),
            _full((HALO, c)), _full((1, c)), _full((1, c)), _full((1, c)),
            _full((c, d)),
        ],
        out_specs=pl.BlockSpec((tt, d), lambda b, t: (b * nt + t, 0)),
        out_shape=jax.ShapeDtypeStruct((n, d), F32),
        scratch_shapes=[pltpu.VMEM((tt + HALO + ROW_TILE, c), F32), pltpu.VMEM((tt, c), F32)],
        compiler_params=_params("parallel", "parallel"),
        name="conv_prompt",
    )(u, u, x, dww, dwb, lng, lnb, wout)


CONV_SB = 4


def _conv_sample_kernel(u_ref, st_ref, x_ref, dww_ref, dwb_ref, lng_ref, lnb_ref,
                        wout_ref, o_ref, up_ref, y_ref):
    bs, ts, c = u_ref.shape
    up_ref[:, 0:PAST, :] = st_ref[...]
    up_ref[:, PAST:PAST + ts, :] = u_ref[...]
    for b0 in range(0, bs, CONV_SB):
        acc = jnp.zeros((CONV_SB, ts, c), F32) + dwb_ref[...][None]
        for k in range(CONV_WIDTH):
            acc = acc + up_ref[b0:b0 + CONV_SB, k:k + ts, :] * dww_ref[k:k + 1, :][None]
        y_ref[b0 * ts:(b0 + CONV_SB) * ts, :] = acc.reshape(CONV_SB * ts, c)
    o_ref[...] = _ln_silu_out(y_ref[...], lng_ref[...], lnb_ref[...], wout_ref, x_ref[...])


def conv_sample(u3, state, x, dww, dwb, lng, lnb, wout, bs):
    nb, ts, c = u3.shape
    d = x.shape[1]
    return pl.pallas_call(
        _conv_sample_kernel,
        grid=(nb // bs,),
        in_specs=[
            pl.BlockSpec((bs, ts, c), lambda i: (i, 0, 0)),
            pl.BlockSpec((bs, PAST, c), lambda i: (i, 0, 0)),
            pl.BlockSpec((bs * ts, d), lambda i: (i, 0)),
            _full((HALO, c)), _full((1, c)), _full((1, c)), _full((1, c)),
            _full((c, d)),
        ],
        out_specs=pl.BlockSpec((bs * ts, d), lambda i: (i, 0)),
        out_shape=jax.ShapeDtypeStruct((nb * ts, d), F32),
        scratch_shapes=[pltpu.VMEM((bs, PAST + ts, c), F32), pltpu.VMEM((bs * ts, c), F32)],
        compiler_params=_params("parallel"),
        name="conv_sample",
    )(u3, state, x, dww, dwb, lng, lnb, wout)


def _router_kernel(x_ref, g_ref, wr_ref, br_ref, hs_ref, route_ref):
    h = _rms(x_ref[...], g_ref[...])
    tm, d = h.shape
    for j in range(d // LANES):
        hs_ref[pl.ds(j, tm, stride=ROW_TILE), :] = h[:, j * LANES:(j + 1) * LANES]
    logits = jnp.dot(h, wr_ref[...], preferred_element_type=F32,
                     precision=lax.Precision.HIGHEST) + br_ref[...]
    lane = lax.broadcasted_iota(jnp.int32, logits.shape, 1).astype(F32)
    big = jnp.float32(LANES)
    is_g = (lane >= N_EXPERTS) & (lane < N_EXPERTS + N_GROUPS)
    gl = jnp.where(is_g, logits, NEG_INF)
    gm = jnp.max(gl, axis=-1, keepdims=True)
    g_sel = jnp.min(jnp.where(gl == gm, lane, big), axis=-1, keepdims=True) - N_EXPERTS
    gate_g = 1.0 / jnp.sum(jnp.where(is_g, jnp.exp(gl - gm), 0.0), axis=-1, keepdims=True)
    lo = g_sel * EPG
    in_grp = (lane >= lo) & (lane < lo + EPG)
    el = jnp.where(in_grp, logits, NEG_INF)
    v1 = jnp.max(el, axis=-1, keepdims=True)
    i1 = jnp.min(jnp.where(el == v1, lane, big), axis=-1, keepdims=True)
    el2 = jnp.where(lane == i1, NEG_INF, el)
    v2 = jnp.max(el2, axis=-1, keepdims=True)
    i2 = jnp.min(jnp.where(el2 == v2, lane, big), axis=-1, keepdims=True)
    e2 = jnp.exp(v2 - v1)
    w1 = gate_g / (1.0 + e2)
    w2 = gate_g * e2 / (1.0 + e2)
    route_ref[...] = jnp.where(lane == 0.0, i1, jnp.where(lane == 1.0, i2, jnp.where(
        lane == 2.0, w1, jnp.where(lane == 3.0, w2, 0.0))))


def router(x, g, wr, br, tm):
    n, d = x.shape
    return pl.pallas_call(
        _router_kernel,
        grid=(n // tm,),
        in_specs=[pl.BlockSpec((tm, d), lambda i: (i, 0)), _full((1, d)),
                  _full((d, LANES)), _full((1, LANES))],
        out_specs=[pl.BlockSpec((tm * ROW_TILE, LANES), lambda i: (i, 0)),
                   pl.BlockSpec((tm, LANES), lambda i: (i, 0))],
        out_shape=[jax.ShapeDtypeStruct((n * ROW_TILE, LANES), F32),
                   jax.ShapeDtypeStruct((n, LANES), F32)],
        compiler_params=_params("parallel"),
        name="router",
    )(x, g, wr, br)


def _plan_kernel(tiles_per_chunk, route_ref, tri_ref, rank_ref, cnt_ref, carry_ref):
    i = pl.program_id(0)

    @pl.when(i % tiles_per_chunk == 0)
    def _():
        carry_ref[...] = jnp.zeros_like(carry_ref)

    route = route_ref[...]
    lane = lax.broadcasted_iota(jnp.int32, route.shape, 1).astype(F32)
    i1 = route[:, 0:1]
    i2 = route[:, 1:2]
    hit1 = lane == i1
    hit2 = lane == i2
    onehot = jnp.where(hit1 | hit2, 1.0, 0.0)
    before = carry_ref[...] + jnp.dot(tri_ref[...], onehot.astype(BF16), preferred_element_type=F32)
    r1 = jnp.sum(jnp.where(hit1, before, 0.0), axis=-1, keepdims=True)
    r2 = jnp.sum(jnp.where(hit2, before, 0.0), axis=-1, keepdims=True)
    rank_ref[...] = jnp.where(lane == 0.0, r1, jnp.where(lane == 1.0, r2, 0.0))
    carry_ref[...] += jnp.sum(onehot, axis=0, keepdims=True)
    cnt_ref[...] = jnp.broadcast_to(carry_ref[...], cnt_ref.shape)


def plan(route, tm, chunk):
    n = route.shape[0]
    tpc = chunk // tm
    tri = jnp.asarray(np.tril(np.ones((tm, tm), np.float32), -1), BF16)
    return pl.pallas_call(
        functools.partial(_plan_kernel, tpc),
        grid=(n // tm,),
        in_specs=[pl.BlockSpec((tm, LANES), lambda i: (i, 0)), _full((tm, tm))],
        out_specs=[pl.BlockSpec((tm, LANES), lambda i: (i, 0)),
                   pl.BlockSpec((ROW_TILE, LANES), lambda i: (i // tpc, 0))],
        out_shape=[jax.ShapeDtypeStruct((n, LANES), F32),
                   jax.ShapeDtypeStruct((n // chunk * ROW_TILE, LANES), F32)],
        scratch_shapes=[pltpu.VMEM((1, LANES), F32)],
        compiler_params=_params("arbitrary"),
        name="moe_plan",
    )(route, tri)


def _gather_rows(idx_ref, src_hbm, dst, sem, n_rows):
    unroll = 8

    def body(c, carry):
        for u in range(unroll):
            m = c * unroll + u
            tok = idx_ref[m]
            pltpu.make_async_copy(
                src_hbm.at[pl.ds(pl.multiple_of(tok * ROW_TILE, ROW_TILE), ROW_TILE), :],
                dst.at[pl.ds(pl.multiple_of(m * ROW_TILE, ROW_TILE), ROW_TILE), :],
                sem).start()
        return carry

    lax.fori_loop(0, n_rows // unroll, body, 0)


def _wait_rows(src_hbm, dst, sem):
    pltpu.make_async_copy(src_hbm.at[pl.ds(0, dst.shape[0]), :], dst, sem).wait()


def _from_token_tiles(buf, start, n_rows, stride):
    return jnp.concatenate(
        [buf[pl.ds(start + j, n_rows, stride=stride), :] for j in range(ROW_TILE)], axis=-1)


def _experts_kernel(tiles_per_chunk, tmg, te_ref, nv_ref, pos_ref, fill_hbm, hs_hbm,
                    win_ref, wout_ref, ys_ref, hsv, gbuf, src_ref, sems, winb, woutb):
    c = pl.program_id(0)
    j = pl.program_id(1)
    t = c * tiles_per_chunk + j
    nv = nv_ref[c]
    chunk = pos_ref.shape[0] // 2
    unroll = 8

    @pl.when(j == 0)
    def _():
        load = pltpu.make_async_copy(
            hs_hbm.at[pl.ds(pl.multiple_of(c * chunk * ROW_TILE, ROW_TILE), chunk * ROW_TILE), :],
            hsv.at[pl.ds(0, chunk * ROW_TILE), :], sems.at[0])
        load.start()
        fill = pltpu.make_async_copy(fill_hbm, src_ref, sems.at[1])
        fill.start()
        hsv[pl.ds(chunk * ROW_TILE, ROW_TILE), :] = jnp.zeros((ROW_TILE, LANES), F32)
        fill.wait()

        def scatter(i, carry):
            for u in range(unroll):
                tok = i * unroll + u
                src_ref[pos_ref[2 * tok]] = tok
                src_ref[pos_ref[2 * tok + 1]] = tok
            return carry

        lax.fori_loop(0, chunk // unroll, scatter, 0)
        load.wait()

    changed = jnp.logical_or(j == 0, te_ref[t] != te_ref[jnp.maximum(t - 1, 0)])

    @pl.when(jnp.logical_and(changed, j < nv))
    def _():
        winb[...] = win_ref[0].astype(BF16)
        woutb[...] = wout_ref[0].astype(BF16)

    @pl.when(j < nv)
    def _():
        base = j * tmg

        def gather(i, carry):
            for u in range(unroll):
                m = i * unroll + u
                tok = src_ref[base + m]
                gbuf[pl.ds(pl.multiple_of(m * ROW_TILE, ROW_TILE), ROW_TILE), :] = hsv[
                    pl.ds(pl.multiple_of(tok * ROW_TILE, ROW_TILE), ROW_TILE), :]
            return carry

        lax.fori_loop(0, tmg // unroll, gather, 0)
        x = _from_token_tiles(gbuf, 0, tmg, ROW_TILE).astype(BF16)
        hid = jnp.dot(x, winb[...], preferred_element_type=F32)
        a = hid[:, :D_EXPERT]
        u = hid[:, D_EXPERT:]
        act = (a * _sigmoid(a) * u).astype(BF16)
        y = jnp.dot(act, woutb[...], preferred_element_type=F32)
        for k in range(ROW_TILE):
            ys_ref[pl.ds(k, tmg, stride=ROW_TILE), :] = y[:, k * LANES:(k + 1) * LANES]

    @pl.when(j >= nv)
    def _():
        ys_ref[...] = jnp.zeros_like(ys_ref)


def experts(hs, pos_local, tile_expert, n_valid, w_in, w_out, tmg, chunk, tiles_per_chunk):
    ne, d, f2 = w_in.shape
    nc = hs.shape[0] // (chunk * ROW_TILE)
    rows_per_chunk = tiles_per_chunk * tmg
    fill = jnp.full((rows_per_chunk,), chunk, jnp.int32)
    grid_spec = pltpu.PrefetchScalarGridSpec(
        num_scalar_prefetch=2,
        grid=(nc, tiles_per_chunk),
        in_specs=[
            pl.BlockSpec((2 * chunk,), lambda c, j, te, nv: (c,), memory_space=pltpu.SMEM),
            pl.BlockSpec(memory_space=pl.ANY),
            pl.BlockSpec(memory_space=pl.ANY),
            pl.BlockSpec((1, d, f2), lambda c, j, te, nv: (te[c * tiles_per_chunk + j], 0, 0)),
            pl.BlockSpec((1, f2 // 2, d), lambda c, j, te, nv: (te[c * tiles_per_chunk + j], 0, 0)),
        ],
        out_specs=pl.BlockSpec((tmg * ROW_TILE, LANES),
                               lambda c, j, te, nv: (c * tiles_per_chunk + j, 0)),
        scratch_shapes=[
            pltpu.VMEM(((chunk + 1) * ROW_TILE, LANES), F32),
            pltpu.VMEM((tmg * ROW_TILE, LANES), F32),
            pltpu.SMEM((rows_per_chunk,), jnp.int32),
            pltpu.SemaphoreType.DMA((2,)),
            pltpu.VMEM((d, f2), BF16),
            pltpu.VMEM((f2 // 2, d), BF16),
        ],
    )
    return pl.pallas_call(
        functools.partial(_experts_kernel, tiles_per_chunk, tmg),
        grid_spec=grid_spec,
        out_shape=jax.ShapeDtypeStruct((nc * rows_per_chunk * ROW_TILE, LANES), F32),
        compiler_params=_params("arbitrary", "arbitrary"),
        name="moe_experts",
    )(tile_expert, n_valid, pos_local, fill, hs, w_in, w_out)


def _combine_kernel(pos_cur_ref, pos_nxt_ref, ys_hbm, route_ref, x_ref, o_ref, ybuf, sems):
    i = pl.program_id(0)
    n_steps = pl.num_programs(0)
    tm = x_ref.shape[0]
    slot = i % 2

    @pl.when(i == 0)
    def _():
        _gather_rows(pos_cur_ref, ys_hbm, ybuf.at[0], sems.at[0], 2 * tm)

    @pl.when(i + 1 < n_steps)
    def _():
        _gather_rows(pos_nxt_ref, ys_hbm, ybuf.at[1 - slot], sems.at[1 - slot], 2 * tm)

    _wait_rows(ys_hbm, ybuf.at[slot], sems.at[slot])
    route = route_ref[...]
    y1 = _from_token_tiles(ybuf.at[slot], 0, tm, 2 * ROW_TILE)
    y2 = _from_token_tiles(ybuf.at[slot], ROW_TILE, tm, 2 * ROW_TILE)
    o_ref[...] = x_ref[...] + route[:, 2:3] * y1 + route[:, 3:4] * y2


def combine(ys, pos, route, x, tm):
    n, d = x.shape
    nsteps = n // tm
    return pl.pallas_call(
        _combine_kernel,
        grid=(nsteps,),
        in_specs=[
            pl.BlockSpec((2 * tm,), lambda i: (i,), memory_space=pltpu.SMEM),
            pl.BlockSpec((2 * tm,), lambda i: (jnp.minimum(i + 1, nsteps - 1),),
                         memory_space=pltpu.SMEM),
            pl.BlockSpec(memory_space=pl.ANY),
            pl.BlockSpec((tm, LANES), lambda i: (i, 0)),
            pl.BlockSpec((tm, d), lambda i: (i, 0)),
        ],
        out_specs=pl.BlockSpec((tm, d), lambda i: (i, 0)),
        out_shape=jax.ShapeDtypeStruct((n, d), F32),
        scratch_shapes=[pltpu.VMEM((2, 2 * tm * ROW_TILE, LANES), F32),
                        pltpu.SemaphoreType.DMA((2,))],
        compiler_params=_params("arbitrary"),
        name="moe_combine",
    )(pos, pos, ys, route, x)


def _moe_fused_kernel(ts, st_ref, cn_ref, pos_ref, wts_ref, hs_hbm, win_ref, wout_ref, acc_hbm,
                      hsv, acc, gbuf, ystage, src_ref, sems, winb, woutb):
    c = pl.program_id(0)
    e = pl.program_id(1)
    n_slots = pos_ref.shape[0]
    chunk = n_slots // 2
    rows = chunk * ROW_TILE
    unroll = 8

    def tile(r):
        return pl.ds(pl.multiple_of(r * ROW_TILE, ROW_TILE), ROW_TILE)

    @pl.when(e == 0)
    def _():
        load = pltpu.make_async_copy(
            hs_hbm.at[pl.ds(pl.multiple_of(c * rows, ROW_TILE), rows), :],
            hsv.at[pl.ds(0, rows), :], sems.at[0])
        load.start()
        hsv[pl.ds(rows, ROW_TILE), :] = jnp.zeros((ROW_TILE, LANES), F32)

        def zero(i, carry):
            acc[pl.ds(pl.multiple_of(i * 64, 64), 64), :] = jnp.zeros((64, LANES), F32)
            return carry

        lax.fori_loop(0, (rows + ROW_TILE) // 64, zero, 0)
        acc[pl.ds(rows + ROW_TILE - 64, 64), :] = jnp.zeros((64, LANES), F32)

        def scatter(i, carry):
            for u in range(unroll):
                s = i * unroll + u
                src_ref[pos_ref[s]] = s
            return carry

        lax.fori_loop(0, n_slots // unroll, scatter, 0)
        load.wait()

    winb[...] = win_ref[0].astype(BF16)
    woutb[...] = wout_ref[0].astype(BF16)
    start = st_ref[c * N_EXPERTS + e]
    cnt = cn_ref[c * N_EXPERTS + e]

    def sub_tile(k, carry):
        base = start + k * ts
        rem = cnt - k * ts

        def slot_of(m):
            s = src_ref[jnp.minimum(base + m, n_slots - 1)]
            valid = m < rem
            return jnp.where(valid, s >> 1, chunk), jnp.where(valid, wts_ref[s], 0.0)

        def gather(i, carry2):
            for u in range(unroll):
                m = i * unroll + u
                tok, _ = slot_of(m)
                gbuf[tile(m), :] = hsv[tile(tok), :]
            return carry2

        lax.fori_loop(0, ts // unroll, gather, 0)
        x = _from_token_tiles(gbuf, 0, ts, ROW_TILE).astype(BF16)
        hid = jnp.dot(x, winb[...], preferred_element_type=F32)
        a = hid[:, :D_EXPERT]
        u_ = hid[:, D_EXPERT:]
        act = (a * _sigmoid(a) * u_).astype(BF16)
        y = jnp.dot(act, woutb[...], preferred_element_type=F32)
        for j in range(ROW_TILE):
            ystage[pl.ds(j, ts, stride=ROW_TILE), :] = y[:, j * LANES:(j + 1) * LANES]

        def scatter_add(i, carry2):
            toks, vals = [], []
            for u in range(unroll):
                m = i * unroll + u
                tok, w = slot_of(m)
                toks.append(tok)
                vals.append(acc[tile(tok), :] + w * ystage[tile(m), :])
            for tok, val in zip(toks, vals):
                acc[tile(tok), :] = val
            return carry2

        lax.fori_loop(0, ts // unroll, scatter_add, 0)
        return carry

    lax.fori_loop(0, (cnt + ts - 1) // ts, sub_tile, 0)

    @pl.when(e == pl.num_programs(1) - 1)
    def _():
        store = pltpu.make_async_copy(
            acc.at[pl.ds(0, rows), :],
            acc_hbm.at[pl.ds(pl.multiple_of(c * rows, ROW_TILE), rows), :], sems.at[1])
        store.start()
        store.wait()


def moe_fused(hs, pos_local, wts, starts, counts, w_in, w_out, chunk, ts):
    ne, d, f2 = w_in.shape
    n_rows = hs.shape[0]
    nc = n_rows // (chunk * ROW_TILE)
    grid_spec = pltpu.PrefetchScalarGridSpec(
        num_scalar_prefetch=2,
        grid=(nc, ne),
        in_specs=[
            pl.BlockSpec((2 * chunk,), lambda c, e, st, cn: (c,), memory_space=pltpu.SMEM),
            pl.BlockSpec((2 * chunk,), lambda c, e, st, cn: (c,), memory_space=pltpu.SMEM),
            pl.BlockSpec(memory_space=pl.ANY),
            pl.BlockSpec((1, d, f2), lambda c, e, st, cn: (e, 0, 0)),
            pl.BlockSpec((1, f2 // 2, d), lambda c, e, st, cn: (e, 0, 0)),
        ],
        out_specs=pl.BlockSpec(memory_space=pl.ANY),
        scratch_shapes=[
            pltpu.VMEM(((chunk + 1) * ROW_TILE, LANES), F32),
            pltpu.VMEM(((chunk + 1) * ROW_TILE, LANES), F32),
            pltpu.VMEM((ts * ROW_TILE, LANES), F32),
            pltpu.VMEM((ts * ROW_TILE, LANES), F32),
            pltpu.SMEM((2 * chunk,), jnp.int32),
            pltpu.SemaphoreType.DMA((2,)),
            pltpu.VMEM((d, f2), BF16),
            pltpu.VMEM((f2 // 2, d), BF16),
        ],
    )
    return pl.pallas_call(
        functools.partial(_moe_fused_kernel, ts),
        grid_spec=grid_spec,
        out_shape=jax.ShapeDtypeStruct((n_rows, LANES), F32),
        compiler_params=pltpu.CompilerParams(dimension_semantics=("arbitrary", "arbitrary"),
                                             vmem_limit_bytes=MOE_VMEM_LIMIT),
        name="moe_fused",
    )(starts, counts, pos_local, wts, hs, w_in, w_out)


def _finish_kernel(acc_ref, x_ref, o_ref):
    tm = x_ref.shape[0]
    o_ref[...] = x_ref[...] + _from_token_tiles(acc_ref, 0, tm, ROW_TILE)


def moe_finish(acc, x, tm):
    n, d = x.shape
    return pl.pallas_call(
        _finish_kernel,
        grid=(n // tm,),
        in_specs=[pl.BlockSpec((tm * ROW_TILE, LANES), lambda i: (i, 0)),
                  pl.BlockSpec((tm, d), lambda i: (i, 0))],
        out_specs=pl.BlockSpec((tm, d), lambda i: (i, 0)),
        out_shape=jax.ShapeDtypeStruct((n, d), F32),
        compiler_params=_params("parallel"),
        name="moe_finish",
    )(acc, x)


def moe_layer(x, g, wr, br, w_in, w_out):
    n = x.shape[0]
    chunk = min(n, MOE_CHUNK)
    nc = n // chunk
    hs, route = router(x, g, wr, br, MOE_TM)
    rank, counts = plan(route, MOE_TM, chunk)
    ids = route[:, 0:2].astype(jnp.int32).reshape(nc, chunk, 2)
    cnt = counts.reshape(nc, ROW_TILE, LANES)[:, 0, :N_EXPERTS].astype(jnp.int32)
    starts = jnp.cumsum(cnt, axis=1) - cnt
    sel = ids[..., None] == jnp.arange(N_EXPERTS, dtype=jnp.int32)
    start_of = jnp.sum(jnp.where(sel, starts[:, None, None, :], 0), axis=-1)
    pos_local = start_of + rank[:, 0:2].astype(jnp.int32).reshape(nc, chunk, 2)
    acc = moe_fused(hs, pos_local.reshape(-1), route[:, 2:4].reshape(-1), starts.reshape(-1),
                    cnt.reshape(-1), w_in, w_out, chunk, MOE_TS)
    return moe_finish(acc, x, MOE_TM)


def moe_layer_unfused(x, g, wr, br, w_in, w_out):
    n = x.shape[0]
    chunk = min(n, MOE_CHUNK)
    nc = n // chunk
    tpc = (2 * chunk + N_EXPERTS * (MOE_TMG - 1)) // MOE_TMG
    hs, route = router(x, g, wr, br, MOE_TM)
    rank, counts = plan(route, MOE_TM, chunk)
    ids = route[:, 0:2].astype(jnp.int32).reshape(nc, chunk, 2)
    cnt = counts.reshape(nc, ROW_TILE, LANES)[:, 0, :N_EXPERTS].astype(jnp.int32)
    padded = (cnt + MOE_TMG - 1) // MOE_TMG * MOE_TMG
    ends = jnp.cumsum(padded, axis=1)
    starts = ends - padded
    sel = ids[..., None] == jnp.arange(N_EXPERTS, dtype=jnp.int32)
    start_of = jnp.sum(jnp.where(sel, starts[:, None, None, :], 0), axis=-1)
    pos_local = start_of + rank[:, 0:2].astype(jnp.int32).reshape(nc, chunk, 2)
    tile_row = jnp.arange(tpc, dtype=jnp.int32) * MOE_TMG
    tile_expert = jnp.minimum(jnp.sum(ends[:, None, :] <= tile_row[None, :, None], axis=-1),
                              N_EXPERTS - 1).astype(jnp.int32)
    n_valid = (ends[:, -1] // MOE_TMG).astype(jnp.int32)
    ys = experts(hs, pos_local.reshape(-1), tile_expert.reshape(-1), n_valid, w_in, w_out,
                 MOE_TMG, chunk, tpc)
    pos = pos_local + (jnp.arange(nc, dtype=jnp.int32) * (tpc * MOE_TMG))[:, None, None]
    return combine(ys, pos.reshape(-1), route, x, MOE_TM)


def _qkv_kernel(x_ref, g_ref, w_ref, qg_ref, kg_ref, seg_ref, q_ref, k_ref, v_ref):
    h = _rms(x_ref[...], g_ref[...]).astype(BF16)
    qkv = jnp.dot(h, w_ref[...], preferred_element_type=F32)
    nq = N_HEADS * HEAD_DIM
    nk = N_KV * HEAD_DIM
    q = qkv[:, :nq]
    k = qkv[:, nq:nq + nk]
    v_ref[...] = qkv[:, nq + nk:]

    def seg_mean_sq(z, seg):
        zz = z * z
        hi = zz.astype(BF16)
        lo = (zz - hi.astype(F32)).astype(BF16)
        return (jnp.dot(hi, seg, preferred_element_type=F32)
                + jnp.dot(lo, seg, preferred_element_type=F32))

    seg = seg_ref[...]
    qn = q * lax.rsqrt(seg_mean_sq(q, seg) + RMS_EPS) * qg_ref[...]
    q_ref[...] = (qn * (HEAD_DIM ** -0.5)).astype(BF16)
    kn = k * lax.rsqrt(seg_mean_sq(k, seg[:nk, :nk]) + RMS_EPS) * kg_ref[...]
    k_ref[...] = kn


def qkv_proj(x, g, w, qg, kg, seg, tm):
    n, d = x.shape
    nq = N_HEADS * HEAD_DIM
    nk = N_KV * HEAD_DIM
    return pl.pallas_call(
        _qkv_kernel,
        grid=(n // tm,),
        in_specs=[pl.BlockSpec((tm, d), lambda i: (i, 0)), _full((1, d)),
                  _full((d, nq + 2 * nk)), _full((1, nq)), _full((1, nk)), _full((nq, nq))],
        out_specs=[pl.BlockSpec((tm, nq), lambda i: (i, 0)),
                   pl.BlockSpec((tm, nk), lambda i: (i, 0)),
                   pl.BlockSpec((tm, nk), lambda i: (i, 0))],
        out_shape=[jax.ShapeDtypeStruct((n, nq), BF16),
                   jax.ShapeDtypeStruct((n, nk), F32),
                   jax.ShapeDtypeStruct((n, nk), F32)],
        compiler_params=_params("parallel"),
        name="qkv_proj",
    )(x, g, w, qg, kg, seg)


def _attn_prompt_kernel(q_ref, kc_ref, kp_ref, vc_ref, vp_ref, tbl_ref, sink_ref, x_ref,
                        wo_ref, o_ref, cat_ref):
    n = pl.program_id(1)
    first = jnp.where(n == 0, NEG_INF, 0.0).astype(F32)
    kk = jnp.concatenate([kp_ref[...], kc_ref[...]], axis=0).astype(BF16)
    vv = jnp.concatenate([vp_ref[...], vc_ref[...]], axis=0).astype(BF16)
    col = lax.broadcasted_iota(jnp.int32, (WINDOW, 2 * WINDOW), 1)
    prev_mask = jnp.where(col < WINDOW, first, 0.0)
    for h in range(N_HEADS):
        kh = h // GROUP
        qh = q_ref[:, h * HEAD_DIM:(h + 1) * HEAD_DIM]
        kslice = kk[:, kh * HEAD_DIM:(kh + 1) * HEAD_DIM]
        vslice = vv[:, kh * HEAD_DIM:(kh + 1) * HEAD_DIM]
        s = lax.dot_general(qh, kslice, (((1,), (1,)), ((), ())), preferred_element_type=F32)
        logits = s + tbl_ref[h] + prev_mask
        sink = sink_ref[h]
        m = jnp.maximum(jnp.max(logits, axis=-1, keepdims=True), sink)
        e = jnp.exp(logits - m)
        denom = jnp.sum(e, axis=-1, keepdims=True) + jnp.exp(sink - m)
        p = e * (1.0 / denom)
        cat_ref[:, h * HEAD_DIM:(h + 1) * HEAD_DIM] = jnp.dot(
            p.astype(BF16), vslice, preferred_element_type=F32).astype(BF16)
    o_ref[...] = x_ref[...] + jnp.dot(cat_ref[...], wo_ref[...], preferred_element_type=F32)


def attn_prompt(q, k, v, tbl, sinks, x, wo, batch, seq):
    n, d = x.shape
    nb = seq // WINDOW
    nk = N_KV * HEAD_DIM
    cur = lambda b, i: (b * nb + i, 0)
    prev = lambda b, i: (b * nb + jnp.maximum(i - 1, 0), 0)
    return pl.pallas_call(
        _attn_prompt_kernel,
        grid=(batch, nb),
        in_specs=[
            pl.BlockSpec((WINDOW, d), cur),
            pl.BlockSpec((WINDOW, nk), cur), pl.BlockSpec((WINDOW, nk), prev),
            pl.BlockSpec((WINDOW, nk), cur), pl.BlockSpec((WINDOW, nk), prev),
            _full((N_HEADS, WINDOW, 2 * WINDOW)),
            pl.BlockSpec(memory_space=pltpu.SMEM),
            pl.BlockSpec((WINDOW, d), cur),
            _full((d, d)),
        ],
        out_specs=pl.BlockSpec((WINDOW, d), cur),
        out_shape=jax.ShapeDtypeStruct((n, d), F32),
        scratch_shapes=[pltpu.VMEM((WINDOW, d), BF16)],
        compiler_params=_params("parallel", "parallel"),
        name="attn_prompt",
    )(q, k, k, v, v, tbl, sinks, x, wo)


def _attn_sample_kernel(q_ref, kn_ref, vn_ref, ck_ref, cv_ref, tbl_ref, sink_ref, o_ref,
                        kf_ref, vf_ref):
    bs = q_ref.shape[0]
    ts = kn_ref.shape[1]
    for b in range(bs):
        kf_ref[0:WINDOW, :] = ck_ref[b]
        kf_ref[WINDOW:WINDOW + ts, :] = kn_ref[b]
        vf_ref[0:WINDOW, :] = cv_ref[b]
        vf_ref[WINDOW:WINDOW + ts, :] = vn_ref[b]
        for kh in range(N_KV):
            kslice = kf_ref[:, kh * HEAD_DIM:(kh + 1) * HEAD_DIM].astype(BF16)
            vslice = vf_ref[:, kh * HEAD_DIM:(kh + 1) * HEAD_DIM].astype(BF16)
            qt = q_ref[b, kh]
            s = lax.dot_general(qt, kslice, (((1,), (1,)), ((), ())),
                                preferred_element_type=F32)
            logits = s + tbl_ref[kh]
            sink = sink_ref[kh]
            m = jnp.maximum(jnp.max(logits, axis=-1, keepdims=True), sink)
            e = jnp.exp(logits - m)
            denom = jnp.sum(e, axis=-1, keepdims=True) + jnp.exp(sink - m)
            p = (e * (1.0 / denom)).astype(BF16)
            o_ref[b, kh] = jnp.dot(p, vslice, preferred_element_type=F32).astype(BF16)


def attn_sample(q4, kn, vn, ck, cv, tbl, sink, bs):
    nb = q4.shape[0]
    ts = kn.shape[1]
    nk = N_KV * HEAD_DIM
    tg = ts * GROUP
    kt = WINDOW + ts
    return pl.pallas_call(
        _attn_sample_kernel,
        grid=(nb // bs,),
        in_specs=[
            pl.BlockSpec((bs, N_KV, tg, HEAD_DIM), lambda i: (i, 0, 0, 0)),
            pl.BlockSpec((bs, ts, nk), lambda i: (i, 0, 0)),
            pl.BlockSpec((bs, ts, nk), lambda i: (i, 0, 0)),
            pl.BlockSpec((bs, WINDOW, nk), lambda i: (i, 0, 0)),
            pl.BlockSpec((bs, WINDOW, nk), lambda i: (i, 0, 0)),
            _full((N_KV, tg, kt)),
            _full((N_KV, tg, 1)),
        ],
        out_specs=pl.BlockSpec((bs, N_KV, tg, HEAD_DIM), lambda i: (i, 0, 0, 0)),
        out_shape=jax.ShapeDtypeStruct((nb, N_KV, tg, HEAD_DIM), BF16),
        scratch_shapes=[pltpu.VMEM((kt, nk), F32), pltpu.VMEM((kt, nk), F32)],
        compiler_params=_params("parallel"),
        name="attn_sample",
    )(q4, kn, vn, ck, cv, tbl, sink)


def _proj_res_kernel(a_ref, w_ref, x_ref, o_ref):
    o_ref[...] = x_ref[...] + jnp.dot(a_ref[...], w_ref[...], preferred_element_type=F32)


def proj_residual(a, w, x, tm):
    n, d = x.shape
    kdim = a.shape[1]
    return pl.pallas_call(
        _proj_res_kernel,
        grid=(n // tm,),
        in_specs=[pl.BlockSpec((tm, kdim), lambda i: (i, 0)), _full((kdim, d)),
                  pl.BlockSpec((tm, d), lambda i: (i, 0))],
        out_specs=pl.BlockSpec((tm, d), lambda i: (i, 0)),
        out_shape=jax.ShapeDtypeStruct((n, d), F32),
        compiler_params=_params("parallel"),
        name="proj_residual",
    )(a, w, x)


def _t5_bucket_np(dist):
    n = np.maximum(dist, 0)
    max_exact = N_BUCKETS // 2
    large = max_exact + (np.log(np.maximum(n, 1).astype(np.float32) / max_exact)
                         / math.log(MAX_DISTANCE / max_exact) * (N_BUCKETS - max_exact)).astype(np.int32)
    large = np.minimum(large, N_BUCKETS - 1)
    return np.where(n < max_exact, n, large)


def _bias_table(rel_bias, dist):
    valid = (dist >= 0) & (dist <= WINDOW)
    onehot = (np.asarray(_t5_bucket_np(dist))[..., None] == np.arange(N_BUCKETS)).astype(np.float32)
    b = jnp.einsum("qkb,bh->hqk", jnp.asarray(onehot), rel_bias.astype(F32),
                   precision=lax.Precision.HIGHEST)
    return jnp.where(jnp.asarray(valid)[None], b, NEG_INF)


def kernel(x_prompt, x_sample, state_conv, cache_swa_k, cache_swa_v, rms_mix_g, rms_ffn_g, conv_w_in, conv_dw_w, conv_dw_b, conv_ln_g, conv_ln_b, conv_w_out, attn_w_qkv, attn_q_norm_g, attn_k_norm_g, attn_sinks, attn_w_o, rel_bias, router_group_w, router_group_b, router_expert_w, router_expert_b, expert_w_in, expert_w_out):
    batch, seq, d = x_prompt.shape
    nsb, ts, _ = x_sample.shape
    xp = x_prompt.reshape(batch * seq, d)
    xs = x_sample.reshape(nsb * ts, d)
    row = lambda a: a.reshape(1, -1).astype(F32)

    def router_w(i):
        we = jnp.transpose(router_expert_w[i], (1, 0, 2)).reshape(d, N_EXPERTS)
        wr = jnp.concatenate([we, router_group_w[i]], axis=1)
        wr = jnp.pad(wr, ((0, 0), (0, LANES - wr.shape[1])))
        br = jnp.concatenate([router_expert_b[i].reshape(-1), router_group_b[i]])
        br = jnp.pad(br, (0, LANES - br.shape[0])).reshape(1, LANES)
        return wr.astype(F32), br.astype(F32)

    g0 = row(rms_mix_g[0])
    w_in = conv_w_in[0].astype(BF16)
    dww = jnp.pad(conv_dw_w[0].astype(F32), ((0, HALO - CONV_WIDTH), (0, 0)))
    dwb, lng, lnb = row(conv_dw_b[0]), row(conv_ln_g[0]), row(conv_ln_b[0])
    w_out = conv_w_out[0].astype(BF16)
    up = glu_proj(xp, g0, w_in, 512)
    us = glu_proj(xs, g0, w_in, 512)
    xp = conv_prompt(up, xp, dww, dwb, lng, lnb, w_out, batch, seq, 256)
    us3 = us.reshape(nsb, ts, -1)
    xs = conv_sample(us3, state_conv[0], xs, dww, dwb, lng, lnb, w_out, 32)
    conv_p = up.reshape(batch, seq, -1)[:, seq - PAST:]
    conv_s = jnp.concatenate([state_conv[0], us3], axis=1)[:, ts:]

    wr0, br0 = router_w(0)
    xp = moe_layer(xp, row(rms_ffn_g[0]), wr0, br0, expert_w_in[0], expert_w_out[0])
    xs = moe_layer(xs, row(rms_ffn_g[0]), wr0, br0, expert_w_in[0], expert_w_out[0])

    g1 = row(rms_mix_g[1])
    w_qkv = attn_w_qkv[0].astype(BF16)
    qg = jnp.tile(attn_q_norm_g[0].astype(F32), N_HEADS).reshape(1, -1)
    kg = jnp.tile(attn_k_norm_g[0].astype(F32), N_KV).reshape(1, -1)
    nq = N_HEADS * HEAD_DIM
    seg = jnp.asarray(np.kron(np.eye(N_HEADS), np.ones((HEAD_DIM, HEAD_DIM))) / HEAD_DIM, BF16)
    w_o = attn_w_o[0].astype(BF16)
    sinks = attn_sinks[0].astype(F32)

    qp, kp, vp = qkv_proj(xp, g1, w_qkv, qg, kg, seg, 512)
    qs, ks, vs = qkv_proj(xs, g1, w_qkv, qg, kg, seg, 512)

    q_off = np.arange(WINDOW)[:, None]
    dist_p = q_off + WINDOW - np.arange(2 * WINDOW)[None, :]
    tbl_p = _bias_table(rel_bias, dist_p)
    xp = attn_prompt(qp, kp, vp, tbl_p, sinks, xp, w_o, batch, seq)

    kt = WINDOW + ts
    dist_s = np.arange(ts)[:, None] + WINDOW - np.arange(kt)[None, :]
    tbl_s = _bias_table(rel_bias, dist_s)
    tbl_s = jnp.transpose(tbl_s.reshape(N_KV, GROUP, ts, kt), (0, 2, 1, 3)).reshape(N_KV, ts * GROUP, kt)
    sink_s = jnp.tile(sinks.reshape(N_KV, 1, GROUP), (1, ts, 1)).reshape(N_KV, ts * GROUP, 1)
    q4 = jnp.transpose(qs.reshape(nsb, ts, N_KV, GROUP, HEAD_DIM), (0, 2, 1, 3, 4))
    q4 = q4.reshape(nsb, N_KV, ts * GROUP, HEAD_DIM)
    nk = N_KV * HEAD_DIM
    ks3, vs3 = ks.reshape(nsb, ts, nk), vs.reshape(nsb, ts, nk)
    ck = cache_swa_k[0].reshape(nsb, WINDOW, nk)
    cv = cache_swa_v[0].reshape(nsb, WINDOW, nk)
    o4 = attn_sample(q4, ks3, vs3, ck, cv, tbl_s, sink_s, 16)
    os_ = jnp.transpose(o4.reshape(nsb, N_KV, ts, GROUP, HEAD_DIM), (0, 2, 1, 3, 4)).reshape(nsb * ts, nq)
    xs = proj_residual(os_, w_o, xs, 512)

    k_p = kp.reshape(batch, seq, N_KV, HEAD_DIM)[:, seq - WINDOW:]
    v_p = vp.reshape(batch, seq, N_KV, HEAD_DIM)[:, seq - WINDOW:]
    k_s = jnp.concatenate([cache_swa_k[0], ks.reshape(nsb, ts, N_KV, HEAD_DIM)], axis=1)[:, ts:]
    v_s = jnp.concatenate([cache_swa_v[0], vs.reshape(nsb, ts, N_KV, HEAD_DIM)], axis=1)[:, ts:]

    wr1, br1 = router_w(1)
    xp = moe_layer(xp, row(rms_ffn_g[1]), wr1, br1, expert_w_in[1], expert_w_out[1])
    xs = moe_layer(xs, row(rms_ffn_g[1]), wr1, br1, expert_w_in[1], expert_w_out[1])

    return (xp.reshape(batch, seq, d), xs.reshape(nsb, ts, d),
            conv_p[None], conv_s[None], k_p[None], v_p[None], k_s[None], v_s[None])
```

```python
import functools
import math

import numpy as np
import jax
import jax.numpy as jnp
from jax import lax
from jax.experimental import pallas as pl
from jax.experimental.pallas import tpu as pltpu

D_MODEL = 1024
DEPTH = 2
CONV_WIDTH = 31
PAST = CONV_WIDTH - 1
HEAD_DIM = 64
N_HEADS = 16
N_KV = 2
GROUP = 8
WINDOW = 128
N_BUCKETS = 32
MAX_DISTANCE = 128
N_GROUPS = 4
EPG = 8
N_EXPERTS = 32
D_EXPERT = 256
RMS_EPS = 1e-6
LN_EPS = 1e-5
NEG_INF = -1e30

F32 = jnp.float32
BF16 = jnp.bfloat16
LANES = 128
ROW_TILE = 8
MOE_TM = 256
MOE_TMG = 256
MOE_CHUNK = 8192
MOE_TS = 128
MOE_VMEM_LIMIT = 56 * 1024 * 1024
VMEM_LIMIT = 48 * 1024 * 1024


def _params(*sem):
    return pltpu.CompilerParams(dimension_semantics=sem, vmem_limit_bytes=VMEM_LIMIT)


def _rms(x, g):
    return x * lax.rsqrt(jnp.mean(x * x, axis=-1, keepdims=True) + RMS_EPS) * g


def _sigmoid(x):
    return 1.0 / (1.0 + jnp.exp(-x))


def _full(shape):
    return pl.BlockSpec(shape, lambda *_: (0,) * len(shape))


def _glu_kernel(x_ref, g_ref, wa_ref, wg_ref, u_ref):
    h = _rms(x_ref[...], g_ref[...]).astype(BF16)
    a = jnp.dot(h, wa_ref[...], preferred_element_type=F32)
    gate = jnp.dot(h, wg_ref[...], preferred_element_type=F32)
    u_ref[...] = a * _sigmoid(gate)


def glu_proj(x, g, w_in, tm):
    n, d = x.shape
    c = w_in.shape[1] // 2
    return pl.pallas_call(
        _glu_kernel,
        grid=(n // tm,),
        in_specs=[
            pl.BlockSpec((tm, d), lambda i: (i, 0)),
            _full((1, d)),
            pl.BlockSpec((d, c), lambda i: (0, 0)),
            pl.BlockSpec((d, c), lambda i: (0, 1)),
        ],
        out_specs=pl.BlockSpec((tm, c), lambda i: (i, 0)),
        out_shape=jax.ShapeDtypeStruct((n, c), F32),
        compiler_params=_params("parallel"),
        name="glu_proj",
    )(x, g, w_in, w_in)


def _ln_silu_out(y, lng, lnb, wout_ref, x):
    mu = jnp.mean(y, axis=-1, keepdims=True)
    yc = y - mu
    z = yc * lax.rsqrt(jnp.mean(yc * yc, axis=-1, keepdims=True) + LN_EPS) * lng + lnb
    z = z * _sigmoid(z)
    return x + jnp.dot(z.astype(BF16), wout_ref[...], preferred_element_type=F32)


HALO = 32
CONV_RC = 64
CONV_CC = 128


def _conv_prompt_kernel(ucur_ref, uprev_ref, x_ref, dww_ref, dwb_ref, lng_ref, lnb_ref,
                        wout_ref, o_ref, up_ref, y_ref):
    t = pl.program_id(1)
    tt, c = ucur_ref.shape
    keep = (t > 0).astype(F32)
    up_ref[0:HALO, :] = uprev_ref[...] * keep
    up_ref[HALO:HALO + tt, :] = ucur_ref[...]
    up_ref[HALO + tt:, :] = jnp.zeros((ROW_TILE, c), F32)
    off = HALO - PAST
    for r0 in range(0, tt, CONV_RC):
        for c0 in range(0, c, CONV_CC):
            y = jnp.zeros((CONV_RC, CONV_CC), F32) + dwb_ref[:, c0:c0 + CONV_CC]
            for s in range(ROW_TILE):
                v = None
                for q in range((off + CONV_WIDTH - 1) // ROW_TILE + 1):
                    k = ROW_TILE * q + s - off
                    if k < 0 or k >= CONV_WIDTH:
                        continue
                    lo = r0 + ROW_TILE * q
                    term = (up_ref[lo:lo + CONV_RC + ROW_TILE, c0:c0 + CONV_CC]
                            * dww_ref[k:k + 1, c0:c0 + CONV_CC])
                    v = term if v is None else v + term
                y = y + v[s:s + CONV_RC]
            y_ref[r0:r0 + CONV_RC, c0:c0 + CONV_CC] = y
    o_ref[...] = _ln_silu_out(y_ref[...], lng_ref[...], lnb_ref[...], wout_ref, x_ref[...])


def conv_prompt(u, x, dww, dwb, lng, lnb, wout, batch, seq, tt):
    n, c = u.shape
    d = x.shape[1]
    nt = seq // tt
    hb = tt // HALO
    return pl.pallas_call(
        _conv_prompt_kernel,
        grid=(batch, nt),
        in_specs=[
            pl.BlockSpec((tt, c), lambda b, t: (b * nt + t, 0)),
            pl.BlockSpec((HALO, c), lambda b, t: (jnp.maximum((b * nt + t) * hb - 1, 0), 0)),
            pl.BlockSpec((tt, d), lambda b, t: (b * nt + t, 0)),
            _full((HALO, c)), _full((1, c)), _full((1, c)), _full((1, c)),
            _full((c, d)),
        ],
        out_specs=pl.BlockSpec((tt, d), lambda b, t: (b * nt + t, 0)),
        out_shape=jax.ShapeDtypeStruct((n, d), F32),
        scratch_shapes=[pltpu.VMEM((tt + HALO + ROW_TILE, c), F32), pltpu.VMEM((tt, c), F32)],
        compiler_params=_params("parallel", "parallel"),
        name="conv_prompt",
    )(u, u, x, dww, dwb, lng, lnb, wout)


CONV_SB = 4


def _conv_sample_kernel(u_ref, st_ref, x_ref, dww_ref, dwb_ref, lng_ref, lnb_ref,
                        wout_ref, o_ref, up_ref, y_ref):
    bs, ts, c = u_ref.shape
    up_ref[:, 0:PAST, :] = st_ref[...]
    up_ref[:, PAST:PAST + ts, :] = u_ref[...]
    for b0 in range(0, bs, CONV_SB):
        acc = jnp.zeros((CONV_SB, ts, c), F32) + dwb_ref[...][None]
        for k in range(CONV_WIDTH):
            acc = acc + up_ref[b0:b0 + CONV_SB, k:k + ts, :] * dww_ref[k:k + 1, :][None]
        y_ref[b0 * ts:(b0 + CONV_SB) * ts, :] = acc.reshape(CONV_SB * ts, c)
    o_ref[...] = _ln_silu_out(y_ref[...], lng_ref[...], lnb_ref[...], wout_ref, x_ref[...])


def conv_sample(u3, state, x, dww, dwb, lng, lnb, wout, bs):
    nb, ts, c = u3.shape
    d = x.shape[1]
    return pl.pallas_call(
        _conv_sample_kernel,
        grid=(nb // bs,),
        in_specs=[
            pl.BlockSpec((bs, ts, c), lambda i: (i, 0, 0)),
            pl.BlockSpec((bs, PAST, c), lambda i: (i, 0, 0)),
            pl.BlockSpec((bs * ts, d), lambda i: (i, 0)),
            _full((HALO, c)), _full((1, c)), _full((1, c)), _full((1, c)),
            _full((c, d)),
        ],
        out_specs=pl.BlockSpec((bs * ts, d), lambda i: (i, 0)),
        out_shape=jax.ShapeDtypeStruct((nb * ts, d), F32),
        scratch_shapes=[pltpu.VMEM((bs, PAST + ts, c), F32), pltpu.VMEM((bs * ts, c), F32)],
        compiler_params=_params("parallel"),
        name="conv_sample",
    )(u3, state, x, dww, dwb, lng, lnb, wout)


def _router_kernel(x_ref, g_ref, wr_ref, br_ref, hs_ref, route_ref):
    h = _rms(x_ref[...], g_ref[...])
    tm, d = h.shape
    for j in range(d // LANES):
        hs_ref[pl.ds(j, tm, stride=ROW_TILE), :] = h[:, j * LANES:(j + 1) * LANES]
    logits = jnp.dot(h, wr_ref[...], preferred_element_type=F32,
                     precision=lax.Precision.HIGHEST) + br_ref[...]
    lane = lax.broadcasted_iota(jnp.int32, logits.shape, 1).astype(F32)
    big = jnp.float32(LANES)
    is_g = (lane >= N_EXPERTS) & (lane < N_EXPERTS + N_GROUPS)
    gl = jnp.where(is_g, logits, NEG_INF)
    gm = jnp.max(gl, axis=-1, keepdims=True)
    g_sel = jnp.min(jnp.where(gl == gm, lane, big), axis=-1, keepdims=True) - N_EXPERTS
    gate_g = 1.0 / jnp.sum(jnp.where(is_g, jnp.exp(gl - gm), 0.0), axis=-1, keepdims=True)
    lo = g_sel * EPG
    in_grp = (lane >= lo) & (lane < lo + EPG)
    el = jnp.where(in_grp, logits, NEG_INF)
    v1 = jnp.max(el, axis=-1, keepdims=True)
    i1 = jnp.min(jnp.where(el == v1, lane, big), axis=-1, keepdims=True)
    el2 = jnp.where(lane == i1, NEG_INF, el)
    v2 = jnp.max(el2, axis=-1, keepdims=True)
    i2 = jnp.min(jnp.where(el2 == v2, lane, big), axis=-1, keepdims=True)
    e2 = jnp.exp(v2 - v1)
    w1 = gate_g / (1.0 + e2)
    w2 = gate_g * e2 / (1.0 + e2)
    route_ref[...] = jnp.where(lane == 0.0, i1, jnp.where(lane == 1.0, i2, jnp.where(
        lane == 2.0, w1, jnp.where(lane == 3.0, w2, 0.0))))


def router(x, g, wr, br, tm):
    n, d = x.shape
    return pl.pallas_call(
        _router_kernel,
        grid=(n // tm,),
        in_specs=[pl.BlockSpec((tm, d), lambda i: (i, 0)), _full((1, d)),
                  _full((d, LANES)), _full((1, LANES))],
        out_specs=[pl.BlockSpec((tm * ROW_TILE, LANES), lambda i: (i, 0)),
                   pl.BlockSpec((tm, LANES), lambda i: (i, 0))],
        out_shape=[jax.ShapeDtypeStruct((n * ROW_TILE, LANES), F32),
                   jax.ShapeDtypeStruct((n, LANES), F32)],
        compiler_params=_params("parallel"),
        name="router",
    )(x, g, wr, br)


def _plan_kernel(tiles_per_chunk, route_ref, tri_ref, rank_ref, cnt_ref, carry_ref):
    i = pl.program_id(0)

    @pl.when(i % tiles_per_chunk == 0)
    def _():
        carry_ref[...] = jnp.zeros_like(carry_ref)

    route = route_ref[...]
    lane = lax.broadcasted_iota(jnp.int32, route.shape, 1).astype(F32)
    i1 = route[:, 0:1]
    i2 = route[:, 1:2]
    hit1 = lane == i1
    hit2 = lane == i2
    onehot = jnp.where(hit1 | hit2, 1.0, 0.0)
    before = carry_ref[...] + jnp.dot(tri_ref[...], onehot.astype(BF16), preferred_element_type=F32)
    r1 = jnp.sum(jnp.where(hit1, before, 0.0), axis=-1, keepdims=True)
    r2 = jnp.sum(jnp.where(hit2, before, 0.0), axis=-1, keepdims=True)
    rank_ref[...] = jnp.where(lane == 0.0, r1, jnp.where(lane == 1.0, r2, 0.0))
    carry_ref[...] += jnp.sum(onehot, axis=0, keepdims=True)
    cnt_ref[...] = jnp.broadcast_to(carry_ref[...], cnt_ref.shape)


def plan(route, tm, chunk):
    n = route.shape[0]
    tpc = chunk // tm
    tri = jnp.asarray(np.tril(np.ones((tm, tm), np.float32), -1), BF16)
    return pl.pallas_call(
        functools.partial(_plan_kernel, tpc),
        grid=(n // tm,),
        in_specs=[pl.BlockSpec((tm, LANES), lambda i: (i, 0)), _full((tm, tm))],
        out_specs=[pl.BlockSpec((tm, LANES), lambda i: (i, 0)),
                   pl.BlockSpec((ROW_TILE, LANES), lambda i: (i // tpc, 0))],
        out_shape=[jax.ShapeDtypeStruct((n, LANES), F32),
                   jax.ShapeDtypeStruct((n // chunk * ROW_TILE, LANES), F32)],
        scratch_shapes=[pltpu.VMEM((1, LANES), F32)],
        compiler_params=_params("arbitrary"),
        name="moe_plan",
    )(route, tri)


def _gather_rows(idx_ref, src_hbm, dst, sem, n_rows):
    unroll = 8

    def body(c, carry):
        for u in range(unroll):
            m = c * unroll + u
            tok = idx_ref[m]
            pltpu.make_async_copy(
                src_hbm.at[pl.ds(pl.multiple_of(tok * ROW_TILE, ROW_TILE), ROW_TILE), :],
                dst.at[pl.ds(pl.multiple_of(m * ROW_TILE, ROW_TILE), ROW_TILE), :],
                sem).start()
        return carry

    lax.fori_loop(0, n_rows // unroll, body, 0)


def _wait_rows(src_hbm, dst, sem):
    pltpu.make_async_copy(src_hbm.at[pl.ds(0, dst.shape[0]), :], dst, sem).wait()


def _from_token_tiles(buf, start, n_rows, stride):
    return jnp.concatenate(
        [buf[pl.ds(start + j, n_rows, stride=stride), :] for j in range(ROW_TILE)], axis=-1)


def _experts_kernel(tiles_per_chunk, tmg, te_ref, nv_ref, pos_ref, fill_hbm, hs_hbm,
                    win_ref, wout_ref, ys_ref, hsv, gbuf, src_ref, sems, winb, woutb):
    c = pl.program_id(0)
    j = pl.program_id(1)
    t = c * tiles_per_chunk + j
    nv = nv_ref[c]
    chunk = pos_ref.shape[0] // 2
    unroll = 8

    @pl.when(j == 0)
    def _():
        load = pltpu.make_async_copy(
            hs_hbm.at[pl.ds(pl.multiple_of(c * chunk * ROW_TILE, ROW_TILE), chunk * ROW_TILE), :],
            hsv.at[pl.ds(0, chunk * ROW_TILE), :], sems.at[0])
        load.start()
        fill = pltpu.make_async_copy(fill_hbm, src_ref, sems.at[1])
        fill.start()
        hsv[pl.ds(chunk * ROW_TILE, ROW_TILE), :] = jnp.zeros((ROW_TILE, LANES), F32)
        fill.wait()

        def scatter(i, carry):
            for u in range(unroll):
                tok = i * unroll + u
                src_ref[pos_ref[2 * tok]] = tok
                src_ref[pos_ref[2 * tok + 1]] = tok
            return carry

        lax.fori_loop(0, chunk // unroll, scatter, 0)
        load.wait()

    changed = jnp.logical_or(j == 0, te_ref[t] != te_ref[jnp.maximum(t - 1, 0)])

    @pl.when(jnp.logical_and(changed, j < nv))
    def _():
        winb[...] = win_ref[0].astype(BF16)
        woutb[...] = wout_ref[0].astype(BF16)

    @pl.when(j < nv)
    def _():
        base = j * tmg

        def gather(i, carry):
            for u in range(unroll):
                m = i * unroll + u
                tok = src_ref[base + m]
                gbuf[pl.ds(pl.multiple_of(m * ROW_TILE, ROW_TILE), ROW_TILE), :] = hsv[
                    pl.ds(pl.multiple_of(tok * ROW_TILE, ROW_TILE), ROW_TILE), :]
            return carry

        lax.fori_loop(0, tmg // unroll, gather, 0)
        x = _from_token_tiles(gbuf, 0, tmg, ROW_TILE).astype(BF16)
        hid = jnp.dot(x, winb[...], preferred_element_type=F32)
        a = hid[:, :D_EXPERT]
        u = hid[:, D_EXPERT:]
        act = (a * _sigmoid(a) * u).astype(BF16)
        y = jnp.dot(act, woutb[...], preferred_element_type=F32)
        for k in range(ROW_TILE):
            ys_ref[pl.ds(k, tmg, stride=ROW_TILE), :] = y[:, k * LANES:(k + 1) * LANES]

    @pl.when(j >= nv)
    def _():
        ys_ref[...] = jnp.zeros_like(ys_ref)


def experts(hs, pos_local, tile_expert, n_valid, w_in, w_out, tmg, chunk, tiles_per_chunk):
    _, d, f2 = w_in.shape
    nc = hs.shape[0] // (chunk * ROW_TILE)
    rows_per_chunk = tiles_per_chunk * tmg
    fill = jnp.full((rows_per_chunk,), chunk, jnp.int32)
    grid_spec = pltpu.PrefetchScalarGridSpec(
        num_scalar_prefetch=2,
        grid=(nc, tiles_per_chunk),
        in_specs=[
            pl.BlockSpec((2 * chunk,), lambda c, j, te, nv: (c,), memory_space=pltpu.SMEM),
            pl.BlockSpec(memory_space=pl.ANY),
            pl.BlockSpec(memory_space=pl.ANY),
            pl.BlockSpec((1, d, f2), lambda c, j, te, nv: (te[c * tiles_per_chunk + j], 0, 0)),
            pl.BlockSpec((1, f2 // 2, d), lambda c, j, te, nv: (te[c * tiles_per_chunk + j], 0, 0)),
        ],
        out_specs=pl.BlockSpec((tmg * ROW_TILE, LANES),
                               lambda c, j, te, nv: (c * tiles_per_chunk + j, 0)),
        scratch_shapes=[
            pltpu.VMEM(((chunk + 1) * ROW_TILE, LANES), F32),
            pltpu.VMEM((tmg * ROW_TILE, LANES), F32),
            pltpu.SMEM((rows_per_chunk,), jnp.int32),
            pltpu.SemaphoreType.DMA((2,)),
            pltpu.VMEM((d, f2), BF16),
            pltpu.VMEM((f2 // 2, d), BF16),
        ],
    )
    return pl.pallas_call(
        functools.partial(_experts_kernel, tiles_per_chunk, tmg),
        grid_spec=grid_spec,
        out_shape=jax.ShapeDtypeStruct((nc * rows_per_chunk * ROW_TILE, LANES), F32),
        compiler_params=pltpu.CompilerParams(dimension_semantics=("arbitrary", "arbitrary"),
                                             vmem_limit_bytes=MOE_VMEM_LIMIT),
        name="moe_experts",
    )(tile_expert, n_valid, pos_local, fill, hs, w_in, w_out)


def _combine_kernel(pos_cur_ref, pos_nxt_ref, ys_hbm, route_ref, x_ref, o_ref, ybuf, sems):
    i = pl.program_id(0)
    n_steps = pl.num_programs(0)
    tm = x_ref.shape[0]
    slot = i % 2

    @pl.when(i == 0)
    def _():
        _gather_rows(pos_cur_ref, ys_hbm, ybuf.at[0], sems.at[0], 2 * tm)

    @pl.when(i + 1 < n_steps)
    def _():
        _gather_rows(pos_nxt_ref, ys_hbm, ybuf.at[1 - slot], sems.at[1 - slot], 2 * tm)

    _wait_rows(ys_hbm, ybuf.at[slot], sems.at[slot])
    route = route_ref[...]
    y1 = _from_token_tiles(ybuf.at[slot], 0, tm, 2 * ROW_TILE)
    y2 = _from_token_tiles(ybuf.at[slot], ROW_TILE, tm, 2 * ROW_TILE)
    o_ref[...] = x_ref[...] + route[:, 2:3] * y1 + route[:, 3:4] * y2


def combine(ys, pos, route, x, tm):
    n, d = x.shape
    nsteps = n // tm
    return pl.pallas_call(
        _combine_kernel,
        grid=(nsteps,),
        in_specs=[
            pl.BlockSpec((2 * tm,), lambda i: (i,), memory_space=pltpu.SMEM),
            pl.BlockSpec((2 * tm,), lambda i: (jnp.minimum(i + 1, nsteps - 1),),
                         memory_space=pltpu.SMEM),
            pl.BlockSpec(memory_space=pl.ANY),
            pl.BlockSpec((tm, LANES), lambda i: (i, 0)),
            pl.BlockSpec((tm, d), lambda i: (i, 0)),
        ],
        out_specs=pl.BlockSpec((tm, d), lambda i: (i, 0)),
        out_shape=jax.ShapeDtypeStruct((n, d), F32),
        scratch_shapes=[pltpu.VMEM((2, 2 * tm * ROW_TILE, LANES), F32),
                        pltpu.SemaphoreType.DMA((2,))],
        compiler_params=_params("arbitrary"),
        name="moe_combine",
    )(pos, pos, ys, route, x)


def _moe_fused_kernel(ts, st_ref, cn_ref, pos_ref, wts_ref, hs_hbm, win_ref, wout_ref, acc_hbm,
                      hsv, acc, gbuf, ystage, src_ref, sems, winb, woutb):
    c = pl.program_id(0)
    e = pl.program_id(1)
    n_slots = pos_ref.shape[0]
    chunk = n_slots // 2
    rows = chunk * ROW_TILE
    unroll = 8

    def tile(r):
        return pl.ds(pl.multiple_of(r * ROW_TILE, ROW_TILE), ROW_TILE)

    @pl.when(e == 0)
    def _():
        load = pltpu.make_async_copy(
            hs_hbm.at[pl.ds(pl.multiple_of(c * rows, ROW_TILE), rows), :],
            hsv.at[pl.ds(0, rows), :], sems.at[0])
        load.start()
        hsv[pl.ds(rows, ROW_TILE), :] = jnp.zeros((ROW_TILE, LANES), F32)

        def zero(i, carry):
            acc[pl.ds(pl.multiple_of(i * 64, 64), 64), :] = jnp.zeros((64, LANES), F32)
            return carry

        lax.fori_loop(0, (rows + ROW_TILE) // 64, zero, 0)
        acc[pl.ds(rows + ROW_TILE - 64, 64), :] = jnp.zeros((64, LANES), F32)

        def scatter(i, carry):
            for u in range(unroll):
                s = i * unroll + u
                src_ref[pos_ref[s]] = s
            return carry

        lax.fori_loop(0, n_slots // unroll, scatter, 0)
        load.wait()

    winb[...] = win_ref[0].astype(BF16)
    woutb[...] = wout_ref[0].astype(BF16)
    start = st_ref[c * N_EXPERTS + e]
    cnt = cn_ref[c * N_EXPERTS + e]

    def sub_tile(k, carry):
        base = start + k * ts
        rem = cnt - k * ts

        def slot_of(m):
            s = src_ref[jnp.minimum(base + m, n_slots - 1)]
            valid = m < rem
            return jnp.where(valid, s >> 1, chunk), jnp.where(valid, wts_ref[s], 0.0)

        def gather(i, carry2):
            for u in range(unroll):
                m = i * unroll + u
                tok, _ = slot_of(m)
                gbuf[tile(m), :] = hsv[tile(tok), :]
            return carry2

        lax.fori_loop(0, ts // unroll, gather, 0)
        x = _from_token_tiles(gbuf, 0, ts, ROW_TILE).astype(BF16)
        hid = jnp.dot(x, winb[...], preferred_element_type=F32)
        a = hid[:, :D_EXPERT]
        u_ = hid[:, D_EXPERT:]
        act = (a * _sigmoid(a) * u_).astype(BF16)
        y = jnp.dot(act, woutb[...], preferred_element_type=F32)
        for j in range(ROW_TILE):
            ystage[pl.ds(j, ts, stride=ROW_TILE), :] = y[:, j * LANES:(j + 1) * LANES]

        def scatter_add(i, carry2):
            toks, vals = [], []
            for u in range(unroll):
                m = i * unroll + u
                tok, w = slot_of(m)
                toks.append(tok)
                vals.append(acc[tile(tok), :] + w * ystage[tile(m), :])
            for tok, val in zip(toks, vals):
                acc[tile(tok), :] = val
            return carry2

        lax.fori_loop(0, ts // unroll, scatter_add, 0)
        return carry

    lax.fori_loop(0, (cnt + ts - 1) // ts, sub_tile, 0)

    @pl.when(e == pl.num_programs(1) - 1)
    def _():
        store = pltpu.make_async_copy(
            acc.at[pl.ds(0, rows), :],
            acc_hbm.at[pl.ds(pl.multiple_of(c * rows, ROW_TILE), rows), :], sems.at[1])
        store.start()
        store.wait()


def moe_fused(hs, pos_local, wts, starts, counts, w_in, w_out, chunk, ts):
    ne, d, f2 = w_in.shape
    n_rows = hs.shape[0]
    nc = n_rows // (chunk * ROW_TILE)
    grid_spec = pltpu.PrefetchScalarGridSpec(
        num_scalar_prefetch=2,
        grid=(nc, ne),
        in_specs=[
            pl.BlockSpec((2 * chunk,), lambda c, e, st, cn: (c,), memory_space=pltpu.SMEM),
            pl.BlockSpec((2 * chunk,), lambda c, e, st, cn: (c,), memory_space=pltpu.SMEM),
            pl.BlockSpec(memory_space=pl.ANY),
            pl.BlockSpec((1, d, f2), lambda c, e, st, cn: (e, 0, 0)),
            pl.BlockSpec((1, f2 // 2, d), lambda c, e, st, cn: (e, 0, 0)),
        ],
        out_specs=pl.BlockSpec(memory_space=pl.ANY),
        scratch_shapes=[
            pltpu.VMEM(((chunk + 1) * ROW_TILE, LANES), F32),
            pltpu.VMEM(((chunk + 1) * ROW_TILE, LANES), F32),
            pltpu.VMEM((ts * ROW_TILE, LANES), F32),
            pltpu.VMEM((ts * ROW_TILE, LANES), F32),
            pltpu.SMEM((2 * chunk,), jnp.int32),
            pltpu.SemaphoreType.DMA((2,)),
            pltpu.VMEM((d, f2), BF16),
            pltpu.VMEM((f2 // 2, d), BF16),
        ],
    )
    return pl.pallas_call(
        functools.partial(_moe_fused_kernel, ts),
        grid_spec=grid_spec,
        out_shape=jax.ShapeDtypeStruct((n_rows, LANES), F32),
        compiler_params=pltpu.CompilerParams(dimension_semantics=("arbitrary", "arbitrary"),
                                             vmem_limit_bytes=MOE_VMEM_LIMIT),
        name="moe_fused",
    )(starts, counts, pos_local, wts, hs, w_in, w_out)


def _finish_kernel(acc_ref, x_ref, o_ref):
    tm = x_ref.shape[0]
    o_ref[...] = x_ref[...] + _from_token_tiles(acc_ref, 0, tm, ROW_TILE)


def moe_finish(acc, x, tm):
    n, d = x.shape
    return pl.pallas_call(
        _finish_kernel,
        grid=(n // tm,),
        in_specs=[pl.BlockSpec((tm * ROW_TILE, LANES), lambda i: (i, 0)),
                  pl.BlockSpec((tm, d), lambda i: (i, 0))],
        out_specs=pl.BlockSpec((tm, d), lambda i: (i, 0)),
        out_shape=jax.ShapeDtypeStruct((n, d), F32),
        compiler_params=_params("parallel"),
        name="moe_finish",
    )(acc, x)


def moe_layer_fused(x, g, wr, br, w_in, w_out):
    n = x.shape[0]
    chunk = min(n, MOE_CHUNK)
    nc = n // chunk
    hs, route = router(x, g, wr, br, MOE_TM)
    rank, counts = plan(route, MOE_TM, chunk)
    ids = route[:, 0:2].astype(jnp.int32).reshape(nc, chunk, 2)
    cnt = counts.reshape(nc, ROW_TILE, LANES)[:, 0, :N_EXPERTS].astype(jnp.int32)
    starts = jnp.cumsum(cnt, axis=1) - cnt
    sel = ids[..., None] == jnp.arange(N_EXPERTS, dtype=jnp.int32)
    start_of = jnp.sum(jnp.where(sel, starts[:, None, None, :], 0), axis=-1)
    pos_local = start_of + rank[:, 0:2].astype(jnp.int32).reshape(nc, chunk, 2)
    acc = moe_fused(hs, pos_local.reshape(-1), route[:, 2:4].reshape(-1), starts.reshape(-1),
                    cnt.reshape(-1), w_in, w_out, chunk, MOE_TS)
    return moe_finish(acc, x, MOE_TM)


def moe_layer(x, g, wr, br, w_in, w_out, layer):
    n = x.shape[0]
    chunk = min(n, MOE_CHUNK)
    nc = n // chunk
    tpc = (2 * chunk + N_EXPERTS * (MOE_TMG - 1)) // MOE_TMG
    hs, route = router(x, g, wr, br, MOE_TM)
    rank, counts = plan(route, MOE_TM, chunk)
    ids = route[:, 0:2].astype(jnp.int32).reshape(nc, chunk, 2)
    cnt = counts.reshape(nc, ROW_TILE, LANES)[:, 0, :N_EXPERTS].astype(jnp.int32)
    padded = (cnt + MOE_TMG - 1) // MOE_TMG * MOE_TMG
    ends = jnp.cumsum(padded, axis=1)
    starts = ends - padded
    sel = ids[..., None] == jnp.arange(N_EXPERTS, dtype=jnp.int32)
    start_of = jnp.sum(jnp.where(sel, starts[:, None, None, :], 0), axis=-1)
    pos_local = start_of + rank[:, 0:2].astype(jnp.int32).reshape(nc, chunk, 2)
    tile_row = jnp.arange(tpc, dtype=jnp.int32) * MOE_TMG
    tile_expert = jnp.minimum(jnp.sum(ends[:, None, :] <= tile_row[None, :, None], axis=-1),
                              N_EXPERTS - 1).astype(jnp.int32)
    n_valid = (ends[:, -1] // MOE_TMG).astype(jnp.int32)
    ys = experts(hs, pos_local.reshape(-1), tile_expert.reshape(-1) + layer * N_EXPERTS, n_valid,
                 w_in, w_out, MOE_TMG, chunk, tpc)
    pos = pos_local + (jnp.arange(nc, dtype=jnp.int32) * (tpc * MOE_TMG))[:, None, None]
    return combine(ys, pos.reshape(-1), route, x, MOE_TM)


def _qkv_kernel(x_ref, g_ref, w_ref, qg_ref, kg_ref, seg_ref, q_ref, k_ref, v_ref):
    h = _rms(x_ref[...], g_ref[...]).astype(BF16)
    qkv = jnp.dot(h, w_ref[...], preferred_element_type=F32)
    nq = N_HEADS * HEAD_DIM
    nk = N_KV * HEAD_DIM
    q = qkv[:, :nq]
    k = qkv[:, nq:nq + nk]
    v_ref[...] = qkv[:, nq + nk:]

    def seg_mean_sq(z, seg, split):
        zz = z * z
        hi = zz.astype(BF16)
        ms = jnp.dot(hi, seg, preferred_element_type=F32)
        if split:
            lo = (zz - hi.astype(F32)).astype(BF16)
            ms = ms + jnp.dot(lo, seg, preferred_element_type=F32)
        return ms

    seg = seg_ref[...]
    qn = q * lax.rsqrt(seg_mean_sq(q, seg, False) + RMS_EPS) * qg_ref[...]
    q_ref[...] = (qn * (HEAD_DIM ** -0.5)).astype(BF16)
    kn = k * lax.rsqrt(seg_mean_sq(k, seg[:nk, :nk], True) + RMS_EPS) * kg_ref[...]
    k_ref[...] = kn


def qkv_proj(x, g, w, qg, kg, seg, tm):
    n, d = x.shape
    nq = N_HEADS * HEAD_DIM
    nk = N_KV * HEAD_DIM
    return pl.pallas_call(
        _qkv_kernel,
        grid=(n // tm,),
        in_specs=[pl.BlockSpec((tm, d), lambda i: (i, 0)), _full((1, d)),
                  _full((d, nq + 2 * nk)), _full((1, nq)), _full((1, nk)), _full((nq, nq))],
        out_specs=[pl.BlockSpec((tm, nq), lambda i: (i, 0)),
                   pl.BlockSpec((tm, nk), lambda i: (i, 0)),
                   pl.BlockSpec((tm, nk), lambda i: (i, 0))],
        out_shape=[jax.ShapeDtypeStruct((n, nq), BF16),
                   jax.ShapeDtypeStruct((n, nk), F32),
                   jax.ShapeDtypeStruct((n, nk), F32)],
        compiler_params=_params("parallel"),
        name="qkv_proj",
    )(x, g, w, qg, kg, seg)


def _attn_prompt_kernel(q_ref, kc_ref, kp_ref, vc_ref, vp_ref, tbl_ref, sink_ref, x_ref,
                        wo_ref, o_ref, cat_ref):
    n = pl.program_id(1)
    first = jnp.where(n == 0, NEG_INF, 0.0).astype(F32)
    kk = jnp.concatenate([kp_ref[...], kc_ref[...]], axis=0).astype(BF16)
    vv = jnp.concatenate([vp_ref[...], vc_ref[...]], axis=0).astype(BF16)
    col = lax.broadcasted_iota(jnp.int32, (WINDOW, 2 * WINDOW), 1)
    prev_mask = jnp.where(col < WINDOW, first, 0.0)
    for h in range(N_HEADS):
        kh = h // GROUP
        qh = q_ref[:, h * HEAD_DIM:(h + 1) * HEAD_DIM]
        kslice = kk[:, kh * HEAD_DIM:(kh + 1) * HEAD_DIM]
        vslice = vv[:, kh * HEAD_DIM:(kh + 1) * HEAD_DIM]
        s = lax.dot_general(qh, kslice, (((1,), (1,)), ((), ())), preferred_element_type=F32)
        logits = s + tbl_ref[h] + prev_mask
        sink = sink_ref[h]
        m = jnp.maximum(jnp.max(logits, axis=-1, keepdims=True), sink)
        e = jnp.exp(logits - m)
        denom = jnp.sum(e, axis=-1, keepdims=True) + jnp.exp(sink - m)
        p = e * (1.0 / denom)
        cat_ref[:, h * HEAD_DIM:(h + 1) * HEAD_DIM] = jnp.dot(
            p.astype(BF16), vslice, preferred_element_type=F32).astype(BF16)
    o_ref[...] = x_ref[...] + jnp.dot(cat_ref[...], wo_ref[...], preferred_element_type=F32)


def attn_prompt(q, k, v, tbl, sinks, x, wo, batch, seq):
    n, d = x.shape
    nb = seq // WINDOW
    nk = N_KV * HEAD_DIM
    cur = lambda b, i: (b * nb + i, 0)
    prev = lambda b, i: (b * nb + jnp.maximum(i - 1, 0), 0)
    return pl.pallas_call(
        _attn_prompt_kernel,
        grid=(batch, nb),
        in_specs=[
            pl.BlockSpec((WINDOW, d), cur),
            pl.BlockSpec((WINDOW, nk), cur), pl.BlockSpec((WINDOW, nk), prev),
            pl.BlockSpec((WINDOW, nk), cur), pl.BlockSpec((WINDOW, nk), prev),
            _full((N_HEADS, WINDOW, 2 * WINDOW)),
            pl.BlockSpec(memory_space=pltpu.SMEM),
            pl.BlockSpec((WINDOW, d), cur),
            _full((d, d)),
        ],
        out_specs=pl.BlockSpec((WINDOW, d), cur),
        out_shape=jax.ShapeDtypeStruct((n, d), F32),
        scratch_shapes=[pltpu.VMEM((WINDOW, d), BF16)],
        compiler_params=_params("parallel", "parallel"),
        name="attn_prompt",
    )(q, k, k, v, v, tbl, sinks, x, wo)


def _attn_sample_kernel(q_ref, kn_ref, vn_ref, ck_ref, cv_ref, tblc_ref, tbln_ref, sink_ref, o_ref):
    bs = q_ref.shape[0]
    nt = (((1,), (1,)), ((), ()))
    for b in range(bs):
        ck = ck_ref[b].astype(BF16)
        cv = cv_ref[b].astype(BF16)
        kn = kn_ref[b].astype(BF16)
        vn = vn_ref[b].astype(BF16)
        for kh in range(N_KV):
            lanes = slice(kh * HEAD_DIM, (kh + 1) * HEAD_DIM)
            qt = q_ref[b, kh]
            sc = lax.dot_general(qt, ck[:, lanes], nt, preferred_element_type=F32) + tblc_ref[kh]
            sn = lax.dot_general(qt, kn[:, lanes], nt, preferred_element_type=F32) + tbln_ref[kh]
            sink = sink_ref[kh]
            m = jnp.maximum(jnp.maximum(jnp.max(sc, axis=-1, keepdims=True),
                                        jnp.max(sn, axis=-1, keepdims=True)), sink)
            ec = jnp.exp(sc - m)
            en = jnp.exp(sn - m)
            denom = (jnp.sum(ec, axis=-1, keepdims=True) + jnp.sum(en, axis=-1, keepdims=True)
                     + jnp.exp(sink - m))
            out = (jnp.dot(ec.astype(BF16), cv[:, lanes], preferred_element_type=F32)
                   + jnp.dot(en.astype(BF16), vn[:, lanes], preferred_element_type=F32))
            o_ref[b, kh] = (out * (1.0 / denom)).astype(BF16)


def attn_sample(q4, kn, vn, ck, cv, tbl, sink, bs):
    nb = q4.shape[0]
    ts = kn.shape[1]
    nk = N_KV * HEAD_DIM
    tg = ts * GROUP
    kt = WINDOW + ts
    return pl.pallas_call(
        _attn_sample_kernel,
        grid=(nb // bs,),
        in_specs=[
            pl.BlockSpec((bs, N_KV, tg, HEAD_DIM), lambda i: (i, 0, 0, 0)),
            pl.BlockSpec((bs, ts, nk), lambda i: (i, 0, 0)),
            pl.BlockSpec((bs, ts, nk), lambda i: (i, 0, 0)),
            pl.BlockSpec((bs, WINDOW, nk), lambda i: (i, 0, 0)),
            pl.BlockSpec((bs, WINDOW, nk), lambda i: (i, 0, 0)),
            _full((N_KV, tg, WINDOW)),
            _full((N_KV, tg, ts)),
            _full((N_KV, tg, 1)),
        ],
        out_specs=pl.BlockSpec((bs, N_KV, tg, HEAD_DIM), lambda i: (i, 0, 0, 0)),
        out_shape=jax.ShapeDtypeStruct((nb, N_KV, tg, HEAD_DIM), BF16),
        compiler_params=_params("parallel"),
        name="attn_sample",
    )(q4, kn, vn, ck, cv, tbl[:, :, :WINDOW], tbl[:, :, WINDOW:], sink)


def _proj_res_kernel(a_ref, w_ref, x_ref, o_ref):
    o_ref[...] = x_ref[...] + jnp.dot(a_ref[...], w_ref[...], preferred_element_type=F32)


def proj_residual(a, w, x, tm):
    n, d = x.shape
    kdim = a.shape[1]
    return pl.pallas_call(
        _proj_res_kernel,
        grid=(n // tm,),
        in_specs=[pl.BlockSpec((tm, kdim), lambda i: (i, 0)), _full((kdim, d)),
                  pl.BlockSpec((tm, d), lambda i: (i, 0))],
        out_specs=pl.BlockSpec((tm, d), lambda i: (i, 0)),
        out_shape=jax.ShapeDtypeStruct((n, d), F32),
        compiler_params=_params("parallel"),
        name="proj_residual",
    )(a, w, x)


def _t5_bucket_np(dist):
    n = np.maximum(dist, 0)
    max_exact = N_BUCKETS // 2
    large = max_exact + (np.log(np.maximum(n, 1).astype(np.float32) / max_exact)
                         / math.log(MAX_DISTANCE / max_exact) * (N_BUCKETS - max_exact)).astype(np.int32)
    large = np.minimum(large, N_BUCKETS - 1)
    return np.where(n < max_exact, n, large)


def _bias_table(rel_bias, dist):
    valid = (dist >= 0) & (dist <= WINDOW)
    onehot = (np.asarray(_t5_bucket_np(dist))[..., None] == np.arange(N_BUCKETS)).astype(np.float32)
    b = jnp.einsum("qkb,bh->hqk", jnp.asarray(onehot), rel_bias.astype(F32),
                   precision=lax.Precision.HIGHEST)
    return jnp.where(jnp.asarray(valid)[None], b, NEG_INF)


def kernel(x_prompt, x_sample, state_conv, cache_swa_k, cache_swa_v, rms_mix_g, rms_ffn_g, conv_w_in, conv_dw_w, conv_dw_b, conv_ln_g, conv_ln_b, conv_w_out, attn_w_qkv, attn_q_norm_g, attn_k_norm_g, attn_sinks, attn_w_o, rel_bias, router_group_w, router_group_b, router_expert_w, router_expert_b, expert_w_in, expert_w_out):
    batch, seq, d = x_prompt.shape
    nsb, ts, _ = x_sample.shape
    xp = x_prompt.reshape(batch * seq, d)
    xs = x_sample.reshape(nsb * ts, d)
    row = lambda a: a.reshape(1, -1).astype(F32)

    def router_w(i):
        we = jnp.transpose(router_expert_w[i], (1, 0, 2)).reshape(d, N_EXPERTS)
        wr = jnp.concatenate([we, router_group_w[i]], axis=1)
        wr = jnp.pad(wr, ((0, 0), (0, LANES - wr.shape[1])))
        br = jnp.concatenate([router_expert_b[i].reshape(-1), router_group_b[i]])
        br = jnp.pad(br, (0, LANES - br.shape[0])).reshape(1, LANES)
        return wr.astype(F32), br.astype(F32)

    g0 = row(rms_mix_g[0])
    w_in = conv_w_in[0].astype(BF16)
    dww = jnp.pad(conv_dw_w[0].astype(F32), ((0, HALO - CONV_WIDTH), (0, 0)))
    dwb, lng, lnb = row(conv_dw_b[0]), row(conv_ln_g[0]), row(conv_ln_b[0])
    w_out = conv_w_out[0].astype(BF16)
    up = glu_proj(xp, g0, w_in, 512)
    us = glu_proj(xs, g0, w_in, 512)
    xp = conv_prompt(up, xp, dww, dwb, lng, lnb, w_out, batch, seq, 256)
    us3 = us.reshape(nsb, ts, -1)
    xs = conv_sample(us3, state_conv[0], xs, dww, dwb, lng, lnb, w_out, 32)
    conv_p = up.reshape(batch, seq, -1)[:, seq - PAST:]
    conv_s = jnp.concatenate([state_conv[0], us3], axis=1)[:, ts:]

    wr0, br0 = router_w(0)
    ew_in = expert_w_in.reshape((DEPTH * N_EXPERTS,) + expert_w_in.shape[2:])
    ew_out = expert_w_out.reshape((DEPTH * N_EXPERTS,) + expert_w_out.shape[2:])
    xp = moe_layer(xp, row(rms_ffn_g[0]), wr0, br0, ew_in, ew_out, 0)
    xs = moe_layer(xs, row(rms_ffn_g[0]), wr0, br0, ew_in, ew_out, 0)

    g1 = row(rms_mix_g[1])
    w_qkv = attn_w_qkv[0].astype(BF16)
    qg = jnp.tile(attn_q_norm_g[0].astype(F32), N_HEADS).reshape(1, -1)
    kg = jnp.tile(attn_k_norm_g[0].astype(F32), N_KV).reshape(1, -1)
    nq = N_HEADS * HEAD_DIM
    seg = jnp.asarray(np.kron(np.eye(N_HEADS), np.ones((HEAD_DIM, HEAD_DIM))) / HEAD_DIM, BF16)
    w_o = attn_w_o[0].astype(BF16)
    sinks = attn_sinks[0].astype(F32)

    qp, kp, vp = qkv_proj(xp, g1, w_qkv, qg, kg, seg, 512)
    qs, ks, vs = qkv_proj(xs, g1, w_qkv, qg, kg, seg, 512)

    q_off = np.arange(WINDOW)[:, None]
    dist_p = q_off + WINDOW - np.arange(2 * WINDOW)[None, :]
    tbl_p = _bias_table(rel_bias, dist_p)
    xp = attn_prompt(qp, kp, vp, tbl_p, sinks, xp, w_o, batch, seq)

    kt = WINDOW + ts
    dist_s = np.arange(ts)[:, None] + WINDOW - np.arange(kt)[None, :]
    tbl_s = _bias_table(rel_bias, dist_s)
    tbl_s = jnp.transpose(tbl_s.reshape(N_KV, GROUP, ts, kt), (0, 2, 1, 3)).reshape(N_KV, ts * GROUP, kt)
    sink_s = jnp.tile(sinks.reshape(N_KV, 1, GROUP), (1, ts, 1)).reshape(N_KV, ts * GROUP, 1)
    q4 = jnp.transpose(qs.reshape(nsb, ts, N_KV, GROUP, HEAD_DIM), (0, 2, 1, 3, 4))
    q4 = q4.reshape(nsb, N_KV, ts * GROUP, HEAD_DIM)
    nk = N_KV * HEAD_DIM
    ks3, vs3 = ks.reshape(nsb, ts, nk), vs.reshape(nsb, ts, nk)
    ck = cache_swa_k[0].reshape(nsb, WINDOW, nk)
    cv = cache_swa_v[0].reshape(nsb, WINDOW, nk)
    o4 = attn_sample(q4, ks3, vs3, ck, cv, tbl_s, sink_s, 16)
    os_ = jnp.transpose(o4.reshape(nsb, N_KV, ts, GROUP, HEAD_DIM), (0, 2, 1, 3, 4)).reshape(nsb * ts, nq)
    xs = proj_residual(os_, w_o, xs, 512)

    k_p = kp.reshape(batch, seq, N_KV, HEAD_DIM)[:, seq - WINDOW:]
    v_p = vp.reshape(batch, seq, N_KV, HEAD_DIM)[:, seq - WINDOW:]
    k_s = jnp.concatenate([cache_swa_k[0], ks.reshape(nsb, ts, N_KV, HEAD_DIM)], axis=1)[:, ts:]
    v_s = jnp.concatenate([cache_swa_v[0], vs.reshape(nsb, ts, N_KV, HEAD_DIM)], axis=1)[:, ts:]

    wr1, br1 = router_w(1)
    xp = moe_layer(xp, row(rms_ffn_g[1]), wr1, br1, ew_in, ew_out, 1)
    xs = moe_layer(xs, row(rms_ffn_g[1]), wr1, br1, ew_in, ew_out, 1)

    return (xp.reshape(batch, seq, d), xs.reshape(nsb, ts, d),
            conv_p[None], conv_s[None], k_p[None], v_p[None], k_s[None], v_s[None])
```

```python
import functools
import math

import numpy as np
import jax
import jax.numpy as jnp
from jax import lax
from jax.experimental import pallas as pl
from jax.experimental.pallas import tpu as pltpu

D_MODEL = 1024
DEPTH = 2
CONV_WIDTH = 31
PAST = CONV_WIDTH - 1
HEAD_DIM = 64
N_HEADS = 16
N_KV = 2
GROUP = 8
WINDOW = 128
N_BUCKETS = 32
MAX_DISTANCE = 128
N_GROUPS = 4
EPG = 8
N_EXPERTS = 32
D_EXPERT = 256
RMS_EPS = 1e-6
LN_EPS = 1e-5
NEG_INF = -1e30

F32 = jnp.float32
BF16 = jnp.bfloat16
LANES = 128
ROW_TILE = 8
MOE_TM = 256
MOE_TMG = 256
MOE_CHUNK = 8192
MOE_TS = 128
MOE_VMEM_LIMIT = 56 * 1024 * 1024
VMEM_LIMIT = 48 * 1024 * 1024


def _params(*sem):
    return pltpu.CompilerParams(dimension_semantics=sem, vmem_limit_bytes=VMEM_LIMIT)


def _rms(x, g):
    return x * lax.rsqrt(jnp.mean(x * x, axis=-1, keepdims=True) + RMS_EPS) * g


def _sigmoid(x):
    return 1.0 / (1.0 + jnp.exp(-x))


def _full(shape):
    return pl.BlockSpec(shape, lambda *_: (0,) * len(shape))


def _glu_kernel(x_ref, g_ref, wa_ref, wg_ref, u_ref):
    h = _rms(x_ref[...], g_ref[...]).astype(BF16)
    a = jnp.dot(h, wa_ref[...], preferred_element_type=F32)
    gate = jnp.dot(h, wg_ref[...], preferred_element_type=F32)
    u_ref[...] = a * _sigmoid(gate)


def glu_proj(x, g, w_in, tm):
    n, d = x.shape
    c = w_in.shape[1] // 2
    return pl.pallas_call(
        _glu_kernel,
        grid=(n // tm,),
        in_specs=[
            pl.BlockSpec((tm, d), lambda i: (i, 0)),
            _full((1, d)),
            pl.BlockSpec((d, c), lambda i: (0, 0)),
            pl.BlockSpec((d, c), lambda i: (0, 1)),
        ],
        out_specs=pl.BlockSpec((tm, c), lambda i: (i, 0)),
        out_shape=jax.ShapeDtypeStruct((n, c), F32),
        compiler_params=_params("parallel"),
        name="glu_proj",
    )(x, g, w_in, w_in)


def _ln_silu_out(y, lng, lnb, wout_ref, x):
    mu = jnp.mean(y, axis=-1, keepdims=True)
    yc = y - mu
    z = yc * lax.rsqrt(jnp.mean(yc * yc, axis=-1, keepdims=True) + LN_EPS) * lng + lnb
    z = z * _sigmoid(z)
    return x + jnp.dot(z.astype(BF16), wout_ref[...], preferred_element_type=F32)


HALO = 32
CONV_RC = 64
CONV_CC = 128


def _conv_prompt_kernel(ucur_ref, uprev_ref, x_ref, dww_ref, dwb_ref, lng_ref, lnb_ref,
                        wout_ref, o_ref, up_ref, y_ref):
    t = pl.program_id(1)
    tt, c = ucur_ref.shape
    keep = (t > 0).astype(F32)
    up_ref[0:HALO, :] = uprev_ref[...] * keep
    up_ref[HALO:HALO + tt, :] = ucur_ref[...]
    up_ref[HALO + tt:, :] = jnp.zeros((ROW_TILE, c), F32)
    off = HALO - PAST
    for r0 in range(0, tt, CONV_RC):
        for c0 in range(0, c, CONV_CC):
            y = jnp.zeros((CONV_RC, CONV_CC), F32) + dwb_ref[:, c0:c0 + CONV_CC]
            for s in range(ROW_TILE):
                v = None
                for q in range((off + CONV_WIDTH - 1) // ROW_TILE + 1):
                    k = ROW_TILE * q + s - off
                    if k < 0 or k >= CONV_WIDTH:
                        continue
                    lo = r0 + ROW_TILE * q
                    term = (up_ref[lo:lo + CONV_RC + ROW_TILE, c0:c0 + CONV_CC]
                            * dww_ref[k:k + 1, c0:c0 + CONV_CC])
                    v = term if v is None else v + term
                y = y + v[s:s + CONV_RC]
            y_ref[r0:r0 + CONV_RC, c0:c0 + CONV_CC] = y
    o_ref[...] = _ln_silu_out(y_ref[...], lng_ref[...], lnb_ref[...], wout_ref, x_ref[...])


def conv_prompt(u, x, dww, dwb, lng, lnb, wout, batch, seq, tt):
    n, c = u.shape
    d = x.shape[1]
    nt = seq // tt
    hb = tt // HALO
    return pl.pallas_call(
        _conv_prompt_kernel,
        grid=(batch, nt),
        in_specs=[
            pl.BlockSpec((tt, c), lambda b, t: (b * nt + t, 0)),
            pl.BlockSpec((HALO, c), lambda b, t: (jnp.maximum((b * nt + t) * hb - 1, 0), 0)),
            pl.BlockSpec((tt, d), lambda b, t: (b * nt + t, 0)),
            _full((HALO, c)), _full((1, c)), _full((1, c)), _full((1, c)),
            _full((c, d)),
        ],
        out_specs=pl.BlockSpec((tt, d), lambda b, t: (b * nt + t, 0)),
        out_shape=jax.ShapeDtypeStruct((n, d), F32),
        scratch_shapes=[pltpu.VMEM((tt + HALO + ROW_TILE, c), F32), pltpu.VMEM((tt, c), F32)],
        compiler_params=_params("parallel", "parallel"),
        name="conv_prompt",
    )(u, u, x, dww, dwb, lng, lnb, wout)


CONV_SB = 4


def _conv_sample_kernel(u_ref, st_ref, x_ref, dww_ref, dwb_ref, lng_ref, lnb_ref,
                        wout_ref, o_ref, up_ref, y_ref):
    bs, ts, c = u_ref.shape
    up_ref[:, 0:PAST, :] = st_ref[...]
    up_ref[:, PAST:PAST + ts, :] = u_ref[...]
    for b0 in range(0, bs, CONV_SB):
        acc = jnp.zeros((CONV_SB, ts, c), F32) + dwb_ref[...][None]
        for k in range(CONV_WIDTH):
            acc = acc + up_ref[b0:b0 + CONV_SB, k:k + ts, :] * dww_ref[k:k + 1, :][None]
        y_ref[b0 * ts:(b0 + CONV_SB) * ts, :] = acc.reshape(CONV_SB * ts, c)
    o_ref[...] = _ln_silu_out(y_ref[...], lng_ref[...], lnb_ref[...], wout_ref, x_ref[...])


def conv_sample(u3, state, x, dww, dwb, lng, lnb, wout, bs):
    nb, ts, c = u3.shape
    d = x.shape[1]
    return pl.pallas_call(
        _conv_sample_kernel,
        grid=(nb // bs,),
        in_specs=[
            pl.BlockSpec((bs, ts, c), lambda i: (i, 0, 0)),
            pl.BlockSpec((bs, PAST, c), lambda i: (i, 0, 0)),
            pl.BlockSpec((bs * ts, d), lambda i: (i, 0)),
            _full((HALO, c)), _full((1, c)), _full((1, c)), _full((1, c)),
            _full((c, d)),
        ],
        out_specs=pl.BlockSpec((bs * ts, d), lambda i: (i, 0)),
        out_shape=jax.ShapeDtypeStruct((nb * ts, d), F32),
        scratch_shapes=[pltpu.VMEM((bs, PAST + ts, c), F32), pltpu.VMEM((bs * ts, c), F32)],
        compiler_params=_params("parallel"),
        name="conv_sample",
    )(u3, state, x, dww, dwb, lng, lnb, wout)


def _router_kernel(tiles_per_chunk, x_ref, g_ref, wr_ref, br_ref, tri_ref, hs_ref, route_ref, cnt_ref,
                   carry_ref):
    h = _rms(x_ref[...], g_ref[...])
    tm, d = h.shape
    for j in range(d // LANES):
        hs_ref[pl.ds(j, tm, stride=ROW_TILE), :] = h[:, j * LANES:(j + 1) * LANES]
    logits = jnp.dot(h, wr_ref[...], preferred_element_type=F32,
                     precision=lax.Precision.HIGHEST) + br_ref[...]
    lane = lax.broadcasted_iota(jnp.int32, logits.shape, 1).astype(F32)
    big = jnp.float32(LANES)
    is_g = (lane >= N_EXPERTS) & (lane < N_EXPERTS + N_GROUPS)
    gl = jnp.where(is_g, logits, NEG_INF)
    gm = jnp.max(gl, axis=-1, keepdims=True)
    g_sel = jnp.min(jnp.where(gl == gm, lane, big), axis=-1, keepdims=True) - N_EXPERTS
    gate_g = 1.0 / jnp.sum(jnp.where(is_g, jnp.exp(gl - gm), 0.0), axis=-1, keepdims=True)
    lo = g_sel * EPG
    in_grp = (lane >= lo) & (lane < lo + EPG)
    el = jnp.where(in_grp, logits, NEG_INF)
    v1 = jnp.max(el, axis=-1, keepdims=True)
    i1 = jnp.min(jnp.where(el == v1, lane, big), axis=-1, keepdims=True)
    el2 = jnp.where(lane == i1, NEG_INF, el)
    v2 = jnp.max(el2, axis=-1, keepdims=True)
    i2 = jnp.min(jnp.where(el2 == v2, lane, big), axis=-1, keepdims=True)
    e2 = jnp.exp(v2 - v1)
    w1 = gate_g / (1.0 + e2)
    w2 = gate_g * e2 / (1.0 + e2)
    @pl.when(pl.program_id(0) % tiles_per_chunk == 0)
    def _():
        carry_ref[...] = jnp.zeros_like(carry_ref)

    hit1 = lane == i1
    hit2 = lane == i2
    onehot = jnp.where(hit1 | hit2, 1.0, 0.0)
    before = carry_ref[...] + jnp.dot(tri_ref[...], onehot.astype(BF16), preferred_element_type=F32)
    r1 = jnp.sum(jnp.where(hit1, before, 0.0), axis=-1, keepdims=True)
    r2 = jnp.sum(jnp.where(hit2, before, 0.0), axis=-1, keepdims=True)
    carry_ref[...] += jnp.sum(onehot, axis=0, keepdims=True)
    cnt_ref[...] = jnp.broadcast_to(carry_ref[...], cnt_ref.shape)
    route_ref[...] = jnp.where(lane == 0.0, i1, jnp.where(lane == 1.0, i2, jnp.where(
        lane == 2.0, w1, jnp.where(lane == 3.0, w2, jnp.where(
            lane == 4.0, r1, jnp.where(lane == 5.0, r2, 0.0))))))


def router(x, g, wr, br, tm, chunk):
    n, d = x.shape
    tpc = chunk // tm
    tri = jnp.asarray(np.tril(np.ones((tm, tm), np.float32), -1), BF16)
    return pl.pallas_call(
        functools.partial(_router_kernel, tpc),
        grid=(n // tm,),
        in_specs=[pl.BlockSpec((tm, d), lambda i: (i, 0)), _full((1, d)),
                  _full((d, LANES)), _full((1, LANES)), _full((tm, tm))],
        out_specs=[pl.BlockSpec((tm * ROW_TILE, LANES), lambda i: (i, 0)),
                   pl.BlockSpec((tm, LANES), lambda i: (i, 0)),
                   pl.BlockSpec((ROW_TILE, LANES), lambda i: (i // tpc, 0))],
        out_shape=[jax.ShapeDtypeStruct((n * ROW_TILE, LANES), F32),
                   jax.ShapeDtypeStruct((n, LANES), F32),
                   jax.ShapeDtypeStruct((n // chunk * ROW_TILE, LANES), F32)],
        scratch_shapes=[pltpu.VMEM((1, LANES), F32)],
        compiler_params=_params("arbitrary"),
        name="router",
    )(x, g, wr, br, tri)


def _plan_kernel(tiles_per_chunk, route_ref, tri_ref, rank_ref, cnt_ref, carry_ref):
    i = pl.program_id(0)

    @pl.when(i % tiles_per_chunk == 0)
    def _():
        carry_ref[...] = jnp.zeros_like(carry_ref)

    route = route_ref[...]
    lane = lax.broadcasted_iota(jnp.int32, route.shape, 1).astype(F32)
    i1 = route[:, 0:1]
    i2 = route[:, 1:2]
    hit1 = lane == i1
    hit2 = lane == i2
    onehot = jnp.where(hit1 | hit2, 1.0, 0.0)
    before = carry_ref[...] + jnp.dot(tri_ref[...], onehot.astype(BF16), preferred_element_type=F32)
    r1 = jnp.sum(jnp.where(hit1, before, 0.0), axis=-1, keepdims=True)
    r2 = jnp.sum(jnp.where(hit2, before, 0.0), axis=-1, keepdims=True)
    rank_ref[...] = jnp.where(lane == 0.0, r1, jnp.where(lane == 1.0, r2, 0.0))
    carry_ref[...] += jnp.sum(onehot, axis=0, keepdims=True)
    cnt_ref[...] = jnp.broadcast_to(carry_ref[...], cnt_ref.shape)


def plan(route, tm, chunk):
    n = route.shape[0]
    tpc = chunk // tm
    tri = jnp.asarray(np.tril(np.ones((tm, tm), np.float32), -1), BF16)
    return pl.pallas_call(
        functools.partial(_plan_kernel, tpc),
        grid=(n // tm,),
        in_specs=[pl.BlockSpec((tm, LANES), lambda i: (i, 0)), _full((tm, tm))],
        out_specs=[pl.BlockSpec((tm, LANES), lambda i: (i, 0)),
                   pl.BlockSpec((ROW_TILE, LANES), lambda i: (i // tpc, 0))],
        out_shape=[jax.ShapeDtypeStruct((n, LANES), F32),
                   jax.ShapeDtypeStruct((n // chunk * ROW_TILE, LANES), F32)],
        scratch_shapes=[pltpu.VMEM((1, LANES), F32)],
        compiler_params=_params("arbitrary"),
        name="moe_plan",
    )(route, tri)


def _gather_rows(idx_ref, src_hbm, dst, sem, n_rows):
    unroll = 8

    def body(c, carry):
        for u in range(unroll):
            m = c * unroll + u
            tok = idx_ref[m]
            pltpu.make_async_copy(
                src_hbm.at[pl.ds(pl.multiple_of(tok * ROW_TILE, ROW_TILE), ROW_TILE), :],
                dst.at[pl.ds(pl.multiple_of(m * ROW_TILE, ROW_TILE), ROW_TILE), :],
                sem).start()
        return carry

    lax.fori_loop(0, n_rows // unroll, body, 0)


def _wait_rows(src_hbm, dst, sem):
    pltpu.make_async_copy(src_hbm.at[pl.ds(0, dst.shape[0]), :], dst, sem).wait()


def _from_token_tiles(buf, start, n_rows, stride):
    return jnp.concatenate(
        [buf[pl.ds(start + j, n_rows, stride=stride), :] for j in range(ROW_TILE)], axis=-1)


def _experts_kernel(tiles_per_chunk, tmg, te_ref, nv_ref, pos_ref, fill_hbm, hs_hbm,
                    win_ref, wout_ref, ys_ref, hsv, gbuf_a, gbuf_b, src_ref, sems, winb, woutb):
    c = pl.program_id(0)
    j = pl.program_id(1)
    t = c * tiles_per_chunk + j
    nv = nv_ref[c]
    chunk = pos_ref.shape[0] // 2
    unroll = 8

    def tile(r):
        return pl.ds(pl.multiple_of(r * ROW_TILE, ROW_TILE), ROW_TILE)

    @pl.when(j == 0)
    def _():
        load = pltpu.make_async_copy(
            hs_hbm.at[pl.ds(pl.multiple_of(c * chunk * ROW_TILE, ROW_TILE), chunk * ROW_TILE), :],
            hsv.at[pl.ds(0, chunk * ROW_TILE), :], sems.at[0])
        load.start()
        fill = pltpu.make_async_copy(fill_hbm, src_ref, sems.at[1])
        fill.start()
        hsv[pl.ds(chunk * ROW_TILE, ROW_TILE), :] = jnp.zeros((ROW_TILE, LANES), F32)
        fill.wait()

        def scatter(i, carry):
            for u in range(unroll):
                tok = i * unroll + u
                src_ref[pos_ref[2 * tok]] = tok
                src_ref[pos_ref[2 * tok + 1]] = tok
            return carry

        lax.fori_loop(0, chunk // unroll, scatter, 0)
        load.wait()

        def gather(i, carry):
            for u in range(unroll):
                m = i * unroll + u
                gbuf_a[tile(m), :] = hsv[tile(src_ref[m]), :]
            return carry

        lax.fori_loop(0, tmg // unroll, gather, 0)

    changed = jnp.logical_or(j == 0, te_ref[t] != te_ref[jnp.maximum(t - 1, 0)])

    @pl.when(jnp.logical_and(changed, j < nv))
    def _():
        winb[...] = win_ref[0].astype(BF16)
        woutb[...] = wout_ref[0].astype(BF16)

    def step(cur, nxt):
        base = jnp.minimum(j + 1, tiles_per_chunk - 1) * tmg
        for m in range(tmg):
            nxt[pl.ds(m * ROW_TILE, ROW_TILE), :] = hsv[tile(src_ref[base + m]), :]
        x = _from_token_tiles(cur, 0, tmg, ROW_TILE).astype(BF16)
        hid = jnp.dot(x, winb[...], preferred_element_type=F32)
        a = hid[:, :D_EXPERT]
        u = hid[:, D_EXPERT:]
        act = (a * _sigmoid(a) * u).astype(BF16)
        y = jnp.dot(act, woutb[...], preferred_element_type=F32)
        for k in range(ROW_TILE):
            ys_ref[pl.ds(k, tmg, stride=ROW_TILE), :] = y[:, k * LANES:(k + 1) * LANES]

    @pl.when(jnp.logical_and(j < nv, j % 2 == 0))
    def _():
        step(gbuf_a, gbuf_b)

    @pl.when(jnp.logical_and(j < nv, j % 2 == 1))
    def _():
        step(gbuf_b, gbuf_a)

    @pl.when(j >= nv)
    def _():
        ys_ref[...] = jnp.zeros_like(ys_ref)


def experts(hs, pos_local, tile_expert, n_valid, w_in, w_out, tmg, chunk, tiles_per_chunk):
    _, d, f2 = w_in.shape
    nc = hs.shape[0] // (chunk * ROW_TILE)
    rows_per_chunk = tiles_per_chunk * tmg
    fill = jnp.full((rows_per_chunk,), chunk, jnp.int32)
    grid_spec = pltpu.PrefetchScalarGridSpec(
        num_scalar_prefetch=2,
        grid=(nc, tiles_per_chunk),
        in_specs=[
            pl.BlockSpec((2 * chunk,), lambda c, j, te, nv: (c,), memory_space=pltpu.SMEM),
            pl.BlockSpec(memory_space=pl.ANY),
            pl.BlockSpec(memory_space=pl.ANY),
            pl.BlockSpec((1, d, f2), lambda c, j, te, nv: (te[c * tiles_per_chunk + j], 0, 0)),
            pl.BlockSpec((1, f2 // 2, d), lambda c, j, te, nv: (te[c * tiles_per_chunk + j], 0, 0)),
        ],
        out_specs=pl.BlockSpec((tmg * ROW_TILE, LANES),
                               lambda c, j, te, nv: (c * tiles_per_chunk + j, 0)),
        scratch_shapes=[
            pltpu.VMEM(((chunk + 1) * ROW_TILE, LANES), F32),
            pltpu.VMEM((tmg * ROW_TILE, LANES), F32),
            pltpu.VMEM((tmg * ROW_TILE, LANES), F32),
            pltpu.SMEM((rows_per_chunk,), jnp.int32),
            pltpu.SemaphoreType.DMA((2,)),
            pltpu.VMEM((d, f2), BF16),
            pltpu.VMEM((f2 // 2, d), BF16),
        ],
    )
    return pl.pallas_call(
        functools.partial(_experts_kernel, tiles_per_chunk, tmg),
        grid_spec=grid_spec,
        out_shape=jax.ShapeDtypeStruct((nc * rows_per_chunk * ROW_TILE, LANES), F32),
        compiler_params=pltpu.CompilerParams(dimension_semantics=("arbitrary", "arbitrary"),
                                             vmem_limit_bytes=MOE_VMEM_LIMIT),
        name="moe_experts",
    )(tile_expert, n_valid, pos_local, fill, hs, w_in, w_out)


def _combine_kernel(pos_cur_ref, pos_nxt_ref, ys_hbm, route_ref, x_ref, o_ref, ybuf_a, ybuf_b, sems):
    i = pl.program_id(0)
    last = i == pl.num_programs(0) - 1
    tm = x_ref.shape[0]

    @pl.when(i == 0)
    def _():
        _gather_rows(pos_cur_ref, ys_hbm, ybuf_a, sems.at[0], 2 * tm)

    def step(cur, cur_sem, nxt, nxt_sem):
        _wait_rows(ys_hbm, cur, cur_sem)
        for m in range(2 * tm):
            pltpu.make_async_copy(
                ys_hbm.at[pl.ds(pl.multiple_of(pos_nxt_ref[m] * ROW_TILE, ROW_TILE), ROW_TILE), :],
                nxt.at[pl.ds(m * ROW_TILE, ROW_TILE), :], nxt_sem).start()
        route = route_ref[...]
        y1 = _from_token_tiles(cur, 0, tm, 2 * ROW_TILE)
        y2 = _from_token_tiles(cur, ROW_TILE, tm, 2 * ROW_TILE)
        o_ref[...] = x_ref[...] + route[:, 2:3] * y1 + route[:, 3:4] * y2

        @pl.when(last)
        def _():
            _wait_rows(ys_hbm, nxt, nxt_sem)

    @pl.when(i % 2 == 0)
    def _():
        step(ybuf_a, sems.at[0], ybuf_b, sems.at[1])

    @pl.when(i % 2 == 1)
    def _():
        step(ybuf_b, sems.at[1], ybuf_a, sems.at[0])


def combine(ys, pos, route, x, tm):
    n, d = x.shape
    nsteps = n // tm
    return pl.pallas_call(
        _combine_kernel,
        grid=(nsteps,),
        in_specs=[
            pl.BlockSpec((2 * tm,), lambda i: (i,), memory_space=pltpu.SMEM),
            pl.BlockSpec((2 * tm,), lambda i: (jnp.minimum(i + 1, nsteps - 1),),
                         memory_space=pltpu.SMEM),
            pl.BlockSpec(memory_space=pl.ANY),
            pl.BlockSpec((tm, LANES), lambda i: (i, 0)),
            pl.BlockSpec((tm, d), lambda i: (i, 0)),
        ],
        out_specs=pl.BlockSpec((tm, d), lambda i: (i, 0)),
        out_shape=jax.ShapeDtypeStruct((n, d), F32),
        scratch_shapes=[pltpu.VMEM((2 * tm * ROW_TILE, LANES), F32),
                        pltpu.VMEM((2 * tm * ROW_TILE, LANES), F32),
                        pltpu.SemaphoreType.DMA((2,))],
        compiler_params=_params("arbitrary"),
        name="moe_combine",
    )(pos, pos, ys, route, x)


def _moe_fused_kernel(ts, st_ref, cn_ref, pos_ref, wts_ref, hs_hbm, win_ref, wout_ref, acc_hbm,
                      hsv, acc, gbuf, ystage, src_ref, sems, winb, woutb):
    c = pl.program_id(0)
    e = pl.program_id(1)
    n_slots = pos_ref.shape[0]
    chunk = n_slots // 2
    rows = chunk * ROW_TILE
    unroll = 8

    def tile(r):
        return pl.ds(pl.multiple_of(r * ROW_TILE, ROW_TILE), ROW_TILE)

    @pl.when(e == 0)
    def _():
        load = pltpu.make_async_copy(
            hs_hbm.at[pl.ds(pl.multiple_of(c * rows, ROW_TILE), rows), :],
            hsv.at[pl.ds(0, rows), :], sems.at[0])
        load.start()
        hsv[pl.ds(rows, ROW_TILE), :] = jnp.zeros((ROW_TILE, LANES), F32)

        def zero(i, carry):
            acc[pl.ds(pl.multiple_of(i * 64, 64), 64), :] = jnp.zeros((64, LANES), F32)
            return carry

        lax.fori_loop(0, (rows + ROW_TILE) // 64, zero, 0)
        acc[pl.ds(rows + ROW_TILE - 64, 64), :] = jnp.zeros((64, LANES), F32)

        def scatter(i, carry):
            for u in range(unroll):
                s = i * unroll + u
                src_ref[pos_ref[s]] = s
            return carry

        lax.fori_loop(0, n_slots // unroll, scatter, 0)
        load.wait()

    winb[...] = win_ref[0].astype(BF16)
    woutb[...] = wout_ref[0].astype(BF16)
    start = st_ref[c * N_EXPERTS + e]
    cnt = cn_ref[c * N_EXPERTS + e]

    def sub_tile(k, carry):
        base = start + k * ts
        rem = cnt - k * ts

        def slot_of(m):
            s = src_ref[jnp.minimum(base + m, n_slots - 1)]
            valid = m < rem
            return jnp.where(valid, s >> 1, chunk), jnp.where(valid, wts_ref[s], 0.0)

        def gather(i, carry2):
            for u in range(unroll):
                m = i * unroll + u
                tok, _ = slot_of(m)
                gbuf[tile(m), :] = hsv[tile(tok), :]
            return carry2

        lax.fori_loop(0, ts // unroll, gather, 0)
        x = _from_token_tiles(gbuf, 0, ts, ROW_TILE).astype(BF16)
        hid = jnp.dot(x, winb[...], preferred_element_type=F32)
        a = hid[:, :D_EXPERT]
        u_ = hid[:, D_EXPERT:]
        act = (a * _sigmoid(a) * u_).astype(BF16)
        y = jnp.dot(act, woutb[...], preferred_element_type=F32)
        for j in range(ROW_TILE):
            ystage[pl.ds(j, ts, stride=ROW_TILE), :] = y[:, j * LANES:(j + 1) * LANES]

        def scatter_add(i, carry2):
            toks, vals = [], []
            for u in range(unroll):
                m = i * unroll + u
                tok, w = slot_of(m)
                toks.append(tok)
                vals.append(acc[tile(tok), :] + w * ystage[tile(m), :])
            for tok, val in zip(toks, vals):
                acc[tile(tok), :] = val
            return carry2

        lax.fori_loop(0, ts // unroll, scatter_add, 0)
        return carry

    lax.fori_loop(0, (cnt + ts - 1) // ts, sub_tile, 0)

    @pl.when(e == pl.num_programs(1) - 1)
    def _():
        store = pltpu.make_async_copy(
            acc.at[pl.ds(0, rows), :],
            acc_hbm.at[pl.ds(pl.multiple_of(c * rows, ROW_TILE), rows), :], sems.at[1])
        store.start()
        store.wait()


def moe_fused(hs, pos_local, wts, starts, counts, w_in, w_out, chunk, ts):
    ne, d, f2 = w_in.shape
    n_rows = hs.shape[0]
    nc = n_rows // (chunk * ROW_TILE)
    grid_spec = pltpu.PrefetchScalarGridSpec(
        num_scalar_prefetch=2,
        grid=(nc, ne),
        in_specs=[
            pl.BlockSpec((2 * chunk,), lambda c, e, st, cn: (c,), memory_space=pltpu.SMEM),
            pl.BlockSpec((2 * chunk,), lambda c, e, st, cn: (c,), memory_space=pltpu.SMEM),
            pl.BlockSpec(memory_space=pl.ANY),
            pl.BlockSpec((1, d, f2), lambda c, e, st, cn: (e, 0, 0)),
            pl.BlockSpec((1, f2 // 2, d), lambda c, e, st, cn: (e, 0, 0)),
        ],
        out_specs=pl.BlockSpec(memory_space=pl.ANY),
        scratch_shapes=[
            pltpu.VMEM(((chunk + 1) * ROW_TILE, LANES), F32),
            pltpu.VMEM(((chunk + 1) * ROW_TILE, LANES), F32),
            pltpu.VMEM((ts * ROW_TILE, LANES), F32),
            pltpu.VMEM((ts * ROW_TILE, LANES), F32),
            pltpu.SMEM((2 * chunk,), jnp.int32),
            pltpu.SemaphoreType.DMA((2,)),
            pltpu.VMEM((d, f2), BF16),
            pltpu.VMEM((f2 // 2, d), BF16),
        ],
    )
    return pl.pallas_call(
        functools.partial(_moe_fused_kernel, ts),
        grid_spec=grid_spec,
        out_shape=jax.ShapeDtypeStruct((n_rows, LANES), F32),
        compiler_params=pltpu.CompilerParams(dimension_semantics=("arbitrary", "arbitrary"),
                                             vmem_limit_bytes=MOE_VMEM_LIMIT),
        name="moe_fused",
    )(starts, counts, pos_local, wts, hs, w_in, w_out)


def _finish_kernel(acc_ref, x_ref, o_ref):
    tm = x_ref.shape[0]
    o_ref[...] = x_ref[...] + _from_token_tiles(acc_ref, 0, tm, ROW_TILE)


def moe_finish(acc, x, tm):
    n, d = x.shape
    return pl.pallas_call(
        _finish_kernel,
        grid=(n // tm,),
        in_specs=[pl.BlockSpec((tm * ROW_TILE, LANES), lambda i: (i, 0)),
                  pl.BlockSpec((tm, d), lambda i: (i, 0))],
        out_specs=pl.BlockSpec((tm, d), lambda i: (i, 0)),
        out_shape=jax.ShapeDtypeStruct((n, d), F32),
        compiler_params=_params("parallel"),
        name="moe_finish",
    )(acc, x)


def moe_layer_fused(x, g, wr, br, w_in, w_out):
    n = x.shape[0]
    chunk = min(n, MOE_CHUNK)
    nc = n // chunk
    hs, route, counts = router(x, g, wr, br, MOE_TM, chunk)
    ids = route[:, 0:2].astype(jnp.int32).reshape(nc, chunk, 2)
    cnt = counts.reshape(nc, ROW_TILE, LANES)[:, 0, :N_EXPERTS].astype(jnp.int32)
    starts = jnp.cumsum(cnt, axis=1) - cnt
    sel = ids[..., None] == jnp.arange(N_EXPERTS, dtype=jnp.int32)
    start_of = jnp.sum(jnp.where(sel, starts[:, None, None, :], 0), axis=-1)
    pos_local = start_of + route[:, 4:6].astype(jnp.int32).reshape(nc, chunk, 2)
    acc = moe_fused(hs, pos_local.reshape(-1), route[:, 2:4].reshape(-1), starts.reshape(-1),
                    cnt.reshape(-1), w_in, w_out, chunk, MOE_TS)
    return moe_finish(acc, x, MOE_TM)


def moe_layer(x, g, wr, br, w_in, w_out, layer):
    n = x.shape[0]
    chunk = min(n, MOE_CHUNK)
    nc = n // chunk
    tpc = (2 * chunk + N_EXPERTS * (MOE_TMG - 1)) // MOE_TMG
    hs, route, counts = router(x, g, wr, br, MOE_TM, chunk)
    ids = route[:, 0:2].astype(jnp.int32).reshape(nc, chunk, 2)
    cnt = counts.reshape(nc, ROW_TILE, LANES)[:, 0, :N_EXPERTS].astype(jnp.int32)
    padded = (cnt + MOE_TMG - 1) // MOE_TMG * MOE_TMG
    ends = jnp.cumsum(padded, axis=1)
    starts = ends - padded
    sel = ids[..., None] == jnp.arange(N_EXPERTS, dtype=jnp.int32)
    start_of = jnp.sum(jnp.where(sel, starts[:, None, None, :], 0), axis=-1)
    pos_local = start_of + route[:, 4:6].astype(jnp.int32).reshape(nc, chunk, 2)
    tile_row = jnp.arange(tpc, dtype=jnp.int32) * MOE_TMG
    tile_expert = jnp.minimum(jnp.sum(ends[:, None, :] <= tile_row[None, :, None], axis=-1),
                              N_EXPERTS - 1).astype(jnp.int32)
    n_valid = (ends[:, -1] // MOE_TMG).astype(jnp.int32)
    ys = experts(hs, pos_local.reshape(-1), tile_expert.reshape(-1) + layer * N_EXPERTS, n_valid,
                 w_in, w_out, MOE_TMG, chunk, tpc)
    pos = pos_local + (jnp.arange(nc, dtype=jnp.int32) * (tpc * MOE_TMG))[:, None, None]
    return combine(ys, pos.reshape(-1), route, x, MOE_TM)


def _qkv_kernel(x_ref, g_ref, w_ref, qg_ref, kg_ref, seg_ref, q_ref, k_ref, v_ref):
    h = _rms(x_ref[...], g_ref[...]).astype(BF16)
    qkv = jnp.dot(h, w_ref[...], preferred_element_type=F32)
    nq = N_HEADS * HEAD_DIM
    nk = N_KV * HEAD_DIM
    q = qkv[:, :nq]
    k = qkv[:, nq:nq + nk]
    v_ref[...] = qkv[:, nq + nk:]

    def seg_mean_sq(z, seg, split):
        zz = z * z
        hi = zz.astype(BF16)
        ms = jnp.dot(hi, seg, preferred_element_type=F32)
        if split:
            lo = (zz - hi.astype(F32)).astype(BF16)
            ms = ms + jnp.dot(lo, seg, preferred_element_type=F32)
        return ms

    seg = seg_ref[...]
    qn = q * lax.rsqrt(seg_mean_sq(q, seg, False) + RMS_EPS) * qg_ref[...]
    q_ref[...] = (qn * (HEAD_DIM ** -0.5)).astype(BF16)
    kn = k * lax.rsqrt(seg_mean_sq(k, seg[:nk, :nk], True) + RMS_EPS) * kg_ref[...]
    k_ref[...] = kn


def qkv_proj(x, g, w, qg, kg, seg, tm):
    n, d = x.shape
    nq = N_HEADS * HEAD_DIM
    nk = N_KV * HEAD_DIM
    return pl.pallas_call(
        _qkv_kernel,
        grid=(n // tm,),
        in_specs=[pl.BlockSpec((tm, d), lambda i: (i, 0)), _full((1, d)),
                  _full((d, nq + 2 * nk)), _full((1, nq)), _full((1, nk)), _full((nq, nq))],
        out_specs=[pl.BlockSpec((tm, nq), lambda i: (i, 0)),
                   pl.BlockSpec((tm, nk), lambda i: (i, 0)),
                   pl.BlockSpec((tm, nk), lambda i: (i, 0))],
        out_shape=[jax.ShapeDtypeStruct((n, nq), BF16),
                   jax.ShapeDtypeStruct((n, nk), F32),
                   jax.ShapeDtypeStruct((n, nk), F32)],
        compiler_params=_params("parallel"),
        name="qkv_proj",
    )(x, g, w, qg, kg, seg)


def _attn_prompt_kernel(q_ref, kc_ref, kp_ref, vc_ref, vp_ref, tbl_ref, sink_ref, x_ref,
                        wo_ref, o_ref, cat_ref):
    n = pl.program_id(1)
    first = jnp.where(n == 0, NEG_INF, 0.0).astype(F32)
    kk = jnp.concatenate([kp_ref[...], kc_ref[...]], axis=0).astype(BF16)
    vv = jnp.concatenate([vp_ref[...], vc_ref[...]], axis=0).astype(BF16)
    col = lax.broadcasted_iota(jnp.int32, (WINDOW, 2 * WINDOW), 1)
    prev_mask = jnp.where(col < WINDOW, first, 0.0)
    for h in range(N_HEADS):
        kh = h // GROUP
        qh = q_ref[:, h * HEAD_DIM:(h + 1) * HEAD_DIM]
        kslice = kk[:, kh * HEAD_DIM:(kh + 1) * HEAD_DIM]
        vslice = vv[:, kh * HEAD_DIM:(kh + 1) * HEAD_DIM]
        s = lax.dot_general(qh, kslice, (((1,), (1,)), ((), ())), preferred_element_type=F32)
        logits = s + tbl_ref[h] + prev_mask
        sink = sink_ref[h]
        m = jnp.maximum(jnp.max(logits, axis=-1, keepdims=True), sink)
        e = jnp.exp(logits - m)
        denom = jnp.sum(e, axis=-1, keepdims=True) + jnp.exp(sink - m)
        p = e * (1.0 / denom)
        cat_ref[:, h * HEAD_DIM:(h + 1) * HEAD_DIM] = jnp.dot(
            p.astype(BF16), vslice, preferred_element_type=F32).astype(BF16)
    o_ref[...] = x_ref[...] + jnp.dot(cat_ref[...], wo_ref[...], preferred_element_type=F32)


def attn_prompt(q, k, v, tbl, sinks, x, wo, batch, seq):
    n, d = x.shape
    nb = seq // WINDOW
    nk = N_KV * HEAD_DIM
    cur = lambda b, i: (b * nb + i, 0)
    prev = lambda b, i: (b * nb + jnp.maximum(i - 1, 0), 0)
    return pl.pallas_call(
        _attn_prompt_kernel,
        grid=(batch, nb),
        in_specs=[
            pl.BlockSpec((WINDOW, d), cur),
            pl.BlockSpec((WINDOW, nk), cur), pl.BlockSpec((WINDOW, nk), prev),
            pl.BlockSpec((WINDOW, nk), cur), pl.BlockSpec((WINDOW, nk), prev),
            _full((N_HEADS, WINDOW, 2 * WINDOW)),
            pl.BlockSpec(memory_space=pltpu.SMEM),
            pl.BlockSpec((WINDOW, d), cur),
            _full((d, d)),
        ],
        out_specs=pl.BlockSpec((WINDOW, d), cur),
        out_shape=jax.ShapeDtypeStruct((n, d), F32),
        scratch_shapes=[pltpu.VMEM((WINDOW, d), BF16)],
        compiler_params=_params("parallel", "parallel"),
        name="attn_prompt",
    )(q, k, k, v, v, tbl, sinks, x, wo)


def _attn_sample_kernel(q_ref, kn_ref, vn_ref, ck_ref, cv_ref, tblc_ref, tbln_ref, sink_ref, o_ref):
    bs = q_ref.shape[0]
    nt = (((1,), (1,)), ((), ()))
    for b in range(bs):
        ck = ck_ref[b].astype(BF16)
        cv = cv_ref[b].astype(BF16)
        kn = kn_ref[b].astype(BF16)
        vn = vn_ref[b].astype(BF16)
        for kh in range(N_KV):
            lanes = slice(kh * HEAD_DIM, (kh + 1) * HEAD_DIM)
            qt = q_ref[b, kh]
            sc = lax.dot_general(qt, ck[:, lanes], nt, preferred_element_type=F32) + tblc_ref[kh]
            sn = lax.dot_general(qt, kn[:, lanes], nt, preferred_element_type=F32) + tbln_ref[kh]
            sink = sink_ref[kh]
            m = jnp.maximum(jnp.maximum(jnp.max(sc, axis=-1, keepdims=True),
                                        jnp.max(sn, axis=-1, keepdims=True)), sink)
            ec = jnp.exp(sc - m)
            en = jnp.exp(sn - m)
            denom = (jnp.sum(ec, axis=-1, keepdims=True) + jnp.sum(en, axis=-1, keepdims=True)
                     + jnp.exp(sink - m))
            out = (jnp.dot(ec.astype(BF16), cv[:, lanes], preferred_element_type=F32)
                   + jnp.dot(en.astype(BF16), vn[:, lanes], preferred_element_type=F32))
            o_ref[b, kh] = (out * (1.0 / denom)).astype(BF16)


def attn_sample(q4, kn, vn, ck, cv, tbl, sink, bs):
    nb = q4.shape[0]
    ts = kn.shape[1]
    nk = N_KV * HEAD_DIM
    tg = ts * GROUP
    kt = WINDOW + ts
    return pl.pallas_call(
        _attn_sample_kernel,
        grid=(nb // bs,),
        in_specs=[
            pl.BlockSpec((bs, N_KV, tg, HEAD_DIM), lambda i: (i, 0, 0, 0)),
            pl.BlockSpec((bs, ts, nk), lambda i: (i, 0, 0)),
            pl.BlockSpec((bs, ts, nk), lambda i: (i, 0, 0)),
            pl.BlockSpec((bs, WINDOW, nk), lambda i: (i, 0, 0)),
            pl.BlockSpec((bs, WINDOW, nk), lambda i: (i, 0, 0)),
            _full((N_KV, tg, WINDOW)),
            _full((N_KV, tg, ts)),
            _full((N_KV, tg, 1)),
        ],
        out_specs=pl.BlockSpec((bs, N_KV, tg, HEAD_DIM), lambda i: (i, 0, 0, 0)),
        out_shape=jax.ShapeDtypeStruct((nb, N_KV, tg, HEAD_DIM), BF16),
        compiler_params=_params("parallel"),
        name="attn_sample",
    )(q4, kn, vn, ck, cv, tbl[:, :, :WINDOW], tbl[:, :, WINDOW:], sink)


def _proj_res_kernel(a_ref, w_ref, x_ref, o_ref):
    o_ref[...] = x_ref[...] + jnp.dot(a_ref[...], w_ref[...], preferred_element_type=F32)


def proj_residual(a, w, x, tm):
    n, d = x.shape
    kdim = a.shape[1]
    return pl.pallas_call(
        _proj_res_kernel,
        grid=(n // tm,),
        in_specs=[pl.BlockSpec((tm, kdim), lambda i: (i, 0)), _full((kdim, d)),
                  pl.BlockSpec((tm, d), lambda i: (i, 0))],
        out_specs=pl.BlockSpec((tm, d), lambda i: (i, 0)),
        out_shape=jax.ShapeDtypeStruct((n, d), F32),
        compiler_params=_params("parallel"),
        name="proj_residual",
    )(a, w, x)


def _t5_bucket_np(dist):
    n = np.maximum(dist, 0)
    max_exact = N_BUCKETS // 2
    large = max_exact + (np.log(np.maximum(n, 1).astype(np.float32) / max_exact)
                         / math.log(MAX_DISTANCE / max_exact) * (N_BUCKETS - max_exact)).astype(np.int32)
    large = np.minimum(large, N_BUCKETS - 1)
    return np.where(n < max_exact, n, large)


def _bias_table(rel_bias, dist):
    valid = (dist >= 0) & (dist <= WINDOW)
    onehot = (np.asarray(_t5_bucket_np(dist))[..., None] == np.arange(N_BUCKETS)).astype(np.float32)
    b = jnp.einsum("qkb,bh->hqk", jnp.asarray(onehot), rel_bias.astype(F32),
                   precision=lax.Precision.HIGHEST)
    return jnp.where(jnp.asarray(valid)[None], b, NEG_INF)


def kernel(x_prompt, x_sample, state_conv, cache_swa_k, cache_swa_v, rms_mix_g, rms_ffn_g, conv_w_in, conv_dw_w, conv_dw_b, conv_ln_g, conv_ln_b, conv_w_out, attn_w_qkv, attn_q_norm_g, attn_k_norm_g, attn_sinks, attn_w_o, rel_bias, router_group_w, router_group_b, router_expert_w, router_expert_b, expert_w_in, expert_w_out):
    batch, seq, d = x_prompt.shape
    nsb, ts, _ = x_sample.shape
    xp = x_prompt.reshape(batch * seq, d)
    xs = x_sample.reshape(nsb * ts, d)
    row = lambda a: a.reshape(1, -1).astype(F32)

    def router_w(i):
        we = jnp.transpose(router_expert_w[i], (1, 0, 2)).reshape(d, N_EXPERTS)
        wr = jnp.concatenate([we, router_group_w[i]], axis=1)
        wr = jnp.pad(wr, ((0, 0), (0, LANES - wr.shape[1])))
        br = jnp.concatenate([router_expert_b[i].reshape(-1), router_group_b[i]])
        br = jnp.pad(br, (0, LANES - br.shape[0])).reshape(1, LANES)
        return wr.astype(F32), br.astype(F32)

    g0 = row(rms_mix_g[0])
    w_in = conv_w_in[0].astype(BF16)
    dww = jnp.pad(conv_dw_w[0].astype(F32), ((0, HALO - CONV_WIDTH), (0, 0)))
    dwb, lng, lnb = row(conv_dw_b[0]), row(conv_ln_g[0]), row(conv_ln_b[0])
    w_out = conv_w_out[0].astype(BF16)
    up = glu_proj(xp, g0, w_in, 512)
    us = glu_proj(xs, g0, w_in, 512)
    xp = conv_prompt(up, xp, dww, dwb, lng, lnb, w_out, batch, seq, 256)
    us3 = us.reshape(nsb, ts, -1)
    xs = conv_sample(us3, state_conv[0], xs, dww, dwb, lng, lnb, w_out, 32)
    conv_p = up.reshape(batch, seq, -1)[:, seq - PAST:]
    conv_s = jnp.concatenate([state_conv[0], us3], axis=1)[:, ts:]

    wr0, br0 = router_w(0)
    ew_in = expert_w_in.reshape((DEPTH * N_EXPERTS,) + expert_w_in.shape[2:])
    ew_out = expert_w_out.reshape((DEPTH * N_EXPERTS,) + expert_w_out.shape[2:])
    xp = moe_layer(xp, row(rms_ffn_g[0]), wr0, br0, ew_in, ew_out, 0)
    xs = moe_layer(xs, row(rms_ffn_g[0]), wr0, br0, ew_in, ew_out, 0)

    g1 = row(rms_mix_g[1])
    w_qkv = attn_w_qkv[0].astype(BF16)
    qg = jnp.tile(attn_q_norm_g[0].astype(F32), N_HEADS).reshape(1, -1)
    kg = jnp.tile(attn_k_norm_g[0].astype(F32), N_KV).reshape(1, -1)
    nq = N_HEADS * HEAD_DIM
    seg = jnp.asarray(np.kron(np.eye(N_HEADS), np.ones((HEAD_DIM, HEAD_DIM))) / HEAD_DIM, BF16)
    w_o = attn_w_o[0].astype(BF16)
    sinks = attn_sinks[0].astype(F32)

    qp, kp, vp = qkv_proj(xp, g1, w_qkv, qg, kg, seg, 512)
    qs, ks, vs = qkv_proj(xs, g1, w_qkv, qg, kg, seg, 512)

    q_off = np.arange(WINDOW)[:, None]
    dist_p = q_off + WINDOW - np.arange(2 * WINDOW)[None, :]
    tbl_p = _bias_table(rel_bias, dist_p)
    xp = attn_prompt(qp, kp, vp, tbl_p, sinks, xp, w_o, batch, seq)

    kt = WINDOW + ts
    dist_s = np.arange(ts)[:, None] + WINDOW - np.arange(kt)[None, :]
    tbl_s = _bias_table(rel_bias, dist_s)
    tbl_s = jnp.transpose(tbl_s.reshape(N_KV, GROUP, ts, kt), (0, 2, 1, 3)).reshape(N_KV, ts * GROUP, kt)
    sink_s = jnp.tile(sinks.reshape(N_KV, 1, GROUP), (1, ts, 1)).reshape(N_KV, ts * GROUP, 1)
    q4 = jnp.transpose(qs.reshape(nsb, ts, N_KV, GROUP, HEAD_DIM), (0, 2, 1, 3, 4))
    q4 = q4.reshape(nsb, N_KV, ts * GROUP, HEAD_DIM)
    nk = N_KV * HEAD_DIM
    ks3, vs3 = ks.reshape(nsb, ts, nk), vs.reshape(nsb, ts, nk)
    ck = cache_swa_k[0].reshape(nsb, WINDOW, nk)
    cv = cache_swa_v[0].reshape(nsb, WINDOW, nk)
    o4 = attn_sample(q4, ks3, vs3, ck, cv, tbl_s, sink_s, 16)
    os_ = jnp.transpose(o4.reshape(nsb, N_KV, ts, GROUP, HEAD_DIM), (0, 2, 1, 3, 4)).reshape(nsb * ts, nq)
    xs = proj_residual(os_, w_o, xs, 512)

    k_p = kp.reshape(batch, seq, N_KV, HEAD_DIM)[:, seq - WINDOW:]
    v_p = vp.reshape(batch, seq, N_KV, HEAD_DIM)[:, seq - WINDOW:]
    k_s = jnp.concatenate([cache_swa_k[0], ks.reshape(nsb, ts, N_KV, HEAD_DIM)], axis=1)[:, ts:]
    v_s = jnp.concatenate([cache_swa_v[0], vs.reshape(nsb, ts, N_KV, HEAD_DIM)], axis=1)[:, ts:]

    wr1, br1 = router_w(1)
    xp = moe_layer(xp, row(rms_ffn_g[1]), wr1, br1, ew_in, ew_out, 1)
    xs = moe_layer(xs, row(rms_ffn_g[1]), wr1, br1, ew_in, ew_out, 1)

    return (xp.reshape(batch, seq, d), xs.reshape(nsb, ts, d),
            conv_p[None], conv_s[None], k_p[None], v_p[None], k_s[None], v_s[None])
```

```python
import functools
import math

import numpy as np
import jax
import jax.numpy as jnp
from jax import lax
from jax.experimental import pallas as pl
from jax.experimental.pallas import tpu as pltpu

D_MODEL = 1024
DEPTH = 2
CONV_WIDTH = 31
PAST = CONV_WIDTH - 1
HEAD_DIM = 64
N_HEADS = 16
N_KV = 2
GROUP = 8
WINDOW = 128
N_BUCKETS = 32
MAX_DISTANCE = 128
N_GROUPS = 4
EPG = 8
N_EXPERTS = 32
D_EXPERT = 256
RMS_EPS = 1e-6
LN_EPS = 1e-5
NEG_INF = -1e30

F32 = jnp.float32
BF16 = jnp.bfloat16
LANES = 128
ROW_TILE = 8
MOE_TM = 256
MOE_TMG = 256
MOE_CHUNK = 8704
MOE_TS = 128
MOE_VMEM_LIMIT = 56 * 1024 * 1024
VMEM_LIMIT = 48 * 1024 * 1024


def _params(*sem):
    return pltpu.CompilerParams(dimension_semantics=sem, vmem_limit_bytes=VMEM_LIMIT)


def _rms(x, g):
    return x * lax.rsqrt(jnp.mean(x * x, axis=-1, keepdims=True) + RMS_EPS) * g


def _sigmoid(x):
    return 1.0 / (1.0 + jnp.exp(-x))


def _full(shape):
    return pl.BlockSpec(shape, lambda *_: (0,) * len(shape))


def _glu_kernel(x_ref, g_ref, wa_ref, wg_ref, u_ref):
    h = _rms(x_ref[...], g_ref[...]).astype(BF16)
    a = jnp.dot(h, wa_ref[...], preferred_element_type=F32)
    gate = jnp.dot(h, wg_ref[...], preferred_element_type=F32)
    u_ref[...] = a * _sigmoid(gate)


def glu_proj(x, g, w_in, tm):
    n, d = x.shape
    c = w_in.shape[1] // 2
    return pl.pallas_call(
        _glu_kernel,
        grid=(n // tm,),
        in_specs=[
            pl.BlockSpec((tm, d), lambda i: (i, 0)),
            _full((1, d)),
            pl.BlockSpec((d, c), lambda i: (0, 0)),
            pl.BlockSpec((d, c), lambda i: (0, 1)),
        ],
        out_specs=pl.BlockSpec((tm, c), lambda i: (i, 0)),
        out_shape=jax.ShapeDtypeStruct((n, c), F32),
        compiler_params=_params("parallel"),
        name="glu_proj",
    )(x, g, w_in, w_in)


def _ln_silu_out(y, lng, lnb, wout_ref, x):
    mu = jnp.mean(y, axis=-1, keepdims=True)
    yc = y - mu
    z = yc * lax.rsqrt(jnp.mean(yc * yc, axis=-1, keepdims=True) + LN_EPS) * lng + lnb
    z = z * _sigmoid(z)
    return x + jnp.dot(z.astype(BF16), wout_ref[...], preferred_element_type=F32)


HALO = 32
CONV_RC = 64
CONV_CC = 128


def _conv_prompt_kernel(ucur_ref, uprev_ref, x_ref, dww_ref, dwb_ref, lng_ref, lnb_ref,
                        wout_ref, o_ref, up_ref, y_ref):
    t = pl.program_id(1)
    tt, c = ucur_ref.shape
    keep = (t > 0).astype(F32)
    up_ref[0:HALO, :] = uprev_ref[...] * keep
    up_ref[HALO:HALO + tt, :] = ucur_ref[...]
    up_ref[HALO + tt:, :] = jnp.zeros((ROW_TILE, c), F32)
    off = HALO - PAST
    for r0 in range(0, tt, CONV_RC):
        for c0 in range(0, c, CONV_CC):
            y = jnp.zeros((CONV_RC, CONV_CC), F32) + dwb_ref[:, c0:c0 + CONV_CC]
            for s in range(ROW_TILE):
                v = None
                for q in range((off + CONV_WIDTH - 1) // ROW_TILE + 1):
                    k = ROW_TILE * q + s - off
                    if k < 0 or k >= CONV_WIDTH:
                        continue
                    lo = r0 + ROW_TILE * q
                    term = (up_ref[lo:lo + CONV_RC + ROW_TILE, c0:c0 + CONV_CC]
                            * dww_ref[k:k + 1, c0:c0 + CONV_CC])
                    v = term if v is None else v + term
                y = y + v[s:s + CONV_RC]
            y_ref[r0:r0 + CONV_RC, c0:c0 + CONV_CC] = y
    o_ref[...] = _ln_silu_out(y_ref[...], lng_ref[...], lnb_ref[...], wout_ref, x_ref[...])


def conv_prompt(u, x, dww, dwb, lng, lnb, wout, batch, seq, tt):
    n, c = u.shape
    d = x.shape[1]
    nt = seq // tt
    hb = tt // HALO
    return pl.pallas_call(
        _conv_prompt_kernel,
        grid=(batch, nt),
        in_specs=[
            pl.BlockSpec((tt, c), lambda b, t: (b * nt + t, 0)),
            pl.BlockSpec((HALO, c), lambda b, t: (jnp.maximum((b * nt + t) * hb - 1, 0), 0)),
            pl.BlockSpec((tt, d), lambda b, t: (b * nt + t, 0)),
            _full((HALO, c)), _full((1, c)), _full((1, c)), _full((1, c)),
            _full((c, d)),
        ],
        out_specs=pl.BlockSpec((tt, d), lambda b, t: (b * nt + t, 0)),
        out_shape=jax.ShapeDtypeStruct((n, d), F32),
        scratch_shapes=[pltpu.VMEM((tt + HALO + ROW_TILE, c), F32), pltpu.VMEM((tt, c), F32)],
        compiler_params=_params("parallel", "parallel"),
        name="conv_prompt",
    )(u, u, x, dww, dwb, lng, lnb, wout)


CONV_SB = 4


def _conv_sample_kernel(u_ref, st_ref, x_ref, dww_ref, dwb_ref, lng_ref, lnb_ref,
                        wout_ref, o_ref, up_ref, y_ref):
    bs, ts, c = u_ref.shape
    up_ref[:, 0:PAST, :] = st_ref[...]
    up_ref[:, PAST:PAST + ts, :] = u_ref[...]
    for b0 in range(0, bs, CONV_SB):
        acc = jnp.zeros((CONV_SB, ts, c), F32) + dwb_ref[...][None]
        for k in range(CONV_WIDTH):
            acc = acc + up_ref[b0:b0 + CONV_SB, k:k + ts, :] * dww_ref[k:k + 1, :][None]
        y_ref[b0 * ts:(b0 + CONV_SB) * ts, :] = acc.reshape(CONV_SB * ts, c)
    o_ref[...] = _ln_silu_out(y_ref[...], lng_ref[...], lnb_ref[...], wout_ref, x_ref[...])


def conv_sample(u3, state, x, dww, dwb, lng, lnb, wout, bs):
    nb, ts, c = u3.shape
    d = x.shape[1]
    return pl.pallas_call(
        _conv_sample_kernel,
        grid=(nb // bs,),
        in_specs=[
            pl.BlockSpec((bs, ts, c), lambda i: (i, 0, 0)),
            pl.BlockSpec((bs, PAST, c), lambda i: (i, 0, 0)),
            pl.BlockSpec((bs * ts, d), lambda i: (i, 0)),
            _full((HALO, c)), _full((1, c)), _full((1, c)), _full((1, c)),
            _full((c, d)),
        ],
        out_specs=pl.BlockSpec((bs * ts, d), lambda i: (i, 0)),
        out_shape=jax.ShapeDtypeStruct((nb * ts, d), F32),
        scratch_shapes=[pltpu.VMEM((bs, PAST + ts, c), F32), pltpu.VMEM((bs * ts, c), F32)],
        compiler_params=_params("parallel"),
        name="conv_sample",
    )(u3, state, x, dww, dwb, lng, lnb, wout)


def _token_specs(xs, tm):
    d = xs[0].shape[1]
    if len(xs) == 1:
        return [pl.BlockSpec((tm, d), lambda i: (i, 0))]
    first = xs[0].shape[0] // tm
    return [pl.BlockSpec((tm, d), lambda i: (jnp.minimum(i, first - 1), 0)),
            pl.BlockSpec((tm, d), lambda i: (jnp.maximum(i - first, 0), 0))]


def _read_tokens(x_refs, first_steps):
    if len(x_refs) == 1:
        return x_refs[0][...]
    return jnp.where(pl.program_id(0) < first_steps, x_refs[0][...], x_refs[1][...])


def _router_kernel(tiles_per_chunk, n_src, first_steps, *refs):
    x_refs = refs[:n_src]
    g_ref, wr_ref, br_ref, tri_ref, hs_ref, route_ref, cnt_ref, carry_ref = refs[n_src:]
    h = _rms(_read_tokens(x_refs, first_steps), g_ref[...])
    tm, d = h.shape
    for j in range(d // LANES):
        hs_ref[pl.ds(j, tm, stride=ROW_TILE), :] = h[:, j * LANES:(j + 1) * LANES]
    logits = jnp.dot(h, wr_ref[...], preferred_element_type=F32,
                     precision=lax.Precision.HIGHEST) + br_ref[...]
    lane = lax.broadcasted_iota(jnp.int32, logits.shape, 1).astype(F32)
    big = jnp.float32(LANES)
    is_g = (lane >= N_EXPERTS) & (lane < N_EXPERTS + N_GROUPS)
    gl = jnp.where(is_g, logits, NEG_INF)
    gm = jnp.max(gl, axis=-1, keepdims=True)
    g_sel = jnp.min(jnp.where(gl == gm, lane, big), axis=-1, keepdims=True) - N_EXPERTS
    gate_g = 1.0 / jnp.sum(jnp.where(is_g, jnp.exp(gl - gm), 0.0), axis=-1, keepdims=True)
    lo = g_sel * EPG
    in_grp = (lane >= lo) & (lane < lo + EPG)
    el = jnp.where(in_grp, logits, NEG_INF)
    v1 = jnp.max(el, axis=-1, keepdims=True)
    i1 = jnp.min(jnp.where(el == v1, lane, big), axis=-1, keepdims=True)
    el2 = jnp.where(lane == i1, NEG_INF, el)
    v2 = jnp.max(el2, axis=-1, keepdims=True)
    i2 = jnp.min(jnp.where(el2 == v2, lane, big), axis=-1, keepdims=True)
    e2 = jnp.exp(v2 - v1)
    w1 = gate_g / (1.0 + e2)
    w2 = gate_g * e2 / (1.0 + e2)
    @pl.when(pl.program_id(0) % tiles_per_chunk == 0)
    def _():
        carry_ref[...] = jnp.zeros_like(carry_ref)

    hit1 = lane == i1
    hit2 = lane == i2
    onehot = jnp.where(hit1 | hit2, 1.0, 0.0)
    before = carry_ref[...] + jnp.dot(tri_ref[...], onehot.astype(BF16), preferred_element_type=F32)
    r1 = jnp.sum(jnp.where(hit1, before, 0.0), axis=-1, keepdims=True)
    r2 = jnp.sum(jnp.where(hit2, before, 0.0), axis=-1, keepdims=True)
    carry_ref[...] += jnp.sum(onehot, axis=0, keepdims=True)
    cnt_ref[...] = jnp.broadcast_to(carry_ref[...], cnt_ref.shape)
    route_ref[...] = jnp.where(lane == 0.0, i1, jnp.where(lane == 1.0, i2, jnp.where(
        lane == 2.0, w1, jnp.where(lane == 3.0, w2, jnp.where(
            lane == 4.0, r1, jnp.where(lane == 5.0, r2, 0.0))))))


def router(xs, g, wr, br, tm, chunk):
    n = sum(x.shape[0] for x in xs)
    d = xs[0].shape[1]
    tpc = chunk // tm
    tri = jnp.asarray(np.tril(np.ones((tm, tm), np.float32), -1), BF16)
    return pl.pallas_call(
        functools.partial(_router_kernel, tpc, len(xs), xs[0].shape[0] // tm),
        grid=(n // tm,),
        in_specs=_token_specs(xs, tm) + [_full((1, d)),
                  _full((d, LANES)), _full((1, LANES)), _full((tm, tm))],
        out_specs=[pl.BlockSpec((tm * ROW_TILE, LANES), lambda i: (i, 0)),
                   pl.BlockSpec((tm, LANES), lambda i: (i, 0)),
                   pl.BlockSpec((ROW_TILE, LANES), lambda i: (i // tpc, 0))],
        out_shape=[jax.ShapeDtypeStruct((n * ROW_TILE, LANES), F32),
                   jax.ShapeDtypeStruct((n, LANES), F32),
                   jax.ShapeDtypeStruct((n // chunk * ROW_TILE, LANES), F32)],
        scratch_shapes=[pltpu.VMEM((1, LANES), F32)],
        compiler_params=_params("arbitrary"),
        name="router",
    )(*xs, g, wr, br, tri)


def _plan_kernel(tiles_per_chunk, route_ref, tri_ref, rank_ref, cnt_ref, carry_ref):
    i = pl.program_id(0)

    @pl.when(i % tiles_per_chunk == 0)
    def _():
        carry_ref[...] = jnp.zeros_like(carry_ref)

    route = route_ref[...]
    lane = lax.broadcasted_iota(jnp.int32, route.shape, 1).astype(F32)
    i1 = route[:, 0:1]
    i2 = route[:, 1:2]
    hit1 = lane == i1
    hit2 = lane == i2
    onehot = jnp.where(hit1 | hit2, 1.0, 0.0)
    before = carry_ref[...] + jnp.dot(tri_ref[...], onehot.astype(BF16), preferred_element_type=F32)
    r1 = jnp.sum(jnp.where(hit1, before, 0.0), axis=-1, keepdims=True)
    r2 = jnp.sum(jnp.where(hit2, before, 0.0), axis=-1, keepdims=True)
    rank_ref[...] = jnp.where(lane == 0.0, r1, jnp.where(lane == 1.0, r2, 0.0))
    carry_ref[...] += jnp.sum(onehot, axis=0, keepdims=True)
    cnt_ref[...] = jnp.broadcast_to(carry_ref[...], cnt_ref.shape)


def plan(route, tm, chunk):
    n = route.shape[0]
    tpc = chunk // tm
    tri = jnp.asarray(np.tril(np.ones((tm, tm), np.float32), -1), BF16)
    return pl.pallas_call(
        functools.partial(_plan_kernel, tpc),
        grid=(n // tm,),
        in_specs=[pl.BlockSpec((tm, LANES), lambda i: (i, 0)), _full((tm, tm))],
        out_specs=[pl.BlockSpec((tm, LANES), lambda i: (i, 0)),
                   pl.BlockSpec((ROW_TILE, LANES), lambda i: (i // tpc, 0))],
        out_shape=[jax.ShapeDtypeStruct((n, LANES), F32),
                   jax.ShapeDtypeStruct((n // chunk * ROW_TILE, LANES), F32)],
        scratch_shapes=[pltpu.VMEM((1, LANES), F32)],
        compiler_params=_params("arbitrary"),
        name="moe_plan",
    )(route, tri)


def _gather_rows(idx_ref, src_hbm, dst, sem, n_rows):
    unroll = 8

    def body(c, carry):
        for u in range(unroll):
            m = c * unroll + u
            tok = idx_ref[m]
            pltpu.make_async_copy(
                src_hbm.at[pl.ds(pl.multiple_of(tok * ROW_TILE, ROW_TILE), ROW_TILE), :],
                dst.at[pl.ds(pl.multiple_of(m * ROW_TILE, ROW_TILE), ROW_TILE), :],
                sem).start()
        return carry

    lax.fori_loop(0, n_rows // unroll, body, 0)


def _wait_rows(src_hbm, dst, sem):
    pltpu.make_async_copy(src_hbm.at[pl.ds(0, dst.shape[0]), :], dst, sem).wait()


def _from_token_tiles(buf, start, n_rows, stride):
    return jnp.concatenate(
        [buf[pl.ds(start + j, n_rows, stride=stride), :] for j in range(ROW_TILE)], axis=-1)


def _experts_kernel(tiles_per_chunk, tmg, te_ref, nv_ref, pos_ref, fill_hbm, hs_hbm,
                    win_ref, wout_ref, ys_ref, hsv, gbuf_a, gbuf_b, src_ref, sems, winb, woutb):
    c = pl.program_id(0)
    j = pl.program_id(1)
    t = c * tiles_per_chunk + j
    nv = nv_ref[c]
    chunk = pos_ref.shape[0] // 2
    unroll = 8

    def tile(r):
        return pl.ds(pl.multiple_of(r * ROW_TILE, ROW_TILE), ROW_TILE)

    @pl.when(j == 0)
    def _():
        load = pltpu.make_async_copy(
            hs_hbm.at[pl.ds(pl.multiple_of(c * chunk * ROW_TILE, ROW_TILE), chunk * ROW_TILE), :],
            hsv.at[pl.ds(0, chunk * ROW_TILE), :], sems.at[0])
        load.start()
        fill = pltpu.make_async_copy(fill_hbm, src_ref, sems.at[1])
        fill.start()
        hsv[pl.ds(chunk * ROW_TILE, ROW_TILE), :] = jnp.zeros((ROW_TILE, LANES), F32)
        fill.wait()

        def scatter(i, carry):
            for u in range(unroll):
                tok = i * unroll + u
                src_ref[pos_ref[2 * tok]] = tok
                src_ref[pos_ref[2 * tok + 1]] = tok
            return carry

        lax.fori_loop(0, chunk // unroll, scatter, 0)
        load.wait()

        def gather(i, carry):
            for u in range(unroll):
                m = i * unroll + u
                gbuf_a[tile(m), :] = hsv[tile(src_ref[m]), :]
            return carry

        lax.fori_loop(0, tmg // unroll, gather, 0)

    changed = jnp.logical_or(j == 0, te_ref[t] != te_ref[jnp.maximum(t - 1, 0)])

    @pl.when(jnp.logical_and(changed, j < nv))
    def _():
        winb[...] = win_ref[0].astype(BF16)
        woutb[...] = wout_ref[0].astype(BF16)

    def step(cur, nxt):
        base = jnp.minimum(j + 1, tiles_per_chunk - 1) * tmg
        for m in range(tmg):
            nxt[pl.ds(m * ROW_TILE, ROW_TILE), :] = hsv[tile(src_ref[base + m]), :]
        x = _from_token_tiles(cur, 0, tmg, ROW_TILE).astype(BF16)
        hid = jnp.dot(x, winb[...], preferred_element_type=F32)
        a = hid[:, :D_EXPERT]
        u = hid[:, D_EXPERT:]
        act = (a * _sigmoid(a) * u).astype(BF16)
        y = jnp.dot(act, woutb[...], preferred_element_type=F32)
        for k in range(ROW_TILE):
            ys_ref[pl.ds(k, tmg, stride=ROW_TILE), :] = y[:, k * LANES:(k + 1) * LANES]

    @pl.when(jnp.logical_and(j < nv, j % 2 == 0))
    def _():
        step(gbuf_a, gbuf_b)

    @pl.when(jnp.logical_and(j < nv, j % 2 == 1))
    def _():
        step(gbuf_b, gbuf_a)

    @pl.when(j >= nv)
    def _():
        ys_ref[...] = jnp.zeros_like(ys_ref)


def experts(hs, pos_local, tile_expert, n_valid, w_in, w_out, tmg, chunk, tiles_per_chunk):
    _, d, f2 = w_in.shape
    nc = hs.shape[0] // (chunk * ROW_TILE)
    rows_per_chunk = tiles_per_chunk * tmg
    fill = jnp.full((rows_per_chunk,), chunk, jnp.int32)
    grid_spec = pltpu.PrefetchScalarGridSpec(
        num_scalar_prefetch=2,
        grid=(nc, tiles_per_chunk),
        in_specs=[
            pl.BlockSpec((2 * chunk,), lambda c, j, te, nv: (c,), memory_space=pltpu.SMEM),
            pl.BlockSpec(memory_space=pl.ANY),
            pl.BlockSpec(memory_space=pl.ANY),
            pl.BlockSpec((1, d, f2), lambda c, j, te, nv: (te[c * tiles_per_chunk + j], 0, 0)),
            pl.BlockSpec((1, f2 // 2, d), lambda c, j, te, nv: (te[c * tiles_per_chunk + j], 0, 0)),
        ],
        out_specs=pl.BlockSpec((tmg * ROW_TILE, LANES),
                               lambda c, j, te, nv: (c * tiles_per_chunk + j, 0)),
        scratch_shapes=[
            pltpu.VMEM(((chunk + 1) * ROW_TILE, LANES), F32),
            pltpu.VMEM((tmg * ROW_TILE, LANES), F32),
            pltpu.VMEM((tmg * ROW_TILE, LANES), F32),
            pltpu.SMEM((rows_per_chunk,), jnp.int32),
            pltpu.SemaphoreType.DMA((2,)),
            pltpu.VMEM((d, f2), BF16),
            pltpu.VMEM((f2 // 2, d), BF16),
        ],
    )
    return pl.pallas_call(
        functools.partial(_experts_kernel, tiles_per_chunk, tmg),
        grid_spec=grid_spec,
        out_shape=jax.ShapeDtypeStruct((nc * rows_per_chunk * ROW_TILE, LANES), F32),
        compiler_params=pltpu.CompilerParams(dimension_semantics=("arbitrary", "arbitrary"),
                                             vmem_limit_bytes=MOE_VMEM_LIMIT),
        name="moe_experts",
    )(tile_expert, n_valid, pos_local, fill, hs, w_in, w_out)


def _combine_kernel(split_steps, n_src, first_steps, pos_cur_ref, pos_nxt_ref, ys_hbm, route_ref, *rest):
    x_refs, o_refs, (ybuf_a, ybuf_b, sems) = rest[:n_src], rest[n_src:-3], rest[-3:]
    i = pl.program_id(0)
    last = i == pl.num_programs(0) - 1
    tm = route_ref.shape[0]

    @pl.when(i == 0)
    def _():
        _gather_rows(pos_cur_ref, ys_hbm, ybuf_a, sems.at[0], 2 * tm)

    def step(cur, cur_sem, nxt, nxt_sem):
        _gather_rows(pos_nxt_ref, ys_hbm, nxt, nxt_sem, 2 * tm)
        _wait_rows(ys_hbm, cur, cur_sem)
        route = route_ref[...]
        y1 = _from_token_tiles(cur, 0, tm, 2 * ROW_TILE)
        y2 = _from_token_tiles(cur, ROW_TILE, tm, 2 * ROW_TILE)
        val = _read_tokens(x_refs, first_steps) + route[:, 2:3] * y1 + route[:, 3:4] * y2
        if split_steps is None:
            o_refs[0][...] = val
        else:
            @pl.when(i < split_steps)
            def _():
                o_refs[0][...] = val

            @pl.when(i >= split_steps)
            def _():
                o_refs[1][...] = val

        @pl.when(last)
        def _():
            _wait_rows(ys_hbm, nxt, nxt_sem)

    @pl.when(i % 2 == 0)
    def _():
        step(ybuf_a, sems.at[0], ybuf_b, sems.at[1])

    @pl.when(i % 2 == 1)
    def _():
        step(ybuf_b, sems.at[1], ybuf_a, sems.at[0])


def combine(ys, pos, route, xs, tm, split_rows=None):
    n = sum(x.shape[0] for x in xs)
    d = xs[0].shape[1]
    nsteps = n // tm
    if split_rows is None:
        split_steps = None
        out_specs = pl.BlockSpec((tm, d), lambda i: (i, 0))
        out_shape = jax.ShapeDtypeStruct((n, d), F32)
    else:
        split_steps = split_rows // tm
        out_specs = [pl.BlockSpec((tm, d), lambda i: (jnp.minimum(i, split_steps - 1), 0)),
                     pl.BlockSpec((tm, d), lambda i: (jnp.maximum(i - split_steps, 0), 0))]
        out_shape = [jax.ShapeDtypeStruct((split_rows, d), F32),
                     jax.ShapeDtypeStruct((n - split_rows, d), F32)]
    return pl.pallas_call(
        functools.partial(_combine_kernel, split_steps, len(xs), xs[0].shape[0] // tm),
        grid=(nsteps,),
        in_specs=[
            pl.BlockSpec((2 * tm,), lambda i: (i,), memory_space=pltpu.SMEM),
            pl.BlockSpec((2 * tm,), lambda i: (jnp.minimum(i + 1, nsteps - 1),),
                         memory_space=pltpu.SMEM),
            pl.BlockSpec(memory_space=pl.ANY),
            pl.BlockSpec((tm, LANES), lambda i: (i, 0)),
        ] + _token_specs(xs, tm),
        out_specs=out_specs,
        out_shape=out_shape,
        scratch_shapes=[pltpu.VMEM((2 * tm * ROW_TILE, LANES), F32),
                        pltpu.VMEM((2 * tm * ROW_TILE, LANES), F32),
                        pltpu.SemaphoreType.DMA((2,))],
        compiler_params=_params("arbitrary"),
        name="moe_combine",
    )(pos, pos, ys, route, *xs)


def _moe_fused_kernel(ts, st_ref, cn_ref, pos_ref, wts_ref, hs_hbm, win_ref, wout_ref, acc_hbm,
                      hsv, acc, gbuf, ystage, src_ref, sems, winb, woutb):
    c = pl.program_id(0)
    e = pl.program_id(1)
    n_slots = pos_ref.shape[0]
    chunk = n_slots // 2
    rows = chunk * ROW_TILE
    unroll = 8

    def tile(r):
        return pl.ds(pl.multiple_of(r * ROW_TILE, ROW_TILE), ROW_TILE)

    @pl.when(e == 0)
    def _():
        load = pltpu.make_async_copy(
            hs_hbm.at[pl.ds(pl.multiple_of(c * rows, ROW_TILE), rows), :],
            hsv.at[pl.ds(0, rows), :], sems.at[0])
        load.start()
        hsv[pl.ds(rows, ROW_TILE), :] = jnp.zeros((ROW_TILE, LANES), F32)

        def zero(i, carry):
            acc[pl.ds(pl.multiple_of(i * 64, 64), 64), :] = jnp.zeros((64, LANES), F32)
            return carry

        lax.fori_loop(0, (rows + ROW_TILE) // 64, zero, 0)
        acc[pl.ds(rows + ROW_TILE - 64, 64), :] = jnp.zeros((64, LANES), F32)

        def scatter(i, carry):
            for u in range(unroll):
                s = i * unroll + u
                src_ref[pos_ref[s]] = s
            return carry

        lax.fori_loop(0, n_slots // unroll, scatter, 0)
        load.wait()

    winb[...] = win_ref[0].astype(BF16)
    woutb[...] = wout_ref[0].astype(BF16)
    start = st_ref[c * N_EXPERTS + e]
    cnt = cn_ref[c * N_EXPERTS + e]

    def sub_tile(k, carry):
        base = start + k * ts
        rem = cnt - k * ts

        def slot_of(m):
            s = src_ref[jnp.minimum(base + m, n_slots - 1)]
            valid = m < rem
            return jnp.where(valid, s >> 1, chunk), jnp.where(valid, wts_ref[s], 0.0)

        def gather(i, carry2):
            for u in range(unroll):
                m = i * unroll + u
                tok, _ = slot_of(m)
                gbuf[tile(m), :] = hsv[tile(tok), :]
            return carry2

        lax.fori_loop(0, ts // unroll, gather, 0)
        x = _from_token_tiles(gbuf, 0, ts, ROW_TILE).astype(BF16)
        hid = jnp.dot(x, winb[...], preferred_element_type=F32)
        a = hid[:, :D_EXPERT]
        u_ = hid[:, D_EXPERT:]
        act = (a * _sigmoid(a) * u_).astype(BF16)
        y = jnp.dot(act, woutb[...], preferred_element_type=F32)
        for j in range(ROW_TILE):
            ystage[pl.ds(j, ts, stride=ROW_TILE), :] = y[:, j * LANES:(j + 1) * LANES]

        def scatter_add(i, carry2):
            toks, vals = [], []
            for u in range(unroll):
                m = i * unroll + u
                tok, w = slot_of(m)
                toks.append(tok)
                vals.append(acc[tile(tok), :] + w * ystage[tile(m), :])
            for tok, val in zip(toks, vals):
                acc[tile(tok), :] = val
            return carry2

        lax.fori_loop(0, ts // unroll, scatter_add, 0)
        return carry

    lax.fori_loop(0, (cnt + ts - 1) // ts, sub_tile, 0)

    @pl.when(e == pl.num_programs(1) - 1)
    def _():
        store = pltpu.make_async_copy(
            acc.at[pl.ds(0, rows), :],
            acc_hbm.at[pl.ds(pl.multiple_of(c * rows, ROW_TILE), rows), :], sems.at[1])
        store.start()
        store.wait()


def moe_fused(hs, pos_local, wts, starts, counts, w_in, w_out, chunk, ts):
    ne, d, f2 = w_in.shape
    n_rows = hs.shape[0]
    nc = n_rows // (chunk * ROW_TILE)
    grid_spec = pltpu.PrefetchScalarGridSpec(
        num_scalar_prefetch=2,
        grid=(nc, ne),
        in_specs=[
            pl.BlockSpec((2 * chunk,), lambda c, e, st, cn: (c,), memory_space=pltpu.SMEM),
            pl.BlockSpec((2 * chunk,), lambda c, e, st, cn: (c,), memory_space=pltpu.SMEM),
            pl.BlockSpec(memory_space=pl.ANY),
            pl.BlockSpec((1, d, f2), lambda c, e, st, cn: (e, 0, 0)),
            pl.BlockSpec((1, f2 // 2, d), lambda c, e, st, cn: (e, 0, 0)),
        ],
        out_specs=pl.BlockSpec(memory_space=pl.ANY),
        scratch_shapes=[
            pltpu.VMEM(((chunk + 1) * ROW_TILE, LANES), F32),
            pltpu.VMEM(((chunk + 1) * ROW_TILE, LANES), F32),
            pltpu.VMEM((ts * ROW_TILE, LANES), F32),
            pltpu.VMEM((ts * ROW_TILE, LANES), F32),
            pltpu.SMEM((2 * chunk,), jnp.int32),
            pltpu.SemaphoreType.DMA((2,)),
            pltpu.VMEM((d, f2), BF16),
            pltpu.VMEM((f2 // 2, d), BF16),
        ],
    )
    return pl.pallas_call(
        functools.partial(_moe_fused_kernel, ts),
        grid_spec=grid_spec,
        out_shape=jax.ShapeDtypeStruct((n_rows, LANES), F32),
        compiler_params=pltpu.CompilerParams(dimension_semantics=("arbitrary", "arbitrary"),
                                             vmem_limit_bytes=MOE_VMEM_LIMIT),
        name="moe_fused",
    )(starts, counts, pos_local, wts, hs, w_in, w_out)


def _finish_kernel(acc_ref, x_ref, o_ref):
    tm = x_ref.shape[0]
    o_ref[...] = x_ref[...] + _from_token_tiles(acc_ref, 0, tm, ROW_TILE)


def moe_finish(acc, x, tm):
    n, d = x.shape
    return pl.pallas_call(
        _finish_kernel,
        grid=(n // tm,),
        in_specs=[pl.BlockSpec((tm * ROW_TILE, LANES), lambda i: (i, 0)),
                  pl.BlockSpec((tm, d), lambda i: (i, 0))],
        out_specs=pl.BlockSpec((tm, d), lambda i: (i, 0)),
        out_shape=jax.ShapeDtypeStruct((n, d), F32),
        compiler_params=_params("parallel"),
        name="moe_finish",
    )(acc, x)


def moe_layer_fused(x, g, wr, br, w_in, w_out):
    n = x.shape[0]
    chunk = min(n, MOE_CHUNK)
    nc = n // chunk
    hs, route, counts = router(x, g, wr, br, MOE_TM, chunk)
    ids = route[:, 0:2].astype(jnp.int32).reshape(nc, chunk, 2)
    cnt = counts.reshape(nc, ROW_TILE, LANES)[:, 0, :N_EXPERTS].astype(jnp.int32)
    starts = jnp.cumsum(cnt, axis=1) - cnt
    sel = ids[..., None] == jnp.arange(N_EXPERTS, dtype=jnp.int32)
    start_of = jnp.sum(jnp.where(sel, starts[:, None, None, :], 0), axis=-1)
    pos_local = start_of + route[:, 4:6].astype(jnp.int32).reshape(nc, chunk, 2)
    acc = moe_fused(hs, pos_local.reshape(-1), route[:, 2:4].reshape(-1), starts.reshape(-1),
                    cnt.reshape(-1), w_in, w_out, chunk, MOE_TS)
    return moe_finish(acc, x, MOE_TM)


def moe_layer(xs, g, wr, br, w_in, w_out, layer, split_rows=None):
    n = sum(x.shape[0] for x in xs)
    nc = -(-n // MOE_CHUNK)
    chunk = n // nc
    assert chunk * nc == n and chunk % MOE_TM == 0
    tpc = (2 * chunk + N_EXPERTS * (MOE_TMG - 1)) // MOE_TMG
    hs, route, counts = router(xs, g, wr, br, MOE_TM, chunk)
    ids = route[:, 0:2].astype(jnp.int32).reshape(nc, chunk, 2)
    cnt = counts.reshape(nc, ROW_TILE, LANES)[:, 0, :N_EXPERTS].astype(jnp.int32)
    padded = (cnt + MOE_TMG - 1) // MOE_TMG * MOE_TMG
    ends = jnp.cumsum(padded, axis=1)
    starts = ends - padded
    sel = ids[..., None] == jnp.arange(N_EXPERTS, dtype=jnp.int32)
    start_of = jnp.sum(jnp.where(sel, starts[:, None, None, :], 0), axis=-1)
    pos_local = start_of + route[:, 4:6].astype(jnp.int32).reshape(nc, chunk, 2)
    tile_row = jnp.arange(tpc, dtype=jnp.int32) * MOE_TMG
    tile_expert = jnp.minimum(jnp.sum(ends[:, None, :] <= tile_row[None, :, None], axis=-1),
                              N_EXPERTS - 1).astype(jnp.int32)
    n_valid = (ends[:, -1] // MOE_TMG).astype(jnp.int32)
    ys = experts(hs, pos_local.reshape(-1), tile_expert.reshape(-1) + layer * N_EXPERTS, n_valid,
                 w_in, w_out, MOE_TMG, chunk, tpc)
    pos = pos_local + (jnp.arange(nc, dtype=jnp.int32) * (tpc * MOE_TMG))[:, None, None]
    return combine(ys, pos.reshape(-1), route, xs, MOE_TM, split_rows)


def _qkv_kernel(x_ref, g_ref, w_ref, qg_ref, kg_ref, seg_ref, q_ref, k_ref, v_ref):
    h = _rms(x_ref[...], g_ref[...]).astype(BF16)
    qkv = jnp.dot(h, w_ref[...], preferred_element_type=F32)
    nq = N_HEADS * HEAD_DIM
    nk = N_KV * HEAD_DIM
    q = qkv[:, :nq]
    k = qkv[:, nq:nq + nk]
    v_ref[...] = qkv[:, nq + nk:]

    def seg_mean_sq(z, seg, split):
        zz = z * z
        hi = zz.astype(BF16)
        ms = jnp.dot(hi, seg, preferred_element_type=F32)
        if split:
            lo = (zz - hi.astype(F32)).astype(BF16)
            ms = ms + jnp.dot(lo, seg, preferred_element_type=F32)
        return ms

    seg = seg_ref[...]
    qn = q * lax.rsqrt(seg_mean_sq(q, seg, False) + RMS_EPS) * qg_ref[...]
    q_ref[...] = (qn * (HEAD_DIM ** -0.5)).astype(BF16)
    kn = k * lax.rsqrt(seg_mean_sq(k, seg[:nk, :nk], True) + RMS_EPS) * kg_ref[...]
    k_ref[...] = kn


def qkv_proj(x, g, w, qg, kg, seg, tm):
    n, d = x.shape
    nq = N_HEADS * HEAD_DIM
    nk = N_KV * HEAD_DIM
    return pl.pallas_call(
        _qkv_kernel,
        grid=(n // tm,),
        in_specs=[pl.BlockSpec((tm, d), lambda i: (i, 0)), _full((1, d)),
                  _full((d, nq + 2 * nk)), _full((1, nq)), _full((1, nk)), _full((nq, nq))],
        out_specs=[pl.BlockSpec((tm, nq), lambda i: (i, 0)),
                   pl.BlockSpec((tm, nk), lambda i: (i, 0)),
                   pl.BlockSpec((tm, nk), lambda i: (i, 0))],
        out_shape=[jax.ShapeDtypeStruct((n, nq), BF16),
                   jax.ShapeDtypeStruct((n, nk), F32),
                   jax.ShapeDtypeStruct((n, nk), F32)],
        compiler_params=_params("parallel"),
        name="qkv_proj",
    )(x, g, w, qg, kg, seg)


def _attn_prompt_kernel(q_ref, kc_ref, kp_ref, vc_ref, vp_ref, tbl_ref, sink_ref, x_ref,
                        wo_ref, o_ref, cat_ref):
    n = pl.program_id(1)
    first = jnp.where(n == 0, NEG_INF, 0.0).astype(F32)
    kk = jnp.concatenate([kp_ref[...], kc_ref[...]], axis=0).astype(BF16)
    vv = jnp.concatenate([vp_ref[...], vc_ref[...]], axis=0).astype(BF16)
    col = lax.broadcasted_iota(jnp.int32, (WINDOW, 2 * WINDOW), 1)
    prev_mask = jnp.where(col < WINDOW, first, 0.0)
    for h in range(N_HEADS):
        kh = h // GROUP
        qh = q_ref[:, h * HEAD_DIM:(h + 1) * HEAD_DIM]
        kslice = kk[:, kh * HEAD_DIM:(kh + 1) * HEAD_DIM]
        vslice = vv[:, kh * HEAD_DIM:(kh + 1) * HEAD_DIM]
        s = lax.dot_general(qh, kslice, (((1,), (1,)), ((), ())), preferred_element_type=F32)
        logits = s + tbl_ref[h] + prev_mask
        sink = sink_ref[h]
        m = jnp.maximum(jnp.max(logits, axis=-1, keepdims=True), sink)
        e = jnp.exp(logits - m)
        denom = jnp.sum(e, axis=-1, keepdims=True) + jnp.exp(sink - m)
        p = e * (1.0 / denom)
        cat_ref[:, h * HEAD_DIM:(h + 1) * HEAD_DIM] = jnp.dot(
            p.astype(BF16), vslice, preferred_element_type=F32).astype(BF16)
    o_ref[...] = x_ref[...] + jnp.dot(cat_ref[...], wo_ref[...], preferred_element_type=F32)


def attn_prompt(q, k, v, tbl, sinks, x, wo, batch, seq):
    n, d = x.shape
    nb = seq // WINDOW
    nk = N_KV * HEAD_DIM
    cur = lambda b, i: (b * nb + i, 0)
    prev = lambda b, i: (b * nb + jnp.maximum(i - 1, 0), 0)
    return pl.pallas_call(
        _attn_prompt_kernel,
        grid=(batch, nb),
        in_specs=[
            pl.BlockSpec((WINDOW, d), cur),
            pl.BlockSpec((WINDOW, nk), cur), pl.BlockSpec((WINDOW, nk), prev),
            pl.BlockSpec((WINDOW, nk), cur), pl.BlockSpec((WINDOW, nk), prev),
            _full((N_HEADS, WINDOW, 2 * WINDOW)),
            pl.BlockSpec(memory_space=pltpu.SMEM),
            pl.BlockSpec((WINDOW, d), cur),
            _full((d, d)),
        ],
        out_specs=pl.BlockSpec((WINDOW, d), cur),
        out_shape=jax.ShapeDtypeStruct((n, d), F32),
        input_output_aliases={7: 0},
        scratch_shapes=[pltpu.VMEM((WINDOW, d), BF16)],
        compiler_params=_params("parallel", "parallel"),
        name="attn_prompt",
    )(q, k, k, v, v, tbl, sinks, x, wo)


def _attn_sample_kernel(q_ref, kn_ref, vn_ref, ck_ref, cv_ref, tblc_ref, tbln_ref, sink_ref, o_ref):
    bs = q_ref.shape[0]
    nt = (((1,), (1,)), ((), ()))
    for b in range(bs):
        ck = ck_ref[b].astype(BF16)
        cv = cv_ref[b].astype(BF16)
        kn = kn_ref[b].astype(BF16)
        vn = vn_ref[b].astype(BF16)
        for kh in range(N_KV):
            lanes = slice(kh * HEAD_DIM, (kh + 1) * HEAD_DIM)
            qt = q_ref[b, kh]
            sc = lax.dot_general(qt, ck[:, lanes], nt, preferred_element_type=F32) + tblc_ref[kh]
            sn = lax.dot_general(qt, kn[:, lanes], nt, preferred_element_type=F32) + tbln_ref[kh]
            sink = sink_ref[kh]
            m = jnp.maximum(jnp.maximum(jnp.max(sc, axis=-1, keepdims=True),
                                        jnp.max(sn, axis=-1, keepdims=True)), sink)
            ec = jnp.exp(sc - m)
            en = jnp.exp(sn - m)
            denom = (jnp.sum(ec, axis=-1, keepdims=True) + jnp.sum(en, axis=-1, keepdims=True)
                     + jnp.exp(sink - m))
            out = (jnp.dot(ec.astype(BF16), cv[:, lanes], preferred_element_type=F32)
                   + jnp.dot(en.astype(BF16), vn[:, lanes], preferred_element_type=F32))
            o_ref[b, kh] = (out * (1.0 / denom)).astype(BF16)


def attn_sample(q4, kn, vn, ck, cv, tbl, sink, bs):
    nb = q4.shape[0]
    ts = kn.shape[1]
    nk = N_KV * HEAD_DIM
    tg = ts * GROUP
    kt = WINDOW + ts
    return pl.pallas_call(
        _attn_sample_kernel,
        grid=(nb // bs,),
        in_specs=[
            pl.BlockSpec((bs, N_KV, tg, HEAD_DIM), lambda i: (i, 0, 0, 0)),
            pl.BlockSpec((bs, ts, nk), lambda i: (i, 0, 0)),
            pl.BlockSpec((bs, ts, nk), lambda i: (i, 0, 0)),
            pl.BlockSpec((bs, WINDOW, nk), lambda i: (i, 0, 0)),
            pl.BlockSpec((bs, WINDOW, nk), lambda i: (i, 0, 0)),
            _full((N_KV, tg, WINDOW)),
            _full((N_KV, tg, ts)),
            _full((N_KV, tg, 1)),
        ],
        out_specs=pl.BlockSpec((bs, N_KV, tg, HEAD_DIM), lambda i: (i, 0, 0, 0)),
        out_shape=jax.ShapeDtypeStruct((nb, N_KV, tg, HEAD_DIM), BF16),
        compiler_params=_params("parallel"),
        name="attn_sample",
    )(q4, kn, vn, ck, cv, tbl[:, :, :WINDOW], tbl[:, :, WINDOW:], sink)


def _proj_res_kernel(a_ref, w_ref, x_ref, o_ref):
    o_ref[...] = x_ref[...] + jnp.dot(a_ref[...], w_ref[...], preferred_element_type=F32)


def proj_residual(a, w, x, tm):
    n, d = x.shape
    na, kdim = a.shape
    off = (n - na) // tm
    return pl.pallas_call(
        _proj_res_kernel,
        grid=(na // tm,),
        in_specs=[pl.BlockSpec((tm, kdim), lambda i: (i, 0)), _full((kdim, d)),
                  pl.BlockSpec((tm, d), lambda i: (off + i, 0))],
        out_specs=pl.BlockSpec((tm, d), lambda i: (off + i, 0)),
        out_shape=jax.ShapeDtypeStruct((n, d), F32),
        input_output_aliases={2: 0},
        compiler_params=_params("parallel"),
        name="proj_residual",
    )(a, w, x)


def _t5_bucket_np(dist):
    n = np.maximum(dist, 0)
    max_exact = N_BUCKETS // 2
    large = max_exact + (np.log(np.maximum(n, 1).astype(np.float32) / max_exact)
                         / math.log(MAX_DISTANCE / max_exact) * (N_BUCKETS - max_exact)).astype(np.int32)
    large = np.minimum(large, N_BUCKETS - 1)
    return np.where(n < max_exact, n, large)


def _bias_table(rel_bias, dist):
    valid = (dist >= 0) & (dist <= WINDOW)
    onehot = (np.asarray(_t5_bucket_np(dist))[..., None] == np.arange(N_BUCKETS)).astype(np.float32)
    b = jnp.einsum("qkb,bh->hqk", jnp.asarray(onehot), rel_bias.astype(F32),
                   precision=lax.Precision.HIGHEST)
    return jnp.where(jnp.asarray(valid)[None], b, NEG_INF)


def kernel(x_prompt, x_sample, state_conv, cache_swa_k, cache_swa_v, rms_mix_g, rms_ffn_g, conv_w_in, conv_dw_w, conv_dw_b, conv_ln_g, conv_ln_b, conv_w_out, attn_w_qkv, attn_q_norm_g, attn_k_norm_g, attn_sinks, attn_w_o, rel_bias, router_group_w, router_group_b, router_expert_w, router_expert_b, expert_w_in, expert_w_out):
    batch, seq, d = x_prompt.shape
    nsb, ts, _ = x_sample.shape
    xp = x_prompt.reshape(batch * seq, d)
    xs = x_sample.reshape(nsb * ts, d)
    row = lambda a: a.reshape(1, -1).astype(F32)

    def router_w(i):
        we = jnp.transpose(router_expert_w[i], (1, 0, 2)).reshape(d, N_EXPERTS)
        wr = jnp.concatenate([we, router_group_w[i]], axis=1)
        wr = jnp.pad(wr, ((0, 0), (0, LANES - wr.shape[1])))
        br = jnp.concatenate([router_expert_b[i].reshape(-1), router_group_b[i]])
        br = jnp.pad(br, (0, LANES - br.shape[0])).reshape(1, LANES)
        return wr.astype(F32), br.astype(F32)

    g0 = row(rms_mix_g[0])
    w_in = conv_w_in[0].astype(BF16)
    dww = jnp.pad(conv_dw_w[0].astype(F32), ((0, HALO - CONV_WIDTH), (0, 0)))
    dwb, lng, lnb = row(conv_dw_b[0]), row(conv_ln_g[0]), row(conv_ln_b[0])
    w_out = conv_w_out[0].astype(BF16)
    up = glu_proj(xp, g0, w_in, 512)
    us = glu_proj(xs, g0, w_in, 512)
    n_p, n_s = batch * seq, nsb * ts
    n_all = n_p + n_s
    xp = conv_prompt(up, xp, dww, dwb, lng, lnb, w_out, batch, seq, 256)
    us3 = us.reshape(nsb, ts, -1)
    xs = conv_sample(us3, state_conv[0], xs, dww, dwb, lng, lnb, w_out, 32)
    conv_p = up.reshape(batch, seq, -1)[:, seq - PAST:]
    conv_s = jnp.concatenate([state_conv[0], us3], axis=1)[:, ts:]

    wr0, br0 = router_w(0)
    ew_in = expert_w_in.reshape((DEPTH * N_EXPERTS,) + expert_w_in.shape[2:])
    ew_out = expert_w_out.reshape((DEPTH * N_EXPERTS,) + expert_w_out.shape[2:])
    x = moe_layer((xp, xs), row(rms_ffn_g[0]), wr0, br0, ew_in, ew_out, 0)

    g1 = row(rms_mix_g[1])
    w_qkv = attn_w_qkv[0].astype(BF16)
    qg = jnp.tile(attn_q_norm_g[0].astype(F32), N_HEADS).reshape(1, -1)
    kg = jnp.tile(attn_k_norm_g[0].astype(F32), N_KV).reshape(1, -1)
    nq = N_HEADS * HEAD_DIM
    seg = jnp.asarray(np.kron(np.eye(N_HEADS), np.ones((HEAD_DIM, HEAD_DIM))) / HEAD_DIM, BF16)
    w_o = attn_w_o[0].astype(BF16)
    sinks = attn_sinks[0].astype(F32)

    q, k, v = qkv_proj(x, g1, w_qkv, qg, kg, seg, 512)
    kp, vp = k[:n_p], v[:n_p]
    qs, ks, vs = q[n_p:], k[n_p:], v[n_p:]

    q_off = np.arange(WINDOW)[:, None]
    dist_p = q_off + WINDOW - np.arange(2 * WINDOW)[None, :]
    tbl_p = _bias_table(rel_bias, dist_p)
    x_attn = attn_prompt(q, k, v, tbl_p, sinks, x, w_o, batch, seq)

    kt = WINDOW + ts
    dist_s = np.arange(ts)[:, None] + WINDOW - np.arange(kt)[None, :]
    tbl_s = _bias_table(rel_bias, dist_s)
    tbl_s = jnp.transpose(tbl_s.reshape(N_KV, GROUP, ts, kt), (0, 2, 1, 3)).reshape(N_KV, ts * GROUP, kt)
    sink_s = jnp.tile(sinks.reshape(N_KV, 1, GROUP), (1, ts, 1)).reshape(N_KV, ts * GROUP, 1)
    q4 = jnp.transpose(qs.reshape(nsb, ts, N_KV, GROUP, HEAD_DIM), (0, 2, 1, 3, 4))
    q4 = q4.reshape(nsb, N_KV, ts * GROUP, HEAD_DIM)
    nk = N_KV * HEAD_DIM
    ks3, vs3 = ks.reshape(nsb, ts, nk), vs.reshape(nsb, ts, nk)
    ck = cache_swa_k[0].reshape(nsb, WINDOW, nk)
    cv = cache_swa_v[0].reshape(nsb, WINDOW, nk)
    o4 = attn_sample(q4, ks3, vs3, ck, cv, tbl_s, sink_s, 16)
    os_ = jnp.transpose(o4.reshape(nsb, N_KV, ts, GROUP, HEAD_DIM), (0, 2, 1, 3, 4)).reshape(nsb * ts, nq)
    x = proj_residual(os_, w_o, x_attn, 512)

    k_p = kp.reshape(batch, seq, N_KV, HEAD_DIM)[:, seq - WINDOW:]
    v_p = vp.reshape(batch, seq, N_KV, HEAD_DIM)[:, seq - WINDOW:]
    k_s = jnp.concatenate([cache_swa_k[0], ks.reshape(nsb, ts, N_KV, HEAD_DIM)], axis=1)[:, ts:]
    v_s = jnp.concatenate([cache_swa_v[0], vs.reshape(nsb, ts, N_KV, HEAD_DIM)], axis=1)[:, ts:]

    wr1, br1 = router_w(1)
    xp, xs = moe_layer((x,), row(rms_ffn_g[1]), wr1, br1, ew_in, ew_out, 1, split_rows=n_p)

    return (xp.reshape(batch, seq, d), xs.reshape(nsb, ts, d),
            conv_p[None], conv_s[None], k_p[None], v_p[None], k_s[None], v_s[None])
```

```python
import functools
import math

import numpy as np
import jax
import jax.numpy as jnp
from jax import lax
from jax.experimental import pallas as pl
from jax.experimental.pallas import tpu as pltpu

D_MODEL = 1024
DEPTH = 2
CONV_WIDTH = 31
PAST = CONV_WIDTH - 1
HEAD_DIM = 64
N_HEADS = 16
N_KV = 2
GROUP = 8
WINDOW = 128
N_BUCKETS = 32
MAX_DISTANCE = 128
N_GROUPS = 4
EPG = 8
N_EXPERTS = 32
D_EXPERT = 256
RMS_EPS = 1e-6
LN_EPS = 1e-5
NEG_INF = -1e30

F32 = jnp.float32
BF16 = jnp.bfloat16
LANES = 128
ROW_TILE = 8
MOE_TM = 256
MOE_TMG = 256
MOE_CHUNK = 8704
MOE_TS = 128
MOE_VMEM_LIMIT = 56 * 1024 * 1024
VMEM_LIMIT = 48 * 1024 * 1024


def _params(*sem):
    return pltpu.CompilerParams(dimension_semantics=sem, vmem_limit_bytes=VMEM_LIMIT)


def _rms(x, g):
    return x * lax.rsqrt(jnp.mean(x * x, axis=-1, keepdims=True) + RMS_EPS) * g


def _sigmoid(x):
    return 1.0 / (1.0 + jnp.exp(-x))


def _full(shape):
    return pl.BlockSpec(shape, lambda *_: (0,) * len(shape))


def _glu_kernel(x_ref, g_ref, wa_ref, wg_ref, u_ref):
    h = _rms(x_ref[...], g_ref[...]).astype(BF16)
    a = jnp.dot(h, wa_ref[...], preferred_element_type=F32)
    gate = jnp.dot(h, wg_ref[...], preferred_element_type=F32)
    u_ref[...] = a * _sigmoid(gate)


def glu_proj(x, g, w_in, tm):
    n, d = x.shape
    c = w_in.shape[1] // 2
    return pl.pallas_call(
        _glu_kernel,
        grid=(n // tm,),
        in_specs=[
            pl.BlockSpec((tm, d), lambda i: (i, 0)),
            _full((1, d)),
            pl.BlockSpec((d, c), lambda i: (0, 0)),
            pl.BlockSpec((d, c), lambda i: (0, 1)),
        ],
        out_specs=pl.BlockSpec((tm, c), lambda i: (i, 0)),
        out_shape=jax.ShapeDtypeStruct((n, c), F32),
        compiler_params=_params("parallel"),
        name="glu_proj",
    )(x, g, w_in, w_in)


def _ln_silu_out(y, lng, lnb, wout_ref, x):
    mu = jnp.mean(y, axis=-1, keepdims=True)
    yc = y - mu
    z = yc * lax.rsqrt(jnp.mean(yc * yc, axis=-1, keepdims=True) + LN_EPS) * lng + lnb
    z = z * _sigmoid(z)
    return x + jnp.dot(z.astype(BF16), wout_ref[...], preferred_element_type=F32)


HALO = 32
CONV_RC = 64
CONV_CC = 128


def _conv_prompt_kernel(ucur_ref, uprev_ref, x_ref, dww_ref, dwb_ref, lng_ref, lnb_ref,
                        wout_ref, o_ref, up_ref, y_ref):
    t = pl.program_id(1)
    tt, c = ucur_ref.shape
    keep = (t > 0).astype(F32)
    up_ref[0:HALO, :] = uprev_ref[...] * keep
    up_ref[HALO:HALO + tt, :] = ucur_ref[...]
    up_ref[HALO + tt:, :] = jnp.zeros((ROW_TILE, c), F32)
    off = HALO - PAST
    for r0 in range(0, tt, CONV_RC):
        for c0 in range(0, c, CONV_CC):
            y = jnp.zeros((CONV_RC, CONV_CC), F32) + dwb_ref[:, c0:c0 + CONV_CC]
            for s in range(ROW_TILE):
                v = None
                for q in range((off + CONV_WIDTH - 1) // ROW_TILE + 1):
                    k = ROW_TILE * q + s - off
                    if k < 0 or k >= CONV_WIDTH:
                        continue
                    lo = r0 + ROW_TILE * q
                    term = (up_ref[lo:lo + CONV_RC + ROW_TILE, c0:c0 + CONV_CC]
                            * dww_ref[k:k + 1, c0:c0 + CONV_CC])
                    v = term if v is None else v + term
                y = y + v[s:s + CONV_RC]
            y_ref[r0:r0 + CONV_RC, c0:c0 + CONV_CC] = y
    o_ref[...] = _ln_silu_out(y_ref[...], lng_ref[...], lnb_ref[...], wout_ref, x_ref[...])


def conv_prompt(u, x, dww, dwb, lng, lnb, wout, batch, seq, tt):
    n, c = u.shape
    d = x.shape[1]
    nt = seq // tt
    hb = tt // HALO
    return pl.pallas_call(
        _conv_prompt_kernel,
        grid=(batch, nt),
        in_specs=[
            pl.BlockSpec((tt, c), lambda b, t: (b * nt + t, 0)),
            pl.BlockSpec((HALO, c), lambda b, t: (jnp.maximum((b * nt + t) * hb - 1, 0), 0)),
            pl.BlockSpec((tt, d), lambda b, t: (b * nt + t, 0)),
            _full((HALO, c)), _full((1, c)), _full((1, c)), _full((1, c)),
            _full((c, d)),
        ],
        out_specs=pl.BlockSpec((tt, d), lambda b, t: (b * nt + t, 0)),
        out_shape=jax.ShapeDtypeStruct((n, d), F32),
        scratch_shapes=[pltpu.VMEM((tt + HALO + ROW_TILE, c), F32), pltpu.VMEM((tt, c), F32)],
        compiler_params=_params("parallel", "parallel"),
        name="conv_prompt",
    )(u, u, x, dww, dwb, lng, lnb, wout)


CONV_SB = 4


def _conv_sample_kernel(u_ref, st_ref, x_ref, dww_ref, dwb_ref, lng_ref, lnb_ref,
                        wout_ref, o_ref, up_ref, y_ref):
    bs, ts, c = u_ref.shape
    up_ref[:, 0:PAST, :] = st_ref[...]
    up_ref[:, PAST:PAST + ts, :] = u_ref[...]
    for b0 in range(0, bs, CONV_SB):
        acc = jnp.zeros((CONV_SB, ts, c), F32) + dwb_ref[...][None]
        for k in range(CONV_WIDTH):
            acc = acc + up_ref[b0:b0 + CONV_SB, k:k + ts, :] * dww_ref[k:k + 1, :][None]
        y_ref[b0 * ts:(b0 + CONV_SB) * ts, :] = acc.reshape(CONV_SB * ts, c)
    o_ref[...] = _ln_silu_out(y_ref[...], lng_ref[...], lnb_ref[...], wout_ref, x_ref[...])


def conv_sample(u3, state, x, dww, dwb, lng, lnb, wout, bs):
    nb, ts, c = u3.shape
    d = x.shape[1]
    return pl.pallas_call(
        _conv_sample_kernel,
        grid=(nb // bs,),
        in_specs=[
            pl.BlockSpec((bs, ts, c), lambda i: (i, 0, 0)),
            pl.BlockSpec((bs, PAST, c), lambda i: (i, 0, 0)),
            pl.BlockSpec((bs * ts, d), lambda i: (i, 0)),
            _full((HALO, c)), _full((1, c)), _full((1, c)), _full((1, c)),
            _full((c, d)),
        ],
        out_specs=pl.BlockSpec((bs * ts, d), lambda i: (i, 0)),
        out_shape=jax.ShapeDtypeStruct((nb * ts, d), F32),
        scratch_shapes=[pltpu.VMEM((bs, PAST + ts, c), F32), pltpu.VMEM((bs * ts, c), F32)],
        compiler_params=_params("parallel"),
        name="conv_sample",
    )(u3, state, x, dww, dwb, lng, lnb, wout)


def _token_specs(xs, tm):
    d = xs[0].shape[1]
    if len(xs) == 1:
        return [pl.BlockSpec((tm, d), lambda i: (i, 0))]
    first = xs[0].shape[0] // tm
    return [pl.BlockSpec((tm, d), lambda i: (jnp.minimum(i, first - 1), 0)),
            pl.BlockSpec((tm, d), lambda i: (jnp.maximum(i - first, 0), 0))]


def _read_tokens(x_refs, first_steps):
    if len(x_refs) == 1:
        return x_refs[0][...]
    return jnp.where(pl.program_id(0) < first_steps, x_refs[0][...], x_refs[1][...])


def _router_kernel(tiles_per_chunk, n_src, first_steps, *refs):
    x_refs = refs[:n_src]
    g_ref, wr_ref, br_ref, tri_ref, hs_ref, route_ref, route_t_ref, cnt_ref, carry_ref = refs[n_src:]
    h = _rms(_read_tokens(x_refs, first_steps), g_ref[...])
    tm, d = h.shape
    for j in range(d // LANES):
        hs_ref[pl.ds(j, tm, stride=ROW_TILE), :] = h[:, j * LANES:(j + 1) * LANES]
    h_hi = h.astype(BF16)
    h_lo = (h - h_hi.astype(F32)).astype(BF16)
    both = jnp.dot(h_hi, wr_ref[...], preferred_element_type=F32)
    cross = jnp.dot(h_lo, wr_ref[:, :LANES], preferred_element_type=F32)
    logits = both[:, :LANES] + both[:, LANES:] + cross + br_ref[...]
    lane = lax.broadcasted_iota(jnp.int32, logits.shape, 1).astype(F32)
    big = jnp.float32(LANES)
    is_g = (lane >= N_EXPERTS) & (lane < N_EXPERTS + N_GROUPS)
    gl = jnp.where(is_g, logits, NEG_INF)
    gm = jnp.max(gl, axis=-1, keepdims=True)
    g_sel = jnp.min(jnp.where(gl == gm, lane, big), axis=-1, keepdims=True) - N_EXPERTS
    gate_g = 1.0 / jnp.sum(jnp.where(is_g, jnp.exp(gl - gm), 0.0), axis=-1, keepdims=True)
    lo = g_sel * EPG
    in_grp = (lane >= lo) & (lane < lo + EPG)
    el = jnp.where(in_grp, logits, NEG_INF)
    v1 = jnp.max(el, axis=-1, keepdims=True)
    i1 = jnp.min(jnp.where(el == v1, lane, big), axis=-1, keepdims=True)
    el2 = jnp.where(lane == i1, NEG_INF, el)
    v2 = jnp.max(el2, axis=-1, keepdims=True)
    i2 = jnp.min(jnp.where(el2 == v2, lane, big), axis=-1, keepdims=True)
    e2 = jnp.exp(v2 - v1)
    w1 = gate_g / (1.0 + e2)
    w2 = gate_g * e2 / (1.0 + e2)
    @pl.when(pl.program_id(0) % tiles_per_chunk == 0)
    def _():
        carry_ref[...] = jnp.zeros_like(carry_ref)

    hit1 = lane == i1
    hit2 = lane == i2
    onehot = jnp.where(hit1 | hit2, 1.0, 0.0)
    before = carry_ref[...] + jnp.dot(tri_ref[...], onehot.astype(BF16), preferred_element_type=F32)
    r1 = jnp.sum(jnp.where(hit1, before, 0.0), axis=-1, keepdims=True)
    r2 = jnp.sum(jnp.where(hit2, before, 0.0), axis=-1, keepdims=True)
    carry_ref[...] += jnp.sum(onehot, axis=0, keepdims=True)
    cnt_ref[...] = jnp.broadcast_to(carry_ref[...], cnt_ref.shape)
    route = jnp.where(lane == 0.0, i1, jnp.where(lane == 1.0, i2, jnp.where(
        lane == 2.0, w1, jnp.where(lane == 3.0, w2, jnp.where(
            lane == 4.0, r1, jnp.where(lane == 5.0, r2, 0.0))))))
    route_ref[...] = route
    route_t_ref[...] = jnp.transpose(route)[0:ROW_TILE, :]


def router(xs, g, wr, br, tm, chunk):
    n = sum(x.shape[0] for x in xs)
    d = xs[0].shape[1]
    tpc = chunk // tm
    tri = jnp.asarray(np.tril(np.ones((tm, tm), np.float32), -1), BF16)
    return pl.pallas_call(
        functools.partial(_router_kernel, tpc, len(xs), xs[0].shape[0] // tm),
        grid=(n // tm,),
        in_specs=_token_specs(xs, tm) + [_full((1, d)),
                  _full((d, 2 * LANES)), _full((1, LANES)), _full((tm, tm))],
        out_specs=[pl.BlockSpec((tm * ROW_TILE, LANES), lambda i: (i, 0)),
                   pl.BlockSpec((tm, LANES), lambda i: (i, 0)),
                   pl.BlockSpec((ROW_TILE, tm), lambda i: (i, 0)),
                   pl.BlockSpec((ROW_TILE, LANES), lambda i: (i // tpc, 0))],
        out_shape=[jax.ShapeDtypeStruct((n * ROW_TILE, LANES), F32),
                   jax.ShapeDtypeStruct((n, LANES), F32),
                   jax.ShapeDtypeStruct((n // tm * ROW_TILE, tm), F32),
                   jax.ShapeDtypeStruct((n // chunk * ROW_TILE, LANES), F32)],
        scratch_shapes=[pltpu.VMEM((1, LANES), F32)],
        compiler_params=_params("arbitrary"),
        name="router",
    )(*xs, g, wr, br, tri)


def _plan_kernel(tiles_per_chunk, route_ref, tri_ref, rank_ref, cnt_ref, carry_ref):
    i = pl.program_id(0)

    @pl.when(i % tiles_per_chunk == 0)
    def _():
        carry_ref[...] = jnp.zeros_like(carry_ref)

    route = route_ref[...]
    lane = lax.broadcasted_iota(jnp.int32, route.shape, 1).astype(F32)
    i1 = route[:, 0:1]
    i2 = route[:, 1:2]
    hit1 = lane == i1
    hit2 = lane == i2
    onehot = jnp.where(hit1 | hit2, 1.0, 0.0)
    before = carry_ref[...] + jnp.dot(tri_ref[...], onehot.astype(BF16), preferred_element_type=F32)
    r1 = jnp.sum(jnp.where(hit1, before, 0.0), axis=-1, keepdims=True)
    r2 = jnp.sum(jnp.where(hit2, before, 0.0), axis=-1, keepdims=True)
    rank_ref[...] = jnp.where(lane == 0.0, r1, jnp.where(lane == 1.0, r2, 0.0))
    carry_ref[...] += jnp.sum(onehot, axis=0, keepdims=True)
    cnt_ref[...] = jnp.broadcast_to(carry_ref[...], cnt_ref.shape)


def plan(route, tm, chunk):
    n = route.shape[0]
    tpc = chunk // tm
    tri = jnp.asarray(np.tril(np.ones((tm, tm), np.float32), -1), BF16)
    return pl.pallas_call(
        functools.partial(_plan_kernel, tpc),
        grid=(n // tm,),
        in_specs=[pl.BlockSpec((tm, LANES), lambda i: (i, 0)), _full((tm, tm))],
        out_specs=[pl.BlockSpec((tm, LANES), lambda i: (i, 0)),
                   pl.BlockSpec((ROW_TILE, LANES), lambda i: (i // tpc, 0))],
        out_shape=[jax.ShapeDtypeStruct((n, LANES), F32),
                   jax.ShapeDtypeStruct((n // chunk * ROW_TILE, LANES), F32)],
        scratch_shapes=[pltpu.VMEM((1, LANES), F32)],
        compiler_params=_params("arbitrary"),
        name="moe_plan",
    )(route, tri)


def _gather_rows(idx_ref, src_hbm, dst, sem, n_rows):
    unroll = 8

    def body(c, carry):
        for u in range(unroll):
            m = c * unroll + u
            tok = idx_ref[m]
            pltpu.make_async_copy(
                src_hbm.at[pl.ds(pl.multiple_of(tok * ROW_TILE, ROW_TILE), ROW_TILE), :],
                dst.at[pl.ds(pl.multiple_of(m * ROW_TILE, ROW_TILE), ROW_TILE), :],
                sem).start()
        return carry

    lax.fori_loop(0, n_rows // unroll, body, 0)


def _wait_rows(src_hbm, dst, sem):
    pltpu.make_async_copy(src_hbm.at[pl.ds(0, dst.shape[0]), :], dst, sem).wait()


def _from_token_tiles(buf, start, n_rows, stride):
    return jnp.concatenate(
        [buf[pl.ds(start + j, n_rows, stride=stride), :] for j in range(ROW_TILE)], axis=-1)


def _experts_kernel(tiles_per_chunk, tmg, te_ref, nv_ref, pos_ref, fill_hbm, hs_hbm,
                    win_ref, wout_ref, ys_ref, hsv, gbuf_a, gbuf_b, src_ref, sems, winb, woutb):
    c = pl.program_id(0)
    j = pl.program_id(1)
    t = c * tiles_per_chunk + j
    nv = nv_ref[c]
    chunk = pos_ref.shape[0] // 2
    unroll = 8

    def tile(r):
        return pl.ds(pl.multiple_of(r * ROW_TILE, ROW_TILE), ROW_TILE)

    @pl.when(j == 0)
    def _():
        load = pltpu.make_async_copy(
            hs_hbm.at[pl.ds(pl.multiple_of(c * chunk * ROW_TILE, ROW_TILE), chunk * ROW_TILE), :],
            hsv.at[pl.ds(0, chunk * ROW_TILE), :], sems.at[0])
        load.start()
        fill = pltpu.make_async_copy(fill_hbm, src_ref, sems.at[1])
        fill.start()
        hsv[pl.ds(chunk * ROW_TILE, ROW_TILE), :] = jnp.zeros((ROW_TILE, LANES), F32)
        fill.wait()

        def scatter(i, carry):
            first = i * unroll
            tok0 = first // (2 * MOE_TM) * MOE_TM + first % MOE_TM
            for u in range(unroll):
                src_ref[pos_ref[first + u]] = tok0 + u
            return carry

        lax.fori_loop(0, 2 * chunk // unroll, scatter, 0)
        load.wait()

        def gather(i, carry):
            for u in range(unroll):
                m = i * unroll + u
                gbuf_a[tile(m), :] = hsv[tile(src_ref[m]), :]
            return carry

        lax.fori_loop(0, tmg // unroll, gather, 0)

    changed = jnp.logical_or(j == 0, te_ref[t] != te_ref[jnp.maximum(t - 1, 0)])

    @pl.when(jnp.logical_and(changed, j < nv))
    def _():
        winb[...] = win_ref[0].astype(BF16)
        woutb[...] = wout_ref[0].astype(BF16)

    def step(cur, nxt):
        base = jnp.minimum(j + 1, tiles_per_chunk - 1) * tmg
        for m in range(tmg):
            nxt[pl.ds(m * ROW_TILE, ROW_TILE), :] = hsv[tile(src_ref[base + m]), :]
        x = _from_token_tiles(cur, 0, tmg, ROW_TILE).astype(BF16)
        hid = jnp.dot(x, winb[...], preferred_element_type=F32)
        a = hid[:, :D_EXPERT]
        u = hid[:, D_EXPERT:]
        act = (a * _sigmoid(a) * u).astype(BF16)
        y = jnp.dot(act, woutb[...], preferred_element_type=F32)
        for k in range(ROW_TILE):
            ys_ref[pl.ds(k, tmg, stride=ROW_TILE), :] = y[:, k * LANES:(k + 1) * LANES]

    @pl.when(jnp.logical_and(j < nv, j % 2 == 0))
    def _():
        step(gbuf_a, gbuf_b)

    @pl.when(jnp.logical_and(j < nv, j % 2 == 1))
    def _():
        step(gbuf_b, gbuf_a)

    @pl.when(j >= nv)
    def _():
        ys_ref[...] = jnp.zeros_like(ys_ref)


def experts(hs, pos_local, tile_expert, n_valid, w_in, w_out, tmg, chunk, tiles_per_chunk):
    _, d, f2 = w_in.shape
    nc = hs.shape[0] // (chunk * ROW_TILE)
    rows_per_chunk = tiles_per_chunk * tmg
    fill = jnp.full((rows_per_chunk,), chunk, jnp.int32)
    grid_spec = pltpu.PrefetchScalarGridSpec(
        num_scalar_prefetch=2,
        grid=(nc, tiles_per_chunk),
        in_specs=[
            pl.BlockSpec((2 * chunk,), lambda c, j, te, nv: (c,), memory_space=pltpu.SMEM),
            pl.BlockSpec(memory_space=pl.ANY),
            pl.BlockSpec(memory_space=pl.ANY),
            pl.BlockSpec((1, d, f2), lambda c, j, te, nv: (te[c * tiles_per_chunk + j], 0, 0)),
            pl.BlockSpec((1, f2 // 2, d), lambda c, j, te, nv: (te[c * tiles_per_chunk + j], 0, 0)),
        ],
        out_specs=pl.BlockSpec((tmg * ROW_TILE, LANES),
                               lambda c, j, te, nv: (c * tiles_per_chunk + j, 0)),
        scratch_shapes=[
            pltpu.VMEM(((chunk + 1) * ROW_TILE, LANES), F32),
            pltpu.VMEM((tmg * ROW_TILE, LANES), F32),
            pltpu.VMEM((tmg * ROW_TILE, LANES), F32),
            pltpu.SMEM((rows_per_chunk,), jnp.int32),
            pltpu.SemaphoreType.DMA((2,)),
            pltpu.VMEM((d, f2), BF16),
            pltpu.VMEM((f2 // 2, d), BF16),
        ],
    )
    return pl.pallas_call(
        functools.partial(_experts_kernel, tiles_per_chunk, tmg),
        grid_spec=grid_spec,
        out_shape=jax.ShapeDtypeStruct((nc * rows_per_chunk * ROW_TILE, LANES), F32),
        compiler_params=pltpu.CompilerParams(dimension_semantics=("arbitrary", "arbitrary"),
                                             vmem_limit_bytes=MOE_VMEM_LIMIT),
        name="moe_experts",
    )(tile_expert, n_valid, pos_local, fill, hs, w_in, w_out)


def _combine_kernel(split_steps, n_src, first_steps, pos_cur_ref, pos_nxt_ref, ys_hbm, route_ref, *rest):
    x_refs, o_refs, (ybuf_a, ybuf_b, sems) = rest[:n_src], rest[n_src:-3], rest[-3:]
    i = pl.program_id(0)
    last = i == pl.num_programs(0) - 1
    tm = route_ref.shape[0]

    @pl.when(i == 0)
    def _():
        _gather_rows(pos_cur_ref, ys_hbm, ybuf_a, sems.at[0], 2 * tm)

    def step(cur, cur_sem, nxt, nxt_sem):
        _gather_rows(pos_nxt_ref, ys_hbm, nxt, nxt_sem, 2 * tm)
        _wait_rows(ys_hbm, cur, cur_sem)
        route = route_ref[...]
        y1 = _from_token_tiles(cur, 0, tm, ROW_TILE)
        y2 = _from_token_tiles(cur, tm * ROW_TILE, tm, ROW_TILE)
        val = _read_tokens(x_refs, first_steps) + route[:, 2:3] * y1 + route[:, 3:4] * y2
        if split_steps is None:
            o_refs[0][...] = val
        else:
            @pl.when(i < split_steps)
            def _():
                o_refs[0][...] = val

            @pl.when(i >= split_steps)
            def _():
                o_refs[1][...] = val

        @pl.when(last)
        def _():
            _wait_rows(ys_hbm, nxt, nxt_sem)

    @pl.when(i % 2 == 0)
    def _():
        step(ybuf_a, sems.at[0], ybuf_b, sems.at[1])

    @pl.when(i % 2 == 1)
    def _():
        step(ybuf_b, sems.at[1], ybuf_a, sems.at[0])


def combine(ys, pos, route, xs, tm, split_rows=None):
    n = sum(x.shape[0] for x in xs)
    d = xs[0].shape[1]
    nsteps = n // tm
    if split_rows is None:
        split_steps = None
        out_specs = pl.BlockSpec((tm, d), lambda i: (i, 0))
        out_shape = jax.ShapeDtypeStruct((n, d), F32)
    else:
        split_steps = split_rows // tm
        out_specs = [pl.BlockSpec((tm, d), lambda i: (jnp.minimum(i, split_steps - 1), 0)),
                     pl.BlockSpec((tm, d), lambda i: (jnp.maximum(i - split_steps, 0), 0))]
        out_shape = [jax.ShapeDtypeStruct((split_rows, d), F32),
                     jax.ShapeDtypeStruct((n - split_rows, d), F32)]
    return pl.pallas_call(
        functools.partial(_combine_kernel, split_steps, len(xs), xs[0].shape[0] // tm),
        grid=(nsteps,),
        in_specs=[
            pl.BlockSpec((2 * tm,), lambda i: (i,), memory_space=pltpu.SMEM),
            pl.BlockSpec((2 * tm,), lambda i: (jnp.minimum(i + 1, nsteps - 1),),
                         memory_space=pltpu.SMEM),
            pl.BlockSpec(memory_space=pl.ANY),
            pl.BlockSpec((tm, LANES), lambda i: (i, 0)),
        ] + _token_specs(xs, tm),
        out_specs=out_specs,
        out_shape=out_shape,
        scratch_shapes=[pltpu.VMEM((2 * tm * ROW_TILE, LANES), F32),
                        pltpu.VMEM((2 * tm * ROW_TILE, LANES), F32),
                        pltpu.SemaphoreType.DMA((2,))],
        compiler_params=_params("arbitrary"),
        name="moe_combine",
    )(pos, pos, ys, route, *xs)


def _moe_fused_kernel(ts, st_ref, cn_ref, pos_ref, wts_ref, hs_hbm, win_ref, wout_ref, acc_hbm,
                      hsv, acc, gbuf, ystage, src_ref, sems, winb, woutb):
    c = pl.program_id(0)
    e = pl.program_id(1)
    n_slots = pos_ref.shape[0]
    chunk = n_slots // 2
    rows = chunk * ROW_TILE
    unroll = 8

    def tile(r):
        return pl.ds(pl.multiple_of(r * ROW_TILE, ROW_TILE), ROW_TILE)

    @pl.when(e == 0)
    def _():
        load = pltpu.make_async_copy(
            hs_hbm.at[pl.ds(pl.multiple_of(c * rows, ROW_TILE), rows), :],
            hsv.at[pl.ds(0, rows), :], sems.at[0])
        load.start()
        hsv[pl.ds(rows, ROW_TILE), :] = jnp.zeros((ROW_TILE, LANES), F32)

        def zero(i, carry):
            acc[pl.ds(pl.multiple_of(i * 64, 64), 64), :] = jnp.zeros((64, LANES), F32)
            return carry

        lax.fori_loop(0, (rows + ROW_TILE) // 64, zero, 0)
        acc[pl.ds(rows + ROW_TILE - 64, 64), :] = jnp.zeros((64, LANES), F32)

        def scatter(i, carry):
            for u in range(unroll):
                s = i * unroll + u
                src_ref[pos_ref[s]] = s
            return carry

        lax.fori_loop(0, n_slots // unroll, scatter, 0)
        load.wait()

    winb[...] = win_ref[0].astype(BF16)
    woutb[...] = wout_ref[0].astype(BF16)
    start = st_ref[c * N_EXPERTS + e]
    cnt = cn_ref[c * N_EXPERTS + e]

    def sub_tile(k, carry):
        base = start + k * ts
        rem = cnt - k * ts

        def slot_of(m):
            s = src_ref[jnp.minimum(base + m, n_slots - 1)]
            valid = m < rem
            return jnp.where(valid, s >> 1, chunk), jnp.where(valid, wts_ref[s], 0.0)

        def gather(i, carry2):
            for u in range(unroll):
                m = i * unroll + u
                tok, _ = slot_of(m)
                gbuf[tile(m), :] = hsv[tile(tok), :]
            return carry2

        lax.fori_loop(0, ts // unroll, gather, 0)
        x = _from_token_tiles(gbuf, 0, ts, ROW_TILE).astype(BF16)
        hid = jnp.dot(x, winb[...], preferred_element_type=F32)
        a = hid[:, :D_EXPERT]
        u_ = hid[:, D_EXPERT:]
        act = (a * _sigmoid(a) * u_).astype(BF16)
        y = jnp.dot(act, woutb[...], preferred_element_type=F32)
        for j in range(ROW_TILE):
            ystage[pl.ds(j, ts, stride=ROW_TILE), :] = y[:, j * LANES:(j + 1) * LANES]

        def scatter_add(i, carry2):
            toks, vals = [], []
            for u in range(unroll):
                m = i * unroll + u
                tok, w = slot_of(m)
                toks.append(tok)
                vals.append(acc[tile(tok), :] + w * ystage[tile(m), :])
            for tok, val in zip(toks, vals):
                acc[tile(tok), :] = val
            return carry2

        lax.fori_loop(0, ts // unroll, scatter_add, 0)
        return carry

    lax.fori_loop(0, (cnt + ts - 1) // ts, sub_tile, 0)

    @pl.when(e == pl.num_programs(1) - 1)
    def _():
        store = pltpu.make_async_copy(
            acc.at[pl.ds(0, rows), :],
            acc_hbm.at[pl.ds(pl.multiple_of(c * rows, ROW_TILE), rows), :], sems.at[1])
        store.start()
        store.wait()


def moe_fused(hs, pos_local, wts, starts, counts, w_in, w_out, chunk, ts):
    ne, d, f2 = w_in.shape
    n_rows = hs.shape[0]
    nc = n_rows // (chunk * ROW_TILE)
    grid_spec = pltpu.PrefetchScalarGridSpec(
        num_scalar_prefetch=2,
        grid=(nc, ne),
        in_specs=[
            pl.BlockSpec((2 * chunk,), lambda c, e, st, cn: (c,), memory_space=pltpu.SMEM),
            pl.BlockSpec((2 * chunk,), lambda c, e, st, cn: (c,), memory_space=pltpu.SMEM),
            pl.BlockSpec(memory_space=pl.ANY),
            pl.BlockSpec((1, d, f2), lambda c, e, st, cn: (e, 0, 0)),
            pl.BlockSpec((1, f2 // 2, d), lambda c, e, st, cn: (e, 0, 0)),
        ],
        out_specs=pl.BlockSpec(memory_space=pl.ANY),
        scratch_shapes=[
            pltpu.VMEM(((chunk + 1) * ROW_TILE, LANES), F32),
            pltpu.VMEM(((chunk + 1) * ROW_TILE, LANES), F32),
            pltpu.VMEM((ts * ROW_TILE, LANES), F32),
            pltpu.VMEM((ts * ROW_TILE, LANES), F32),
            pltpu.SMEM((2 * chunk,), jnp.int32),
            pltpu.SemaphoreType.DMA((2,)),
            pltpu.VMEM((d, f2), BF16),
            pltpu.VMEM((f2 // 2, d), BF16),
        ],
    )
    return pl.pallas_call(
        functools.partial(_moe_fused_kernel, ts),
        grid_spec=grid_spec,
        out_shape=jax.ShapeDtypeStruct((n_rows, LANES), F32),
        compiler_params=pltpu.CompilerParams(dimension_semantics=("arbitrary", "arbitrary"),
                                             vmem_limit_bytes=MOE_VMEM_LIMIT),
        name="moe_fused",
    )(starts, counts, pos_local, wts, hs, w_in, w_out)


def _finish_kernel(acc_ref, x_ref, o_ref):
    tm = x_ref.shape[0]
    o_ref[...] = x_ref[...] + _from_token_tiles(acc_ref, 0, tm, ROW_TILE)


def moe_finish(acc, x, tm):
    n, d = x.shape
    return pl.pallas_call(
        _finish_kernel,
        grid=(n // tm,),
        in_specs=[pl.BlockSpec((tm * ROW_TILE, LANES), lambda i: (i, 0)),
                  pl.BlockSpec((tm, d), lambda i: (i, 0))],
        out_specs=pl.BlockSpec((tm, d), lambda i: (i, 0)),
        out_shape=jax.ShapeDtypeStruct((n, d), F32),
        compiler_params=_params("parallel"),
        name="moe_finish",
    )(acc, x)


def moe_layer_fused(x, g, wr, br, w_in, w_out):
    n = x.shape[0]
    chunk = min(n, MOE_CHUNK)
    nc = n // chunk
    hs, route, counts = router(x, g, wr, br, MOE_TM, chunk)
    ids = route[:, 0:2].astype(jnp.int32).reshape(nc, chunk, 2)
    cnt = counts.reshape(nc, ROW_TILE, LANES)[:, 0, :N_EXPERTS].astype(jnp.int32)
    starts = jnp.cumsum(cnt, axis=1) - cnt
    sel = ids[..., None] == jnp.arange(N_EXPERTS, dtype=jnp.int32)
    start_of = jnp.sum(jnp.where(sel, starts[:, None, None, :], 0), axis=-1)
    pos_local = start_of + route[:, 4:6].astype(jnp.int32).reshape(nc, chunk, 2)
    acc = moe_fused(hs, pos_local.reshape(-1), route[:, 2:4].reshape(-1), starts.reshape(-1),
                    cnt.reshape(-1), w_in, w_out, chunk, MOE_TS)
    return moe_finish(acc, x, MOE_TM)


def moe_layer(xs, g, wr, br, w_in, w_out, layer, split_rows=None):
    n = sum(x.shape[0] for x in xs)
    nc = -(-n // MOE_CHUNK)
    chunk = n // nc
    assert chunk * nc == n and chunk % MOE_TM == 0
    tpc = (2 * chunk + N_EXPERTS * (MOE_TMG - 1)) // MOE_TMG
    hs, route, route_t, counts = router(xs, g, wr, br, MOE_TM, chunk)
    rt = route_t.reshape(nc, chunk // MOE_TM, ROW_TILE, MOE_TM)
    ids = rt[:, :, 0:2, :].astype(jnp.int32)
    ranks = rt[:, :, 4:6, :].astype(jnp.int32)
    cnt = counts.reshape(nc, ROW_TILE, LANES)[:, 0, :N_EXPERTS].astype(jnp.int32)
    padded = (cnt + MOE_TMG - 1) // MOE_TMG * MOE_TMG
    ends = jnp.cumsum(padded, axis=1)
    starts = ends - padded
    start_of = jnp.zeros_like(ids)
    for e in range(N_EXPERTS):
        start_of = jnp.where(ids == e, starts[:, e][:, None, None, None], start_of)
    pos_local = start_of + ranks
    tile_row = jnp.arange(tpc, dtype=jnp.int32) * MOE_TMG
    tile_expert = jnp.minimum(jnp.sum(ends[:, None, :] <= tile_row[None, :, None], axis=-1),
                              N_EXPERTS - 1).astype(jnp.int32)
    n_valid = (ends[:, -1] // MOE_TMG).astype(jnp.int32)
    ys = experts(hs, pos_local.reshape(-1), tile_expert.reshape(-1) + layer * N_EXPERTS, n_valid,
                 w_in, w_out, MOE_TMG, chunk, tpc)
    pos = pos_local + (jnp.arange(nc, dtype=jnp.int32) * (tpc * MOE_TMG))[:, None, None, None]
    return combine(ys, pos.reshape(-1), route, xs, MOE_TM, split_rows)


def _qkv_kernel(x_ref, g_ref, w_ref, qg_ref, kg_ref, seg_ref, q_ref, k_ref, v_ref):
    h = _rms(x_ref[...], g_ref[...]).astype(BF16)
    qkv = jnp.dot(h, w_ref[...], preferred_element_type=F32)
    nq = N_HEADS * HEAD_DIM
    nk = N_KV * HEAD_DIM
    q = qkv[:, :nq]
    k = qkv[:, nq:nq + nk]
    v_ref[...] = qkv[:, nq + nk:]

    def seg_mean_sq(z, seg, split):
        zz = z * z
        hi = zz.astype(BF16)
        ms = jnp.dot(hi, seg, preferred_element_type=F32)
        if split:
            lo = (zz - hi.astype(F32)).astype(BF16)
            ms = ms + jnp.dot(lo, seg, preferred_element_type=F32)
        return ms

    seg = seg_ref[...]
    qn = q * lax.rsqrt(seg_mean_sq(q, seg, False) + RMS_EPS) * qg_ref[...]
    q_ref[...] = (qn * (HEAD_DIM ** -0.5)).astype(BF16)
    kn = k * lax.rsqrt(seg_mean_sq(k, seg[:nk, :nk], True) + RMS_EPS) * kg_ref[...]
    k_ref[...] = kn


def qkv_proj(x, g, w, qg, kg, seg, tm):
    n, d = x.shape
    nq = N_HEADS * HEAD_DIM
    nk = N_KV * HEAD_DIM
    return pl.pallas_call(
        _qkv_kernel,
        grid=(n // tm,),
        in_specs=[pl.BlockSpec((tm, d), lambda i: (i, 0)), _full((1, d)),
                  _full((d, nq + 2 * nk)), _full((1, nq)), _full((1, nk)), _full((nq, nq))],
        out_specs=[pl.BlockSpec((tm, nq), lambda i: (i, 0)),
                   pl.BlockSpec((tm, nk), lambda i: (i, 0)),
                   pl.BlockSpec((tm, nk), lambda i: (i, 0))],
        out_shape=[jax.ShapeDtypeStruct((n, nq), BF16),
                   jax.ShapeDtypeStruct((n, nk), F32),
                   jax.ShapeDtypeStruct((n, nk), F32)],
        compiler_params=_params("parallel"),
        name="qkv_proj",
    )(x, g, w, qg, kg, seg)


def _attn_prompt_kernel(q_ref, kc_ref, kp_ref, vc_ref, vp_ref, tbl_ref, sink_ref, x_ref,
                        wo_ref, o_ref, cat_ref):
    n = pl.program_id(1)
    first = jnp.where(n == 0, NEG_INF, 0.0).astype(F32)
    kk = jnp.concatenate([kp_ref[...], kc_ref[...]], axis=0).astype(BF16)
    vv = jnp.concatenate([vp_ref[...], vc_ref[...]], axis=0).astype(BF16)
    col = lax.broadcasted_iota(jnp.int32, (WINDOW, 2 * WINDOW), 1)
    prev_mask = jnp.where(col < WINDOW, first, 0.0)
    for h in range(N_HEADS):
        kh = h // GROUP
        qh = q_ref[:, h * HEAD_DIM:(h + 1) * HEAD_DIM]
        kslice = kk[:, kh * HEAD_DIM:(kh + 1) * HEAD_DIM]
        vslice = vv[:, kh * HEAD_DIM:(kh + 1) * HEAD_DIM]
        s = lax.dot_general(qh, kslice, (((1,), (1,)), ((), ())), preferred_element_type=F32)
        logits = s + tbl_ref[h] + prev_mask
        sink = sink_ref[h]
        m = jnp.maximum(jnp.max(logits, axis=-1, keepdims=True), sink)
        e = jnp.exp(logits - m)
        denom = jnp.sum(e, axis=-1, keepdims=True) + jnp.exp(sink - m)
        p = e * (1.0 / denom)
        cat_ref[:, h * HEAD_DIM:(h + 1) * HEAD_DIM] = jnp.dot(
            p.astype(BF16), vslice, preferred_element_type=F32).astype(BF16)
    o_ref[...] = x_ref[...] + jnp.dot(cat_ref[...], wo_ref[...], preferred_element_type=F32)


def attn_prompt(q, k, v, tbl, sinks, x, wo, batch, seq):
    n, d = x.shape
    nb = seq // WINDOW
    nk = N_KV * HEAD_DIM
    cur = lambda b, i: (b * nb + i, 0)
    prev = lambda b, i: (b * nb + jnp.maximum(i - 1, 0), 0)
    return pl.pallas_call(
        _attn_prompt_kernel,
        grid=(batch, nb),
        in_specs=[
            pl.BlockSpec((WINDOW, d), cur),
            pl.BlockSpec((WINDOW, nk), cur), pl.BlockSpec((WINDOW, nk), prev),
            pl.BlockSpec((WINDOW, nk), cur), pl.BlockSpec((WINDOW, nk), prev),
            _full((N_HEADS, WINDOW, 2 * WINDOW)),
            pl.BlockSpec(memory_space=pltpu.SMEM),
            pl.BlockSpec((WINDOW, d), cur),
            _full((d, d)),
        ],
        out_specs=pl.BlockSpec((WINDOW, d), cur),
        out_shape=jax.ShapeDtypeStruct((n, d), F32),
        input_output_aliases={7: 0},
        scratch_shapes=[pltpu.VMEM((WINDOW, d), BF16)],
        compiler_params=_params("parallel", "parallel"),
        name="attn_prompt",
    )(q, k, k, v, v, tbl, sinks, x, wo)


def _attn_sample_kernel(q_ref, kn_ref, vn_ref, ck_ref, cv_ref, tblc_ref, tbln_ref, sink_ref, o_ref):
    bs = q_ref.shape[0]
    nt = (((1,), (1,)), ((), ()))
    for b in range(bs):
        ck = ck_ref[b].astype(BF16)
        cv = cv_ref[b].astype(BF16)
        kn = kn_ref[b].astype(BF16)
        vn = vn_ref[b].astype(BF16)
        for kh in range(N_KV):
            lanes = slice(kh * HEAD_DIM, (kh + 1) * HEAD_DIM)
            qt = q_ref[b, kh]
            sc = lax.dot_general(qt, ck[:, lanes], nt, preferred_element_type=F32) + tblc_ref[kh]
            sn = lax.dot_general(qt, kn[:, lanes], nt, preferred_element_type=F32) + tbln_ref[kh]
            sink = sink_ref[kh]
            m = jnp.maximum(jnp.maximum(jnp.max(sc, axis=-1, keepdims=True),
                                        jnp.max(sn, axis=-1, keepdims=True)), sink)
            ec = jnp.exp(sc - m)
            en = jnp.exp(sn - m)
            denom = (jnp.sum(ec, axis=-1, keepdims=True) + jnp.sum(en, axis=-1, keepdims=True)
                     + jnp.exp(sink - m))
            out = (jnp.dot(ec.astype(BF16), cv[:, lanes], preferred_element_type=F32)
                   + jnp.dot(en.astype(BF16), vn[:, lanes], preferred_element_type=F32))
            o_ref[b, kh] = (out * (1.0 / denom)).astype(BF16)


def attn_sample(q4, kn, vn, ck, cv, tbl, sink, bs):
    nb = q4.shape[0]
    ts = kn.shape[1]
    nk = N_KV * HEAD_DIM
    tg = ts * GROUP
    kt = WINDOW + ts
    return pl.pallas_call(
        _attn_sample_kernel,
        grid=(nb // bs,),
        in_specs=[
            pl.BlockSpec((bs, N_KV, tg, HEAD_DIM), lambda i: (i, 0, 0, 0)),
            pl.BlockSpec((bs, ts, nk), lambda i: (i, 0, 0)),
            pl.BlockSpec((bs, ts, nk), lambda i: (i, 0, 0)),
            pl.BlockSpec((bs, WINDOW, nk), lambda i: (i, 0, 0)),
            pl.BlockSpec((bs, WINDOW, nk), lambda i: (i, 0, 0)),
            _full((N_KV, tg, WINDOW)),
            _full((N_KV, tg, ts)),
            _full((N_KV, tg, 1)),
        ],
        out_specs=pl.BlockSpec((bs, N_KV, tg, HEAD_DIM), lambda i: (i, 0, 0, 0)),
        out_shape=jax.ShapeDtypeStruct((nb, N_KV, tg, HEAD_DIM), BF16),
        compiler_params=_params("parallel"),
        name="attn_sample",
    )(q4, kn, vn, ck, cv, tbl[:, :, :WINDOW], tbl[:, :, WINDOW:], sink)


def _proj_res_kernel(a_ref, w_ref, x_ref, o_ref):
    o_ref[...] = x_ref[...] + jnp.dot(a_ref[...], w_ref[...], preferred_element_type=F32)


def proj_residual(a, w, x, tm):
    n, d = x.shape
    na, kdim = a.shape
    off = (n - na) // tm
    return pl.pallas_call(
        _proj_res_kernel,
        grid=(na // tm,),
        in_specs=[pl.BlockSpec((tm, kdim), lambda i: (i, 0)), _full((kdim, d)),
                  pl.BlockSpec((tm, d), lambda i: (off + i, 0))],
        out_specs=pl.BlockSpec((tm, d), lambda i: (off + i, 0)),
        out_shape=jax.ShapeDtypeStruct((n, d), F32),
        input_output_aliases={2: 0},
        compiler_params=_params("parallel"),
        name="proj_residual",
    )(a, w, x)


def _t5_bucket_np(dist):
    n = np.maximum(dist, 0)
    max_exact = N_BUCKETS // 2
    large = max_exact + (np.log(np.maximum(n, 1).astype(np.float32) / max_exact)
                         / math.log(MAX_DISTANCE / max_exact) * (N_BUCKETS - max_exact)).astype(np.int32)
    large = np.minimum(large, N_BUCKETS - 1)
    return np.where(n < max_exact, n, large)


def _bias_table(rel_bias, dist):
    valid = (dist >= 0) & (dist <= WINDOW)
    onehot = (np.asarray(_t5_bucket_np(dist))[..., None] == np.arange(N_BUCKETS)).astype(np.float32)
    b = jnp.einsum("qkb,bh->hqk", jnp.asarray(onehot), rel_bias.astype(F32),
                   precision=lax.Precision.HIGHEST)
    return jnp.where(jnp.asarray(valid)[None], b, NEG_INF)


def kernel(x_prompt, x_sample, state_conv, cache_swa_k, cache_swa_v, rms_mix_g, rms_ffn_g, conv_w_in, conv_dw_w, conv_dw_b, conv_ln_g, conv_ln_b, conv_w_out, attn_w_qkv, attn_q_norm_g, attn_k_norm_g, attn_sinks, attn_w_o, rel_bias, router_group_w, router_group_b, router_expert_w, router_expert_b, expert_w_in, expert_w_out):
    batch, seq, d = x_prompt.shape
    nsb, ts, _ = x_sample.shape
    xp = x_prompt.reshape(batch * seq, d)
    xs = x_sample.reshape(nsb * ts, d)
    row = lambda a: a.reshape(1, -1).astype(F32)

    def router_w(i):
        we = jnp.transpose(router_expert_w[i], (1, 0, 2)).reshape(d, N_EXPERTS)
        wr = jnp.concatenate([we, router_group_w[i]], axis=1)
        wr = jnp.pad(wr, ((0, 0), (0, LANES - wr.shape[1])))
        br = jnp.concatenate([router_expert_b[i].reshape(-1), router_group_b[i]])
        br = jnp.pad(br, (0, LANES - br.shape[0])).reshape(1, LANES)
        wr = wr.astype(F32)
        w_hi = wr.astype(BF16)
        w_lo = (wr - w_hi.astype(F32)).astype(BF16)
        return jnp.concatenate([w_hi, w_lo], axis=1), br.astype(F32)

    g0 = row(rms_mix_g[0])
    w_in = conv_w_in[0].astype(BF16)
    dww = jnp.pad(conv_dw_w[0].astype(F32), ((0, HALO - CONV_WIDTH), (0, 0)))
    dwb, lng, lnb = row(conv_dw_b[0]), row(conv_ln_g[0]), row(conv_ln_b[0])
    w_out = conv_w_out[0].astype(BF16)
    up = glu_proj(xp, g0, w_in, 512)
    us = glu_proj(xs, g0, w_in, 512)
    n_p, n_s = batch * seq, nsb * ts
    n_all = n_p + n_s
    xp = conv_prompt(up, xp, dww, dwb, lng, lnb, w_out, batch, seq, 256)
    us3 = us.reshape(nsb, ts, -1)
    xs = conv_sample(us3, state_conv[0], xs, dww, dwb, lng, lnb, w_out, 32)
    conv_p = up.reshape(batch, seq, -1)[:, seq - PAST:]
    conv_s = jnp.concatenate([state_conv[0], us3], axis=1)[:, ts:]

    wr0, br0 = router_w(0)
    ew_in = expert_w_in.reshape((DEPTH * N_EXPERTS,) + expert_w_in.shape[2:])
    ew_out = expert_w_out.reshape((DEPTH * N_EXPERTS,) + expert_w_out.shape[2:])
    x = moe_layer((xp, xs), row(rms_ffn_g[0]), wr0, br0, ew_in, ew_out, 0)

    g1 = row(rms_mix_g[1])
    w_qkv = attn_w_qkv[0].astype(BF16)
    qg = jnp.tile(attn_q_norm_g[0].astype(F32), N_HEADS).reshape(1, -1)
    kg = jnp.tile(attn_k_norm_g[0].astype(F32), N_KV).reshape(1, -1)
    nq = N_HEADS * HEAD_DIM
    seg = jnp.asarray(np.kron(np.eye(N_HEADS), np.ones((HEAD_DIM, HEAD_DIM))) / HEAD_DIM, BF16)
    w_o = attn_w_o[0].astype(BF16)
    sinks = attn_sinks[0].astype(F32)

    q, k, v = qkv_proj(x, g1, w_qkv, qg, kg, seg, 512)
    kp, vp = k[:n_p], v[:n_p]
    qs, ks, vs = q[n_p:], k[n_p:], v[n_p:]

    q_off = np.arange(WINDOW)[:, None]
    dist_p = q_off + WINDOW - np.arange(2 * WINDOW)[None, :]
    tbl_p = _bias_table(rel_bias, dist_p)
    x_attn = attn_prompt(q, k, v, tbl_p, sinks, x, w_o, batch, seq)

    kt = WINDOW + ts
    dist_s = np.arange(ts)[:, None] + WINDOW - np.arange(kt)[None, :]
    tbl_s = _bias_table(rel_bias, dist_s)
    tbl_s = jnp.transpose(tbl_s.reshape(N_KV, GROUP, ts, kt), (0, 2, 1, 3)).reshape(N_KV, ts * GROUP, kt)
    sink_s = jnp.tile(sinks.reshape(N_KV, 1, GROUP), (1, ts, 1)).reshape(N_KV, ts * GROUP, 1)
    q4 = jnp.transpose(qs.reshape(nsb, ts, N_KV, GROUP, HEAD_DIM), (0, 2, 1, 3, 4))
    q4 = q4.reshape(nsb, N_KV, ts * GROUP, HEAD_DIM)
    nk = N_KV * HEAD_DIM
    ks3, vs3 = ks.reshape(nsb, ts, nk), vs.reshape(nsb, ts, nk)
    ck = cache_swa_k[0].reshape(nsb, WINDOW, nk)
    cv = cache_swa_v[0].reshape(nsb, WINDOW, nk)
    o4 = attn_sample(q4, ks3, vs3, ck, cv, tbl_s, sink_s, 16)
    os_ = jnp.transpose(o4.reshape(nsb, N_KV, ts, GROUP, HEAD_DIM), (0, 2, 1, 3, 4)).reshape(nsb * ts, nq)
    x = proj_residual(os_, w_o, x_attn, 512)

    k_p = kp.reshape(batch, seq, N_KV, HEAD_DIM)[:, seq - WINDOW:]
    v_p = vp.reshape(batch, seq, N_KV, HEAD_DIM)[:, seq - WINDOW:]
    k_s = jnp.concatenate([cache_swa_k[0], ks.reshape(nsb, ts, N_KV, HEAD_DIM)], axis=1)[:, ts:]
    v_s = jnp.concatenate([cache_swa_v[0], vs.reshape(nsb, ts, N_KV, HEAD_DIM)], axis=1)[:, ts:]

    wr1, br1 = router_w(1)
    xp, xs = moe_layer((x,), row(rms_ffn_g[1]), wr1, br1, ew_in, ew_out, 1, split_rows=n_p)

    return (xp.reshape(batch, seq, d), xs.reshape(nsb, ts, d),
            conv_p[None], conv_s[None], k_p[None], v_p[None], k_s[None], v_s[None])
```

```python
import functools
import math

import numpy as np
import jax
import jax.numpy as jnp
from jax import lax
from jax.experimental import pallas as pl
from jax.experimental.pallas import tpu as pltpu

D_MODEL = 1024
DEPTH = 2
CONV_WIDTH = 31
PAST = CONV_WIDTH - 1
HEAD_DIM = 64
N_HEADS = 16
N_KV = 2
GROUP = 8
WINDOW = 128
N_BUCKETS = 32
MAX_DISTANCE = 128
N_GROUPS = 4
EPG = 8
N_EXPERTS = 32
D_EXPERT = 256
RMS_EPS = 1e-6
LN_EPS = 1e-5
NEG_INF = -1e30

F32 = jnp.float32
BF16 = jnp.bfloat16
LANES = 128
ROW_TILE = 8
MOE_TM = 256
MOE_TMG = 256
MOE_CHUNK = 8704
MOE_TS = 128
MOE_VMEM_LIMIT = 56 * 1024 * 1024
VMEM_LIMIT = 48 * 1024 * 1024


def _params(*sem):
    return pltpu.CompilerParams(dimension_semantics=sem, vmem_limit_bytes=VMEM_LIMIT)


def _rms(x, g):
    return x * lax.rsqrt(jnp.mean(x * x, axis=-1, keepdims=True) + RMS_EPS) * g


def _sigmoid(x):
    return 1.0 / (1.0 + jnp.exp(-x))


def _full(shape):
    return pl.BlockSpec(shape, lambda *_: (0,) * len(shape))


def _glu_kernel(x_ref, g_ref, wa_ref, wg_ref, u_ref):
    h = _rms(x_ref[...], g_ref[...]).astype(BF16)
    a = jnp.dot(h, wa_ref[...], preferred_element_type=F32)
    gate = jnp.dot(h, wg_ref[...], preferred_element_type=F32)
    u_ref[...] = a * _sigmoid(gate)


def glu_proj(x, g, w_in, tm):
    n, d = x.shape
    c = w_in.shape[1] // 2
    return pl.pallas_call(
        _glu_kernel,
        grid=(n // tm,),
        in_specs=[
            pl.BlockSpec((tm, d), lambda i: (i, 0)),
            _full((1, d)),
            pl.BlockSpec((d, c), lambda i: (0, 0)),
            pl.BlockSpec((d, c), lambda i: (0, 1)),
        ],
        out_specs=pl.BlockSpec((tm, c), lambda i: (i, 0)),
        out_shape=jax.ShapeDtypeStruct((n, c), F32),
        compiler_params=_params("parallel"),
        name="glu_proj",
    )(x, g, w_in, w_in)


def _ln_silu_out(y, lng, lnb, wout_ref, x):
    mu = jnp.mean(y, axis=-1, keepdims=True)
    yc = y - mu
    z = yc * lax.rsqrt(jnp.mean(yc * yc, axis=-1, keepdims=True) + LN_EPS) * lng + lnb
    z = z * _sigmoid(z)
    return x + jnp.dot(z.astype(BF16), wout_ref[...], preferred_element_type=F32)


HALO = 32
CONV_RC = 64
CONV_CC = 128


def _conv_prompt_kernel(ucur_ref, uprev_ref, x_ref, dww_ref, dwb_ref, lng_ref, lnb_ref,
                        wout_ref, o_ref, up_ref, y_ref):
    t = pl.program_id(1)
    tt, c = ucur_ref.shape
    keep = (t > 0).astype(F32)
    up_ref[0:HALO, :] = uprev_ref[...] * keep
    up_ref[HALO:HALO + tt, :] = ucur_ref[...]
    up_ref[HALO + tt:, :] = jnp.zeros((ROW_TILE, c), F32)
    off = HALO - PAST
    for r0 in range(0, tt, CONV_RC):
        for c0 in range(0, c, CONV_CC):
            y = jnp.zeros((CONV_RC, CONV_CC), F32) + dwb_ref[:, c0:c0 + CONV_CC]
            for s in range(ROW_TILE):
                v = None
                for q in range((off + CONV_WIDTH - 1) // ROW_TILE + 1):
                    k = ROW_TILE * q + s - off
                    if k < 0 or k >= CONV_WIDTH:
                        continue
                    lo = r0 + ROW_TILE * q
                    term = (up_ref[lo:lo + CONV_RC + ROW_TILE, c0:c0 + CONV_CC]
                            * dww_ref[k:k + 1, c0:c0 + CONV_CC])
                    v = term if v is None else v + term
                y = y + v[s:s + CONV_RC]
            y_ref[r0:r0 + CONV_RC, c0:c0 + CONV_CC] = y
    o_ref[...] = _ln_silu_out(y_ref[...], lng_ref[...], lnb_ref[...], wout_ref, x_ref[...])


def conv_prompt(u, x, dww, dwb, lng, lnb, wout, batch, seq, tt):
    n, c = u.shape
    d = x.shape[1]
    nt = seq // tt
    hb = tt // HALO
    return pl.pallas_call(
        _conv_prompt_kernel,
        grid=(batch, nt),
        in_specs=[
            pl.BlockSpec((tt, c), lambda b, t: (b * nt + t, 0)),
            pl.BlockSpec((HALO, c), lambda b, t: (jnp.maximum((b * nt + t) * hb - 1, 0), 0)),
            pl.BlockSpec((tt, d), lambda b, t: (b * nt + t, 0)),
            _full((HALO, c)), _full((1, c)), _full((1, c)), _full((1, c)),
            _full((c, d)),
        ],
        out_specs=pl.BlockSpec((tt, d), lambda b, t: (b * nt + t, 0)),
        out_shape=jax.ShapeDtypeStruct((n, d), F32),
        scratch_shapes=[pltpu.VMEM((tt + HALO + ROW_TILE, c), F32), pltpu.VMEM((tt, c), F32)],
        compiler_params=_params("parallel", "parallel"),
        name="conv_prompt",
    )(u, u, x, dww, dwb, lng, lnb, wout)


CONV_SB = 4


def _conv_sample_kernel(u_ref, st_ref, x_ref, dww_ref, dwb_ref, lng_ref, lnb_ref,
                        wout_ref, o_ref, up_ref, y_ref):
    bs, ts, c = u_ref.shape
    up_ref[:, 0:PAST, :] = st_ref[...]
    up_ref[:, PAST:PAST + ts, :] = u_ref[...]
    for b0 in range(0, bs, CONV_SB):
        acc = jnp.zeros((CONV_SB, ts, c), F32) + dwb_ref[...][None]
        for k in range(CONV_WIDTH):
            acc = acc + up_ref[b0:b0 + CONV_SB, k:k + ts, :] * dww_ref[k:k + 1, :][None]
        y_ref[b0 * ts:(b0 + CONV_SB) * ts, :] = acc.reshape(CONV_SB * ts, c)
    o_ref[...] = _ln_silu_out(y_ref[...], lng_ref[...], lnb_ref[...], wout_ref, x_ref[...])


def conv_sample(u3, state, x, dww, dwb, lng, lnb, wout, bs):
    nb, ts, c = u3.shape
    d = x.shape[1]
    return pl.pallas_call(
        _conv_sample_kernel,
        grid=(nb // bs,),
        in_specs=[
            pl.BlockSpec((bs, ts, c), lambda i: (i, 0, 0)),
            pl.BlockSpec((bs, PAST, c), lambda i: (i, 0, 0)),
            pl.BlockSpec((bs * ts, d), lambda i: (i, 0)),
            _full((HALO, c)), _full((1, c)), _full((1, c)), _full((1, c)),
            _full((c, d)),
        ],
        out_specs=pl.BlockSpec((bs * ts, d), lambda i: (i, 0)),
        out_shape=jax.ShapeDtypeStruct((nb * ts, d), F32),
        scratch_shapes=[pltpu.VMEM((bs, PAST + ts, c), F32), pltpu.VMEM((bs * ts, c), F32)],
        compiler_params=_params("parallel"),
        name="conv_sample",
    )(u3, state, x, dww, dwb, lng, lnb, wout)


def _token_specs(xs, tm):
    d = xs[0].shape[1]
    if len(xs) == 1:
        return [pl.BlockSpec((tm, d), lambda i: (i, 0))]
    first = xs[0].shape[0] // tm
    return [pl.BlockSpec((tm, d), lambda i: (jnp.minimum(i, first - 1), 0)),
            pl.BlockSpec((tm, d), lambda i: (jnp.maximum(i - first, 0), 0))]


def _read_tokens(x_refs, first_steps):
    if len(x_refs) == 1:
        return x_refs[0][...]
    return jnp.where(pl.program_id(0) < first_steps, x_refs[0][...], x_refs[1][...])


def _router_kernel(tiles_per_chunk, n_src, first_steps, *refs):
    x_refs = refs[:n_src]
    g_ref, wr_ref, br_ref, tri_ref, hs_ref, route_ref, route_t_ref, cnt_ref, carry_ref = refs[n_src:]
    h = _rms(_read_tokens(x_refs, first_steps), g_ref[...])
    tm, d = h.shape
    for j in range(d // LANES):
        hs_ref[pl.ds(j, tm, stride=ROW_TILE), :] = h[:, j * LANES:(j + 1) * LANES]
    h_hi = h.astype(BF16)
    h_lo = (h - h_hi.astype(F32)).astype(BF16)
    both = jnp.dot(h_hi, wr_ref[...], preferred_element_type=F32)
    cross = jnp.dot(h_lo, wr_ref[:, :LANES], preferred_element_type=F32)
    logits = both[:, :LANES] + both[:, LANES:] + cross + br_ref[...]
    lane = lax.broadcasted_iota(jnp.int32, logits.shape, 1).astype(F32)
    big = jnp.float32(LANES)
    is_g = (lane >= N_EXPERTS) & (lane < N_EXPERTS + N_GROUPS)
    gl = jnp.where(is_g, logits, NEG_INF)
    gm = jnp.max(gl, axis=-1, keepdims=True)
    g_sel = jnp.min(jnp.where(gl == gm, lane, big), axis=-1, keepdims=True) - N_EXPERTS
    gate_g = 1.0 / jnp.sum(jnp.where(is_g, jnp.exp(gl - gm), 0.0), axis=-1, keepdims=True)
    lo = g_sel * EPG
    in_grp = (lane >= lo) & (lane < lo + EPG)
    el = jnp.where(in_grp, logits, NEG_INF)
    v1 = jnp.max(el, axis=-1, keepdims=True)
    i1 = jnp.min(jnp.where(el == v1, lane, big), axis=-1, keepdims=True)
    el2 = jnp.where(lane == i1, NEG_INF, el)
    v2 = jnp.max(el2, axis=-1, keepdims=True)
    i2 = jnp.min(jnp.where(el2 == v2, lane, big), axis=-1, keepdims=True)
    e2 = jnp.exp(v2 - v1)
    w1 = gate_g / (1.0 + e2)
    w2 = gate_g * e2 / (1.0 + e2)
    @pl.when(pl.program_id(0) % tiles_per_chunk == 0)
    def _():
        carry_ref[...] = jnp.zeros_like(carry_ref)

    hit1 = lane == i1
    hit2 = lane == i2
    onehot = jnp.where(hit1 | hit2, 1.0, 0.0)
    before = carry_ref[...] + jnp.dot(tri_ref[...], onehot.astype(BF16), preferred_element_type=F32)
    r1 = jnp.sum(jnp.where(hit1, before, 0.0), axis=-1, keepdims=True)
    r2 = jnp.sum(jnp.where(hit2, before, 0.0), axis=-1, keepdims=True)
    carry_ref[...] += jnp.sum(onehot, axis=0, keepdims=True)
    cnt_ref[...] = jnp.broadcast_to(carry_ref[...], cnt_ref.shape)
    route = jnp.where(lane == 0.0, i1, jnp.where(lane == 1.0, i2, jnp.where(
        lane == 2.0, w1, jnp.where(lane == 3.0, w2, jnp.where(
            lane == 4.0, r1, jnp.where(lane == 5.0, r2, 0.0))))))
    route_ref[...] = route
    route_t_ref[...] = jnp.transpose(route)[0:ROW_TILE, :]


def router(xs, g, wr, br, tm, chunk):
    n = sum(x.shape[0] for x in xs)
    d = xs[0].shape[1]
    tpc = chunk // tm
    tri = jnp.asarray(np.tril(np.ones((tm, tm), np.float32), -1), BF16)
    return pl.pallas_call(
        functools.partial(_router_kernel, tpc, len(xs), xs[0].shape[0] // tm),
        grid=(n // tm,),
        in_specs=_token_specs(xs, tm) + [_full((1, d)),
                  _full((d, 2 * LANES)), _full((1, LANES)), _full((tm, tm))],
        out_specs=[pl.BlockSpec((tm * ROW_TILE, LANES), lambda i: (i, 0)),
                   pl.BlockSpec((tm, LANES), lambda i: (i, 0)),
                   pl.BlockSpec((ROW_TILE, tm), lambda i: (i, 0)),
                   pl.BlockSpec((ROW_TILE, LANES), lambda i: (i // tpc, 0))],
        out_shape=[jax.ShapeDtypeStruct((n * ROW_TILE, LANES), F32),
                   jax.ShapeDtypeStruct((n, LANES), F32),
                   jax.ShapeDtypeStruct((n // tm * ROW_TILE, tm), F32),
                   jax.ShapeDtypeStruct((n // chunk * ROW_TILE, LANES), F32)],
        scratch_shapes=[pltpu.VMEM((1, LANES), F32)],
        compiler_params=_params("arbitrary"),
        name="router",
    )(*xs, g, wr, br, tri)


def _plan_kernel(tiles_per_chunk, route_ref, tri_ref, rank_ref, cnt_ref, carry_ref):
    i = pl.program_id(0)

    @pl.when(i % tiles_per_chunk == 0)
    def _():
        carry_ref[...] = jnp.zeros_like(carry_ref)

    route = route_ref[...]
    lane = lax.broadcasted_iota(jnp.int32, route.shape, 1).astype(F32)
    i1 = route[:, 0:1]
    i2 = route[:, 1:2]
    hit1 = lane == i1
    hit2 = lane == i2
    onehot = jnp.where(hit1 | hit2, 1.0, 0.0)
    before = carry_ref[...] + jnp.dot(tri_ref[...], onehot.astype(BF16), preferred_element_type=F32)
    r1 = jnp.sum(jnp.where(hit1, before, 0.0), axis=-1, keepdims=True)
    r2 = jnp.sum(jnp.where(hit2, before, 0.0), axis=-1, keepdims=True)
    rank_ref[...] = jnp.where(lane == 0.0, r1, jnp.where(lane == 1.0, r2, 0.0))
    carry_ref[...] += jnp.sum(onehot, axis=0, keepdims=True)
    cnt_ref[...] = jnp.broadcast_to(carry_ref[...], cnt_ref.shape)


def plan(route, tm, chunk):
    n = route.shape[0]
    tpc = chunk // tm
    tri = jnp.asarray(np.tril(np.ones((tm, tm), np.float32), -1), BF16)
    return pl.pallas_call(
        functools.partial(_plan_kernel, tpc),
        grid=(n // tm,),
        in_specs=[pl.BlockSpec((tm, LANES), lambda i: (i, 0)), _full((tm, tm))],
        out_specs=[pl.BlockSpec((tm, LANES), lambda i: (i, 0)),
                   pl.BlockSpec((ROW_TILE, LANES), lambda i: (i // tpc, 0))],
        out_shape=[jax.ShapeDtypeStruct((n, LANES), F32),
                   jax.ShapeDtypeStruct((n // chunk * ROW_TILE, LANES), F32)],
        scratch_shapes=[pltpu.VMEM((1, LANES), F32)],
        compiler_params=_params("arbitrary"),
        name="moe_plan",
    )(route, tri)


def _gather_rows(idx_ref, src_hbm, dst, sem, n_rows):
    unroll = 8

    def body(c, carry):
        for u in range(unroll):
            m = c * unroll + u
            tok = idx_ref[m]
            pltpu.make_async_copy(
                src_hbm.at[pl.ds(pl.multiple_of(tok * ROW_TILE, ROW_TILE), ROW_TILE), :],
                dst.at[pl.ds(pl.multiple_of(m * ROW_TILE, ROW_TILE), ROW_TILE), :],
                sem).start()
        return carry

    lax.fori_loop(0, n_rows // unroll, body, 0)


def _wait_rows(src_hbm, dst, sem):
    pltpu.make_async_copy(src_hbm.at[pl.ds(0, dst.shape[0]), :], dst, sem).wait()


def _from_token_tiles(buf, start, n_rows, stride):
    return jnp.concatenate(
        [buf[pl.ds(start + j, n_rows, stride=stride), :] for j in range(ROW_TILE)], axis=-1)


def _experts_kernel(tiles_per_chunk, tmg, te_ref, nv_ref, pos_ref, fill_hbm, hs_hbm,
                    win_ref, wout_ref, ys_ref, hsv, gbuf_a, gbuf_b, src_ref, sems, winb, woutb):
    c = pl.program_id(0)
    j = pl.program_id(1)
    t = c * tiles_per_chunk + j
    nv = nv_ref[c]
    chunk = pos_ref.shape[0] // 2
    unroll = 8

    def tile(r):
        return pl.ds(pl.multiple_of(r * ROW_TILE, ROW_TILE), ROW_TILE)

    @pl.when(j == 0)
    def _():
        load = pltpu.make_async_copy(
            hs_hbm.at[pl.ds(pl.multiple_of(c * chunk * ROW_TILE, ROW_TILE), chunk * ROW_TILE), :],
            hsv.at[pl.ds(0, chunk * ROW_TILE), :], sems.at[0])
        load.start()
        fill = pltpu.make_async_copy(fill_hbm, src_ref, sems.at[1])
        fill.start()
        hsv[pl.ds(chunk * ROW_TILE, ROW_TILE), :] = jnp.zeros((ROW_TILE, LANES), F32)
        fill.wait()

        def scatter(i, carry):
            first = i * scatter_unroll
            tok0 = first // (2 * MOE_TM) * MOE_TM + first % MOE_TM
            for u in range(scatter_unroll):
                src_ref[pos_ref[first + u]] = tok0 + u
            return carry

        scatter_unroll = 32
        lax.fori_loop(0, 2 * chunk // scatter_unroll, scatter, 0)
        load.wait()

        def gather(i, carry):
            for u in range(unroll):
                m = i * unroll + u
                gbuf_a[tile(m), :] = hsv[tile(src_ref[m]), :]
            return carry

        lax.fori_loop(0, tmg // unroll, gather, 0)

    changed = jnp.logical_or(j == 0, te_ref[t] != te_ref[jnp.maximum(t - 1, 0)])

    @pl.when(jnp.logical_and(changed, j < nv))
    def _():
        winb[...] = win_ref[0].astype(BF16)
        woutb[...] = wout_ref[0].astype(BF16)

    def step(cur, nxt):
        base = jnp.minimum(j + 1, tiles_per_chunk - 1) * tmg
        for m in range(tmg):
            nxt[pl.ds(m * ROW_TILE, ROW_TILE), :] = hsv[tile(src_ref[base + m]), :]
        x = _from_token_tiles(cur, 0, tmg, ROW_TILE).astype(BF16)
        hid = jnp.dot(x, winb[...], preferred_element_type=F32)
        a = hid[:, :D_EXPERT]
        u = hid[:, D_EXPERT:]
        act = (a * _sigmoid(a) * u).astype(BF16)
        y = jnp.dot(act, woutb[...], preferred_element_type=F32)
        for k in range(ROW_TILE):
            ys_ref[pl.ds(k, tmg, stride=ROW_TILE), :] = y[:, k * LANES:(k + 1) * LANES]

    @pl.when(jnp.logical_and(j < nv, j % 2 == 0))
    def _():
        step(gbuf_a, gbuf_b)

    @pl.when(jnp.logical_and(j < nv, j % 2 == 1))
    def _():
        step(gbuf_b, gbuf_a)

    @pl.when(j >= nv)
    def _():
        ys_ref[...] = jnp.zeros_like(ys_ref)


def experts(hs, pos_local, tile_expert, n_valid, w_in, w_out, tmg, chunk, tiles_per_chunk):
    _, d, f2 = w_in.shape
    nc = hs.shape[0] // (chunk * ROW_TILE)
    rows_per_chunk = tiles_per_chunk * tmg
    fill = jnp.full((rows_per_chunk,), chunk, jnp.int32)
    grid_spec = pltpu.PrefetchScalarGridSpec(
        num_scalar_prefetch=2,
        grid=(nc, tiles_per_chunk),
        in_specs=[
            pl.BlockSpec((2 * chunk,), lambda c, j, te, nv: (c,), memory_space=pltpu.SMEM),
            pl.BlockSpec(memory_space=pl.ANY),
            pl.BlockSpec(memory_space=pl.ANY),
            pl.BlockSpec((1, d, f2), lambda c, j, te, nv: (te[c * tiles_per_chunk + j], 0, 0)),
            pl.BlockSpec((1, f2 // 2, d), lambda c, j, te, nv: (te[c * tiles_per_chunk + j], 0, 0)),
        ],
        out_specs=pl.BlockSpec((tmg * ROW_TILE, LANES),
                               lambda c, j, te, nv: (c * tiles_per_chunk + j, 0)),
        scratch_shapes=[
            pltpu.VMEM(((chunk + 1) * ROW_TILE, LANES), F32),
            pltpu.VMEM((tmg * ROW_TILE, LANES), F32),
            pltpu.VMEM((tmg * ROW_TILE, LANES), F32),
            pltpu.SMEM((rows_per_chunk,), jnp.int32),
            pltpu.SemaphoreType.DMA((2,)),
            pltpu.VMEM((d, f2), BF16),
            pltpu.VMEM((f2 // 2, d), BF16),
        ],
    )
    return pl.pallas_call(
        functools.partial(_experts_kernel, tiles_per_chunk, tmg),
        grid_spec=grid_spec,
        out_shape=jax.ShapeDtypeStruct((nc * rows_per_chunk * ROW_TILE, LANES), F32),
        compiler_params=pltpu.CompilerParams(dimension_semantics=("arbitrary", "arbitrary"),
                                             vmem_limit_bytes=MOE_VMEM_LIMIT),
        name="moe_experts",
    )(tile_expert, n_valid, pos_local, fill, hs, w_in, w_out)


def _combine_kernel(split_steps, n_src, first_steps, pos_cur_ref, pos_nxt_ref, ys_hbm, route_ref, *rest):
    x_refs, o_refs, (ybuf_a, ybuf_b, sems) = rest[:n_src], rest[n_src:-3], rest[-3:]
    i = pl.program_id(0)
    last = i == pl.num_programs(0) - 1
    tm = route_ref.shape[0]

    @pl.when(i == 0)
    def _():
        _gather_rows(pos_cur_ref, ys_hbm, ybuf_a, sems.at[0], 2 * tm)

    def step(cur, cur_sem, nxt, nxt_sem):
        _gather_rows(pos_nxt_ref, ys_hbm, nxt, nxt_sem, 2 * tm)
        _wait_rows(ys_hbm, cur, cur_sem)
        route = route_ref[...]
        y1 = _from_token_tiles(cur, 0, tm, ROW_TILE)
        y2 = _from_token_tiles(cur, tm * ROW_TILE, tm, ROW_TILE)
        val = _read_tokens(x_refs, first_steps) + route[:, 2:3] * y1 + route[:, 3:4] * y2
        if split_steps is None:
            o_refs[0][...] = val
        else:
            @pl.when(i < split_steps)
            def _():
                o_refs[0][...] = val

            @pl.when(i >= split_steps)
            def _():
                o_refs[1][...] = val

        @pl.when(last)
        def _():
            _wait_rows(ys_hbm, nxt, nxt_sem)

    @pl.when(i % 2 == 0)
    def _():
        step(ybuf_a, sems.at[0], ybuf_b, sems.at[1])

    @pl.when(i % 2 == 1)
    def _():
        step(ybuf_b, sems.at[1], ybuf_a, sems.at[0])


def combine(ys, pos, route, xs, tm, split_rows=None):
    n = sum(x.shape[0] for x in xs)
    d = xs[0].shape[1]
    nsteps = n // tm
    if split_rows is None:
        split_steps = None
        out_specs = pl.BlockSpec((tm, d), lambda i: (i, 0))
        out_shape = jax.ShapeDtypeStruct((n, d), F32)
    else:
        split_steps = split_rows // tm
        out_specs = [pl.BlockSpec((tm, d), lambda i: (jnp.minimum(i, split_steps - 1), 0)),
                     pl.BlockSpec((tm, d), lambda i: (jnp.maximum(i - split_steps, 0), 0))]
        out_shape = [jax.ShapeDtypeStruct((split_rows, d), F32),
                     jax.ShapeDtypeStruct((n - split_rows, d), F32)]
    return pl.pallas_call(
        functools.partial(_combine_kernel, split_steps, len(xs), xs[0].shape[0] // tm),
        grid=(nsteps,),
        in_specs=[
            pl.BlockSpec((2 * tm,), lambda i: (i,), memory_space=pltpu.SMEM),
            pl.BlockSpec((2 * tm,), lambda i: (jnp.minimum(i + 1, nsteps - 1),),
                         memory_space=pltpu.SMEM),
            pl.BlockSpec(memory_space=pl.ANY),
            pl.BlockSpec((tm, LANES), lambda i: (i, 0)),
        ] + _token_specs(xs, tm),
        out_specs=out_specs,
        out_shape=out_shape,
        scratch_shapes=[pltpu.VMEM((2 * tm * ROW_TILE, LANES), F32),
                        pltpu.VMEM((2 * tm * ROW_TILE, LANES), F32),
                        pltpu.SemaphoreType.DMA((2,))],
        compiler_params=_params("arbitrary"),
        name="moe_combine",
    )(pos, pos, ys, route, *xs)


def _moe_fused_kernel(ts, st_ref, cn_ref, pos_ref, wts_ref, hs_hbm, win_ref, wout_ref, acc_hbm,
                      hsv, acc, gbuf, ystage, src_ref, sems, winb, woutb):
    c = pl.program_id(0)
    e = pl.program_id(1)
    n_slots = pos_ref.shape[0]
    chunk = n_slots // 2
    rows = chunk * ROW_TILE
    unroll = 8

    def tile(r):
        return pl.ds(pl.multiple_of(r * ROW_TILE, ROW_TILE), ROW_TILE)

    @pl.when(e == 0)
    def _():
        load = pltpu.make_async_copy(
            hs_hbm.at[pl.ds(pl.multiple_of(c * rows, ROW_TILE), rows), :],
            hsv.at[pl.ds(0, rows), :], sems.at[0])
        load.start()
        hsv[pl.ds(rows, ROW_TILE), :] = jnp.zeros((ROW_TILE, LANES), F32)

        def zero(i, carry):
            acc[pl.ds(pl.multiple_of(i * 64, 64), 64), :] = jnp.zeros((64, LANES), F32)
            return carry

        lax.fori_loop(0, (rows + ROW_TILE) // 64, zero, 0)
        acc[pl.ds(rows + ROW_TILE - 64, 64), :] = jnp.zeros((64, LANES), F32)

        def scatter(i, carry):
            for u in range(unroll):
                s = i * unroll + u
                src_ref[pos_ref[s]] = s
            return carry

        lax.fori_loop(0, n_slots // unroll, scatter, 0)
        load.wait()

    winb[...] = win_ref[0].astype(BF16)
    woutb[...] = wout_ref[0].astype(BF16)
    start = st_ref[c * N_EXPERTS + e]
    cnt = cn_ref[c * N_EXPERTS + e]

    def sub_tile(k, carry):
        base = start + k * ts
        rem = cnt - k * ts

        def slot_of(m):
            s = src_ref[jnp.minimum(base + m, n_slots - 1)]
            valid = m < rem
            return jnp.where(valid, s >> 1, chunk), jnp.where(valid, wts_ref[s], 0.0)

        def gather(i, carry2):
            for u in range(unroll):
                m = i * unroll + u
                tok, _ = slot_of(m)
                gbuf[tile(m), :] = hsv[tile(tok), :]
            return carry2

        lax.fori_loop(0, ts // unroll, gather, 0)
        x = _from_token_tiles(gbuf, 0, ts, ROW_TILE).astype(BF16)
        hid = jnp.dot(x, winb[...], preferred_element_type=F32)
        a = hid[:, :D_EXPERT]
        u_ = hid[:, D_EXPERT:]
        act = (a * _sigmoid(a) * u_).astype(BF16)
        y = jnp.dot(act, woutb[...], preferred_element_type=F32)
        for j in range(ROW_TILE):
            ystage[pl.ds(j, ts, stride=ROW_TILE), :] = y[:, j * LANES:(j + 1) * LANES]

        def scatter_add(i, carry2):
            toks, vals = [], []
            for u in range(unroll):
                m = i * unroll + u
                tok, w = slot_of(m)
                toks.append(tok)
                vals.append(acc[tile(tok), :] + w * ystage[tile(m), :])
            for tok, val in zip(toks, vals):
                acc[tile(tok), :] = val
            return carry2

        lax.fori_loop(0, ts // unroll, scatter_add, 0)
        return carry

    lax.fori_loop(0, (cnt + ts - 1) // ts, sub_tile, 0)

    @pl.when(e == pl.num_programs(1) - 1)
    def _():
        store = pltpu.make_async_copy(
            acc.at[pl.ds(0, rows), :],
            acc_hbm.at[pl.ds(pl.multiple_of(c * rows, ROW_TILE), rows), :], sems.at[1])
        store.start()
        store.wait()


def moe_fused(hs, pos_local, wts, starts, counts, w_in, w_out, chunk, ts):
    ne, d, f2 = w_in.shape
    n_rows = hs.shape[0]
    nc = n_rows // (chunk * ROW_TILE)
    grid_spec = pltpu.PrefetchScalarGridSpec(
        num_scalar_prefetch=2,
        grid=(nc, ne),
        in_specs=[
            pl.BlockSpec((2 * chunk,), lambda c, e, st, cn: (c,), memory_space=pltpu.SMEM),
            pl.BlockSpec((2 * chunk,), lambda c, e, st, cn: (c,), memory_space=pltpu.SMEM),
            pl.BlockSpec(memory_space=pl.ANY),
            pl.BlockSpec((1, d, f2), lambda c, e, st, cn: (e, 0, 0)),
            pl.BlockSpec((1, f2 // 2, d), lambda c, e, st, cn: (e, 0, 0)),
        ],
        out_specs=pl.BlockSpec(memory_space=pl.ANY),
        scratch_shapes=[
            pltpu.VMEM(((chunk + 1) * ROW_TILE, LANES), F32),
            pltpu.VMEM(((chunk + 1) * ROW_TILE, LANES), F32),
            pltpu.VMEM((ts * ROW_TILE, LANES), F32),
            pltpu.VMEM((ts * ROW_TILE, LANES), F32),
            pltpu.SMEM((2 * chunk,), jnp.int32),
            pltpu.SemaphoreType.DMA((2,)),
            pltpu.VMEM((d, f2), BF16),
            pltpu.VMEM((f2 // 2, d), BF16),
        ],
    )
    return pl.pallas_call(
        functools.partial(_moe_fused_kernel, ts),
        grid_spec=grid_spec,
        out_shape=jax.ShapeDtypeStruct((n_rows, LANES), F32),
        compiler_params=pltpu.CompilerParams(dimension_semantics=("arbitrary", "arbitrary"),
                                             vmem_limit_bytes=MOE_VMEM_LIMIT),
        name="moe_fused",
    )(starts, counts, pos_local, wts, hs, w_in, w_out)


def _finish_kernel(acc_ref, x_ref, o_ref):
    tm = x_ref.shape[0]
    o_ref[...] = x_ref[...] + _from_token_tiles(acc_ref, 0, tm, ROW_TILE)


def moe_finish(acc, x, tm):
    n, d = x.shape
    return pl.pallas_call(
        _finish_kernel,
        grid=(n // tm,),
        in_specs=[pl.BlockSpec((tm * ROW_TILE, LANES), lambda i: (i, 0)),
                  pl.BlockSpec((tm, d), lambda i: (i, 0))],
        out_specs=pl.BlockSpec((tm, d), lambda i: (i, 0)),
        out_shape=jax.ShapeDtypeStruct((n, d), F32),
        compiler_params=_params("parallel"),
        name="moe_finish",
    )(acc, x)


def moe_layer_fused(x, g, wr, br, w_in, w_out):
    n = x.shape[0]
    chunk = min(n, MOE_CHUNK)
    nc = n // chunk
    hs, route, counts = router(x, g, wr, br, MOE_TM, chunk)
    ids = route[:, 0:2].astype(jnp.int32).reshape(nc, chunk, 2)
    cnt = counts.reshape(nc, ROW_TILE, LANES)[:, 0, :N_EXPERTS].astype(jnp.int32)
    starts = jnp.cumsum(cnt, axis=1) - cnt
    sel = ids[..., None] == jnp.arange(N_EXPERTS, dtype=jnp.int32)
    start_of = jnp.sum(jnp.where(sel, starts[:, None, None, :], 0), axis=-1)
    pos_local = start_of + route[:, 4:6].astype(jnp.int32).reshape(nc, chunk, 2)
    acc = moe_fused(hs, pos_local.reshape(-1), route[:, 2:4].reshape(-1), starts.reshape(-1),
                    cnt.reshape(-1), w_in, w_out, chunk, MOE_TS)
    return moe_finish(acc, x, MOE_TM)


def moe_layer(xs, g, wr, br, w_in, w_out, layer, split_rows=None):
    n = sum(x.shape[0] for x in xs)
    nc = -(-n // MOE_CHUNK)
    chunk = n // nc
    assert chunk * nc == n and chunk % MOE_TM == 0
    tpc = (2 * chunk + N_EXPERTS * (MOE_TMG - 1)) // MOE_TMG
    hs, route, route_t, counts = router(xs, g, wr, br, MOE_TM, chunk)
    rt = route_t.reshape(nc, chunk // MOE_TM, ROW_TILE, MOE_TM)
    ids = rt[:, :, 0:2, :].astype(jnp.int32)
    ranks = rt[:, :, 4:6, :].astype(jnp.int32)
    cnt = counts.reshape(nc, ROW_TILE, LANES)[:, 0, :N_EXPERTS].astype(jnp.int32)
    padded = (cnt + MOE_TMG - 1) // MOE_TMG * MOE_TMG
    ends = jnp.cumsum(padded, axis=1)
    starts = ends - padded
    start_of = jnp.zeros_like(ids)
    for e in range(N_EXPERTS):
        start_of = jnp.where(ids == e, starts[:, e][:, None, None, None], start_of)
    pos_local = start_of + ranks
    tile_row = jnp.arange(tpc, dtype=jnp.int32) * MOE_TMG
    tile_expert = jnp.minimum(jnp.sum(ends[:, None, :] <= tile_row[None, :, None], axis=-1),
                              N_EXPERTS - 1).astype(jnp.int32)
    n_valid = (ends[:, -1] // MOE_TMG).astype(jnp.int32)
    ys = experts(hs, pos_local.reshape(-1), tile_expert.reshape(-1) + layer * N_EXPERTS, n_valid,
                 w_in, w_out, MOE_TMG, chunk, tpc)
    pos = pos_local + (jnp.arange(nc, dtype=jnp.int32) * (tpc * MOE_TMG))[:, None, None, None]
    return combine(ys, pos.reshape(-1), route, xs, MOE_TM, split_rows)


def _qkv_kernel(x_ref, g_ref, w_ref, qg_ref, kg_ref, seg_ref, q_ref, k_ref, v_ref):
    h = _rms(x_ref[...], g_ref[...]).astype(BF16)
    qkv = jnp.dot(h, w_ref[...], preferred_element_type=F32)
    nq = N_HEADS * HEAD_DIM
    nk = N_KV * HEAD_DIM
    q = qkv[:, :nq]
    k = qkv[:, nq:nq + nk]
    v_ref[...] = qkv[:, nq + nk:]

    def seg_mean_sq(z, seg, split):
        zz = z * z
        hi = zz.astype(BF16)
        ms = jnp.dot(hi, seg, preferred_element_type=F32)
        if split:
            lo = (zz - hi.astype(F32)).astype(BF16)
            ms = ms + jnp.dot(lo, seg, preferred_element_type=F32)
        return ms

    seg = seg_ref[...]
    qn = q * lax.rsqrt(seg_mean_sq(q, seg, False) + RMS_EPS) * qg_ref[...]
    q_ref[...] = (qn * (HEAD_DIM ** -0.5)).astype(BF16)
    kn = k * lax.rsqrt(seg_mean_sq(k, seg[:nk, :nk], True) + RMS_EPS) * kg_ref[...]
    k_ref[...] = kn


def qkv_proj(x, g, w, qg, kg, seg, tm):
    n, d = x.shape
    nq = N_HEADS * HEAD_DIM
    nk = N_KV * HEAD_DIM
    return pl.pallas_call(
        _qkv_kernel,
        grid=(n // tm,),
        in_specs=[pl.BlockSpec((tm, d), lambda i: (i, 0)), _full((1, d)),
                  _full((d, nq + 2 * nk)), _full((1, nq)), _full((1, nk)), _full((nq, nq))],
        out_specs=[pl.BlockSpec((tm, nq), lambda i: (i, 0)),
                   pl.BlockSpec((tm, nk), lambda i: (i, 0)),
                   pl.BlockSpec((tm, nk), lambda i: (i, 0))],
        out_shape=[jax.ShapeDtypeStruct((n, nq), BF16),
                   jax.ShapeDtypeStruct((n, nk), F32),
                   jax.ShapeDtypeStruct((n, nk), F32)],
        compiler_params=_params("parallel"),
        name="qkv_proj",
    )(x, g, w, qg, kg, seg)


def _attn_prompt_kernel(q_ref, kc_ref, kp_ref, vc_ref, vp_ref, tbl_ref, sink_ref, x_ref,
                        wo_ref, o_ref, att_ref, s_ref, e_ref, inv_ref):
    n = pl.program_id(1)
    first = jnp.where(n == 0, NEG_INF, 0.0).astype(F32)
    nt = (((1,), (1,)), ((), ()))
    lo_half = lax.broadcasted_iota(jnp.int32, (WINDOW, LANES), 1) < HEAD_DIM
    kp, kc = kp_ref[...], kc_ref[...]
    keys = (kp.astype(BF16), kc.astype(BF16))
    keys_swapped = (pltpu.roll(kp, HEAD_DIM, axis=1).astype(BF16),
                    pltpu.roll(kc, HEAD_DIM, axis=1).astype(BF16))
    vt_prev = jnp.transpose(vp_ref[...]).astype(BF16)
    vt_cur = jnp.transpose(vc_ref[...]).astype(BF16)
    for h in range(N_HEADS):
        kh = h // GROUP
        odd = h % 2
        pair = q_ref[:, (h // 2) * LANES:(h // 2 + 1) * LANES]
        qm = jnp.where(lo_half if odd == 0 else ~lo_half, pair, jnp.zeros_like(pair))
        k_prev, k_cur = keys_swapped if (kh == 0) == (odd == 1) else keys
        s_ref[h, 0:WINDOW, :] = (lax.dot_general(k_prev, qm, nt, preferred_element_type=F32)
                                 + tbl_ref[h, 0:WINDOW, :] + first)
        s_ref[h, WINDOW:, :] = (lax.dot_general(k_cur, qm, nt, preferred_element_type=F32)
                                + tbl_ref[h, WINDOW:, :])
    for h in range(N_HEADS):
        s = s_ref[h]
        sink = sink_ref[h]
        m = jnp.maximum(jnp.max(s, axis=0, keepdims=True), sink)
        e = jnp.exp(s - m)
        e_ref[h] = e.astype(BF16)
        inv_ref[h] = 1.0 / (jnp.sum(e, axis=0, keepdims=True) + jnp.exp(sink - m))
    for h in range(N_HEADS):
        kh = h // GROUP
        dims = slice(kh * HEAD_DIM, (kh + 1) * HEAD_DIM)
        out_t = (jnp.dot(vt_prev[dims, :], e_ref[h, 0:WINDOW, :], preferred_element_type=F32)
                 + jnp.dot(vt_cur[dims, :], e_ref[h, WINDOW:, :], preferred_element_type=F32))
        att_ref[h * HEAD_DIM:(h + 1) * HEAD_DIM, :] = out_t * inv_ref[h]
    att = jnp.transpose(att_ref[...]).astype(BF16)
    o_ref[...] = x_ref[...] + jnp.dot(att, wo_ref[...], preferred_element_type=F32)


def attn_prompt(q, k, v, tbl, sinks, x, wo, batch, seq):
    n, d = x.shape
    nb = seq // WINDOW
    nk = N_KV * HEAD_DIM
    cur = lambda b, i: (b * nb + i, 0)
    prev = lambda b, i: (b * nb + jnp.maximum(i - 1, 0), 0)
    return pl.pallas_call(
        _attn_prompt_kernel,
        grid=(batch, nb),
        in_specs=[
            pl.BlockSpec((WINDOW, d), cur),
            pl.BlockSpec((WINDOW, nk), cur), pl.BlockSpec((WINDOW, nk), prev),
            pl.BlockSpec((WINDOW, nk), cur), pl.BlockSpec((WINDOW, nk), prev),
            _full((N_HEADS, 2 * WINDOW, WINDOW)),
            pl.BlockSpec(memory_space=pltpu.SMEM),
            pl.BlockSpec((WINDOW, d), cur),
            _full((d, d)),
        ],
        out_specs=pl.BlockSpec((WINDOW, d), cur),
        out_shape=jax.ShapeDtypeStruct((n, d), F32),
        input_output_aliases={7: 0},
        scratch_shapes=[pltpu.VMEM((N_HEADS * HEAD_DIM, WINDOW), F32),
                        pltpu.VMEM((N_HEADS, 2 * WINDOW, WINDOW), F32),
                        pltpu.VMEM((N_HEADS, 2 * WINDOW, WINDOW), BF16),
                        pltpu.VMEM((N_HEADS, 1, WINDOW), F32)],
        compiler_params=_params("parallel", "parallel"),
        name="attn_prompt",
    )(q, k, k, v, v, tbl, sinks, x, wo)


def _attn_sample_kernel(q_ref, kn_ref, vn_ref, ck_ref, cv_ref, tblc_ref, tbln_ref, sink_ref, o_ref):
    bs = q_ref.shape[0]
    nt = (((1,), (1,)), ((), ()))
    for b in range(bs):
        ck = ck_ref[b].astype(BF16)
        cv = cv_ref[b].astype(BF16)
        kn = kn_ref[b].astype(BF16)
        vn = vn_ref[b].astype(BF16)
        for kh in range(N_KV):
            lanes = slice(kh * HEAD_DIM, (kh + 1) * HEAD_DIM)
            qt = q_ref[b, kh]
            sc = lax.dot_general(qt, ck[:, lanes], nt, preferred_element_type=F32) + tblc_ref[kh]
            sn = lax.dot_general(qt, kn[:, lanes], nt, preferred_element_type=F32) + tbln_ref[kh]
            sink = sink_ref[kh]
            m = jnp.maximum(jnp.maximum(jnp.max(sc, axis=-1, keepdims=True),
                                        jnp.max(sn, axis=-1, keepdims=True)), sink)
            ec = jnp.exp(sc - m)
            en = jnp.exp(sn - m)
            denom = (jnp.sum(ec, axis=-1, keepdims=True) + jnp.sum(en, axis=-1, keepdims=True)
                     + jnp.exp(sink - m))
            out = (jnp.dot(ec.astype(BF16), cv[:, lanes], preferred_element_type=F32)
                   + jnp.dot(en.astype(BF16), vn[:, lanes], preferred_element_type=F32))
            o_ref[b, kh] = (out * (1.0 / denom)).astype(BF16)


def attn_sample(q4, kn, vn, ck, cv, tbl, sink, bs):
    nb = q4.shape[0]
    ts = kn.shape[1]
    nk = N_KV * HEAD_DIM
    tg = ts * GROUP
    kt = WINDOW + ts
    return pl.pallas_call(
        _attn_sample_kernel,
        grid=(nb // bs,),
        in_specs=[
            pl.BlockSpec((bs, N_KV, tg, HEAD_DIM), lambda i: (i, 0, 0, 0)),
            pl.BlockSpec((bs, ts, nk), lambda i: (i, 0, 0)),
            pl.BlockSpec((bs, ts, nk), lambda i: (i, 0, 0)),
            pl.BlockSpec((bs, WINDOW, nk), lambda i: (i, 0, 0)),
            pl.BlockSpec((bs, WINDOW, nk), lambda i: (i, 0, 0)),
            _full((N_KV, tg, WINDOW)),
            _full((N_KV, tg, ts)),
            _full((N_KV, tg, 1)),
        ],
        out_specs=pl.BlockSpec((bs, N_KV, tg, HEAD_DIM), lambda i: (i, 0, 0, 0)),
        out_shape=jax.ShapeDtypeStruct((nb, N_KV, tg, HEAD_DIM), BF16),
        compiler_params=_params("parallel"),
        name="attn_sample",
    )(q4, kn, vn, ck, cv, tbl[:, :, :WINDOW], tbl[:, :, WINDOW:], sink)


def _proj_res_kernel(a_ref, w_ref, x_ref, o_ref):
    o_ref[...] = x_ref[...] + jnp.dot(a_ref[...], w_ref[...], preferred_element_type=F32)


def proj_residual(a, w, x, tm):
    n, d = x.shape
    na, kdim = a.shape
    off = (n - na) // tm
    return pl.pallas_call(
        _proj_res_kernel,
        grid=(na // tm,),
        in_specs=[pl.BlockSpec((tm, kdim), lambda i: (i, 0)), _full((kdim, d)),
                  pl.BlockSpec((tm, d), lambda i: (off + i, 0))],
        out_specs=pl.BlockSpec((tm, d), lambda i: (off + i, 0)),
        out_shape=jax.ShapeDtypeStruct((n, d), F32),
        input_output_aliases={2: 0},
        compiler_params=_params("parallel"),
        name="proj_residual",
    )(a, w, x)


def _t5_bucket_np(dist):
    n = np.maximum(dist, 0)
    max_exact = N_BUCKETS // 2
    large = max_exact + (np.log(np.maximum(n, 1).astype(np.float32) / max_exact)
                         / math.log(MAX_DISTANCE / max_exact) * (N_BUCKETS - max_exact)).astype(np.int32)
    large = np.minimum(large, N_BUCKETS - 1)
    return np.where(n < max_exact, n, large)


def _bias_table(rel_bias, dist):
    valid = (dist >= 0) & (dist <= WINDOW)
    onehot = (np.asarray(_t5_bucket_np(dist))[..., None] == np.arange(N_BUCKETS)).astype(np.float32)
    b = jnp.einsum("qkb,bh->hqk", jnp.asarray(onehot), rel_bias.astype(F32),
                   precision=lax.Precision.HIGHEST)
    return jnp.where(jnp.asarray(valid)[None], b, NEG_INF)


def kernel(x_prompt, x_sample, state_conv, cache_swa_k, cache_swa_v, rms_mix_g, rms_ffn_g, conv_w_in, conv_dw_w, conv_dw_b, conv_ln_g, conv_ln_b, conv_w_out, attn_w_qkv, attn_q_norm_g, attn_k_norm_g, attn_sinks, attn_w_o, rel_bias, router_group_w, router_group_b, router_expert_w, router_expert_b, expert_w_in, expert_w_out):
    batch, seq, d = x_prompt.shape
    nsb, ts, _ = x_sample.shape
    xp = x_prompt.reshape(batch * seq, d)
    xs = x_sample.reshape(nsb * ts, d)
    row = lambda a: a.reshape(1, -1).astype(F32)

    def router_w(i):
        we = jnp.transpose(router_expert_w[i], (1, 0, 2)).reshape(d, N_EXPERTS)
        wr = jnp.concatenate([we, router_group_w[i]], axis=1)
        wr = jnp.pad(wr, ((0, 0), (0, LANES - wr.shape[1])))
        br = jnp.concatenate([router_expert_b[i].reshape(-1), router_group_b[i]])
        br = jnp.pad(br, (0, LANES - br.shape[0])).reshape(1, LANES)
        wr = wr.astype(F32)
        w_hi = wr.astype(BF16)
        w_lo = (wr - w_hi.astype(F32)).astype(BF16)
        return jnp.concatenate([w_hi, w_lo], axis=1), br.astype(F32)

    g0 = row(rms_mix_g[0])
    w_in = conv_w_in[0].astype(BF16)
    dww = jnp.pad(conv_dw_w[0].astype(F32), ((0, HALO - CONV_WIDTH), (0, 0)))
    dwb, lng, lnb = row(conv_dw_b[0]), row(conv_ln_g[0]), row(conv_ln_b[0])
    w_out = conv_w_out[0].astype(BF16)
    up = glu_proj(xp, g0, w_in, 512)
    us = glu_proj(xs, g0, w_in, 512)
    n_p, n_s = batch * seq, nsb * ts
    n_all = n_p + n_s
    xp = conv_prompt(up, xp, dww, dwb, lng, lnb, w_out, batch, seq, 256)
    us3 = us.reshape(nsb, ts, -1)
    xs = conv_sample(us3, state_conv[0], xs, dww, dwb, lng, lnb, w_out, 32)
    conv_p = up.reshape(batch, seq, -1)[:, seq - PAST:]
    conv_s = jnp.concatenate([state_conv[0], us3], axis=1)[:, ts:]

    wr0, br0 = router_w(0)
    ew_in = expert_w_in.reshape((DEPTH * N_EXPERTS,) + expert_w_in.shape[2:])
    ew_out = expert_w_out.reshape((DEPTH * N_EXPERTS,) + expert_w_out.shape[2:])
    x = moe_layer((xp, xs), row(rms_ffn_g[0]), wr0, br0, ew_in, ew_out, 0)

    g1 = row(rms_mix_g[1])
    w_qkv = attn_w_qkv[0].astype(BF16)
    qg = jnp.tile(attn_q_norm_g[0].astype(F32), N_HEADS).reshape(1, -1)
    kg = jnp.tile(attn_k_norm_g[0].astype(F32), N_KV).reshape(1, -1)
    nq = N_HEADS * HEAD_DIM
    seg = jnp.asarray(np.kron(np.eye(N_HEADS), np.ones((HEAD_DIM, HEAD_DIM))) / HEAD_DIM, BF16)
    w_o = attn_w_o[0].astype(BF16)
    sinks = attn_sinks[0].astype(F32)

    q, k, v = qkv_proj(x, g1, w_qkv, qg, kg, seg, 512)
    kp, vp = k[:n_p], v[:n_p]
    qs, ks, vs = q[n_p:], k[n_p:], v[n_p:]

    q_off = np.arange(WINDOW)[:, None]
    dist_p = q_off + WINDOW - np.arange(2 * WINDOW)[None, :]
    tbl_p = jnp.transpose(_bias_table(rel_bias, dist_p), (0, 2, 1))
    x_attn = attn_prompt(q, k, v, tbl_p, sinks, x, w_o, batch, seq)

    kt = WINDOW + ts
    dist_s = np.arange(ts)[:, None] + WINDOW - np.arange(kt)[None, :]
    tbl_s = _bias_table(rel_bias, dist_s)
    tbl_s = jnp.transpose(tbl_s.reshape(N_KV, GROUP, ts, kt), (0, 2, 1, 3)).reshape(N_KV, ts * GROUP, kt)
    sink_s = jnp.tile(sinks.reshape(N_KV, 1, GROUP), (1, ts, 1)).reshape(N_KV, ts * GROUP, 1)
    q4 = jnp.transpose(qs.reshape(nsb, ts, N_KV, GROUP, HEAD_DIM), (0, 2, 1, 3, 4))
    q4 = q4.reshape(nsb, N_KV, ts * GROUP, HEAD_DIM)
    nk = N_KV * HEAD_DIM
    ks3, vs3 = ks.reshape(nsb, ts, nk), vs.reshape(nsb, ts, nk)
    ck = cache_swa_k[0].reshape(nsb, WINDOW, nk)
    cv = cache_swa_v[0].reshape(nsb, WINDOW, nk)
    o4 = attn_sample(q4, ks3, vs3, ck, cv, tbl_s, sink_s, 16)
    os_ = jnp.transpose(o4.reshape(nsb, N_KV, ts, GROUP, HEAD_DIM), (0, 2, 1, 3, 4)).reshape(nsb * ts, nq)
    x = proj_residual(os_, w_o, x_attn, 512)

    k_p = kp.reshape(batch, seq, N_KV, HEAD_DIM)[:, seq - WINDOW:]
    v_p = vp.reshape(batch, seq, N_KV, HEAD_DIM)[:, seq - WINDOW:]
    k_s = jnp.concatenate([cache_swa_k[0], ks.reshape(nsb, ts, N_KV, HEAD_DIM)], axis=1)[:, ts:]
    v_s = jnp.concatenate([cache_swa_v[0], vs.reshape(nsb, ts, N_KV, HEAD_DIM)], axis=1)[:, ts:]

    wr1, br1 = router_w(1)
    xp, xs = moe_layer((x,), row(rms_ffn_g[1]), wr1, br1, ew_in, ew_out, 1, split_rows=n_p)

    return (xp.reshape(batch, seq, d), xs.reshape(nsb, ts, d),
            conv_p[None], conv_s[None], k_p[None], v_p[None], k_s[None], v_s[None])
```

```python
import functools
import math

import numpy as np
import jax
import jax.numpy as jnp
from jax import lax
from jax.experimental import pallas as pl
from jax.experimental.pallas import tpu as pltpu

D_MODEL = 1024
DEPTH = 2
CONV_WIDTH = 31
PAST = CONV_WIDTH - 1
HEAD_DIM = 64
N_HEADS = 16
N_KV = 2
GROUP = 8
WINDOW = 128
N_BUCKETS = 32
MAX_DISTANCE = 128
N_GROUPS = 4
EPG = 8
N_EXPERTS = 32
D_EXPERT = 256
RMS_EPS = 1e-6
LN_EPS = 1e-5
NEG_INF = -1e30

F32 = jnp.float32
BF16 = jnp.bfloat16
LANES = 128
ROW_TILE = 8
MOE_TM = 256
MOE_TMG = 256
MOE_CHUNK = 8704
MOE_TS = 128
MOE_VMEM_LIMIT = 56 * 1024 * 1024
VMEM_LIMIT = 48 * 1024 * 1024


def _params(*sem):
    return pltpu.CompilerParams(dimension_semantics=sem, vmem_limit_bytes=VMEM_LIMIT)


def _rms(x, g):
    return x * lax.rsqrt(jnp.mean(x * x, axis=-1, keepdims=True) + RMS_EPS) * g


def _sigmoid(x):
    return 1.0 / (1.0 + jnp.exp(-x))


def _full(shape):
    return pl.BlockSpec(shape, lambda *_: (0,) * len(shape))


def _glu_kernel(x_ref, g_ref, wa_ref, wg_ref, u_ref):
    h = _rms(x_ref[...], g_ref[...]).astype(BF16)
    a = jnp.dot(h, wa_ref[...], preferred_element_type=F32)
    gate = jnp.dot(h, wg_ref[...], preferred_element_type=F32)
    u_ref[...] = a * _sigmoid(gate)


def glu_proj(x, g, w_in, tm):
    n, d = x.shape
    c = w_in.shape[1] // 2
    return pl.pallas_call(
        _glu_kernel,
        grid=(n // tm,),
        in_specs=[
            pl.BlockSpec((tm, d), lambda i: (i, 0)),
            _full((1, d)),
            pl.BlockSpec((d, c), lambda i: (0, 0)),
            pl.BlockSpec((d, c), lambda i: (0, 1)),
        ],
        out_specs=pl.BlockSpec((tm, c), lambda i: (i, 0)),
        out_shape=jax.ShapeDtypeStruct((n, c), F32),
        compiler_params=_params("parallel"),
        name="glu_proj",
    )(x, g, w_in, w_in)


def _ln_silu_out(y, lng, lnb, wout_ref, x):
    mu = jnp.mean(y, axis=-1, keepdims=True)
    yc = y - mu
    z = yc * lax.rsqrt(jnp.mean(yc * yc, axis=-1, keepdims=True) + LN_EPS) * lng + lnb
    z = z * _sigmoid(z)
    return x + jnp.dot(z.astype(BF16), wout_ref[...], preferred_element_type=F32)


HALO = 32
CONV_RC = 64
CONV_CC = 128


def _conv_prompt_kernel(ucur_ref, uprev_ref, x_ref, dww_ref, dwb_ref, lng_ref, lnb_ref,
                        wout_ref, o_ref, up_ref, y_ref):
    t = pl.program_id(1)
    tt, c = ucur_ref.shape
    keep = (t > 0).astype(F32)
    up_ref[0:HALO, :] = uprev_ref[...] * keep
    up_ref[HALO:HALO + tt, :] = ucur_ref[...]
    up_ref[HALO + tt:, :] = jnp.zeros((ROW_TILE, c), F32)
    off = HALO - PAST
    for r0 in range(0, tt, CONV_RC):
        for c0 in range(0, c, CONV_CC):
            y = jnp.zeros((CONV_RC, CONV_CC), F32) + dwb_ref[:, c0:c0 + CONV_CC]
            for s in range(ROW_TILE):
                v = None
                for q in range((off + CONV_WIDTH - 1) // ROW_TILE + 1):
                    k = ROW_TILE * q + s - off
                    if k < 0 or k >= CONV_WIDTH:
                        continue
                    lo = r0 + ROW_TILE * q
                    term = (up_ref[lo:lo + CONV_RC + ROW_TILE, c0:c0 + CONV_CC]
                            * dww_ref[k:k + 1, c0:c0 + CONV_CC])
                    v = term if v is None else v + term
                y = y + v[s:s + CONV_RC]
            y_ref[r0:r0 + CONV_RC, c0:c0 + CONV_CC] = y
    o_ref[...] = _ln_silu_out(y_ref[...], lng_ref[...], lnb_ref[...], wout_ref, x_ref[...])


def conv_prompt(u, x, dww, dwb, lng, lnb, wout, batch, seq, tt):
    n, c = u.shape
    d = x.shape[1]
    nt = seq // tt
    hb = tt // HALO
    return pl.pallas_call(
        _conv_prompt_kernel,
        grid=(batch, nt),
        in_specs=[
            pl.BlockSpec((tt, c), lambda b, t: (b * nt + t, 0)),
            pl.BlockSpec((HALO, c), lambda b, t: (jnp.maximum((b * nt + t) * hb - 1, 0), 0)),
            pl.BlockSpec((tt, d), lambda b, t: (b * nt + t, 0)),
            _full((HALO, c)), _full((1, c)), _full((1, c)), _full((1, c)),
            _full((c, d)),
        ],
        out_specs=pl.BlockSpec((tt, d), lambda b, t: (b * nt + t, 0)),
        out_shape=jax.ShapeDtypeStruct((n, d), F32),
        scratch_shapes=[pltpu.VMEM((tt + HALO + ROW_TILE, c), F32), pltpu.VMEM((tt, c), F32)],
        compiler_params=_params("parallel", "parallel"),
        name="conv_prompt",
    )(u, u, x, dww, dwb, lng, lnb, wout)


CONV_SB = 4


def _conv_sample_kernel(u_ref, st_ref, x_ref, dww_ref, dwb_ref, lng_ref, lnb_ref,
                        wout_ref, o_ref, up_ref, y_ref):
    bs, ts, c = u_ref.shape
    up_ref[:, 0:PAST, :] = st_ref[...]
    up_ref[:, PAST:PAST + ts, :] = u_ref[...]
    for b0 in range(0, bs, CONV_SB):
        acc = jnp.zeros((CONV_SB, ts, c), F32) + dwb_ref[...][None]
        for k in range(CONV_WIDTH):
            acc = acc + up_ref[b0:b0 + CONV_SB, k:k + ts, :] * dww_ref[k:k + 1, :][None]
        y_ref[b0 * ts:(b0 + CONV_SB) * ts, :] = acc.reshape(CONV_SB * ts, c)
    o_ref[...] = _ln_silu_out(y_ref[...], lng_ref[...], lnb_ref[...], wout_ref, x_ref[...])


def conv_sample(u3, state, x, dww, dwb, lng, lnb, wout, bs):
    nb, ts, c = u3.shape
    d = x.shape[1]
    return pl.pallas_call(
        _conv_sample_kernel,
        grid=(nb // bs,),
        in_specs=[
            pl.BlockSpec((bs, ts, c), lambda i: (i, 0, 0)),
            pl.BlockSpec((bs, PAST, c), lambda i: (i, 0, 0)),
            pl.BlockSpec((bs * ts, d), lambda i: (i, 0)),
            _full((HALO, c)), _full((1, c)), _full((1, c)), _full((1, c)),
            _full((c, d)),
        ],
        out_specs=pl.BlockSpec((bs * ts, d), lambda i: (i, 0)),
        out_shape=jax.ShapeDtypeStruct((nb * ts, d), F32),
        scratch_shapes=[pltpu.VMEM((bs, PAST + ts, c), F32), pltpu.VMEM((bs * ts, c), F32)],
        compiler_params=_params("parallel"),
        name="conv_sample",
    )(u3, state, x, dww, dwb, lng, lnb, wout)


def _token_specs(xs, tm):
    d = xs[0].shape[1]
    if len(xs) == 1:
        return [pl.BlockSpec((tm, d), lambda i: (i, 0))]
    first = xs[0].shape[0] // tm
    return [pl.BlockSpec((tm, d), lambda i: (jnp.minimum(i, first - 1), 0)),
            pl.BlockSpec((tm, d), lambda i: (jnp.maximum(i - first, 0), 0))]


def _read_tokens(x_refs, first_steps):
    if len(x_refs) == 1:
        return x_refs[0][...]
    return jnp.where(pl.program_id(0) < first_steps, x_refs[0][...], x_refs[1][...])


def _router_kernel(tiles_per_chunk, n_src, first_steps, *refs):
    x_refs = refs[:n_src]
    g_ref, wr_ref, br_ref, tri_ref, hs_ref, route_ref, route_t_ref, cnt_ref, carry_ref = refs[n_src:]
    h = _rms(_read_tokens(x_refs, first_steps), g_ref[...])
    tm, d = h.shape
    for j in range(d // LANES):
        hs_ref[pl.ds(j, tm, stride=ROW_TILE), :] = h[:, j * LANES:(j + 1) * LANES]
    h_hi = h.astype(BF16)
    h_lo = (h - h_hi.astype(F32)).astype(BF16)
    both = jnp.dot(h_hi, wr_ref[...], preferred_element_type=F32)
    cross = jnp.dot(h_lo, wr_ref[:, :LANES], preferred_element_type=F32)
    logits = both[:, :LANES] + both[:, LANES:] + cross + br_ref[...]
    lane = lax.broadcasted_iota(jnp.int32, logits.shape, 1).astype(F32)
    big = jnp.float32(LANES)
    is_g = (lane >= N_EXPERTS) & (lane < N_EXPERTS + N_GROUPS)
    gl = jnp.where(is_g, logits, NEG_INF)
    gm = jnp.max(gl, axis=-1, keepdims=True)
    g_sel = jnp.min(jnp.where(gl == gm, lane, big), axis=-1, keepdims=True) - N_EXPERTS
    gate_g = 1.0 / jnp.sum(jnp.where(is_g, jnp.exp(gl - gm), 0.0), axis=-1, keepdims=True)
    lo = g_sel * EPG
    in_grp = (lane >= lo) & (lane < lo + EPG)
    el = jnp.where(in_grp, logits, NEG_INF)
    v1 = jnp.max(el, axis=-1, keepdims=True)
    i1 = jnp.min(jnp.where(el == v1, lane, big), axis=-1, keepdims=True)
    el2 = jnp.where(lane == i1, NEG_INF, el)
    v2 = jnp.max(el2, axis=-1, keepdims=True)
    i2 = jnp.min(jnp.where(el2 == v2, lane, big), axis=-1, keepdims=True)
    e2 = jnp.exp(v2 - v1)
    w1 = gate_g / (1.0 + e2)
    w2 = gate_g * e2 / (1.0 + e2)
    @pl.when(pl.program_id(0) % tiles_per_chunk == 0)
    def _():
        carry_ref[...] = jnp.zeros_like(carry_ref)

    hit1 = lane == i1
    hit2 = lane == i2
    onehot = jnp.where(hit1 | hit2, 1.0, 0.0)
    before = carry_ref[...] + jnp.dot(tri_ref[...], onehot.astype(BF16), preferred_element_type=F32)
    r1 = jnp.sum(jnp.where(hit1, before, 0.0), axis=-1, keepdims=True)
    r2 = jnp.sum(jnp.where(hit2, before, 0.0), axis=-1, keepdims=True)
    carry_ref[...] += jnp.sum(onehot, axis=0, keepdims=True)
    cnt_ref[...] = jnp.broadcast_to(carry_ref[...], cnt_ref.shape)
    route = jnp.where(lane == 0.0, i1, jnp.where(lane == 1.0, i2, jnp.where(
        lane == 2.0, w1, jnp.where(lane == 3.0, w2, jnp.where(
            lane == 4.0, r1, jnp.where(lane == 5.0, r2, 0.0))))))
    route_ref[...] = route
    route_t_ref[...] = jnp.transpose(route)[0:ROW_TILE, :]


def router(xs, g, wr, br, tm, chunk):
    n = sum(x.shape[0] for x in xs)
    d = xs[0].shape[1]
    tpc = chunk // tm
    tri = jnp.asarray(np.tril(np.ones((tm, tm), np.float32), -1), BF16)
    return pl.pallas_call(
        functools.partial(_router_kernel, tpc, len(xs), xs[0].shape[0] // tm),
        grid=(n // tm,),
        in_specs=_token_specs(xs, tm) + [_full((1, d)),
                  _full((d, 2 * LANES)), _full((1, LANES)), _full((tm, tm))],
        out_specs=[pl.BlockSpec((tm * ROW_TILE, LANES), lambda i: (i, 0)),
                   pl.BlockSpec((tm, LANES), lambda i: (i, 0)),
                   pl.BlockSpec((ROW_TILE, tm), lambda i: (i, 0)),
                   pl.BlockSpec((ROW_TILE, LANES), lambda i: (i // tpc, 0))],
        out_shape=[jax.ShapeDtypeStruct((n * ROW_TILE, LANES), F32),
                   jax.ShapeDtypeStruct((n, LANES), F32),
                   jax.ShapeDtypeStruct((n // tm * ROW_TILE, tm), F32),
                   jax.ShapeDtypeStruct((n // chunk * ROW_TILE, LANES), F32)],
        scratch_shapes=[pltpu.VMEM((1, LANES), F32)],
        compiler_params=_params("arbitrary"),
        name="router",
    )(*xs, g, wr, br, tri)


def _plan_kernel(tiles_per_chunk, route_ref, tri_ref, rank_ref, cnt_ref, carry_ref):
    i = pl.program_id(0)

    @pl.when(i % tiles_per_chunk == 0)
    def _():
        carry_ref[...] = jnp.zeros_like(carry_ref)

    route = route_ref[...]
    lane = lax.broadcasted_iota(jnp.int32, route.shape, 1).astype(F32)
    i1 = route[:, 0:1]
    i2 = route[:, 1:2]
    hit1 = lane == i1
    hit2 = lane == i2
    onehot = jnp.where(hit1 | hit2, 1.0, 0.0)
    before = carry_ref[...] + jnp.dot(tri_ref[...], onehot.astype(BF16), preferred_element_type=F32)
    r1 = jnp.sum(jnp.where(hit1, before, 0.0), axis=-1, keepdims=True)
    r2 = jnp.sum(jnp.where(hit2, before, 0.0), axis=-1, keepdims=True)
    rank_ref[...] = jnp.where(lane == 0.0, r1, jnp.where(lane == 1.0, r2, 0.0))
    carry_ref[...] += jnp.sum(onehot, axis=0, keepdims=True)
    cnt_ref[...] = jnp.broadcast_to(carry_ref[...], cnt_ref.shape)


def plan(route, tm, chunk):
    n = route.shape[0]
    tpc = chunk // tm
    tri = jnp.asarray(np.tril(np.ones((tm, tm), np.float32), -1), BF16)
    return pl.pallas_call(
        functools.partial(_plan_kernel, tpc),
        grid=(n // tm,),
        in_specs=[pl.BlockSpec((tm, LANES), lambda i: (i, 0)), _full((tm, tm))],
        out_specs=[pl.BlockSpec((tm, LANES), lambda i: (i, 0)),
                   pl.BlockSpec((ROW_TILE, LANES), lambda i: (i // tpc, 0))],
        out_shape=[jax.ShapeDtypeStruct((n, LANES), F32),
                   jax.ShapeDtypeStruct((n // chunk * ROW_TILE, LANES), F32)],
        scratch_shapes=[pltpu.VMEM((1, LANES), F32)],
        compiler_params=_params("arbitrary"),
        name="moe_plan",
    )(route, tri)


def _gather_rows(idx_ref, src_hbm, dst, sem, n_rows):
    unroll = 8

    def body(c, carry):
        for u in range(unroll):
            m = c * unroll + u
            tok = idx_ref[m]
            pltpu.make_async_copy(
                src_hbm.at[pl.ds(pl.multiple_of(tok * ROW_TILE, ROW_TILE), ROW_TILE), :],
                dst.at[pl.ds(pl.multiple_of(m * ROW_TILE, ROW_TILE), ROW_TILE), :],
                sem).start()
        return carry

    lax.fori_loop(0, n_rows // unroll, body, 0)


def _wait_rows(src_hbm, dst, sem):
    pltpu.make_async_copy(src_hbm.at[pl.ds(0, dst.shape[0]), :], dst, sem).wait()


def _from_token_tiles(buf, start, n_rows, stride):
    return jnp.concatenate(
        [buf[pl.ds(start + j, n_rows, stride=stride), :] for j in range(ROW_TILE)], axis=-1)


def _experts_kernel(tiles_per_chunk, tmg, te_ref, nv_ref, pos_ref, fill_hbm, hs_hbm,
                    win_ref, wout_ref, ys_ref, hsv, gbuf_a, gbuf_b, src_ref, sems, winb, woutb):
    c = pl.program_id(0)
    j = pl.program_id(1)
    t = c * tiles_per_chunk + j
    nv = nv_ref[c]
    chunk = pos_ref.shape[0] // 2
    unroll = 8

    def tile(r):
        return pl.ds(pl.multiple_of(r * ROW_TILE, ROW_TILE), ROW_TILE)

    @pl.when(j == 0)
    def _():
        load = pltpu.make_async_copy(
            hs_hbm.at[pl.ds(pl.multiple_of(c * chunk * ROW_TILE, ROW_TILE), chunk * ROW_TILE), :],
            hsv.at[pl.ds(0, chunk * ROW_TILE), :], sems.at[0])
        load.start()
        fill = pltpu.make_async_copy(fill_hbm, src_ref, sems.at[1])
        fill.start()
        hsv[pl.ds(chunk * ROW_TILE, ROW_TILE), :] = jnp.zeros((ROW_TILE, LANES), F32)
        fill.wait()

        def scatter(i, carry):
            first = i * scatter_unroll
            tok0 = first // (2 * MOE_TM) * MOE_TM + first % MOE_TM
            for u in range(scatter_unroll):
                src_ref[pos_ref[first + u]] = tok0 + u
            return carry

        scatter_unroll = 32
        lax.fori_loop(0, 2 * chunk // scatter_unroll, scatter, 0)
        load.wait()

        def gather(i, carry):
            for u in range(unroll):
                m = i * unroll + u
                gbuf_a[tile(m), :] = hsv[tile(src_ref[m]), :]
            return carry

        lax.fori_loop(0, tmg // unroll, gather, 0)

    changed = jnp.logical_or(j == 0, te_ref[t] != te_ref[jnp.maximum(t - 1, 0)])

    @pl.when(jnp.logical_and(changed, j < nv))
    def _():
        winb[...] = win_ref[0].astype(BF16)
        woutb[...] = wout_ref[0].astype(BF16)

    def step(cur, nxt):
        base = jnp.minimum(j + 1, tiles_per_chunk - 1) * tmg
        for m in range(tmg):
            nxt[pl.ds(m * ROW_TILE, ROW_TILE), :] = hsv[tile(src_ref[base + m]), :]
        x = _from_token_tiles(cur, 0, tmg, ROW_TILE).astype(BF16)
        hid = jnp.dot(x, winb[...], preferred_element_type=F32)
        a = hid[:, :D_EXPERT]
        u = hid[:, D_EXPERT:]
        act = (a * _sigmoid(a) * u).astype(BF16)
        y = jnp.dot(act, woutb[...], preferred_element_type=F32)
        for k in range(ROW_TILE):
            ys_ref[pl.ds(k, tmg, stride=ROW_TILE), :] = y[:, k * LANES:(k + 1) * LANES]

    @pl.when(jnp.logical_and(j < nv, j % 2 == 0))
    def _():
        step(gbuf_a, gbuf_b)

    @pl.when(jnp.logical_and(j < nv, j % 2 == 1))
    def _():
        step(gbuf_b, gbuf_a)

    @pl.when(j >= nv)
    def _():
        ys_ref[...] = jnp.zeros_like(ys_ref)


def _experts_by_expert_kernel(tiles_per_chunk, tmg, t0_ref, nt_ref, pos_ref, fill_hbm, hs_hbm,
                              win_ref, wout_ref, ys_hbm, hsv, gbuf_a, gbuf_b, ystage_a, ystage_b,
                              src_ref, sems, winb, woutb):
    c = pl.program_id(0)
    e = pl.program_id(1)
    chunk = pos_ref.shape[0] // 2
    tile_rows = tmg * ROW_TILE
    scatter_unroll = 32
    gather_unroll = 8

    def tile(r):
        return pl.ds(pl.multiple_of(r * ROW_TILE, ROW_TILE), ROW_TILE)

    def ys_tile(t):
        return ys_hbm.at[pl.ds(pl.multiple_of((c * tiles_per_chunk + t) * tile_rows, tile_rows),
                               tile_rows), :]

    @pl.when(e == 0)
    def _():
        load = pltpu.make_async_copy(
            hs_hbm.at[pl.ds(pl.multiple_of(c * chunk * ROW_TILE, ROW_TILE), chunk * ROW_TILE), :],
            hsv.at[pl.ds(0, chunk * ROW_TILE), :], sems.at[0])
        load.start()
        fill = pltpu.make_async_copy(fill_hbm, src_ref, sems.at[1])
        fill.start()
        hsv[pl.ds(chunk * ROW_TILE, ROW_TILE), :] = jnp.zeros((ROW_TILE, LANES), F32)
        fill.wait()

        def scatter(i, carry):
            first = i * scatter_unroll
            tok0 = first // (2 * MOE_TM) * MOE_TM + first % MOE_TM
            for u in range(scatter_unroll):
                src_ref[pos_ref[first + u]] = tok0 + u
            return carry

        lax.fori_loop(0, 2 * chunk // scatter_unroll, scatter, 0)
        load.wait()

        def gather(i, carry):
            for u in range(gather_unroll):
                m = i * gather_unroll + u
                gbuf_a[tile(m), :] = hsv[tile(src_ref[m]), :]
            return carry

        lax.fori_loop(0, tmg // gather_unroll, gather, 0)

    t0 = t0_ref[c * N_EXPERTS + e]
    nt = nt_ref[c * N_EXPERTS + e]

    @pl.when(nt > 0)
    def _():
        winb[...] = win_ref[0].astype(BF16)
        woutb[...] = wout_ref[0].astype(BF16)

    def step(t, cur, nxt, ystage, sem):
        base = jnp.minimum(t + 1, tiles_per_chunk - 1) * tmg
        for m in range(tmg):
            nxt[pl.ds(m * ROW_TILE, ROW_TILE), :] = hsv[tile(src_ref[base + m]), :]
        x = _from_token_tiles(cur, 0, tmg, ROW_TILE).astype(BF16)
        hid = jnp.dot(x, winb[...], preferred_element_type=F32)
        a = hid[:, :D_EXPERT]
        u = hid[:, D_EXPERT:]
        act = (a * _sigmoid(a) * u).astype(BF16)
        y = jnp.dot(act, woutb[...], preferred_element_type=F32)

        @pl.when(t >= 2)
        def _():
            pltpu.make_async_copy(ystage, ys_tile(t), sem).wait()

        for k in range(ROW_TILE):
            ystage[pl.ds(k, tmg, stride=ROW_TILE), :] = y[:, k * LANES:(k + 1) * LANES]
        pltpu.make_async_copy(ystage, ys_tile(t), sem).start()

    def one_tile(k, carry):
        t = t0 + k

        @pl.when(t % 2 == 0)
        def _():
            step(t, gbuf_a, gbuf_b, ystage_a, sems.at[2])

        @pl.when(t % 2 == 1)
        def _():
            step(t, gbuf_b, gbuf_a, ystage_b, sems.at[3])

        return carry

    lax.fori_loop(0, nt, one_tile, 0)

    @pl.when(e == pl.num_programs(1) - 1)
    def _():
        total = t0 + nt

        @pl.when(total >= 1)
        def _():
            pltpu.make_async_copy(ystage_a, ys_tile(0), sems.at[2]).wait()

        @pl.when(total >= 2)
        def _():
            pltpu.make_async_copy(ystage_b, ys_tile(0), sems.at[3]).wait()

        ystage_a[...] = jnp.zeros_like(ystage_a)

        def zero_tile(t, carry):
            cp = pltpu.make_async_copy(ystage_a, ys_tile(t), sems.at[2])
            cp.start()
            cp.wait()
            return carry

        lax.fori_loop(total, tiles_per_chunk, zero_tile, 0)


def experts_by_expert(hs, pos_local, tile_start, n_tiles, w_in, w_out, layer, tmg, chunk, tiles_per_chunk):
    _, d, f2 = w_in.shape
    nc = hs.shape[0] // (chunk * ROW_TILE)
    rows_per_chunk = tiles_per_chunk * tmg
    fill = jnp.full((rows_per_chunk,), chunk, jnp.int32)
    off = layer * N_EXPERTS
    grid_spec = pltpu.PrefetchScalarGridSpec(
        num_scalar_prefetch=2,
        grid=(nc, N_EXPERTS),
        in_specs=[
            pl.BlockSpec((2 * chunk,), lambda c, e, t0, nt: (c,), memory_space=pltpu.SMEM),
            pl.BlockSpec(memory_space=pl.ANY),
            pl.BlockSpec(memory_space=pl.ANY),
            pl.BlockSpec((1, d, f2), lambda c, e, t0, nt: (off + e, 0, 0)),
            pl.BlockSpec((1, f2 // 2, d), lambda c, e, t0, nt: (off + e, 0, 0)),
        ],
        out_specs=pl.BlockSpec(memory_space=pl.ANY),
        scratch_shapes=[
            pltpu.VMEM(((chunk + 1) * ROW_TILE, LANES), F32),
            pltpu.VMEM((tmg * ROW_TILE, LANES), F32),
            pltpu.VMEM((tmg * ROW_TILE, LANES), F32),
            pltpu.VMEM((tmg * ROW_TILE, LANES), F32),
            pltpu.VMEM((tmg * ROW_TILE, LANES), F32),
            pltpu.SMEM((rows_per_chunk,), jnp.int32),
            pltpu.SemaphoreType.DMA((4,)),
            pltpu.VMEM((d, f2), BF16),
            pltpu.VMEM((f2 // 2, d), BF16),
        ],
    )
    return pl.pallas_call(
        functools.partial(_experts_by_expert_kernel, tiles_per_chunk, tmg),
        grid_spec=grid_spec,
        out_shape=jax.ShapeDtypeStruct((nc * rows_per_chunk * ROW_TILE, LANES), F32),
        compiler_params=pltpu.CompilerParams(dimension_semantics=("arbitrary", "arbitrary"),
                                             vmem_limit_bytes=MOE_VMEM_LIMIT),
        name="moe_experts",
    )(tile_start, n_tiles, pos_local, fill, hs, w_in, w_out)


def experts(hs, pos_local, tile_expert, n_valid, w_in, w_out, tmg, chunk, tiles_per_chunk):
    _, d, f2 = w_in.shape
    nc = hs.shape[0] // (chunk * ROW_TILE)
    rows_per_chunk = tiles_per_chunk * tmg
    fill = jnp.full((rows_per_chunk,), chunk, jnp.int32)
    grid_spec = pltpu.PrefetchScalarGridSpec(
        num_scalar_prefetch=2,
        grid=(nc, tiles_per_chunk),
        in_specs=[
            pl.BlockSpec((2 * chunk,), lambda c, j, te, nv: (c,), memory_space=pltpu.SMEM),
            pl.BlockSpec(memory_space=pl.ANY),
            pl.BlockSpec(memory_space=pl.ANY),
            pl.BlockSpec((1, d, f2), lambda c, j, te, nv: (te[c * tiles_per_chunk + j], 0, 0)),
            pl.BlockSpec((1, f2 // 2, d), lambda c, j, te, nv: (te[c * tiles_per_chunk + j], 0, 0)),
        ],
        out_specs=pl.BlockSpec((tmg * ROW_TILE, LANES),
                               lambda c, j, te, nv: (c * tiles_per_chunk + j, 0)),
        scratch_shapes=[
            pltpu.VMEM(((chunk + 1) * ROW_TILE, LANES), F32),
            pltpu.VMEM((tmg * ROW_TILE, LANES), F32),
            pltpu.VMEM((tmg * ROW_TILE, LANES), F32),
            pltpu.SMEM((rows_per_chunk,), jnp.int32),
            pltpu.SemaphoreType.DMA((2,)),
            pltpu.VMEM((d, f2), BF16),
            pltpu.VMEM((f2 // 2, d), BF16),
        ],
    )
    return pl.pallas_call(
        functools.partial(_experts_kernel, tiles_per_chunk, tmg),
        grid_spec=grid_spec,
        out_shape=jax.ShapeDtypeStruct((nc * rows_per_chunk * ROW_TILE, LANES), F32),
        compiler_params=pltpu.CompilerParams(dimension_semantics=("arbitrary", "arbitrary"),
                                             vmem_limit_bytes=MOE_VMEM_LIMIT),
        name="moe_experts",
    )(tile_expert, n_valid, pos_local, fill, hs, w_in, w_out)


def _combine_kernel(split_steps, n_src, first_steps, pos_cur_ref, pos_nxt_ref, ys_hbm, route_ref, *rest):
    x_refs, o_refs, (ybuf_a, ybuf_b, sems) = rest[:n_src], rest[n_src:-3], rest[-3:]
    i = pl.program_id(0)
    last = i == pl.num_programs(0) - 1
    tm = route_ref.shape[0]

    @pl.when(i == 0)
    def _():
        _gather_rows(pos_cur_ref, ys_hbm, ybuf_a, sems.at[0], 2 * tm)

    def step(cur, cur_sem, nxt, nxt_sem):
        _gather_rows(pos_nxt_ref, ys_hbm, nxt, nxt_sem, 2 * tm)
        _wait_rows(ys_hbm, cur, cur_sem)
        route = route_ref[...]
        y1 = _from_token_tiles(cur, 0, tm, ROW_TILE)
        y2 = _from_token_tiles(cur, tm * ROW_TILE, tm, ROW_TILE)
        val = _read_tokens(x_refs, first_steps) + route[:, 2:3] * y1 + route[:, 3:4] * y2
        if split_steps is None:
            o_refs[0][...] = val
        else:
            @pl.when(i < split_steps)
            def _():
                o_refs[0][...] = val

            @pl.when(i >= split_steps)
            def _():
                o_refs[1][...] = val

        @pl.when(last)
        def _():
            _wait_rows(ys_hbm, nxt, nxt_sem)

    @pl.when(i % 2 == 0)
    def _():
        step(ybuf_a, sems.at[0], ybuf_b, sems.at[1])

    @pl.when(i % 2 == 1)
    def _():
        step(ybuf_b, sems.at[1], ybuf_a, sems.at[0])


def combine(ys, pos, route, xs, tm, split_rows=None):
    n = sum(x.shape[0] for x in xs)
    d = xs[0].shape[1]
    nsteps = n // tm
    if split_rows is None:
        split_steps = None
        out_specs = pl.BlockSpec((tm, d), lambda i: (i, 0))
        out_shape = jax.ShapeDtypeStruct((n, d), F32)
    else:
        split_steps = split_rows // tm
        out_specs = [pl.BlockSpec((tm, d), lambda i: (jnp.minimum(i, split_steps - 1), 0)),
                     pl.BlockSpec((tm, d), lambda i: (jnp.maximum(i - split_steps, 0), 0))]
        out_shape = [jax.ShapeDtypeStruct((split_rows, d), F32),
                     jax.ShapeDtypeStruct((n - split_rows, d), F32)]
    return pl.pallas_call(
        functools.partial(_combine_kernel, split_steps, len(xs), xs[0].shape[0] // tm),
        grid=(nsteps,),
        in_specs=[
            pl.BlockSpec((2 * tm,), lambda i: (i,), memory_space=pltpu.SMEM),
            pl.BlockSpec((2 * tm,), lambda i: (jnp.minimum(i + 1, nsteps - 1),),
                         memory_space=pltpu.SMEM),
            pl.BlockSpec(memory_space=pl.ANY),
            pl.BlockSpec((tm, LANES), lambda i: (i, 0)),
        ] + _token_specs(xs, tm),
        out_specs=out_specs,
        out_shape=out_shape,
        scratch_shapes=[pltpu.VMEM((2 * tm * ROW_TILE, LANES), F32),
                        pltpu.VMEM((2 * tm * ROW_TILE, LANES), F32),
                        pltpu.SemaphoreType.DMA((2,))],
        compiler_params=_params("arbitrary"),
        name="moe_combine",
    )(pos, pos, ys, route, *xs)


def _moe_fused_kernel(ts, st_ref, cn_ref, pos_ref, wts_ref, hs_hbm, win_ref, wout_ref, acc_hbm,
                      hsv, acc, gbuf, ystage, src_ref, sems, winb, woutb):
    c = pl.program_id(0)
    e = pl.program_id(1)
    n_slots = pos_ref.shape[0]
    chunk = n_slots // 2
    rows = chunk * ROW_TILE
    unroll = 8

    def tile(r):
        return pl.ds(pl.multiple_of(r * ROW_TILE, ROW_TILE), ROW_TILE)

    @pl.when(e == 0)
    def _():
        load = pltpu.make_async_copy(
            hs_hbm.at[pl.ds(pl.multiple_of(c * rows, ROW_TILE), rows), :],
            hsv.at[pl.ds(0, rows), :], sems.at[0])
        load.start()
        hsv[pl.ds(rows, ROW_TILE), :] = jnp.zeros((ROW_TILE, LANES), F32)

        def zero(i, carry):
            acc[pl.ds(pl.multiple_of(i * 64, 64), 64), :] = jnp.zeros((64, LANES), F32)
            return carry

        lax.fori_loop(0, (rows + ROW_TILE) // 64, zero, 0)
        acc[pl.ds(rows + ROW_TILE - 64, 64), :] = jnp.zeros((64, LANES), F32)

        def scatter(i, carry):
            for u in range(unroll):
                s = i * unroll + u
                src_ref[pos_ref[s]] = s
            return carry

        lax.fori_loop(0, n_slots // unroll, scatter, 0)
        load.wait()

    winb[...] = win_ref[0].astype(BF16)
    woutb[...] = wout_ref[0].astype(BF16)
    start = st_ref[c * N_EXPERTS + e]
    cnt = cn_ref[c * N_EXPERTS + e]

    def sub_tile(k, carry):
        base = start + k * ts
        rem = cnt - k * ts

        def slot_of(m):
            s = src_ref[jnp.minimum(base + m, n_slots - 1)]
            valid = m < rem
            return jnp.where(valid, s >> 1, chunk), jnp.where(valid, wts_ref[s], 0.0)

        def gather(i, carry2):
            for u in range(unroll):
                m = i * unroll + u
                tok, _ = slot_of(m)
                gbuf[tile(m), :] = hsv[tile(tok), :]
            return carry2

        lax.fori_loop(0, ts // unroll, gather, 0)
        x = _from_token_tiles(gbuf, 0, ts, ROW_TILE).astype(BF16)
        hid = jnp.dot(x, winb[...], preferred_element_type=F32)
        a = hid[:, :D_EXPERT]
        u_ = hid[:, D_EXPERT:]
        act = (a * _sigmoid(a) * u_).astype(BF16)
        y = jnp.dot(act, woutb[...], preferred_element_type=F32)
        for j in range(ROW_TILE):
            ystage[pl.ds(j, ts, stride=ROW_TILE), :] = y[:, j * LANES:(j + 1) * LANES]

        def scatter_add(i, carry2):
            toks, vals = [], []
            for u in range(unroll):
                m = i * unroll + u
                tok, w = slot_of(m)
                toks.append(tok)
                vals.append(acc[tile(tok), :] + w * ystage[tile(m), :])
            for tok, val in zip(toks, vals):
                acc[tile(tok), :] = val
            return carry2

        lax.fori_loop(0, ts // unroll, scatter_add, 0)
        return carry

    lax.fori_loop(0, (cnt + ts - 1) // ts, sub_tile, 0)

    @pl.when(e == pl.num_programs(1) - 1)
    def _():
        store = pltpu.make_async_copy(
            acc.at[pl.ds(0, rows), :],
            acc_hbm.at[pl.ds(pl.multiple_of(c * rows, ROW_TILE), rows), :], sems.at[1])
        store.start()
        store.wait()


def moe_fused(hs, pos_local, wts, starts, counts, w_in, w_out, chunk, ts):
    ne, d, f2 = w_in.shape
    n_rows = hs.shape[0]
    nc = n_rows // (chunk * ROW_TILE)
    grid_spec = pltpu.PrefetchScalarGridSpec(
        num_scalar_prefetch=2,
        grid=(nc, ne),
        in_specs=[
            pl.BlockSpec((2 * chunk,), lambda c, e, st, cn: (c,), memory_space=pltpu.SMEM),
            pl.BlockSpec((2 * chunk,), lambda c, e, st, cn: (c,), memory_space=pltpu.SMEM),
            pl.BlockSpec(memory_space=pl.ANY),
            pl.BlockSpec((1, d, f2), lambda c, e, st, cn: (e, 0, 0)),
            pl.BlockSpec((1, f2 // 2, d), lambda c, e, st, cn: (e, 0, 0)),
        ],
        out_specs=pl.BlockSpec(memory_space=pl.ANY),
        scratch_shapes=[
            pltpu.VMEM(((chunk + 1) * ROW_TILE, LANES), F32),
            pltpu.VMEM(((chunk + 1) * ROW_TILE, LANES), F32),
            pltpu.VMEM((ts * ROW_TILE, LANES), F32),
            pltpu.VMEM((ts * ROW_TILE, LANES), F32),
            pltpu.SMEM((2 * chunk,), jnp.int32),
            pltpu.SemaphoreType.DMA((2,)),
            pltpu.VMEM((d, f2), BF16),
            pltpu.VMEM((f2 // 2, d), BF16),
        ],
    )
    return pl.pallas_call(
        functools.partial(_moe_fused_kernel, ts),
        grid_spec=grid_spec,
        out_shape=jax.ShapeDtypeStruct((n_rows, LANES), F32),
        compiler_params=pltpu.CompilerParams(dimension_semantics=("arbitrary", "arbitrary"),
                                             vmem_limit_bytes=MOE_VMEM_LIMIT),
        name="moe_fused",
    )(starts, counts, pos_local, wts, hs, w_in, w_out)


def _finish_kernel(acc_ref, x_ref, o_ref):
    tm = x_ref.shape[0]
    o_ref[...] = x_ref[...] + _from_token_tiles(acc_ref, 0, tm, ROW_TILE)


def moe_finish(acc, x, tm):
    n, d = x.shape
    return pl.pallas_call(
        _finish_kernel,
        grid=(n // tm,),
        in_specs=[pl.BlockSpec((tm * ROW_TILE, LANES), lambda i: (i, 0)),
                  pl.BlockSpec((tm, d), lambda i: (i, 0))],
        out_specs=pl.BlockSpec((tm, d), lambda i: (i, 0)),
        out_shape=jax.ShapeDtypeStruct((n, d), F32),
        compiler_params=_params("parallel"),
        name="moe_finish",
    )(acc, x)


def moe_layer_fused(x, g, wr, br, w_in, w_out):
    n = x.shape[0]
    chunk = min(n, MOE_CHUNK)
    nc = n // chunk
    hs, route, counts = router(x, g, wr, br, MOE_TM, chunk)
    ids = route[:, 0:2].astype(jnp.int32).reshape(nc, chunk, 2)
    cnt = counts.reshape(nc, ROW_TILE, LANES)[:, 0, :N_EXPERTS].astype(jnp.int32)
    starts = jnp.cumsum(cnt, axis=1) - cnt
    sel = ids[..., None] == jnp.arange(N_EXPERTS, dtype=jnp.int32)
    start_of = jnp.sum(jnp.where(sel, starts[:, None, None, :], 0), axis=-1)
    pos_local = start_of + route[:, 4:6].astype(jnp.int32).reshape(nc, chunk, 2)
    acc = moe_fused(hs, pos_local.reshape(-1), route[:, 2:4].reshape(-1), starts.reshape(-1),
                    cnt.reshape(-1), w_in, w_out, chunk, MOE_TS)
    return moe_finish(acc, x, MOE_TM)


def moe_layer(xs, g, wr, br, w_in, w_out, layer, split_rows=None):
    n = sum(x.shape[0] for x in xs)
    nc = -(-n // MOE_CHUNK)
    chunk = n // nc
    assert chunk * nc == n and chunk % MOE_TM == 0
    tpc = (2 * chunk + N_EXPERTS * (MOE_TMG - 1)) // MOE_TMG
    hs, route, route_t, counts = router(xs, g, wr, br, MOE_TM, chunk)
    rt = route_t.reshape(nc, chunk // MOE_TM, ROW_TILE, MOE_TM)
    ids = rt[:, :, 0:2, :].astype(jnp.int32)
    ranks = rt[:, :, 4:6, :].astype(jnp.int32)
    cnt = counts.reshape(nc, ROW_TILE, LANES)[:, 0, :N_EXPERTS].astype(jnp.int32)
    padded = (cnt + MOE_TMG - 1) // MOE_TMG * MOE_TMG
    ends = jnp.cumsum(padded, axis=1)
    starts = ends - padded
    start_of = jnp.zeros_like(ids)
    for e in range(N_EXPERTS):
        start_of = jnp.where(ids == e, starts[:, e][:, None, None, None], start_of)
    pos_local = start_of + ranks
    tile_row = jnp.arange(tpc, dtype=jnp.int32) * MOE_TMG
    tile_expert = jnp.minimum(jnp.sum(ends[:, None, :] <= tile_row[None, :, None], axis=-1),
                              N_EXPERTS - 1).astype(jnp.int32)
    n_valid = (ends[:, -1] // MOE_TMG).astype(jnp.int32)
    ys = experts_by_expert(hs, pos_local.reshape(-1), (starts // MOE_TMG).reshape(-1),
                           (padded // MOE_TMG).reshape(-1), w_in, w_out, layer, MOE_TMG, chunk, tpc)
    pos = pos_local + (jnp.arange(nc, dtype=jnp.int32) * (tpc * MOE_TMG))[:, None, None, None]
    return combine(ys, pos.reshape(-1), route, xs, MOE_TM, split_rows)


def _qkv_kernel(x_ref, g_ref, w_ref, qg_ref, kg_ref, seg_ref, q_ref, k_ref, v_ref):
    h = _rms(x_ref[...], g_ref[...]).astype(BF16)
    qkv = jnp.dot(h, w_ref[...], preferred_element_type=F32)
    nq = N_HEADS * HEAD_DIM
    nk = N_KV * HEAD_DIM
    q = qkv[:, :nq]
    k = qkv[:, nq:nq + nk]
    v_ref[...] = qkv[:, nq + nk:]

    def seg_mean_sq(z, seg, split):
        zz = z * z
        hi = zz.astype(BF16)
        ms = jnp.dot(hi, seg, preferred_element_type=F32)
        if split:
            lo = (zz - hi.astype(F32)).astype(BF16)
            ms = ms + jnp.dot(lo, seg, preferred_element_type=F32)
        return ms

    seg = seg_ref[...]
    qn = q * lax.rsqrt(seg_mean_sq(q, seg, False) + RMS_EPS) * qg_ref[...]
    q_ref[...] = (qn * (HEAD_DIM ** -0.5)).astype(BF16)
    kn = k * lax.rsqrt(seg_mean_sq(k, seg[:nk, :nk], True) + RMS_EPS) * kg_ref[...]
    k_ref[...] = kn


def qkv_proj(x, g, w, qg, kg, seg, tm):
    n, d = x.shape
    nq = N_HEADS * HEAD_DIM
    nk = N_KV * HEAD_DIM
    return pl.pallas_call(
        _qkv_kernel,
        grid=(n // tm,),
        in_specs=[pl.BlockSpec((tm, d), lambda i: (i, 0)), _full((1, d)),
                  _full((d, nq + 2 * nk)), _full((1, nq)), _full((1, nk)), _full((nq, nq))],
        out_specs=[pl.BlockSpec((tm, nq), lambda i: (i, 0)),
                   pl.BlockSpec((tm, nk), lambda i: (i, 0)),
                   pl.BlockSpec((tm, nk), lambda i: (i, 0))],
        out_shape=[jax.ShapeDtypeStruct((n, nq), BF16),
                   jax.ShapeDtypeStruct((n, nk), F32),
                   jax.ShapeDtypeStruct((n, nk), F32)],
        compiler_params=_params("parallel"),
        name="qkv_proj",
    )(x, g, w, qg, kg, seg)


def _attn_prompt_kernel(q_ref, kc_ref, kp_ref, vc_ref, vp_ref, tbl_ref, sink_ref, x_ref,
                        wo_ref, o_ref, att_ref, s_ref, e_ref, inv_ref):
    n = pl.program_id(1)
    first = jnp.where(n == 0, NEG_INF, 0.0).astype(F32)
    nt = (((1,), (1,)), ((), ()))
    lo_half = lax.broadcasted_iota(jnp.int32, (WINDOW, LANES), 1) < HEAD_DIM
    kp, kc = kp_ref[...], kc_ref[...]
    keys = (kp.astype(BF16), kc.astype(BF16))
    keys_swapped = (pltpu.roll(kp, HEAD_DIM, axis=1).astype(BF16),
                    pltpu.roll(kc, HEAD_DIM, axis=1).astype(BF16))
    vt_prev = jnp.transpose(vp_ref[...]).astype(BF16)
    vt_cur = jnp.transpose(vc_ref[...]).astype(BF16)
    for h in range(N_HEADS):
        kh = h // GROUP
        odd = h % 2
        pair = q_ref[:, (h // 2) * LANES:(h // 2 + 1) * LANES]
        qm = jnp.where(lo_half if odd == 0 else ~lo_half, pair, jnp.zeros_like(pair))
        k_prev, k_cur = keys_swapped if (kh == 0) == (odd == 1) else keys
        s_ref[h, 0:WINDOW, :] = (lax.dot_general(k_prev, qm, nt, preferred_element_type=F32)
                                 + tbl_ref[h, 0:WINDOW, :] + first)
        s_ref[h, WINDOW:, :] = (lax.dot_general(k_cur, qm, nt, preferred_element_type=F32)
                                + tbl_ref[h, WINDOW:, :])
    for h in range(N_HEADS):
        s = s_ref[h]
        sink = sink_ref[h]
        m = jnp.maximum(jnp.max(s, axis=0, keepdims=True), sink)
        e = jnp.exp(s - m)
        e_ref[h] = e.astype(BF16)
        inv_ref[h] = 1.0 / (jnp.sum(e, axis=0, keepdims=True) + jnp.exp(sink - m))
    for h in range(N_HEADS):
        kh = h // GROUP
        dims = slice(kh * HEAD_DIM, (kh + 1) * HEAD_DIM)
        out_t = (jnp.dot(vt_prev[dims, :], e_ref[h, 0:WINDOW, :], preferred_element_type=F32)
                 + jnp.dot(vt_cur[dims, :], e_ref[h, WINDOW:, :], preferred_element_type=F32))
        att_ref[h * HEAD_DIM:(h + 1) * HEAD_DIM, :] = out_t * inv_ref[h]
    att = jnp.transpose(att_ref[...]).astype(BF16)
    o_ref[...] = x_ref[...] + jnp.dot(att, wo_ref[...], preferred_element_type=F32)


def attn_prompt(q, k, v, tbl, sinks, x, wo, batch, seq):
    n, d = x.shape
    nb = seq // WINDOW
    nk = N_KV * HEAD_DIM
    cur = lambda b, i: (b * nb + i, 0)
    prev = lambda b, i: (b * nb + jnp.maximum(i - 1, 0), 0)
    return pl.pallas_call(
        _attn_prompt_kernel,
        grid=(batch, nb),
        in_specs=[
            pl.BlockSpec((WINDOW, d), cur),
            pl.BlockSpec((WINDOW, nk), cur), pl.BlockSpec((WINDOW, nk), prev),
            pl.BlockSpec((WINDOW, nk), cur), pl.BlockSpec((WINDOW, nk), prev),
            _full((N_HEADS, 2 * WINDOW, WINDOW)),
            pl.BlockSpec(memory_space=pltpu.SMEM),
            pl.BlockSpec((WINDOW, d), cur),
            _full((d, d)),
        ],
        out_specs=pl.BlockSpec((WINDOW, d), cur),
        out_shape=jax.ShapeDtypeStruct((n, d), F32),
        input_output_aliases={7: 0},
        scratch_shapes=[pltpu.VMEM((N_HEADS * HEAD_DIM, WINDOW), F32),
                        pltpu.VMEM((N_HEADS, 2 * WINDOW, WINDOW), F32),
                        pltpu.VMEM((N_HEADS, 2 * WINDOW, WINDOW), BF16),
                        pltpu.VMEM((N_HEADS, 1, WINDOW), F32)],
        compiler_params=_params("parallel", "parallel"),
        name="attn_prompt",
    )(q, k, k, v, v, tbl, sinks, x, wo)


def _attn_sample_kernel(q_ref, kn_ref, vn_ref, ck_ref, cv_ref, tblc_ref, tbln_ref, sink_ref, o_ref):
    bs = q_ref.shape[0]
    nt = (((1,), (1,)), ((), ()))
    for b in range(bs):
        ck = ck_ref[b].astype(BF16)
        cv = cv_ref[b].astype(BF16)
        kn = kn_ref[b].astype(BF16)
        vn = vn_ref[b].astype(BF16)
        for kh in range(N_KV):
            lanes = slice(kh * HEAD_DIM, (kh + 1) * HEAD_DIM)
            qt = q_ref[b, kh]
            sc = lax.dot_general(qt, ck[:, lanes], nt, preferred_element_type=F32) + tblc_ref[kh]
            sn = lax.dot_general(qt, kn[:, lanes], nt, preferred_element_type=F32) + tbln_ref[kh]
            sink = sink_ref[kh]
            m = jnp.maximum(jnp.maximum(jnp.max(sc, axis=-1, keepdims=True),
                                        jnp.max(sn, axis=-1, keepdims=True)), sink)
            ec = jnp.exp(sc - m)
            en = jnp.exp(sn - m)
            denom = (jnp.sum(ec, axis=-1, keepdims=True) + jnp.sum(en, axis=-1, keepdims=True)
                     + jnp.exp(sink - m))
            out = (jnp.dot(ec.astype(BF16), cv[:, lanes], preferred_element_type=F32)
                   + jnp.dot(en.astype(BF16), vn[:, lanes], preferred_element_type=F32))
            o_ref[b, kh] = (out * (1.0 / denom)).astype(BF16)


def attn_sample(q4, kn, vn, ck, cv, tbl, sink, bs):
    nb = q4.shape[0]
    ts = kn.shape[1]
    nk = N_KV * HEAD_DIM
    tg = ts * GROUP
    kt = WINDOW + ts
    return pl.pallas_call(
        _attn_sample_kernel,
        grid=(nb // bs,),
        in_specs=[
            pl.BlockSpec((bs, N_KV, tg, HEAD_DIM), lambda i: (i, 0, 0, 0)),
            pl.BlockSpec((bs, ts, nk), lambda i: (i, 0, 0)),
            pl.BlockSpec((bs, ts, nk), lambda i: (i, 0, 0)),
            pl.BlockSpec((bs, WINDOW, nk), lambda i: (i, 0, 0)),
            pl.BlockSpec((bs, WINDOW, nk), lambda i: (i, 0, 0)),
            _full((N_KV, tg, WINDOW)),
            _full((N_KV, tg, ts)),
            _full((N_KV, tg, 1)),
        ],
        out_specs=pl.BlockSpec((bs, N_KV, tg, HEAD_DIM), lambda i: (i, 0, 0, 0)),
        out_shape=jax.ShapeDtypeStruct((nb, N_KV, tg, HEAD_DIM), BF16),
        compiler_params=_params("parallel"),
        name="attn_sample",
    )(q4, kn, vn, ck, cv, tbl[:, :, :WINDOW], tbl[:, :, WINDOW:], sink)


def _proj_res_kernel(a_ref, w_ref, x_ref, o_ref):
    o_ref[...] = x_ref[...] + jnp.dot(a_ref[...], w_ref[...], preferred_element_type=F32)


def proj_residual(a, w, x, tm):
    n, d = x.shape
    na, kdim = a.shape
    off = (n - na) // tm
    return pl.pallas_call(
        _proj_res_kernel,
        grid=(na // tm,),
        in_specs=[pl.BlockSpec((tm, kdim), lambda i: (i, 0)), _full((kdim, d)),
                  pl.BlockSpec((tm, d), lambda i: (off + i, 0))],
        out_specs=pl.BlockSpec((tm, d), lambda i: (off + i, 0)),
        out_shape=jax.ShapeDtypeStruct((n, d), F32),
        input_output_aliases={2: 0},
        compiler_params=_params("parallel"),
        name="proj_residual",
    )(a, w, x)


def _t5_bucket_np(dist):
    n = np.maximum(dist, 0)
    max_exact = N_BUCKETS // 2
    large = max_exact + (np.log(np.maximum(n, 1).astype(np.float32) / max_exact)
                         / math.log(MAX_DISTANCE / max_exact) * (N_BUCKETS - max_exact)).astype(np.int32)
    large = np.minimum(large, N_BUCKETS - 1)
    return np.where(n < max_exact, n, large)


def _bias_table(rel_bias, dist):
    valid = (dist >= 0) & (dist <= WINDOW)
    onehot = (np.asarray(_t5_bucket_np(dist))[..., None] == np.arange(N_BUCKETS)).astype(np.float32)
    b = jnp.einsum("qkb,bh->hqk", jnp.asarray(onehot), rel_bias.astype(F32),
                   precision=lax.Precision.HIGHEST)
    return jnp.where(jnp.asarray(valid)[None], b, NEG_INF)


def kernel(x_prompt, x_sample, state_conv, cache_swa_k, cache_swa_v, rms_mix_g, rms_ffn_g, conv_w_in, conv_dw_w, conv_dw_b, conv_ln_g, conv_ln_b, conv_w_out, attn_w_qkv, attn_q_norm_g, attn_k_norm_g, attn_sinks, attn_w_o, rel_bias, router_group_w, router_group_b, router_expert_w, router_expert_b, expert_w_in, expert_w_out):
    batch, seq, d = x_prompt.shape
    nsb, ts, _ = x_sample.shape
    xp = x_prompt.reshape(batch * seq, d)
    xs = x_sample.reshape(nsb * ts, d)
    row = lambda a: a.reshape(1, -1).astype(F32)

    def router_w(i):
        we = jnp.transpose(router_expert_w[i], (1, 0, 2)).reshape(d, N_EXPERTS)
        wr = jnp.concatenate([we, router_group_w[i]], axis=1)
        wr = jnp.pad(wr, ((0, 0), (0, LANES - wr.shape[1])))
        br = jnp.concatenate([router_expert_b[i].reshape(-1), router_group_b[i]])
        br = jnp.pad(br, (0, LANES - br.shape[0])).reshape(1, LANES)
        wr = wr.astype(F32)
        w_hi = wr.astype(BF16)
        w_lo = (wr - w_hi.astype(F32)).astype(BF16)
        return jnp.concatenate([w_hi, w_lo], axis=1), br.astype(F32)

    g0 = row(rms_mix_g[0])
    w_in = conv_w_in[0].astype(BF16)
    dww = jnp.pad(conv_dw_w[0].astype(F32), ((0, HALO - CONV_WIDTH), (0, 0)))
    dwb, lng, lnb = row(conv_dw_b[0]), row(conv_ln_g[0]), row(conv_ln_b[0])
    w_out = conv_w_out[0].astype(BF16)
    up = glu_proj(xp, g0, w_in, 512)
    us = glu_proj(xs, g0, w_in, 512)
    n_p, n_s = batch * seq, nsb * ts
    n_all = n_p + n_s
    xp = conv_prompt(up, xp, dww, dwb, lng, lnb, w_out, batch, seq, 256)
    us3 = us.reshape(nsb, ts, -1)
    xs = conv_sample(us3, state_conv[0], xs, dww, dwb, lng, lnb, w_out, 32)
    conv_p = up.reshape(batch, seq, -1)[:, seq - PAST:]
    conv_s = jnp.concatenate([state_conv[0], us3], axis=1)[:, ts:]

    wr0, br0 = router_w(0)
    ew_in = expert_w_in.reshape((DEPTH * N_EXPERTS,) + expert_w_in.shape[2:])
    ew_out = expert_w_out.reshape((DEPTH * N_EXPERTS,) + expert_w_out.shape[2:])
    x = moe_layer((xp, xs), row(rms_ffn_g[0]), wr0, br0, ew_in, ew_out, 0)

    g1 = row(rms_mix_g[1])
    w_qkv = attn_w_qkv[0].astype(BF16)
    qg = jnp.tile(attn_q_norm_g[0].astype(F32), N_HEADS).reshape(1, -1)
    kg = jnp.tile(attn_k_norm_g[0].astype(F32), N_KV).reshape(1, -1)
    nq = N_HEADS * HEAD_DIM
    seg = jnp.asarray(np.kron(np.eye(N_HEADS), np.ones((HEAD_DIM, HEAD_DIM))) / HEAD_DIM, BF16)
    w_o = attn_w_o[0].astype(BF16)
    sinks = attn_sinks[0].astype(F32)

    q, k, v = qkv_proj(x, g1, w_qkv, qg, kg, seg, 512)
    kp, vp = k[:n_p], v[:n_p]
    qs, ks, vs = q[n_p:], k[n_p:], v[n_p:]

    q_off = np.arange(WINDOW)[:, None]
    dist_p = q_off + WINDOW - np.arange(2 * WINDOW)[None, :]
    tbl_p = jnp.transpose(_bias_table(rel_bias, dist_p), (0, 2, 1))
    x_attn = attn_prompt(q, k, v, tbl_p, sinks, x, w_o, batch, seq)

    kt = WINDOW + ts
    dist_s = np.arange(ts)[:, None] + WINDOW - np.arange(kt)[None, :]
    tbl_s = _bias_table(rel_bias, dist_s)
    tbl_s = jnp.transpose(tbl_s.reshape(N_KV, GROUP, ts, kt), (0, 2, 1, 3)).reshape(N_KV, ts * GROUP, kt)
    sink_s = jnp.tile(sinks.reshape(N_KV, 1, GROUP), (1, ts, 1)).reshape(N_KV, ts * GROUP, 1)
    q4 = jnp.transpose(qs.reshape(nsb, ts, N_KV, GROUP, HEAD_DIM), (0, 2, 1, 3, 4))
    q4 = q4.reshape(nsb, N_KV, ts * GROUP, HEAD_DIM)
    nk = N_KV * HEAD_DIM
    ks3, vs3 = ks.reshape(nsb, ts, nk), vs.reshape(nsb, ts, nk)
    ck = cache_swa_k[0].reshape(nsb, WINDOW, nk)
    cv = cache_swa_v[0].reshape(nsb, WINDOW, nk)
    o4 = attn_sample(q4, ks3, vs3, ck, cv, tbl_s, sink_s, 16)
    os_ = jnp.transpose(o4.reshape(nsb, N_KV, ts, GROUP, HEAD_DIM), (0, 2, 1, 3, 4)).reshape(nsb * ts, nq)
    x = proj_residual(os_, w_o, x_attn, 512)

    k_p = kp.reshape(batch, seq, N_KV, HEAD_DIM)[:, seq - WINDOW:]
    v_p = vp.reshape(batch, seq, N_KV, HEAD_DIM)[:, seq - WINDOW:]
    k_s = jnp.concatenate([cache_swa_k[0], ks.reshape(nsb, ts, N_KV, HEAD_DIM)], axis=1)[:, ts:]
    v_s = jnp.concatenate([cache_swa_v[0], vs.reshape(nsb, ts, N_KV, HEAD_DIM)], axis=1)[:, ts:]

    wr1, br1 = router_w(1)
    xp, xs = moe_layer((x,), row(rms_ffn_g[1]), wr1, br1, ew_in, ew_out, 1, split_rows=n_p)

    return (xp.reshape(batch, seq, d), xs.reshape(nsb, ts, d),
            conv_p[None], conv_s[None], k_p[None], v_p[None], k_s[None], v_s[None])
```

```python
import functools
import math

import numpy as np
import jax
import jax.numpy as jnp
from jax import lax
from jax.experimental import pallas as pl
from jax.experimental.pallas import tpu as pltpu

D_MODEL = 1024
DEPTH = 2
CONV_WIDTH = 31
PAST = CONV_WIDTH - 1
HEAD_DIM = 64
N_HEADS = 16
N_KV = 2
GROUP = 8
WINDOW = 128
N_BUCKETS = 32
MAX_DISTANCE = 128
N_GROUPS = 4
EPG = 8
N_EXPERTS = 32
D_EXPERT = 256
RMS_EPS = 1e-6
LN_EPS = 1e-5
NEG_INF = -1e30

F32 = jnp.float32
BF16 = jnp.bfloat16
LANES = 128
ROW_TILE = 8
MOE_TM = 512
MOE_TMG = 256
MOE_CHUNK = 8704
MOE_TS = 128
MOE_VMEM_LIMIT = 56 * 1024 * 1024
VMEM_LIMIT = 48 * 1024 * 1024


def _params(*sem):
    return pltpu.CompilerParams(dimension_semantics=sem, vmem_limit_bytes=VMEM_LIMIT)


def _rms(x, g):
    return x * lax.rsqrt(jnp.mean(x * x, axis=-1, keepdims=True) + RMS_EPS) * g


def _sigmoid(x):
    return 1.0 / (1.0 + jnp.exp(-x))


def _full(shape):
    return pl.BlockSpec(shape, lambda *_: (0,) * len(shape))


def _glu_kernel(x_ref, g_ref, wa_ref, wg_ref, u_ref):
    h = _rms(x_ref[...], g_ref[...]).astype(BF16)
    a = jnp.dot(h, wa_ref[...], preferred_element_type=F32)
    gate = jnp.dot(h, wg_ref[...], preferred_element_type=F32)
    u_ref[...] = a * _sigmoid(gate)


def glu_proj(x, g, w_in, tm):
    n, d = x.shape
    c = w_in.shape[1] // 2
    return pl.pallas_call(
        _glu_kernel,
        grid=(n // tm,),
        in_specs=[
            pl.BlockSpec((tm, d), lambda i: (i, 0)),
            _full((1, d)),
            pl.BlockSpec((d, c), lambda i: (0, 0)),
            pl.BlockSpec((d, c), lambda i: (0, 1)),
        ],
        out_specs=pl.BlockSpec((tm, c), lambda i: (i, 0)),
        out_shape=jax.ShapeDtypeStruct((n, c), F32),
        compiler_params=_params("parallel"),
        name="glu_proj",
    )(x, g, w_in, w_in)


def _ln_silu_out(y, lng, lnb, wout_ref, x):
    mu = jnp.mean(y, axis=-1, keepdims=True)
    yc = y - mu
    z = yc * lax.rsqrt(jnp.mean(yc * yc, axis=-1, keepdims=True) + LN_EPS) * lng + lnb
    z = z * _sigmoid(z)
    return x + jnp.dot(z.astype(BF16), wout_ref[...], preferred_element_type=F32)


HALO = 32
CONV_RC = 64
CONV_CC = 128


def _conv_prompt_kernel(ucur_ref, uprev_ref, x_ref, dww_ref, dwb_ref, lng_ref, lnb_ref,
                        wout_ref, o_ref, up_ref, y_ref):
    t = pl.program_id(1)
    tt, c = ucur_ref.shape
    keep = (t > 0).astype(F32)
    up_ref[0:HALO, :] = uprev_ref[...] * keep
    up_ref[HALO:HALO + tt, :] = ucur_ref[...]
    up_ref[HALO + tt:, :] = jnp.zeros((ROW_TILE, c), F32)
    off = HALO - PAST
    for r0 in range(0, tt, CONV_RC):
        for c0 in range(0, c, CONV_CC):
            y = jnp.zeros((CONV_RC, CONV_CC), F32) + dwb_ref[:, c0:c0 + CONV_CC]
            for s in range(ROW_TILE):
                v = None
                for q in range((off + CONV_WIDTH - 1) // ROW_TILE + 1):
                    k = ROW_TILE * q + s - off
                    if k < 0 or k >= CONV_WIDTH:
                        continue
                    lo = r0 + ROW_TILE * q
                    term = (up_ref[lo:lo + CONV_RC + ROW_TILE, c0:c0 + CONV_CC]
                            * dww_ref[k:k + 1, c0:c0 + CONV_CC])
                    v = term if v is None else v + term
                y = y + v[s:s + CONV_RC]
            y_ref[r0:r0 + CONV_RC, c0:c0 + CONV_CC] = y
    o_ref[...] = _ln_silu_out(y_ref[...], lng_ref[...], lnb_ref[...], wout_ref, x_ref[...])


def conv_prompt(u, x, dww, dwb, lng, lnb, wout, batch, seq, tt):
    n, c = u.shape
    d = x.shape[1]
    nt = seq // tt
    hb = tt // HALO
    return pl.pallas_call(
        _conv_prompt_kernel,
        grid=(batch, nt),
        in_specs=[
            pl.BlockSpec((tt, c), lambda b, t: (b * nt + t, 0)),
            pl.BlockSpec((HALO, c), lambda b, t: (jnp.maximum((b * nt + t) * hb - 1, 0), 0)),
            pl.BlockSpec((tt, d), lambda b, t: (b * nt + t, 0)),
            _full((HALO, c)), _full((1, c)), _full((1, c)), _full((1, c)),
            _full((c, d)),
        ],
        out_specs=pl.BlockSpec((tt, d), lambda b, t: (b * nt + t, 0)),
        out_shape=jax.ShapeDtypeStruct((n, d), F32),
        scratch_shapes=[pltpu.VMEM((tt + HALO + ROW_TILE, c), F32), pltpu.VMEM((tt, c), F32)],
        compiler_params=_params("parallel", "parallel"),
        name="conv_prompt",
    )(u, u, x, dww, dwb, lng, lnb, wout)


CONV_SB = 4


def _conv_sample_kernel(u_ref, st_ref, x_ref, dww_ref, dwb_ref, lng_ref, lnb_ref,
                        wout_ref, o_ref, up_ref, y_ref):
    bs, ts, c = u_ref.shape
    up_ref[:, 0:PAST, :] = st_ref[...]
    up_ref[:, PAST:PAST + ts, :] = u_ref[...]
    for b0 in range(0, bs, CONV_SB):
        acc = jnp.zeros((CONV_SB, ts, c), F32) + dwb_ref[...][None]
        for k in range(CONV_WIDTH):
            acc = acc + up_ref[b0:b0 + CONV_SB, k:k + ts, :] * dww_ref[k:k + 1, :][None]
        y_ref[b0 * ts:(b0 + CONV_SB) * ts, :] = acc.reshape(CONV_SB * ts, c)
    o_ref[...] = _ln_silu_out(y_ref[...], lng_ref[...], lnb_ref[...], wout_ref, x_ref[...])


def conv_sample(u3, state, x, dww, dwb, lng, lnb, wout, bs):
    nb, ts, c = u3.shape
    d = x.shape[1]
    return pl.pallas_call(
        _conv_sample_kernel,
        grid=(nb // bs,),
        in_specs=[
            pl.BlockSpec((bs, ts, c), lambda i: (i, 0, 0)),
            pl.BlockSpec((bs, PAST, c), lambda i: (i, 0, 0)),
            pl.BlockSpec((bs * ts, d), lambda i: (i, 0)),
            _full((HALO, c)), _full((1, c)), _full((1, c)), _full((1, c)),
            _full((c, d)),
        ],
        out_specs=pl.BlockSpec((bs * ts, d), lambda i: (i, 0)),
        out_shape=jax.ShapeDtypeStruct((nb * ts, d), F32),
        scratch_shapes=[pltpu.VMEM((bs, PAST + ts, c), F32), pltpu.VMEM((bs * ts, c), F32)],
        compiler_params=_params("parallel"),
        name="conv_sample",
    )(u3, state, x, dww, dwb, lng, lnb, wout)


def _token_specs(xs, tm):
    d = xs[0].shape[1]
    if len(xs) == 1:
        return [pl.BlockSpec((tm, d), lambda i: (i, 0))]
    first = xs[0].shape[0] // tm
    return [pl.BlockSpec((tm, d), lambda i: (jnp.minimum(i, first - 1), 0)),
            pl.BlockSpec((tm, d), lambda i: (jnp.maximum(i - first, 0), 0))]


def _read_tokens(x_refs, first_steps):
    if len(x_refs) == 1:
        return x_refs[0][...]
    return jnp.where(pl.program_id(0) < first_steps, x_refs[0][...], x_refs[1][...])


def _router_kernel(tiles_per_chunk, n_src, first_steps, *refs):
    x_refs = refs[:n_src]
    g_ref, wr_ref, br_ref, tri_ref, hs_ref, route_ref, route_t_ref, cnt_ref, carry_ref = refs[n_src:]
    h = _rms(_read_tokens(x_refs, first_steps), g_ref[...])
    tm, d = h.shape
    for j in range(d // LANES):
        hs_ref[pl.ds(j, tm, stride=ROW_TILE), :] = h[:, j * LANES:(j + 1) * LANES]
    h_hi = h.astype(BF16)
    h_lo = (h - h_hi.astype(F32)).astype(BF16)
    both = jnp.dot(h_hi, wr_ref[...], preferred_element_type=F32)
    cross = jnp.dot(h_lo, wr_ref[:, :LANES], preferred_element_type=F32)
    logits = both[:, :LANES] + both[:, LANES:] + cross + br_ref[...]
    lane = lax.broadcasted_iota(jnp.int32, logits.shape, 1).astype(F32)
    big = jnp.float32(LANES)
    is_g = (lane >= N_EXPERTS) & (lane < N_EXPERTS + N_GROUPS)
    gl = jnp.where(is_g, logits, NEG_INF)
    gm = jnp.max(gl, axis=-1, keepdims=True)
    g_sel = jnp.min(jnp.where(gl == gm, lane, big), axis=-1, keepdims=True) - N_EXPERTS
    gate_g = 1.0 / jnp.sum(jnp.where(is_g, jnp.exp(gl - gm), 0.0), axis=-1, keepdims=True)
    lo = g_sel * EPG
    in_grp = (lane >= lo) & (lane < lo + EPG)
    el = jnp.where(in_grp, logits, NEG_INF)
    v1 = jnp.max(el, axis=-1, keepdims=True)
    i1 = jnp.min(jnp.where(el == v1, lane, big), axis=-1, keepdims=True)
    el2 = jnp.where(lane == i1, NEG_INF, el)
    v2 = jnp.max(el2, axis=-1, keepdims=True)
    i2 = jnp.min(jnp.where(el2 == v2, lane, big), axis=-1, keepdims=True)
    e2 = jnp.exp(v2 - v1)
    w1 = gate_g / (1.0 + e2)
    w2 = gate_g * e2 / (1.0 + e2)
    @pl.when(pl.program_id(0) % tiles_per_chunk == 0)
    def _():
        carry_ref[...] = jnp.zeros_like(carry_ref)

    hit1 = lane == i1
    hit2 = lane == i2
    onehot = jnp.where(hit1 | hit2, 1.0, 0.0)
    before = carry_ref[...] + jnp.dot(tri_ref[...], onehot.astype(BF16), preferred_element_type=F32)
    r1 = jnp.sum(jnp.where(hit1, before, 0.0), axis=-1, keepdims=True)
    r2 = jnp.sum(jnp.where(hit2, before, 0.0), axis=-1, keepdims=True)
    carry_ref[...] += jnp.sum(onehot, axis=0, keepdims=True)
    cnt_ref[...] = jnp.broadcast_to(carry_ref[...], cnt_ref.shape)
    route = jnp.where(lane == 0.0, i1, jnp.where(lane == 1.0, i2, jnp.where(
        lane == 2.0, w1, jnp.where(lane == 3.0, w2, jnp.where(
            lane == 4.0, r1, jnp.where(lane == 5.0, r2, 0.0))))))
    route_ref[...] = route
    route_t_ref[...] = jnp.transpose(route)[0:ROW_TILE, :]


def router(xs, g, wr, br, tm, chunk):
    n = sum(x.shape[0] for x in xs)
    d = xs[0].shape[1]
    tpc = chunk // tm
    tri = jnp.asarray(np.tril(np.ones((tm, tm), np.float32), -1), BF16)
    return pl.pallas_call(
        functools.partial(_router_kernel, tpc, len(xs), xs[0].shape[0] // tm),
        grid=(n // tm,),
        in_specs=_token_specs(xs, tm) + [_full((1, d)),
                  _full((d, 2 * LANES)), _full((1, LANES)), _full((tm, tm))],
        out_specs=[pl.BlockSpec((tm * ROW_TILE, LANES), lambda i: (i, 0)),
                   pl.BlockSpec((tm, LANES), lambda i: (i, 0)),
                   pl.BlockSpec((ROW_TILE, tm), lambda i: (i, 0)),
                   pl.BlockSpec((ROW_TILE, LANES), lambda i: (i // tpc, 0))],
        out_shape=[jax.ShapeDtypeStruct((n * ROW_TILE, LANES), F32),
                   jax.ShapeDtypeStruct((n, LANES), F32),
                   jax.ShapeDtypeStruct((n // tm * ROW_TILE, tm), F32),
                   jax.ShapeDtypeStruct((n // chunk * ROW_TILE, LANES), F32)],
        scratch_shapes=[pltpu.VMEM((1, LANES), F32)],
        compiler_params=_params("arbitrary"),
        name="router",
    )(*xs, g, wr, br, tri)


def _plan_kernel(tiles_per_chunk, route_ref, tri_ref, rank_ref, cnt_ref, carry_ref):
    i = pl.program_id(0)

    @pl.when(i % tiles_per_chunk == 0)
    def _():
        carry_ref[...] = jnp.zeros_like(carry_ref)

    route = route_ref[...]
    lane = lax.broadcasted_iota(jnp.int32, route.shape, 1).astype(F32)
    i1 = route[:, 0:1]
    i2 = route[:, 1:2]
    hit1 = lane == i1
    hit2 = lane == i2
    onehot = jnp.where(hit1 | hit2, 1.0, 0.0)
    before = carry_ref[...] + jnp.dot(tri_ref[...], onehot.astype(BF16), preferred_element_type=F32)
    r1 = jnp.sum(jnp.where(hit1, before, 0.0), axis=-1, keepdims=True)
    r2 = jnp.sum(jnp.where(hit2, before, 0.0), axis=-1, keepdims=True)
    rank_ref[...] = jnp.where(lane == 0.0, r1, jnp.where(lane == 1.0, r2, 0.0))
    carry_ref[...] += jnp.sum(onehot, axis=0, keepdims=True)
    cnt_ref[...] = jnp.broadcast_to(carry_ref[...], cnt_ref.shape)


def plan(route, tm, chunk):
    n = route.shape[0]
    tpc = chunk // tm
    tri = jnp.asarray(np.tril(np.ones((tm, tm), np.float32), -1), BF16)
    return pl.pallas_call(
        functools.partial(_plan_kernel, tpc),
        grid=(n // tm,),
        in_specs=[pl.BlockSpec((tm, LANES), lambda i: (i, 0)), _full((tm, tm))],
        out_specs=[pl.BlockSpec((tm, LANES), lambda i: (i, 0)),
                   pl.BlockSpec((ROW_TILE, LANES), lambda i: (i // tpc, 0))],
        out_shape=[jax.ShapeDtypeStruct((n, LANES), F32),
                   jax.ShapeDtypeStruct((n // chunk * ROW_TILE, LANES), F32)],
        scratch_shapes=[pltpu.VMEM((1, LANES), F32)],
        compiler_params=_params("arbitrary"),
        name="moe_plan",
    )(route, tri)


def _gather_rows(idx_ref, src_hbm, dst, sem, n_rows):
    unroll = 8

    def body(c, carry):
        for u in range(unroll):
            m = c * unroll + u
            tok = idx_ref[m]
            pltpu.make_async_copy(
                src_hbm.at[pl.ds(pl.multiple_of(tok * ROW_TILE, ROW_TILE), ROW_TILE), :],
                dst.at[pl.ds(pl.multiple_of(m * ROW_TILE, ROW_TILE), ROW_TILE), :],
                sem).start()
        return carry

    lax.fori_loop(0, n_rows // unroll, body, 0)


def _wait_rows(src_hbm, dst, sem):
    pltpu.make_async_copy(src_hbm.at[pl.ds(0, dst.shape[0]), :], dst, sem).wait()


def _from_token_tiles(buf, start, n_rows, stride):
    return jnp.concatenate(
        [buf[pl.ds(start + j, n_rows, stride=stride), :] for j in range(ROW_TILE)], axis=-1)


def _experts_kernel(tiles_per_chunk, tmg, te_ref, nv_ref, pos_ref, fill_hbm, hs_hbm,
                    win_ref, wout_ref, ys_ref, hsv, gbuf_a, gbuf_b, src_ref, sems, winb, woutb):
    c = pl.program_id(0)
    j = pl.program_id(1)
    t = c * tiles_per_chunk + j
    nv = nv_ref[c]
    chunk = pos_ref.shape[0] // 2
    unroll = 8

    def tile(r):
        return pl.ds(pl.multiple_of(r * ROW_TILE, ROW_TILE), ROW_TILE)

    @pl.when(j == 0)
    def _():
        load = pltpu.make_async_copy(
            hs_hbm.at[pl.ds(pl.multiple_of(c * chunk * ROW_TILE, ROW_TILE), chunk * ROW_TILE), :],
            hsv.at[pl.ds(0, chunk * ROW_TILE), :], sems.at[0])
        load.start()
        fill = pltpu.make_async_copy(fill_hbm, src_ref, sems.at[1])
        fill.start()
        hsv[pl.ds(chunk * ROW_TILE, ROW_TILE), :] = jnp.zeros((ROW_TILE, LANES), F32)
        fill.wait()

        def scatter(i, carry):
            first = i * scatter_unroll
            tok0 = first // (2 * MOE_TM) * MOE_TM + first % MOE_TM
            for u in range(scatter_unroll):
                src_ref[pos_ref[first + u]] = tok0 + u
            return carry

        scatter_unroll = 32
        lax.fori_loop(0, 2 * chunk // scatter_unroll, scatter, 0)
        load.wait()

        def gather(i, carry):
            for u in range(unroll):
                m = i * unroll + u
                gbuf_a[tile(m), :] = hsv[tile(src_ref[m]), :]
            return carry

        lax.fori_loop(0, tmg // unroll, gather, 0)

    changed = jnp.logical_or(j == 0, te_ref[t] != te_ref[jnp.maximum(t - 1, 0)])

    @pl.when(jnp.logical_and(changed, j < nv))
    def _():
        winb[...] = win_ref[0].astype(BF16)
        woutb[...] = wout_ref[0].astype(BF16)

    def step(cur, nxt):
        base = jnp.minimum(j + 1, tiles_per_chunk - 1) * tmg
        for m in range(tmg):
            nxt[pl.ds(m * ROW_TILE, ROW_TILE), :] = hsv[tile(src_ref[base + m]), :]
        x = _from_token_tiles(cur, 0, tmg, ROW_TILE).astype(BF16)
        hid = jnp.dot(x, winb[...], preferred_element_type=F32)
        a = hid[:, :D_EXPERT]
        u = hid[:, D_EXPERT:]
        act = (a * _sigmoid(a) * u).astype(BF16)
        y = jnp.dot(act, woutb[...], preferred_element_type=F32)
        for k in range(ROW_TILE):
            ys_ref[pl.ds(k, tmg, stride=ROW_TILE), :] = y[:, k * LANES:(k + 1) * LANES]

    @pl.when(jnp.logical_and(j < nv, j % 2 == 0))
    def _():
        step(gbuf_a, gbuf_b)

    @pl.when(jnp.logical_and(j < nv, j % 2 == 1))
    def _():
        step(gbuf_b, gbuf_a)

    @pl.when(j >= nv)
    def _():
        ys_ref[...] = jnp.zeros_like(ys_ref)


def _experts_by_expert_kernel(tiles_per_chunk, tmg, t0_ref, nt_ref, pos_ref, fill_hbm, hs_hbm,
                              win_ref, wout_ref, ys_hbm, hsv, gbuf_a, gbuf_b, ystage_a, ystage_b,
                              src_ref, sems, winb, woutb):
    c = pl.program_id(0)
    e = pl.program_id(1)
    chunk = pos_ref.shape[0] // 2
    tile_rows = tmg * ROW_TILE
    scatter_unroll = 32
    gather_unroll = 8

    def tile(r):
        return pl.ds(pl.multiple_of(r * ROW_TILE, ROW_TILE), ROW_TILE)

    def ys_tile(t):
        return ys_hbm.at[pl.ds(pl.multiple_of((c * tiles_per_chunk + t) * tile_rows, tile_rows),
                               tile_rows), :]

    @pl.when(e == 0)
    def _():
        load = pltpu.make_async_copy(
            hs_hbm.at[pl.ds(pl.multiple_of(c * chunk * ROW_TILE, ROW_TILE), chunk * ROW_TILE), :],
            hsv.at[pl.ds(0, chunk * ROW_TILE), :], sems.at[0])
        load.start()
        fill = pltpu.make_async_copy(fill_hbm, src_ref, sems.at[1])
        fill.start()
        hsv[pl.ds(chunk * ROW_TILE, ROW_TILE), :] = jnp.zeros((ROW_TILE, LANES), F32)
        fill.wait()

        def scatter(i, carry):
            first = i * scatter_unroll
            tok0 = first // (2 * MOE_TM) * MOE_TM + first % MOE_TM
            for u in range(scatter_unroll):
                src_ref[pos_ref[first + u]] = tok0 + u
            return carry

        lax.fori_loop(0, 2 * chunk // scatter_unroll, scatter, 0)
        load.wait()

        def gather(i, carry):
            for u in range(gather_unroll):
                m = i * gather_unroll + u
                gbuf_a[tile(m), :] = hsv[tile(src_ref[m]), :]
            return carry

        lax.fori_loop(0, tmg // gather_unroll, gather, 0)

    t0 = t0_ref[c * N_EXPERTS + e]
    nt = nt_ref[c * N_EXPERTS + e]

    @pl.when(nt > 0)
    def _():
        winb[...] = win_ref[0].astype(BF16)
        woutb[...] = wout_ref[0].astype(BF16)

    def step(t, cur, nxt, ystage, sem):
        base = jnp.minimum(t + 1, tiles_per_chunk - 1) * tmg
        for m in range(tmg):
            nxt[pl.ds(m * ROW_TILE, ROW_TILE), :] = hsv[tile(src_ref[base + m]), :]
        x = _from_token_tiles(cur, 0, tmg, ROW_TILE).astype(BF16)
        hid = jnp.dot(x, winb[...], preferred_element_type=F32)
        a = hid[:, :D_EXPERT]
        u = hid[:, D_EXPERT:]
        act = (a * _sigmoid(a) * u).astype(BF16)
        y = jnp.dot(act, woutb[...], preferred_element_type=F32)

        @pl.when(t >= 2)
        def _():
            pltpu.make_async_copy(ystage, ys_tile(t), sem).wait()

        for k in range(ROW_TILE):
            ystage[pl.ds(k, tmg, stride=ROW_TILE), :] = y[:, k * LANES:(k + 1) * LANES]
        pltpu.make_async_copy(ystage, ys_tile(t), sem).start()

    def one_tile(k, carry):
        t = t0 + k

        @pl.when(t % 2 == 0)
        def _():
            step(t, gbuf_a, gbuf_b, ystage_a, sems.at[2])

        @pl.when(t % 2 == 1)
        def _():
            step(t, gbuf_b, gbuf_a, ystage_b, sems.at[3])

        return carry

    lax.fori_loop(0, nt, one_tile, 0)

    @pl.when(e == pl.num_programs(1) - 1)
    def _():
        total = t0 + nt

        @pl.when(total >= 1)
        def _():
            pltpu.make_async_copy(ystage_a, ys_tile(0), sems.at[2]).wait()

        @pl.when(total >= 2)
        def _():
            pltpu.make_async_copy(ystage_b, ys_tile(0), sems.at[3]).wait()

        ystage_a[...] = jnp.zeros_like(ystage_a)

        def zero_tile(t, carry):
            cp = pltpu.make_async_copy(ystage_a, ys_tile(t), sems.at[2])
            cp.start()
            cp.wait()
            return carry

        lax.fori_loop(total, tiles_per_chunk, zero_tile, 0)


def experts_by_expert(hs, pos_local, tile_start, n_tiles, w_in, w_out, layer, tmg, chunk, tiles_per_chunk):
    _, d, f2 = w_in.shape
    nc = hs.shape[0] // (chunk * ROW_TILE)
    rows_per_chunk = tiles_per_chunk * tmg
    fill = jnp.full((rows_per_chunk,), chunk, jnp.int32)
    off = layer * N_EXPERTS
    grid_spec = pltpu.PrefetchScalarGridSpec(
        num_scalar_prefetch=2,
        grid=(nc, N_EXPERTS),
        in_specs=[
            pl.BlockSpec((2 * chunk,), lambda c, e, t0, nt: (c,), memory_space=pltpu.SMEM),
            pl.BlockSpec(memory_space=pl.ANY),
            pl.BlockSpec(memory_space=pl.ANY),
            pl.BlockSpec((1, d, f2), lambda c, e, t0, nt: (off + e, 0, 0)),
            pl.BlockSpec((1, f2 // 2, d), lambda c, e, t0, nt: (off + e, 0, 0)),
        ],
        out_specs=pl.BlockSpec(memory_space=pl.ANY),
        scratch_shapes=[
            pltpu.VMEM(((chunk + 1) * ROW_TILE, LANES), F32),
            pltpu.VMEM((tmg * ROW_TILE, LANES), F32),
            pltpu.VMEM((tmg * ROW_TILE, LANES), F32),
            pltpu.VMEM((tmg * ROW_TILE, LANES), F32),
            pltpu.VMEM((tmg * ROW_TILE, LANES), F32),
            pltpu.SMEM((rows_per_chunk,), jnp.int32),
            pltpu.SemaphoreType.DMA((4,)),
            pltpu.VMEM((d, f2), BF16),
            pltpu.VMEM((f2 // 2, d), BF16),
        ],
    )
    return pl.pallas_call(
        functools.partial(_experts_by_expert_kernel, tiles_per_chunk, tmg),
        grid_spec=grid_spec,
        out_shape=jax.ShapeDtypeStruct((nc * rows_per_chunk * ROW_TILE, LANES), F32),
        compiler_params=pltpu.CompilerParams(dimension_semantics=("arbitrary", "arbitrary"),
                                             vmem_limit_bytes=MOE_VMEM_LIMIT),
        name="moe_experts",
    )(tile_start, n_tiles, pos_local, fill, hs, w_in, w_out)


def experts(hs, pos_local, tile_expert, n_valid, w_in, w_out, tmg, chunk, tiles_per_chunk):
    _, d, f2 = w_in.shape
    nc = hs.shape[0] // (chunk * ROW_TILE)
    rows_per_chunk = tiles_per_chunk * tmg
    fill = jnp.full((rows_per_chunk,), chunk, jnp.int32)
    grid_spec = pltpu.PrefetchScalarGridSpec(
        num_scalar_prefetch=2,
        grid=(nc, tiles_per_chunk),
        in_specs=[
            pl.BlockSpec((2 * chunk,), lambda c, j, te, nv: (c,), memory_space=pltpu.SMEM),
            pl.BlockSpec(memory_space=pl.ANY),
            pl.BlockSpec(memory_space=pl.ANY),
            pl.BlockSpec((1, d, f2), lambda c, j, te, nv: (te[c * tiles_per_chunk + j], 0, 0)),
            pl.BlockSpec((1, f2 // 2, d), lambda c, j, te, nv: (te[c * tiles_per_chunk + j], 0, 0)),
        ],
        out_specs=pl.BlockSpec((tmg * ROW_TILE, LANES),
                               lambda c, j, te, nv: (c * tiles_per_chunk + j, 0)),
        scratch_shapes=[
            pltpu.VMEM(((chunk + 1) * ROW_TILE, LANES), F32),
            pltpu.VMEM((tmg * ROW_TILE, LANES), F32),
            pltpu.VMEM((tmg * ROW_TILE, LANES), F32),
            pltpu.SMEM((rows_per_chunk,), jnp.int32),
            pltpu.SemaphoreType.DMA((2,)),
            pltpu.VMEM((d, f2), BF16),
            pltpu.VMEM((f2 // 2, d), BF16),
        ],
    )
    return pl.pallas_call(
        functools.partial(_experts_kernel, tiles_per_chunk, tmg),
        grid_spec=grid_spec,
        out_shape=jax.ShapeDtypeStruct((nc * rows_per_chunk * ROW_TILE, LANES), F32),
        compiler_params=pltpu.CompilerParams(dimension_semantics=("arbitrary", "arbitrary"),
                                             vmem_limit_bytes=MOE_VMEM_LIMIT),
        name="moe_experts",
    )(tile_expert, n_valid, pos_local, fill, hs, w_in, w_out)


def _combine_kernel(split_steps, n_src, first_steps, pos_cur_ref, pos_nxt_ref, ys_hbm, route_ref, *rest):
    x_refs, o_refs, (ybuf_a, ybuf_b, sems) = rest[:n_src], rest[n_src:-3], rest[-3:]
    i = pl.program_id(0)
    last = i == pl.num_programs(0) - 1
    tm = route_ref.shape[0]

    @pl.when(i == 0)
    def _():
        _gather_rows(pos_cur_ref, ys_hbm, ybuf_a, sems.at[0], 2 * tm)

    def step(cur, cur_sem, nxt, nxt_sem):
        _gather_rows(pos_nxt_ref, ys_hbm, nxt, nxt_sem, 2 * tm)
        _wait_rows(ys_hbm, cur, cur_sem)
        route = route_ref[...]
        y1 = _from_token_tiles(cur, 0, tm, ROW_TILE)
        y2 = _from_token_tiles(cur, tm * ROW_TILE, tm, ROW_TILE)
        val = _read_tokens(x_refs, first_steps) + route[:, 2:3] * y1 + route[:, 3:4] * y2
        if split_steps is None:
            o_refs[0][...] = val
        else:
            @pl.when(i < split_steps)
            def _():
                o_refs[0][...] = val

            @pl.when(i >= split_steps)
            def _():
                o_refs[1][...] = val

        @pl.when(last)
        def _():
            _wait_rows(ys_hbm, nxt, nxt_sem)

    @pl.when(i % 2 == 0)
    def _():
        step(ybuf_a, sems.at[0], ybuf_b, sems.at[1])

    @pl.when(i % 2 == 1)
    def _():
        step(ybuf_b, sems.at[1], ybuf_a, sems.at[0])


def combine(ys, pos, route, xs, tm, split_rows=None):
    n = sum(x.shape[0] for x in xs)
    d = xs[0].shape[1]
    nsteps = n // tm
    if split_rows is None:
        split_steps = None
        out_specs = pl.BlockSpec((tm, d), lambda i: (i, 0))
        out_shape = jax.ShapeDtypeStruct((n, d), F32)
    else:
        split_steps = split_rows // tm
        out_specs = [pl.BlockSpec((tm, d), lambda i: (jnp.minimum(i, split_steps - 1), 0)),
                     pl.BlockSpec((tm, d), lambda i: (jnp.maximum(i - split_steps, 0), 0))]
        out_shape = [jax.ShapeDtypeStruct((split_rows, d), F32),
                     jax.ShapeDtypeStruct((n - split_rows, d), F32)]
    return pl.pallas_call(
        functools.partial(_combine_kernel, split_steps, len(xs), xs[0].shape[0] // tm),
        grid=(nsteps,),
        in_specs=[
            pl.BlockSpec((2 * tm,), lambda i: (i,), memory_space=pltpu.SMEM),
            pl.BlockSpec((2 * tm,), lambda i: (jnp.minimum(i + 1, nsteps - 1),),
                         memory_space=pltpu.SMEM),
            pl.BlockSpec(memory_space=pl.ANY),
            pl.BlockSpec((tm, LANES), lambda i: (i, 0)),
        ] + _token_specs(xs, tm),
        out_specs=out_specs,
        out_shape=out_shape,
        scratch_shapes=[pltpu.VMEM((2 * tm * ROW_TILE, LANES), F32),
                        pltpu.VMEM((2 * tm * ROW_TILE, LANES), F32),
                        pltpu.SemaphoreType.DMA((2,))],
        compiler_params=_params("arbitrary"),
        name="moe_combine",
    )(pos, pos, ys, route, *xs)


def _moe_fused_kernel(ts, st_ref, cn_ref, pos_ref, wts_ref, hs_hbm, win_ref, wout_ref, acc_hbm,
                      hsv, acc, gbuf, ystage, src_ref, sems, winb, woutb):
    c = pl.program_id(0)
    e = pl.program_id(1)
    n_slots = pos_ref.shape[0]
    chunk = n_slots // 2
    rows = chunk * ROW_TILE
    unroll = 8

    def tile(r):
        return pl.ds(pl.multiple_of(r * ROW_TILE, ROW_TILE), ROW_TILE)

    @pl.when(e == 0)
    def _():
        load = pltpu.make_async_copy(
            hs_hbm.at[pl.ds(pl.multiple_of(c * rows, ROW_TILE), rows), :],
            hsv.at[pl.ds(0, rows), :], sems.at[0])
        load.start()
        hsv[pl.ds(rows, ROW_TILE), :] = jnp.zeros((ROW_TILE, LANES), F32)

        def zero(i, carry):
            acc[pl.ds(pl.multiple_of(i * 64, 64), 64), :] = jnp.zeros((64, LANES), F32)
            return carry

        lax.fori_loop(0, (rows + ROW_TILE) // 64, zero, 0)
        acc[pl.ds(rows + ROW_TILE - 64, 64), :] = jnp.zeros((64, LANES), F32)

        def scatter(i, carry):
            for u in range(unroll):
                s = i * unroll + u
                src_ref[pos_ref[s]] = s
            return carry

        lax.fori_loop(0, n_slots // unroll, scatter, 0)
        load.wait()

    winb[...] = win_ref[0].astype(BF16)
    woutb[...] = wout_ref[0].astype(BF16)
    start = st_ref[c * N_EXPERTS + e]
    cnt = cn_ref[c * N_EXPERTS + e]

    def sub_tile(k, carry):
        base = start + k * ts
        rem = cnt - k * ts

        def slot_of(m):
            s = src_ref[jnp.minimum(base + m, n_slots - 1)]
            valid = m < rem
            return jnp.where(valid, s >> 1, chunk), jnp.where(valid, wts_ref[s], 0.0)

        def gather(i, carry2):
            for u in range(unroll):
                m = i * unroll + u
                tok, _ = slot_of(m)
                gbuf[tile(m), :] = hsv[tile(tok), :]
            return carry2

        lax.fori_loop(0, ts // unroll, gather, 0)
        x = _from_token_tiles(gbuf, 0, ts, ROW_TILE).astype(BF16)
        hid = jnp.dot(x, winb[...], preferred_element_type=F32)
        a = hid[:, :D_EXPERT]
        u_ = hid[:, D_EXPERT:]
        act = (a * _sigmoid(a) * u_).astype(BF16)
        y = jnp.dot(act, woutb[...], preferred_element_type=F32)
        for j in range(ROW_TILE):
            ystage[pl.ds(j, ts, stride=ROW_TILE), :] = y[:, j * LANES:(j + 1) * LANES]

        def scatter_add(i, carry2):
            toks, vals = [], []
            for u in range(unroll):
                m = i * unroll + u
                tok, w = slot_of(m)
                toks.append(tok)
                vals.append(acc[tile(tok), :] + w * ystage[tile(m), :])
            for tok, val in zip(toks, vals):
                acc[tile(tok), :] = val
            return carry2

        lax.fori_loop(0, ts // unroll, scatter_add, 0)
        return carry

    lax.fori_loop(0, (cnt + ts - 1) // ts, sub_tile, 0)

    @pl.when(e == pl.num_programs(1) - 1)
    def _():
        store = pltpu.make_async_copy(
            acc.at[pl.ds(0, rows), :],
            acc_hbm.at[pl.ds(pl.multiple_of(c * rows, ROW_TILE), rows), :], sems.at[1])
        store.start()
        store.wait()


def moe_fused(hs, pos_local, wts, starts, counts, w_in, w_out, chunk, ts):
    ne, d, f2 = w_in.shape
    n_rows = hs.shape[0]
    nc = n_rows // (chunk * ROW_TILE)
    grid_spec = pltpu.PrefetchScalarGridSpec(
        num_scalar_prefetch=2,
        grid=(nc, ne),
        in_specs=[
            pl.BlockSpec((2 * chunk,), lambda c, e, st, cn: (c,), memory_space=pltpu.SMEM),
            pl.BlockSpec((2 * chunk,), lambda c, e, st, cn: (c,), memory_space=pltpu.SMEM),
            pl.BlockSpec(memory_space=pl.ANY),
            pl.BlockSpec((1, d, f2), lambda c, e, st, cn: (e, 0, 0)),
            pl.BlockSpec((1, f2 // 2, d), lambda c, e, st, cn: (e, 0, 0)),
        ],
        out_specs=pl.BlockSpec(memory_space=pl.ANY),
        scratch_shapes=[
            pltpu.VMEM(((chunk + 1) * ROW_TILE, LANES), F32),
            pltpu.VMEM(((chunk + 1) * ROW_TILE, LANES), F32),
            pltpu.VMEM((ts * ROW_TILE, LANES), F32),
            pltpu.VMEM((ts * ROW_TILE, LANES), F32),
            pltpu.SMEM((2 * chunk,), jnp.int32),
            pltpu.SemaphoreType.DMA((2,)),
            pltpu.VMEM((d, f2), BF16),
            pltpu.VMEM((f2 // 2, d), BF16),
        ],
    )
    return pl.pallas_call(
        functools.partial(_moe_fused_kernel, ts),
        grid_spec=grid_spec,
        out_shape=jax.ShapeDtypeStruct((n_rows, LANES), F32),
        compiler_params=pltpu.CompilerParams(dimension_semantics=("arbitrary", "arbitrary"),
                                             vmem_limit_bytes=MOE_VMEM_LIMIT),
        name="moe_fused",
    )(starts, counts, pos_local, wts, hs, w_in, w_out)


def _finish_kernel(acc_ref, x_ref, o_ref):
    tm = x_ref.shape[0]
    o_ref[...] = x_ref[...] + _from_token_tiles(acc_ref, 0, tm, ROW_TILE)


def moe_finish(acc, x, tm):
    n, d = x.shape
    return pl.pallas_call(
        _finish_kernel,
        grid=(n // tm,),
        in_specs=[pl.BlockSpec((tm * ROW_TILE, LANES), lambda i: (i, 0)),
                  pl.BlockSpec((tm, d), lambda i: (i, 0))],
        out_specs=pl.BlockSpec((tm, d), lambda i: (i, 0)),
        out_shape=jax.ShapeDtypeStruct((n, d), F32),
        compiler_params=_params("parallel"),
        name="moe_finish",
    )(acc, x)


def moe_layer_fused(x, g, wr, br, w_in, w_out):
    n = x.shape[0]
    chunk = min(n, MOE_CHUNK)
    nc = n // chunk
    hs, route, counts = router(x, g, wr, br, MOE_TM, chunk)
    ids = route[:, 0:2].astype(jnp.int32).reshape(nc, chunk, 2)
    cnt = counts.reshape(nc, ROW_TILE, LANES)[:, 0, :N_EXPERTS].astype(jnp.int32)
    starts = jnp.cumsum(cnt, axis=1) - cnt
    sel = ids[..., None] == jnp.arange(N_EXPERTS, dtype=jnp.int32)
    start_of = jnp.sum(jnp.where(sel, starts[:, None, None, :], 0), axis=-1)
    pos_local = start_of + route[:, 4:6].astype(jnp.int32).reshape(nc, chunk, 2)
    acc = moe_fused(hs, pos_local.reshape(-1), route[:, 2:4].reshape(-1), starts.reshape(-1),
                    cnt.reshape(-1), w_in, w_out, chunk, MOE_TS)
    return moe_finish(acc, x, MOE_TM)


def moe_layer(xs, g, wr, br, w_in, w_out, layer, split_rows=None):
    n = sum(x.shape[0] for x in xs)
    nc = -(-n // MOE_CHUNK)
    chunk = n // nc
    assert chunk * nc == n and chunk % MOE_TM == 0
    tpc = (2 * chunk + N_EXPERTS * (MOE_TMG - 1)) // MOE_TMG
    hs, route, route_t, counts = router(xs, g, wr, br, MOE_TM, chunk)
    rt = route_t.reshape(nc, chunk // MOE_TM, ROW_TILE, MOE_TM)
    ids = rt[:, :, 0:2, :].astype(jnp.int32)
    ranks = rt[:, :, 4:6, :].astype(jnp.int32)
    cnt = counts.reshape(nc, ROW_TILE, LANES)[:, 0, :N_EXPERTS].astype(jnp.int32)
    padded = (cnt + MOE_TMG - 1) // MOE_TMG * MOE_TMG
    ends = jnp.cumsum(padded, axis=1)
    starts = ends - padded
    start_of = jnp.zeros_like(ids)
    for e in range(N_EXPERTS):
        start_of = jnp.where(ids == e, starts[:, e][:, None, None, None], start_of)
    pos_local = start_of + ranks
    tile_row = jnp.arange(tpc, dtype=jnp.int32) * MOE_TMG
    tile_expert = jnp.minimum(jnp.sum(ends[:, None, :] <= tile_row[None, :, None], axis=-1),
                              N_EXPERTS - 1).astype(jnp.int32)
    n_valid = (ends[:, -1] // MOE_TMG).astype(jnp.int32)
    ys = experts_by_expert(hs, pos_local.reshape(-1), (starts // MOE_TMG).reshape(-1),
                           (padded // MOE_TMG).reshape(-1), w_in, w_out, layer, MOE_TMG, chunk, tpc)
    pos = pos_local + (jnp.arange(nc, dtype=jnp.int32) * (tpc * MOE_TMG))[:, None, None, None]
    return combine(ys, pos.reshape(-1), route, xs, MOE_TM, split_rows)


def _qkv_kernel(x_ref, g_ref, w_ref, qg_ref, kg_ref, seg_ref, q_ref, k_ref, v_ref):
    h = _rms(x_ref[...], g_ref[...]).astype(BF16)
    qkv = jnp.dot(h, w_ref[...], preferred_element_type=F32)
    nq = N_HEADS * HEAD_DIM
    nk = N_KV * HEAD_DIM
    q = qkv[:, :nq]
    k = qkv[:, nq:nq + nk]
    v_ref[...] = qkv[:, nq + nk:]

    def seg_mean_sq(z, seg, split):
        zz = z * z
        hi = zz.astype(BF16)
        ms = jnp.dot(hi, seg, preferred_element_type=F32)
        if split:
            lo = (zz - hi.astype(F32)).astype(BF16)
            ms = ms + jnp.dot(lo, seg, preferred_element_type=F32)
        return ms

    seg = seg_ref[...]
    qn = q * lax.rsqrt(seg_mean_sq(q, seg, False) + RMS_EPS) * qg_ref[...]
    q_ref[...] = (qn * (HEAD_DIM ** -0.5)).astype(BF16)
    kn = k * lax.rsqrt(seg_mean_sq(k, seg[:nk, :nk], True) + RMS_EPS) * kg_ref[...]
    k_ref[...] = kn


def qkv_proj(x, g, w, qg, kg, seg, tm):
    n, d = x.shape
    nq = N_HEADS * HEAD_DIM
    nk = N_KV * HEAD_DIM
    return pl.pallas_call(
        _qkv_kernel,
        grid=(n // tm,),
        in_specs=[pl.BlockSpec((tm, d), lambda i: (i, 0)), _full((1, d)),
                  _full((d, nq + 2 * nk)), _full((1, nq)), _full((1, nk)), _full((nq, nq))],
        out_specs=[pl.BlockSpec((tm, nq), lambda i: (i, 0)),
                   pl.BlockSpec((tm, nk), lambda i: (i, 0)),
                   pl.BlockSpec((tm, nk), lambda i: (i, 0))],
        out_shape=[jax.ShapeDtypeStruct((n, nq), BF16),
                   jax.ShapeDtypeStruct((n, nk), F32),
                   jax.ShapeDtypeStruct((n, nk), F32)],
        compiler_params=_params("parallel"),
        name="qkv_proj",
    )(x, g, w, qg, kg, seg)


def _attn_prompt_kernel(q_ref, kc_ref, kp_ref, vc_ref, vp_ref, tbl_ref, sink_ref, x_ref,
                        wo_ref, o_ref, att_ref, s_ref, e_ref, inv_ref):
    n = pl.program_id(1)
    first = jnp.where(n == 0, NEG_INF, 0.0).astype(F32)
    nt = (((1,), (1,)), ((), ()))
    lo_half = lax.broadcasted_iota(jnp.int32, (WINDOW, LANES), 1) < HEAD_DIM
    kp, kc = kp_ref[...], kc_ref[...]
    keys = (kp.astype(BF16), kc.astype(BF16))
    keys_swapped = (pltpu.roll(kp, HEAD_DIM, axis=1).astype(BF16),
                    pltpu.roll(kc, HEAD_DIM, axis=1).astype(BF16))
    vt_prev = jnp.transpose(vp_ref[...]).astype(BF16)
    vt_cur = jnp.transpose(vc_ref[...]).astype(BF16)
    for h in range(N_HEADS):
        kh = h // GROUP
        odd = h % 2
        pair = q_ref[:, (h // 2) * LANES:(h // 2 + 1) * LANES]
        qm = jnp.where(lo_half if odd == 0 else ~lo_half, pair, jnp.zeros_like(pair))
        k_prev, k_cur = keys_swapped if (kh == 0) == (odd == 1) else keys
        s_ref[h, 0:WINDOW, :] = (lax.dot_general(k_prev, qm, nt, preferred_element_type=F32)
                                 + tbl_ref[h, 0:WINDOW, :] + first)
        s_ref[h, WINDOW:, :] = (lax.dot_general(k_cur, qm, nt, preferred_element_type=F32)
                                + tbl_ref[h, WINDOW:, :])
    for h in range(N_HEADS):
        s = s_ref[h]
        sink = sink_ref[h]
        m = jnp.maximum(jnp.max(s, axis=0, keepdims=True), sink)
        e = jnp.exp(s - m)
        e_ref[h] = e.astype(BF16)
        inv_ref[h] = 1.0 / (jnp.sum(e, axis=0, keepdims=True) + jnp.exp(sink - m))
    for h in range(N_HEADS):
        kh = h // GROUP
        dims = slice(kh * HEAD_DIM, (kh + 1) * HEAD_DIM)
        out_t = (jnp.dot(vt_prev[dims, :], e_ref[h, 0:WINDOW, :], preferred_element_type=F32)
                 + jnp.dot(vt_cur[dims, :], e_ref[h, WINDOW:, :], preferred_element_type=F32))
        att_ref[h * HEAD_DIM:(h + 1) * HEAD_DIM, :] = out_t * inv_ref[h]
    att = jnp.transpose(att_ref[...]).astype(BF16)
    o_ref[...] = x_ref[...] + jnp.dot(att, wo_ref[...], preferred_element_type=F32)


def attn_prompt(q, k, v, tbl, sinks, x, wo, batch, seq):
    n, d = x.shape
    nb = seq // WINDOW
    nk = N_KV * HEAD_DIM
    cur = lambda b, i: (b * nb + i, 0)
    prev = lambda b, i: (b * nb + jnp.maximum(i - 1, 0), 0)
    return pl.pallas_call(
        _attn_prompt_kernel,
        grid=(batch, nb),
        in_specs=[
            pl.BlockSpec((WINDOW, d), cur),
            pl.BlockSpec((WINDOW, nk), cur), pl.BlockSpec((WINDOW, nk), prev),
            pl.BlockSpec((WINDOW, nk), cur), pl.BlockSpec((WINDOW, nk), prev),
            _full((N_HEADS, 2 * WINDOW, WINDOW)),
            pl.BlockSpec(memory_space=pltpu.SMEM),
            pl.BlockSpec((WINDOW, d), cur),
            _full((d, d)),
        ],
        out_specs=pl.BlockSpec((WINDOW, d), cur),
        out_shape=jax.ShapeDtypeStruct((n, d), F32),
        input_output_aliases={7: 0},
        scratch_shapes=[pltpu.VMEM((N_HEADS * HEAD_DIM, WINDOW), F32),
                        pltpu.VMEM((N_HEADS, 2 * WINDOW, WINDOW), F32),
                        pltpu.VMEM((N_HEADS, 2 * WINDOW, WINDOW), BF16),
                        pltpu.VMEM((N_HEADS, 1, WINDOW), F32)],
        compiler_params=_params("parallel", "parallel"),
        name="attn_prompt",
    )(q, k, k, v, v, tbl, sinks, x, wo)


def _attn_sample_kernel(q_ref, kn_ref, vn_ref, ck_ref, cv_ref, tblc_ref, tbln_ref, sink_ref, o_ref,
                        sc_ref, sn_ref, ec_ref, en_ref, inv_ref):
    bs = q_ref.shape[0]
    nt = (((1,), (1,)), ((), ()))
    for b in range(bs):
        ck = ck_ref[b].astype(BF16)
        kn = kn_ref[b].astype(BF16)
        for kh in range(N_KV):
            lanes = slice(kh * HEAD_DIM, (kh + 1) * HEAD_DIM)
            qt = q_ref[b, kh]
            sc_ref[b, kh] = lax.dot_general(qt, ck[:, lanes], nt, preferred_element_type=F32) + tblc_ref[kh]
            sn_ref[b, kh] = lax.dot_general(qt, kn[:, lanes], nt, preferred_element_type=F32) + tbln_ref[kh]
    for b in range(bs):
        for kh in range(N_KV):
            sc = sc_ref[b, kh]
            sn = sn_ref[b, kh]
            sink = sink_ref[kh]
            m = jnp.maximum(jnp.maximum(jnp.max(sc, axis=-1, keepdims=True),
                                        jnp.max(sn, axis=-1, keepdims=True)), sink)
            ec = jnp.exp(sc - m)
            en = jnp.exp(sn - m)
            ec_ref[b, kh] = ec.astype(BF16)
            en_ref[b, kh] = en.astype(BF16)
            inv_ref[b, kh] = 1.0 / (jnp.sum(ec, axis=-1, keepdims=True)
                                    + jnp.sum(en, axis=-1, keepdims=True) + jnp.exp(sink - m))
    for b in range(bs):
        cv = cv_ref[b].astype(BF16)
        vn = vn_ref[b].astype(BF16)
        for kh in range(N_KV):
            lanes = slice(kh * HEAD_DIM, (kh + 1) * HEAD_DIM)
            out = (jnp.dot(ec_ref[b, kh], cv[:, lanes], preferred_element_type=F32)
                   + jnp.dot(en_ref[b, kh], vn[:, lanes], preferred_element_type=F32))
            o_ref[b, kh] = (out * inv_ref[b, kh]).astype(BF16)


def attn_sample(q4, kn, vn, ck, cv, tbl, sink, bs):
    nb = q4.shape[0]
    ts = kn.shape[1]
    nk = N_KV * HEAD_DIM
    tg = ts * GROUP
    kt = WINDOW + ts
    return pl.pallas_call(
        _attn_sample_kernel,
        grid=(nb // bs,),
        in_specs=[
            pl.BlockSpec((bs, N_KV, tg, HEAD_DIM), lambda i: (i, 0, 0, 0)),
            pl.BlockSpec((bs, ts, nk), lambda i: (i, 0, 0)),
            pl.BlockSpec((bs, ts, nk), lambda i: (i, 0, 0)),
            pl.BlockSpec((bs, WINDOW, nk), lambda i: (i, 0, 0)),
            pl.BlockSpec((bs, WINDOW, nk), lambda i: (i, 0, 0)),
            _full((N_KV, tg, WINDOW)),
            _full((N_KV, tg, ts)),
            _full((N_KV, tg, 1)),
        ],
        out_specs=pl.BlockSpec((bs, N_KV, tg, HEAD_DIM), lambda i: (i, 0, 0, 0)),
        out_shape=jax.ShapeDtypeStruct((nb, N_KV, tg, HEAD_DIM), BF16),
        scratch_shapes=[pltpu.VMEM((bs, N_KV, tg, WINDOW), F32),
                        pltpu.VMEM((bs, N_KV, tg, ts), F32),
                        pltpu.VMEM((bs, N_KV, tg, WINDOW), BF16),
                        pltpu.VMEM((bs, N_KV, tg, ts), BF16),
                        pltpu.VMEM((bs, N_KV, tg, 1), F32)],
        compiler_params=_params("parallel"),
        name="attn_sample",
    )(q4, kn, vn, ck, cv, tbl[:, :, :WINDOW], tbl[:, :, WINDOW:], sink)


def _proj_res_kernel(a_ref, w_ref, x_ref, o_ref):
    o_ref[...] = x_ref[...] + jnp.dot(a_ref[...], w_ref[...], preferred_element_type=F32)


def proj_residual(a, w, x, tm):
    n, d = x.shape
    na, kdim = a.shape
    off = (n - na) // tm
    return pl.pallas_call(
        _proj_res_kernel,
        grid=(na // tm,),
        in_specs=[pl.BlockSpec((tm, kdim), lambda i: (i, 0)), _full((kdim, d)),
                  pl.BlockSpec((tm, d), lambda i: (off + i, 0))],
        out_specs=pl.BlockSpec((tm, d), lambda i: (off + i, 0)),
        out_shape=jax.ShapeDtypeStruct((n, d), F32),
        input_output_aliases={2: 0},
        compiler_params=_params("parallel"),
        name="proj_residual",
    )(a, w, x)


def _t5_bucket_np(dist):
    n = np.maximum(dist, 0)
    max_exact = N_BUCKETS // 2
    large = max_exact + (np.log(np.maximum(n, 1).astype(np.float32) / max_exact)
                         / math.log(MAX_DISTANCE / max_exact) * (N_BUCKETS - max_exact)).astype(np.int32)
    large = np.minimum(large, N_BUCKETS - 1)
    return np.where(n < max_exact, n, large)


def _bias_table(rel_bias, dist):
    valid = (dist >= 0) & (dist <= WINDOW)
    onehot = (np.asarray(_t5_bucket_np(dist))[..., None] == np.arange(N_BUCKETS)).astype(np.float32)
    b = jnp.einsum("qkb,bh->hqk", jnp.asarray(onehot), rel_bias.astype(F32),
                   precision=lax.Precision.HIGHEST)
    return jnp.where(jnp.asarray(valid)[None], b, NEG_INF)


def kernel(x_prompt, x_sample, state_conv, cache_swa_k, cache_swa_v, rms_mix_g, rms_ffn_g, conv_w_in, conv_dw_w, conv_dw_b, conv_ln_g, conv_ln_b, conv_w_out, attn_w_qkv, attn_q_norm_g, attn_k_norm_g, attn_sinks, attn_w_o, rel_bias, router_group_w, router_group_b, router_expert_w, router_expert_b, expert_w_in, expert_w_out):
    batch, seq, d = x_prompt.shape
    nsb, ts, _ = x_sample.shape
    xp = x_prompt.reshape(batch * seq, d)
    xs = x_sample.reshape(nsb * ts, d)
    row = lambda a: a.reshape(1, -1).astype(F32)

    def router_w(i):
        we = jnp.transpose(router_expert_w[i], (1, 0, 2)).reshape(d, N_EXPERTS)
        wr = jnp.concatenate([we, router_group_w[i]], axis=1)
        wr = jnp.pad(wr, ((0, 0), (0, LANES - wr.shape[1])))
        br = jnp.concatenate([router_expert_b[i].reshape(-1), router_group_b[i]])
        br = jnp.pad(br, (0, LANES - br.shape[0])).reshape(1, LANES)
        wr = wr.astype(F32)
        w_hi = wr.astype(BF16)
        w_lo = (wr - w_hi.astype(F32)).astype(BF16)
        return jnp.concatenate([w_hi, w_lo], axis=1), br.astype(F32)

    g0 = row(rms_mix_g[0])
    w_in = conv_w_in[0].astype(BF16)
    dww = jnp.pad(conv_dw_w[0].astype(F32), ((0, HALO - CONV_WIDTH), (0, 0)))
    dwb, lng, lnb = row(conv_dw_b[0]), row(conv_ln_g[0]), row(conv_ln_b[0])
    w_out = conv_w_out[0].astype(BF16)
    up = glu_proj(xp, g0, w_in, 512)
    us = glu_proj(xs, g0, w_in, 512)
    n_p, n_s = batch * seq, nsb * ts
    n_all = n_p + n_s
    xp = conv_prompt(up, xp, dww, dwb, lng, lnb, w_out, batch, seq, 256)
    us3 = us.reshape(nsb, ts, -1)
    xs = conv_sample(us3, state_conv[0], xs, dww, dwb, lng, lnb, w_out, 32)
    conv_p = up.reshape(batch, seq, -1)[:, seq - PAST:]
    conv_s = jnp.concatenate([state_conv[0], us3], axis=1)[:, ts:]

    wr0, br0 = router_w(0)
    ew_in = expert_w_in.reshape((DEPTH * N_EXPERTS,) + expert_w_in.shape[2:])
    ew_out = expert_w_out.reshape((DEPTH * N_EXPERTS,) + expert_w_out.shape[2:])
    x = moe_layer((xp, xs), row(rms_ffn_g[0]), wr0, br0, ew_in, ew_out, 0)

    g1 = row(rms_mix_g[1])
    w_qkv = attn_w_qkv[0].astype(BF16)
    qg = jnp.tile(attn_q_norm_g[0].astype(F32), N_HEADS).reshape(1, -1)
    kg = jnp.tile(attn_k_norm_g[0].astype(F32), N_KV).reshape(1, -1)
    nq = N_HEADS * HEAD_DIM
    seg = jnp.asarray(np.kron(np.eye(N_HEADS), np.ones((HEAD_DIM, HEAD_DIM))) / HEAD_DIM, BF16)
    w_o = attn_w_o[0].astype(BF16)
    sinks = attn_sinks[0].astype(F32)

    q, k, v = qkv_proj(x, g1, w_qkv, qg, kg, seg, 512)
    kp, vp = k[:n_p], v[:n_p]
    qs, ks, vs = q[n_p:], k[n_p:], v[n_p:]

    q_off = np.arange(WINDOW)[:, None]
    dist_p = q_off + WINDOW - np.arange(2 * WINDOW)[None, :]
    tbl_p = jnp.transpose(_bias_table(rel_bias, dist_p), (0, 2, 1))
    x_attn = attn_prompt(q, k, v, tbl_p, sinks, x, w_o, batch, seq)

    kt = WINDOW + ts
    dist_s = np.arange(ts)[:, None] + WINDOW - np.arange(kt)[None, :]
    tbl_s = _bias_table(rel_bias, dist_s)
    tbl_s = jnp.transpose(tbl_s.reshape(N_KV, GROUP, ts, kt), (0, 2, 1, 3)).reshape(N_KV, ts * GROUP, kt)
    sink_s = jnp.tile(sinks.reshape(N_KV, 1, GROUP), (1, ts, 1)).reshape(N_KV, ts * GROUP, 1)
    q4 = jnp.transpose(qs.reshape(nsb, ts, N_KV, GROUP, HEAD_DIM), (0, 2, 1, 3, 4))
    q4 = q4.reshape(nsb, N_KV, ts * GROUP, HEAD_DIM)
    nk = N_KV * HEAD_DIM
    ks3, vs3 = ks.reshape(nsb, ts, nk), vs.reshape(nsb, ts, nk)
    ck = cache_swa_k[0].reshape(nsb, WINDOW, nk)
    cv = cache_swa_v[0].reshape(nsb, WINDOW, nk)
    o4 = attn_sample(q4, ks3, vs3, ck, cv, tbl_s, sink_s, 16)
    os_ = jnp.transpose(o4.reshape(nsb, N_KV, ts, GROUP, HEAD_DIM), (0, 2, 1, 3, 4)).reshape(nsb * ts, nq)
    x = proj_residual(os_, w_o, x_attn, 512)

    k_p = kp.reshape(batch, seq, N_KV, HEAD_DIM)[:, seq - WINDOW:]
    v_p = vp.reshape(batch, seq, N_KV, HEAD_DIM)[:, seq - WINDOW:]
    k_s = jnp.concatenate([cache_swa_k[0], ks.reshape(nsb, ts, N_KV, HEAD_DIM)], axis=1)[:, ts:]
    v_s = jnp.concatenate([cache_swa_v[0], vs.reshape(nsb, ts, N_KV, HEAD_DIM)], axis=1)[:, ts:]

    wr1, br1 = router_w(1)
    xp, xs = moe_layer((x,), row(rms_ffn_g[1]), wr1, br1, ew_in, ew_out, 1, split_rows=n_p)

    return (xp.reshape(batch, seq, d), xs.reshape(nsb, ts, d),
            conv_p[None], conv_s[None], k_p[None], v_p[None], k_s[None], v_s[None])
```

```python
import functools
import math

import numpy as np
import jax
import jax.numpy as jnp
from jax import lax
from jax.experimental import pallas as pl
from jax.experimental.pallas import tpu as pltpu

D_MODEL = 1024
DEPTH = 2
CONV_WIDTH = 31
PAST = CONV_WIDTH - 1
HEAD_DIM = 64
N_HEADS = 16
N_KV = 2
GROUP = 8
WINDOW = 128
N_BUCKETS = 32
MAX_DISTANCE = 128
N_GROUPS = 4
EPG = 8
N_EXPERTS = 32
D_EXPERT = 256
RMS_EPS = 1e-6
LN_EPS = 1e-5
NEG_INF = -1e30

F32 = jnp.float32
BF16 = jnp.bfloat16
LANES = 128
ROW_TILE = 8
MOE_TM = 512
MOE_TMG = 256
MOE_CHUNK = 8704
TOKEN_TM = 512
CONV_TT = 256
CONV_SAMPLE_BS = 32
ATTN_SAMPLE_BS = 16
V7X_VMEM_BYTES = 64 * 1024 * 1024
MOE_VMEM_LIMIT = V7X_VMEM_BYTES - 8 * 1024 * 1024
VMEM_LIMIT = V7X_VMEM_BYTES - 16 * 1024 * 1024


def _params(*sem):
    return pltpu.CompilerParams(dimension_semantics=sem, vmem_limit_bytes=VMEM_LIMIT)


def _rms(x, g):
    return x * lax.rsqrt(jnp.mean(x * x, axis=-1, keepdims=True) + RMS_EPS) * g


def _sigmoid(x):
    return 1.0 / (1.0 + jnp.exp(-x))


def _full(shape):
    return pl.BlockSpec(shape, lambda *_: (0,) * len(shape))


def _glu_kernel(x_ref, g_ref, wa_ref, wg_ref, u_ref):
    h = _rms(x_ref[...], g_ref[...]).astype(BF16)
    a = jnp.dot(h, wa_ref[...], preferred_element_type=F32)
    gate = jnp.dot(h, wg_ref[...], preferred_element_type=F32)
    u_ref[...] = a * _sigmoid(gate)


def glu_proj(x, g, w_in, tm):
    n, d = x.shape
    c = w_in.shape[1] // 2
    return pl.pallas_call(
        _glu_kernel,
        grid=(n // tm,),
        in_specs=[
            pl.BlockSpec((tm, d), lambda i: (i, 0)),
            _full((1, d)),
            pl.BlockSpec((d, c), lambda i: (0, 0)),
            pl.BlockSpec((d, c), lambda i: (0, 1)),
        ],
        out_specs=pl.BlockSpec((tm, c), lambda i: (i, 0)),
        out_shape=jax.ShapeDtypeStruct((n, c), F32),
        compiler_params=_params("parallel"),
        name="glu_proj",
    )(x, g, w_in, w_in)


def _ln_silu_out(y, lng, lnb, wout_ref, x):
    mu = jnp.mean(y, axis=-1, keepdims=True)
    yc = y - mu
    z = yc * lax.rsqrt(jnp.mean(yc * yc, axis=-1, keepdims=True) + LN_EPS) * lng + lnb
    z = z * _sigmoid(z)
    return x + jnp.dot(z.astype(BF16), wout_ref[...], preferred_element_type=F32)


HALO = 32
CONV_RC = 64
CONV_CC = 128


def _conv_prompt_kernel(ucur_ref, uprev_ref, x_ref, dww_ref, dwb_ref, lng_ref, lnb_ref,
                        wout_ref, o_ref, up_ref, y_ref):
    t = pl.program_id(1)
    tt, c = ucur_ref.shape
    keep = (t > 0).astype(F32)
    up_ref[0:HALO, :] = uprev_ref[...] * keep
    up_ref[HALO:HALO + tt, :] = ucur_ref[...]
    up_ref[HALO + tt:, :] = jnp.zeros((ROW_TILE, c), F32)
    off = HALO - PAST
    for r0 in range(0, tt, CONV_RC):
        for c0 in range(0, c, CONV_CC):
            y = jnp.zeros((CONV_RC, CONV_CC), F32) + dwb_ref[:, c0:c0 + CONV_CC]
            for s in range(ROW_TILE):
                v = None
                for q in range((off + CONV_WIDTH - 1) // ROW_TILE + 1):
                    k = ROW_TILE * q + s - off
                    if k < 0 or k >= CONV_WIDTH:
                        continue
                    lo = r0 + ROW_TILE * q
                    term = (up_ref[lo:lo + CONV_RC + ROW_TILE, c0:c0 + CONV_CC]
                            * dww_ref[k:k + 1, c0:c0 + CONV_CC])
                    v = term if v is None else v + term
                y = y + v[s:s + CONV_RC]
            y_ref[r0:r0 + CONV_RC, c0:c0 + CONV_CC] = y
    o_ref[...] = _ln_silu_out(y_ref[...], lng_ref[...], lnb_ref[...], wout_ref, x_ref[...])


def conv_prompt(u, x, dww, dwb, lng, lnb, wout, batch, seq, tt):
    n, c = u.shape
    d = x.shape[1]
    nt = seq // tt
    hb = tt // HALO
    return pl.pallas_call(
        _conv_prompt_kernel,
        grid=(batch, nt),
        in_specs=[
            pl.BlockSpec((tt, c), lambda b, t: (b * nt + t, 0)),
            pl.BlockSpec((HALO, c), lambda b, t: (jnp.maximum((b * nt + t) * hb - 1, 0), 0)),
            pl.BlockSpec((tt, d), lambda b, t: (b * nt + t, 0)),
            _full((HALO, c)), _full((1, c)), _full((1, c)), _full((1, c)),
            _full((c, d)),
        ],
        out_specs=pl.BlockSpec((tt, d), lambda b, t: (b * nt + t, 0)),
        out_shape=jax.ShapeDtypeStruct((n, d), F32),
        scratch_shapes=[pltpu.VMEM((tt + HALO + ROW_TILE, c), F32), pltpu.VMEM((tt, c), F32)],
        compiler_params=_params("parallel", "parallel"),
        name="conv_prompt",
    )(u, u, x, dww, dwb, lng, lnb, wout)


CONV_SB = 4


def _conv_sample_kernel(u_ref, st_ref, x_ref, dww_ref, dwb_ref, lng_ref, lnb_ref,
                        wout_ref, o_ref, up_ref, y_ref):
    bs, ts, c = u_ref.shape
    up_ref[:, 0:PAST, :] = st_ref[...]
    up_ref[:, PAST:PAST + ts, :] = u_ref[...]
    for b0 in range(0, bs, CONV_SB):
        acc = jnp.zeros((CONV_SB, ts, c), F32) + dwb_ref[...][None]
        for k in range(CONV_WIDTH):
            acc = acc + up_ref[b0:b0 + CONV_SB, k:k + ts, :] * dww_ref[k:k + 1, :][None]
        y_ref[b0 * ts:(b0 + CONV_SB) * ts, :] = acc.reshape(CONV_SB * ts, c)
    o_ref[...] = _ln_silu_out(y_ref[...], lng_ref[...], lnb_ref[...], wout_ref, x_ref[...])


def conv_sample(u3, state, x, dww, dwb, lng, lnb, wout, bs):
    nb, ts, c = u3.shape
    d = x.shape[1]
    return pl.pallas_call(
        _conv_sample_kernel,
        grid=(nb // bs,),
        in_specs=[
            pl.BlockSpec((bs, ts, c), lambda i: (i, 0, 0)),
            pl.BlockSpec((bs, PAST, c), lambda i: (i, 0, 0)),
            pl.BlockSpec((bs * ts, d), lambda i: (i, 0)),
            _full((HALO, c)), _full((1, c)), _full((1, c)), _full((1, c)),
            _full((c, d)),
        ],
        out_specs=pl.BlockSpec((bs * ts, d), lambda i: (i, 0)),
        out_shape=jax.ShapeDtypeStruct((nb * ts, d), F32),
        scratch_shapes=[pltpu.VMEM((bs, PAST + ts, c), F32), pltpu.VMEM((bs * ts, c), F32)],
        compiler_params=_params("parallel"),
        name="conv_sample",
    )(u3, state, x, dww, dwb, lng, lnb, wout)


def _token_specs(xs, tm):
    d = xs[0].shape[1]
    if len(xs) == 1:
        return [pl.BlockSpec((tm, d), lambda i: (i, 0))]
    first = xs[0].shape[0] // tm
    return [pl.BlockSpec((tm, d), lambda i: (jnp.minimum(i, first - 1), 0)),
            pl.BlockSpec((tm, d), lambda i: (jnp.maximum(i - first, 0), 0))]


def _read_tokens(x_refs, first_steps):
    if len(x_refs) == 1:
        return x_refs[0][...]
    return jnp.where(pl.program_id(0) < first_steps, x_refs[0][...], x_refs[1][...])


def _router_kernel(tiles_per_chunk, n_src, first_steps, *refs):
    x_refs = refs[:n_src]
    g_ref, wr_ref, br_ref, tri_ref, hs_ref, route_ref, route_t_ref, cnt_ref, carry_ref = refs[n_src:]
    h = _rms(_read_tokens(x_refs, first_steps), g_ref[...])
    tm, d = h.shape
    for j in range(d // LANES):
        hs_ref[pl.ds(j, tm, stride=ROW_TILE), :] = h[:, j * LANES:(j + 1) * LANES]
    h_hi = h.astype(BF16)
    h_lo = (h - h_hi.astype(F32)).astype(BF16)
    both = jnp.dot(h_hi, wr_ref[...], preferred_element_type=F32)
    cross = jnp.dot(h_lo, wr_ref[:, :LANES], preferred_element_type=F32)
    logits = both[:, :LANES] + both[:, LANES:] + cross + br_ref[...]
    lane = lax.broadcasted_iota(jnp.int32, logits.shape, 1).astype(F32)
    big = jnp.float32(LANES)
    is_g = (lane >= N_EXPERTS) & (lane < N_EXPERTS + N_GROUPS)
    gl = jnp.where(is_g, logits, NEG_INF)
    gm = jnp.max(gl, axis=-1, keepdims=True)
    g_sel = jnp.min(jnp.where(gl == gm, lane, big), axis=-1, keepdims=True) - N_EXPERTS
    gate_g = 1.0 / jnp.sum(jnp.where(is_g, jnp.exp(gl - gm), 0.0), axis=-1, keepdims=True)
    lo = g_sel * EPG
    in_grp = (lane >= lo) & (lane < lo + EPG)
    el = jnp.where(in_grp, logits, NEG_INF)
    v1 = jnp.max(el, axis=-1, keepdims=True)
    i1 = jnp.min(jnp.where(el == v1, lane, big), axis=-1, keepdims=True)
    el2 = jnp.where(lane == i1, NEG_INF, el)
    v2 = jnp.max(el2, axis=-1, keepdims=True)
    i2 = jnp.min(jnp.where(el2 == v2, lane, big), axis=-1, keepdims=True)
    e2 = jnp.exp(v2 - v1)
    w1 = gate_g / (1.0 + e2)
    w2 = gate_g * e2 / (1.0 + e2)
    @pl.when(pl.program_id(0) % tiles_per_chunk == 0)
    def _():
        carry_ref[...] = jnp.zeros_like(carry_ref)

    hit1 = lane == i1
    hit2 = lane == i2
    onehot = jnp.where(hit1 | hit2, 1.0, 0.0)
    before = carry_ref[...] + jnp.dot(tri_ref[...], onehot.astype(BF16), preferred_element_type=F32)
    r1 = jnp.sum(jnp.where(hit1, before, 0.0), axis=-1, keepdims=True)
    r2 = jnp.sum(jnp.where(hit2, before, 0.0), axis=-1, keepdims=True)
    carry_ref[...] += jnp.sum(onehot, axis=0, keepdims=True)
    cnt_ref[...] = jnp.broadcast_to(carry_ref[...], cnt_ref.shape)
    route = jnp.where(lane == 0.0, i1, jnp.where(lane == 1.0, i2, jnp.where(
        lane == 2.0, w1, jnp.where(lane == 3.0, w2, jnp.where(
            lane == 4.0, r1, jnp.where(lane == 5.0, r2, 0.0))))))
    route_ref[...] = route
    route_t_ref[...] = jnp.transpose(route)[0:ROW_TILE, :]


def router(xs, g, wr, br, tm, chunk):
    n = sum(x.shape[0] for x in xs)
    d = xs[0].shape[1]
    tpc = chunk // tm
    tri = jnp.asarray(np.tril(np.ones((tm, tm), np.float32), -1), BF16)
    return pl.pallas_call(
        functools.partial(_router_kernel, tpc, len(xs), xs[0].shape[0] // tm),
        grid=(n // tm,),
        in_specs=_token_specs(xs, tm) + [_full((1, d)),
                  _full((d, 2 * LANES)), _full((1, LANES)), _full((tm, tm))],
        out_specs=[pl.BlockSpec((tm * ROW_TILE, LANES), lambda i: (i, 0)),
                   pl.BlockSpec((tm, LANES), lambda i: (i, 0)),
                   pl.BlockSpec((ROW_TILE, tm), lambda i: (i, 0)),
                   pl.BlockSpec((ROW_TILE, LANES), lambda i: (i // tpc, 0))],
        out_shape=[jax.ShapeDtypeStruct((n * ROW_TILE, LANES), F32),
                   jax.ShapeDtypeStruct((n, LANES), F32),
                   jax.ShapeDtypeStruct((n // tm * ROW_TILE, tm), F32),
                   jax.ShapeDtypeStruct((n // chunk * ROW_TILE, LANES), F32)],
        scratch_shapes=[pltpu.VMEM((1, LANES), F32)],
        compiler_params=_params("arbitrary"),
        name="router",
    )(*xs, g, wr, br, tri)


def _gather_rows(idx_ref, src_hbm, dst, sem, n_rows):
    unroll = 8

    def body(c, carry):
        for u in range(unroll):
            m = c * unroll + u
            tok = idx_ref[m]
            pltpu.async_copy(
                src_hbm.at[pl.ds(pl.multiple_of(tok * ROW_TILE, ROW_TILE), ROW_TILE), :],
                dst.at[pl.ds(pl.multiple_of(m * ROW_TILE, ROW_TILE), ROW_TILE), :],
                sem, priority=u % 2)
        return carry

    lax.fori_loop(0, n_rows // unroll, body, 0)


def _wait_rows(src_hbm, dst, sem):
    pltpu.make_async_copy(src_hbm.at[pl.ds(0, dst.shape[0]), :], dst, sem).wait()


def _from_token_tiles(buf, start, n_rows, stride):
    return jnp.concatenate(
        [buf[pl.ds(start + j, n_rows, stride=stride), :] for j in range(ROW_TILE)], axis=-1)


def _experts_by_expert_kernel(tiles_per_chunk, tmg, t0_ref, nt_ref, pos_ref, fill_hbm, hs_hbm,
                              win_ref, wout_ref, ys_hbm, hsv, gbuf_a, gbuf_b, ystage_a, ystage_b,
                              src_ref, sems, winb, woutb):
    c = pl.program_id(0)
    e = pl.program_id(1)
    chunk = pos_ref.shape[0] // 2
    tile_rows = tmg * ROW_TILE
    scatter_unroll = 32
    gather_unroll = 8

    def tile(r):
        return pl.ds(pl.multiple_of(r * ROW_TILE, ROW_TILE), ROW_TILE)

    def ys_tile(t):
        return ys_hbm.at[pl.ds(pl.multiple_of((c * tiles_per_chunk + t) * tile_rows, tile_rows),
                               tile_rows), :]

    @pl.when(e == 0)
    def _():
        load = pltpu.make_async_copy(
            hs_hbm.at[pl.ds(pl.multiple_of(c * chunk * ROW_TILE, ROW_TILE), chunk * ROW_TILE), :],
            hsv.at[pl.ds(0, chunk * ROW_TILE), :], sems.at[0])
        load.start()
        fill = pltpu.make_async_copy(fill_hbm, src_ref, sems.at[1])
        fill.start()
        hsv[pl.ds(chunk * ROW_TILE, ROW_TILE), :] = jnp.zeros((ROW_TILE, LANES), F32)
        fill.wait()

        def scatter(i, carry):
            first = i * scatter_unroll
            tok0 = first // (2 * MOE_TM) * MOE_TM + first % MOE_TM
            for u in range(scatter_unroll):
                src_ref[pos_ref[first + u]] = tok0 + u
            return carry

        lax.fori_loop(0, 2 * chunk // scatter_unroll, scatter, 0)
        load.wait()

        def gather(i, carry):
            for u in range(gather_unroll):
                m = i * gather_unroll + u
                gbuf_a[tile(m), :] = hsv[tile(src_ref[m]), :]
            return carry

        lax.fori_loop(0, tmg // gather_unroll, gather, 0)

    t0 = t0_ref[c * N_EXPERTS + e]
    nt = nt_ref[c * N_EXPERTS + e]

    @pl.when(nt > 0)
    def _():
        winb[...] = win_ref[0].astype(BF16)
        woutb[...] = wout_ref[0].astype(BF16)

    def step(t, cur, nxt, ystage, sem):
        base = jnp.minimum(t + 1, tiles_per_chunk - 1) * tmg
        for m in range(tmg):
            nxt[pl.ds(m * ROW_TILE, ROW_TILE), :] = hsv[tile(src_ref[base + m]), :]
        x = _from_token_tiles(cur, 0, tmg, ROW_TILE).astype(BF16)
        hid = jnp.dot(x, winb[...], preferred_element_type=F32)
        a = hid[:, :D_EXPERT]
        u = hid[:, D_EXPERT:]
        act = (a * _sigmoid(a) * u).astype(BF16)
        y = jnp.dot(act, woutb[...], preferred_element_type=F32)

        @pl.when(t >= 2)
        def _():
            pltpu.make_async_copy(ystage, ys_tile(t), sem).wait()

        for k in range(ROW_TILE):
            ystage[pl.ds(k, tmg, stride=ROW_TILE), :] = y[:, k * LANES:(k + 1) * LANES]
        pltpu.make_async_copy(ystage, ys_tile(t), sem).start()

    def one_tile(k, carry):
        t = t0 + k

        @pl.when(t % 2 == 0)
        def _():
            step(t, gbuf_a, gbuf_b, ystage_a, sems.at[2])

        @pl.when(t % 2 == 1)
        def _():
            step(t, gbuf_b, gbuf_a, ystage_b, sems.at[3])

        return carry

    lax.fori_loop(0, nt, one_tile, 0)

    @pl.when(e == pl.num_programs(1) - 1)
    def _():
        total = t0 + nt

        @pl.when(total >= 1)
        def _():
            pltpu.make_async_copy(ystage_a, ys_tile(0), sems.at[2]).wait()

        @pl.when(total >= 2)
        def _():
            pltpu.make_async_copy(ystage_b, ys_tile(0), sems.at[3]).wait()

        ystage_a[...] = jnp.zeros_like(ystage_a)

        def zero_tile(t, carry):
            cp = pltpu.make_async_copy(ystage_a, ys_tile(t), sems.at[2])
            cp.start()
            cp.wait()
            return carry

        lax.fori_loop(total, tiles_per_chunk, zero_tile, 0)


def experts_by_expert(hs, pos_local, tile_start, n_tiles, w_in, w_out, layer, tmg, chunk, tiles_per_chunk):
    _, d, f2 = w_in.shape
    nc = hs.shape[0] // (chunk * ROW_TILE)
    rows_per_chunk = tiles_per_chunk * tmg
    fill = jnp.full((rows_per_chunk,), chunk, jnp.int32)
    off = layer * N_EXPERTS
    grid_spec = pltpu.PrefetchScalarGridSpec(
        num_scalar_prefetch=2,
        grid=(nc, N_EXPERTS),
        in_specs=[
            pl.BlockSpec((2 * chunk,), lambda c, e, t0, nt: (c,), memory_space=pltpu.SMEM),
            pl.BlockSpec(memory_space=pl.ANY),
            pl.BlockSpec(memory_space=pl.ANY),
            pl.BlockSpec((1, d, f2), lambda c, e, t0, nt: (off + e, 0, 0)),
            pl.BlockSpec((1, f2 // 2, d), lambda c, e, t0, nt: (off + e, 0, 0)),
        ],
        out_specs=pl.BlockSpec(memory_space=pl.ANY),
        scratch_shapes=[
            pltpu.VMEM(((chunk + 1) * ROW_TILE, LANES), F32),
            pltpu.VMEM((tmg * ROW_TILE, LANES), F32),
            pltpu.VMEM((tmg * ROW_TILE, LANES), F32),
            pltpu.VMEM((tmg * ROW_TILE, LANES), F32),
            pltpu.VMEM((tmg * ROW_TILE, LANES), F32),
            pltpu.SMEM((rows_per_chunk,), jnp.int32),
            pltpu.SemaphoreType.DMA((4,)),
            pltpu.VMEM((d, f2), BF16),
            pltpu.VMEM((f2 // 2, d), BF16),
        ],
    )
    return pl.pallas_call(
        functools.partial(_experts_by_expert_kernel, tiles_per_chunk, tmg),
        grid_spec=grid_spec,
        out_shape=jax.ShapeDtypeStruct((nc * rows_per_chunk * ROW_TILE, LANES), F32),
        compiler_params=pltpu.CompilerParams(dimension_semantics=("arbitrary", "arbitrary"),
                                             vmem_limit_bytes=MOE_VMEM_LIMIT),
        name="moe_experts",
    )(tile_start, n_tiles, pos_local, fill, hs, w_in, w_out)


def _combine_kernel(split_steps, n_src, first_steps, pos_cur_ref, pos_nxt_ref, ys_hbm, route_ref, *rest):
    x_refs, o_refs, (ybuf_a, ybuf_b, sems) = rest[:n_src], rest[n_src:-3], rest[-3:]
    i = pl.program_id(0)
    last = i == pl.num_programs(0) - 1
    tm = route_ref.shape[0]

    @pl.when(i == 0)
    def _():
        _gather_rows(pos_cur_ref, ys_hbm, ybuf_a, sems.at[0], 2 * tm)

    def step(cur, cur_sem, nxt, nxt_sem):
        _gather_rows(pos_nxt_ref, ys_hbm, nxt, nxt_sem, 2 * tm)
        _wait_rows(ys_hbm, cur, cur_sem)
        route = route_ref[...]
        y1 = _from_token_tiles(cur, 0, tm, ROW_TILE)
        y2 = _from_token_tiles(cur, tm * ROW_TILE, tm, ROW_TILE)
        val = _read_tokens(x_refs, first_steps) + route[:, 2:3] * y1 + route[:, 3:4] * y2
        if split_steps is None:
            o_refs[0][...] = val
        else:
            @pl.when(i < split_steps)
            def _():
                o_refs[0][...] = val

            @pl.when(i >= split_steps)
            def _():
                o_refs[1][...] = val

        @pl.when(last)
        def _():
            _wait_rows(ys_hbm, nxt, nxt_sem)

    @pl.when(i % 2 == 0)
    def _():
        step(ybuf_a, sems.at[0], ybuf_b, sems.at[1])

    @pl.when(i % 2 == 1)
    def _():
        step(ybuf_b, sems.at[1], ybuf_a, sems.at[0])


def combine(ys, pos, route, xs, tm, split_rows=None):
    n = sum(x.shape[0] for x in xs)
    d = xs[0].shape[1]
    nsteps = n // tm
    if split_rows is None:
        split_steps = None
        out_specs = pl.BlockSpec((tm, d), lambda i: (i, 0))
        out_shape = jax.ShapeDtypeStruct((n, d), F32)
    else:
        split_steps = split_rows // tm
        out_specs = [pl.BlockSpec((tm, d), lambda i: (jnp.minimum(i, split_steps - 1), 0)),
                     pl.BlockSpec((tm, d), lambda i: (jnp.maximum(i - split_steps, 0), 0))]
        out_shape = [jax.ShapeDtypeStruct((split_rows, d), F32),
                     jax.ShapeDtypeStruct((n - split_rows, d), F32)]
    return pl.pallas_call(
        functools.partial(_combine_kernel, split_steps, len(xs), xs[0].shape[0] // tm),
        grid=(nsteps,),
        in_specs=[
            pl.BlockSpec((2 * tm,), lambda i: (i,), memory_space=pltpu.SMEM),
            pl.BlockSpec((2 * tm,), lambda i: (jnp.minimum(i + 1, nsteps - 1),),
                         memory_space=pltpu.SMEM),
            pl.BlockSpec(memory_space=pl.ANY),
            pl.BlockSpec((tm, LANES), lambda i: (i, 0)),
        ] + _token_specs(xs, tm),
        out_specs=out_specs,
        out_shape=out_shape,
        scratch_shapes=[pltpu.VMEM((2 * tm * ROW_TILE, LANES), F32),
                        pltpu.VMEM((2 * tm * ROW_TILE, LANES), F32),
                        pltpu.SemaphoreType.DMA((2,))],
        compiler_params=_params("arbitrary"),
        name="moe_combine",
    )(pos, pos, ys, route, *xs)


def moe_layer(xs, g, wr, br, w_in, w_out, layer, split_rows=None):
    n = sum(x.shape[0] for x in xs)
    nc = -(-n // MOE_CHUNK)
    chunk = n // nc
    assert chunk * nc == n and chunk % MOE_TM == 0
    tpc = (2 * chunk + N_EXPERTS * (MOE_TMG - 1)) // MOE_TMG
    hs, route, route_t, counts = router(xs, g, wr, br, MOE_TM, chunk)
    rt = route_t.reshape(nc, chunk // MOE_TM, ROW_TILE, MOE_TM)
    ids = rt[:, :, 0:2, :].astype(jnp.int32)
    ranks = rt[:, :, 4:6, :].astype(jnp.int32)
    cnt = counts.reshape(nc, ROW_TILE, LANES)[:, 0, :N_EXPERTS].astype(jnp.int32)
    padded = (cnt + MOE_TMG - 1) // MOE_TMG * MOE_TMG
    ends = jnp.cumsum(padded, axis=1)
    starts = ends - padded
    start_of = jnp.zeros_like(ids)
    for e in range(N_EXPERTS):
        start_of = jnp.where(ids == e, starts[:, e][:, None, None, None], start_of)
    pos_local = start_of + ranks
    ys = experts_by_expert(hs, pos_local.reshape(-1), (starts // MOE_TMG).reshape(-1),
                           (padded // MOE_TMG).reshape(-1), w_in, w_out, layer, MOE_TMG, chunk, tpc)
    pos = pos_local + (jnp.arange(nc, dtype=jnp.int32) * (tpc * MOE_TMG))[:, None, None, None]
    return combine(ys, pos.reshape(-1), route, xs, MOE_TM, split_rows)


def _qkv_kernel(x_ref, g_ref, w_ref, qg_ref, kg_ref, seg_ref, q_ref, k_ref, v_ref):
    h = _rms(x_ref[...], g_ref[...]).astype(BF16)
    qkv = jnp.dot(h, w_ref[...], preferred_element_type=F32)
    nq = N_HEADS * HEAD_DIM
    nk = N_KV * HEAD_DIM
    q = qkv[:, :nq]
    k = qkv[:, nq:nq + nk]
    v_ref[...] = qkv[:, nq + nk:]

    def seg_mean_sq(z, seg, split):
        zz = z * z
        hi = zz.astype(BF16)
        ms = jnp.dot(hi, seg, preferred_element_type=F32)
        if split:
            lo = (zz - hi.astype(F32)).astype(BF16)
            ms = ms + jnp.dot(lo, seg, preferred_element_type=F32)
        return ms

    seg = seg_ref[...]
    qn = q * lax.rsqrt(seg_mean_sq(q, seg, False) + RMS_EPS) * qg_ref[...]
    q_ref[...] = (qn * (HEAD_DIM ** -0.5)).astype(BF16)
    kn = k * lax.rsqrt(seg_mean_sq(k, seg[:nk, :nk], True) + RMS_EPS) * kg_ref[...]
    k_ref[...] = kn


def qkv_proj(x, g, w, qg, kg, seg, tm):
    n, d = x.shape
    nq = N_HEADS * HEAD_DIM
    nk = N_KV * HEAD_DIM
    return pl.pallas_call(
        _qkv_kernel,
        grid=(n // tm,),
        in_specs=[pl.BlockSpec((tm, d), lambda i: (i, 0)), _full((1, d)),
                  _full((d, nq + 2 * nk)), _full((1, nq)), _full((1, nk)), _full((nq, nq))],
        out_specs=[pl.BlockSpec((tm, nq), lambda i: (i, 0)),
                   pl.BlockSpec((tm, nk), lambda i: (i, 0)),
                   pl.BlockSpec((tm, nk), lambda i: (i, 0))],
        out_shape=[jax.ShapeDtypeStruct((n, nq), BF16),
                   jax.ShapeDtypeStruct((n, nk), F32),
                   jax.ShapeDtypeStruct((n, nk), F32)],
        compiler_params=_params("parallel"),
        name="qkv_proj",
    )(x, g, w, qg, kg, seg)


def _attn_prompt_kernel(q_ref, kc_ref, kp_ref, vc_ref, vp_ref, tbl_ref, sink_ref, x_ref,
                        wo_ref, o_ref, att_ref, s_ref, e_ref, inv_ref):
    n = pl.program_id(1)
    first = jnp.where(n == 0, NEG_INF, 0.0).astype(F32)
    nt = (((1,), (1,)), ((), ()))
    lo_half = lax.broadcasted_iota(jnp.int32, (WINDOW, LANES), 1) < HEAD_DIM
    kp, kc = kp_ref[...], kc_ref[...]
    keys = (kp.astype(BF16), kc.astype(BF16))
    keys_swapped = (pltpu.roll(kp, HEAD_DIM, axis=1).astype(BF16),
                    pltpu.roll(kc, HEAD_DIM, axis=1).astype(BF16))
    vt_prev = jnp.transpose(vp_ref[...]).astype(BF16)
    vt_cur = jnp.transpose(vc_ref[...]).astype(BF16)
    for h in range(N_HEADS):
        kh = h // GROUP
        odd = h % 2
        pair = q_ref[:, (h // 2) * LANES:(h // 2 + 1) * LANES]
        qm = jnp.where(lo_half if odd == 0 else ~lo_half, pair, jnp.zeros_like(pair))
        k_prev, k_cur = keys_swapped if (kh == 0) == (odd == 1) else keys
        s_ref[h, 0:WINDOW, :] = (lax.dot_general(k_prev, qm, nt, preferred_element_type=F32)
                                 + tbl_ref[h, 0:WINDOW, :] + first)
        s_ref[h, WINDOW:, :] = (lax.dot_general(k_cur, qm, nt, preferred_element_type=F32)
                                + tbl_ref[h, WINDOW:, :])
    for h in range(N_HEADS):
        s = s_ref[h]
        sink = sink_ref[h]
        m = jnp.maximum(jnp.max(s, axis=0, keepdims=True), sink)
        e = jnp.exp(s - m)
        e_ref[h] = e.astype(BF16)
        inv_ref[h] = 1.0 / (jnp.sum(e, axis=0, keepdims=True) + jnp.exp(sink - m))
    for h in range(N_HEADS):
        kh = h // GROUP
        dims = slice(kh * HEAD_DIM, (kh + 1) * HEAD_DIM)
        out_t = (jnp.dot(vt_prev[dims, :], e_ref[h, 0:WINDOW, :], preferred_element_type=F32)
                 + jnp.dot(vt_cur[dims, :], e_ref[h, WINDOW:, :], preferred_element_type=F32))
        att_ref[h * HEAD_DIM:(h + 1) * HEAD_DIM, :] = out_t * inv_ref[h]
    att = jnp.transpose(att_ref[...]).astype(BF16)
    o_ref[...] = x_ref[...] + jnp.dot(att, wo_ref[...], preferred_element_type=F32)


def attn_prompt(q, k, v, tbl, sinks, x, wo, batch, seq):
    n, d = x.shape
    nb = seq // WINDOW
    nk = N_KV * HEAD_DIM
    cur = lambda b, i: (b * nb + i, 0)
    prev = lambda b, i: (b * nb + jnp.maximum(i - 1, 0), 0)
    return pl.pallas_call(
        _attn_prompt_kernel,
        grid=(batch, nb),
        in_specs=[
            pl.BlockSpec((WINDOW, d), cur),
            pl.BlockSpec((WINDOW, nk), cur), pl.BlockSpec((WINDOW, nk), prev),
            pl.BlockSpec((WINDOW, nk), cur), pl.BlockSpec((WINDOW, nk), prev),
            _full((N_HEADS, 2 * WINDOW, WINDOW)),
            pl.BlockSpec(memory_space=pltpu.SMEM),
            pl.BlockSpec((WINDOW, d), cur),
            _full((d, d)),
        ],
        out_specs=pl.BlockSpec((WINDOW, d), cur),
        out_shape=jax.ShapeDtypeStruct((n, d), F32),
        input_output_aliases={7: 0},
        scratch_shapes=[pltpu.VMEM((N_HEADS * HEAD_DIM, WINDOW), F32),
                        pltpu.VMEM((N_HEADS, 2 * WINDOW, WINDOW), F32),
                        pltpu.VMEM((N_HEADS, 2 * WINDOW, WINDOW), BF16),
                        pltpu.VMEM((N_HEADS, 1, WINDOW), F32)],
        compiler_params=_params("parallel", "parallel"),
        name="attn_prompt",
    )(q, k, k, v, v, tbl, sinks, x, wo)


def _attn_sample_kernel(q_ref, kn_ref, vn_ref, ck_ref, cv_ref, tblc_ref, tbln_ref, sink_ref, o_ref,
                        sc_ref, sn_ref, ec_ref, en_ref, inv_ref):
    bs = q_ref.shape[0]
    nt = (((1,), (1,)), ((), ()))
    for b in range(bs):
        ck = ck_ref[b].astype(BF16)
        kn = kn_ref[b].astype(BF16)
        for kh in range(N_KV):
            lanes = slice(kh * HEAD_DIM, (kh + 1) * HEAD_DIM)
            qt = q_ref[b, kh]
            sc_ref[b, kh] = lax.dot_general(qt, ck[:, lanes], nt, preferred_element_type=F32) + tblc_ref[kh]
            sn_ref[b, kh] = lax.dot_general(qt, kn[:, lanes], nt, preferred_element_type=F32) + tbln_ref[kh]
    for b in range(bs):
        for kh in range(N_KV):
            sc = sc_ref[b, kh]
            sn = sn_ref[b, kh]
            sink = sink_ref[kh]
            m = jnp.maximum(jnp.maximum(jnp.max(sc, axis=-1, keepdims=True),
                                        jnp.max(sn, axis=-1, keepdims=True)), sink)
            ec = jnp.exp(sc - m)
            en = jnp.exp(sn - m)
            ec_ref[b, kh] = ec.astype(BF16)
            en_ref[b, kh] = en.astype(BF16)
            inv_ref[b, kh] = 1.0 / (jnp.sum(ec, axis=-1, keepdims=True)
                                    + jnp.sum(en, axis=-1, keepdims=True) + jnp.exp(sink - m))
    for b in range(bs):
        cv = cv_ref[b].astype(BF16)
        vn = vn_ref[b].astype(BF16)
        for kh in range(N_KV):
            lanes = slice(kh * HEAD_DIM, (kh + 1) * HEAD_DIM)
            out = (jnp.dot(ec_ref[b, kh], cv[:, lanes], preferred_element_type=F32)
                   + jnp.dot(en_ref[b, kh], vn[:, lanes], preferred_element_type=F32))
            o_ref[b, kh] = (out * inv_ref[b, kh]).astype(BF16)


def attn_sample(q4, kn, vn, ck, cv, tbl, sink, bs):
    nb = q4.shape[0]
    ts = kn.shape[1]
    nk = N_KV * HEAD_DIM
    tg = ts * GROUP
    return pl.pallas_call(
        _attn_sample_kernel,
        grid=(nb // bs,),
        in_specs=[
            pl.BlockSpec((bs, N_KV, tg, HEAD_DIM), lambda i: (i, 0, 0, 0)),
            pl.BlockSpec((bs, ts, nk), lambda i: (i, 0, 0)),
            pl.BlockSpec((bs, ts, nk), lambda i: (i, 0, 0)),
            pl.BlockSpec((bs, WINDOW, nk), lambda i: (i, 0, 0)),
            pl.BlockSpec((bs, WINDOW, nk), lambda i: (i, 0, 0)),
            _full((N_KV, tg, WINDOW)),
            _full((N_KV, tg, ts)),
            _full((N_KV, tg, 1)),
        ],
        out_specs=pl.BlockSpec((bs, N_KV, tg, HEAD_DIM), lambda i: (i, 0, 0, 0)),
        out_shape=jax.ShapeDtypeStruct((nb, N_KV, tg, HEAD_DIM), BF16),
        scratch_shapes=[pltpu.VMEM((bs, N_KV, tg, WINDOW), F32),
                        pltpu.VMEM((bs, N_KV, tg, ts), F32),
                        pltpu.VMEM((bs, N_KV, tg, WINDOW), BF16),
                        pltpu.VMEM((bs, N_KV, tg, ts), BF16),
                        pltpu.VMEM((bs, N_KV, tg, 1), F32)],
        compiler_params=_params("parallel"),
        name="attn_sample",
    )(q4, kn, vn, ck, cv, tbl[:, :, :WINDOW], tbl[:, :, WINDOW:], sink)


def _proj_res_kernel(a_ref, w_ref, x_ref, o_ref):
    o_ref[...] = x_ref[...] + jnp.dot(a_ref[...], w_ref[...], preferred_element_type=F32)


def proj_residual(a, w, x, tm):
    n, d = x.shape
    na, kdim = a.shape
    off = (n - na) // tm
    return pl.pallas_call(
        _proj_res_kernel,
        grid=(na // tm,),
        in_specs=[pl.BlockSpec((tm, kdim), lambda i: (i, 0)), _full((kdim, d)),
                  pl.BlockSpec((tm, d), lambda i: (off + i, 0))],
        out_specs=pl.BlockSpec((tm, d), lambda i: (off + i, 0)),
        out_shape=jax.ShapeDtypeStruct((n, d), F32),
        input_output_aliases={2: 0},
        compiler_params=_params("parallel"),
        name="proj_residual",
    )(a, w, x)


def _t5_bucket_np(dist):
    n = np.maximum(dist, 0)
    max_exact = N_BUCKETS // 2
    large = max_exact + (np.log(np.maximum(n, 1).astype(np.float32) / max_exact)
                         / math.log(MAX_DISTANCE / max_exact) * (N_BUCKETS - max_exact)).astype(np.int32)
    large = np.minimum(large, N_BUCKETS - 1)
    return np.where(n < max_exact, n, large)


def _bias_table(rel_bias, dist):
    valid = (dist >= 0) & (dist <= WINDOW)
    onehot = (np.asarray(_t5_bucket_np(dist))[..., None] == np.arange(N_BUCKETS)).astype(np.float32)
    b = jnp.einsum("qkb,bh->hqk", jnp.asarray(onehot), rel_bias.astype(F32),
                   precision=lax.Precision.HIGHEST)
    return jnp.where(jnp.asarray(valid)[None], b, NEG_INF)


def kernel(x_prompt, x_sample, state_conv, cache_swa_k, cache_swa_v, rms_mix_g, rms_ffn_g, conv_w_in, conv_dw_w, conv_dw_b, conv_ln_g, conv_ln_b, conv_w_out, attn_w_qkv, attn_q_norm_g, attn_k_norm_g, attn_sinks, attn_w_o, rel_bias, router_group_w, router_group_b, router_expert_w, router_expert_b, expert_w_in, expert_w_out):
    batch, seq, d = x_prompt.shape
    nsb, ts, _ = x_sample.shape
    xp = x_prompt.reshape(batch * seq, d)
    xs = x_sample.reshape(nsb * ts, d)
    row = lambda a: a.reshape(1, -1).astype(F32)

    def router_w(i):
        we = jnp.transpose(router_expert_w[i], (1, 0, 2)).reshape(d, N_EXPERTS)
        wr = jnp.concatenate([we, router_group_w[i]], axis=1)
        wr = jnp.pad(wr, ((0, 0), (0, LANES - wr.shape[1])))
        br = jnp.concatenate([router_expert_b[i].reshape(-1), router_group_b[i]])
        br = jnp.pad(br, (0, LANES - br.shape[0])).reshape(1, LANES)
        wr = wr.astype(F32)
        w_hi = wr.astype(BF16)
        w_lo = (wr - w_hi.astype(F32)).astype(BF16)
        return jnp.concatenate([w_hi, w_lo], axis=1), br.astype(F32)

    g0 = row(rms_mix_g[0])
    w_in = conv_w_in[0].astype(BF16)
    dww = jnp.pad(conv_dw_w[0].astype(F32), ((0, HALO - CONV_WIDTH), (0, 0)))
    dwb, lng, lnb = row(conv_dw_b[0]), row(conv_ln_g[0]), row(conv_ln_b[0])
    w_out = conv_w_out[0].astype(BF16)
    up = glu_proj(xp, g0, w_in, TOKEN_TM)
    us = glu_proj(xs, g0, w_in, TOKEN_TM)
    n_p = batch * seq
    xp = conv_prompt(up, xp, dww, dwb, lng, lnb, w_out, batch, seq, CONV_TT)
    us3 = us.reshape(nsb, ts, -1)
    xs = conv_sample(us3, state_conv[0], xs, dww, dwb, lng, lnb, w_out, CONV_SAMPLE_BS)
    conv_p = up.reshape(batch, seq, -1)[:, seq - PAST:]
    conv_s = jnp.concatenate([state_conv[0], us3], axis=1)[:, ts:]

    wr0, br0 = router_w(0)
    ew_in = expert_w_in.reshape((DEPTH * N_EXPERTS,) + expert_w_in.shape[2:])
    ew_out = expert_w_out.reshape((DEPTH * N_EXPERTS,) + expert_w_out.shape[2:])
    x = moe_layer((xp, xs), row(rms_ffn_g[0]), wr0, br0, ew_in, ew_out, 0)

    g1 = row(rms_mix_g[1])
    w_qkv = attn_w_qkv[0].astype(BF16)
    qg = jnp.tile(attn_q_norm_g[0].astype(F32), N_HEADS).reshape(1, -1)
    kg = jnp.tile(attn_k_norm_g[0].astype(F32), N_KV).reshape(1, -1)
    nq = N_HEADS * HEAD_DIM
    seg = jnp.asarray(np.kron(np.eye(N_HEADS), np.ones((HEAD_DIM, HEAD_DIM))) / HEAD_DIM, BF16)
    w_o = attn_w_o[0].astype(BF16)
    sinks = attn_sinks[0].astype(F32)

    q, k, v = qkv_proj(x, g1, w_qkv, qg, kg, seg, TOKEN_TM)
    qs, ks, vs = q[n_p:], k[n_p:], v[n_p:]

    q_off = np.arange(WINDOW)[:, None]
    dist_p = q_off + WINDOW - np.arange(2 * WINDOW)[None, :]
    tbl_p = jnp.transpose(_bias_table(rel_bias, dist_p), (0, 2, 1))
    x_attn = attn_prompt(q, k, v, tbl_p, sinks, x, w_o, batch, seq)

    kt = WINDOW + ts
    dist_s = np.arange(ts)[:, None] + WINDOW - np.arange(kt)[None, :]
    tbl_s = _bias_table(rel_bias, dist_s)
    tbl_s = jnp.transpose(tbl_s.reshape(N_KV, GROUP, ts, kt), (0, 2, 1, 3)).reshape(N_KV, ts * GROUP, kt)
    sink_s = jnp.tile(sinks.reshape(N_KV, 1, GROUP), (1, ts, 1)).reshape(N_KV, ts * GROUP, 1)
    q4 = jnp.transpose(qs.reshape(nsb, ts, N_KV, GROUP, HEAD_DIM), (0, 2, 1, 3, 4))
    q4 = q4.reshape(nsb, N_KV, ts * GROUP, HEAD_DIM)
    nk = N_KV * HEAD_DIM
    ks3, vs3 = ks.reshape(nsb, ts, nk), vs.reshape(nsb, ts, nk)
    ck = cache_swa_k[0].reshape(nsb, WINDOW, nk)
    cv = cache_swa_v[0].reshape(nsb, WINDOW, nk)
    o4 = attn_sample(q4, ks3, vs3, ck, cv, tbl_s, sink_s, ATTN_SAMPLE_BS)
    os_ = jnp.transpose(o4.reshape(nsb, N_KV, ts, GROUP, HEAD_DIM), (0, 2, 1, 3, 4)).reshape(nsb * ts, nq)
    x = proj_residual(os_, w_o, x_attn, TOKEN_TM)

    last_window = lambda a: jnp.stack([a[(b + 1) * seq - WINDOW:(b + 1) * seq] for b in range(batch)]
                                      ).reshape(batch, WINDOW, N_KV, HEAD_DIM)
    k_p, v_p = last_window(k), last_window(v)
    k_s = jnp.concatenate([cache_swa_k[0], ks.reshape(nsb, ts, N_KV, HEAD_DIM)], axis=1)[:, ts:]
    v_s = jnp.concatenate([cache_swa_v[0], vs.reshape(nsb, ts, N_KV, HEAD_DIM)], axis=1)[:, ts:]

    wr1, br1 = router_w(1)
    xp, xs = moe_layer((x,), row(rms_ffn_g[1]), wr1, br1, ew_in, ew_out, 1, split_rows=n_p)

    return (xp.reshape(batch, seq, d), xs.reshape(nsb, ts, d),
            conv_p[None], conv_s[None], k_p[None], v_p[None], k_s[None], v_s[None])
```

```python
import functools
import math

import numpy as np
import jax
import jax.numpy as jnp
from jax import lax
from jax.experimental import pallas as pl
from jax.experimental.pallas import tpu as pltpu

D_MODEL = 1024
DEPTH = 2
CONV_WIDTH = 31
PAST = CONV_WIDTH - 1
HEAD_DIM = 64
N_HEADS = 16
N_KV = 2
GROUP = 8
WINDOW = 128
N_BUCKETS = 32
MAX_DISTANCE = 128
N_GROUPS = 4
EPG = 8
N_EXPERTS = 32
D_EXPERT = 256
RMS_EPS = 1e-6
LN_EPS = 1e-5
NEG_INF = -1e30

F32 = jnp.float32
BF16 = jnp.bfloat16
LANES = 128
ROW_TILE = 8
MOE_TM = 512
MOE_TMG = 256
PACK_ROWS = ROW_TILE // 2
MOE_CHUNK = 17408
TOKEN_TM = 512
CONV_TT = 256
CONV_SAMPLE_BS = 32
ATTN_SAMPLE_BS = 16
V7X_VMEM_BYTES = 64 * 1024 * 1024
MOE_VMEM_LIMIT = V7X_VMEM_BYTES - 8 * 1024 * 1024
VMEM_LIMIT = V7X_VMEM_BYTES - 16 * 1024 * 1024


def _params(*sem):
    return pltpu.CompilerParams(dimension_semantics=sem, vmem_limit_bytes=VMEM_LIMIT)


def _rms(x, g):
    return x * lax.rsqrt(jnp.mean(x * x, axis=-1, keepdims=True) + RMS_EPS) * g


def _sigmoid(x):
    return 1.0 / (1.0 + jnp.exp(-x))


def _full(shape):
    return pl.BlockSpec(shape, lambda *_: (0,) * len(shape))


def _pack_bf16_pair(lo, hi):
    lo_bits = lax.bitcast_convert_type(lo.astype(BF16).astype(F32), jnp.uint32) >> 16
    hi_bits = lax.bitcast_convert_type(hi.astype(BF16).astype(F32), jnp.uint32) & jnp.uint32(0xFFFF0000)
    return lo_bits | hi_bits


def _unpack_bf16_pair(words):
    lo = lax.bitcast_convert_type(words << 16, F32)
    hi = lax.bitcast_convert_type(words & jnp.uint32(0xFFFF0000), F32)
    return lo, hi


def _glu_kernel(x_ref, g_ref, wa_ref, wg_ref, u_ref):
    h = _rms(x_ref[...], g_ref[...]).astype(BF16)
    a = jnp.dot(h, wa_ref[...], preferred_element_type=F32)
    gate = jnp.dot(h, wg_ref[...], preferred_element_type=F32)
    u_ref[...] = a * _sigmoid(gate)


def glu_proj(x, g, w_in, tm):
    n, d = x.shape
    c = w_in.shape[1] // 2
    return pl.pallas_call(
        _glu_kernel,
        grid=(n // tm,),
        in_specs=[
            pl.BlockSpec((tm, d), lambda i: (i, 0)),
            _full((1, d)),
            pl.BlockSpec((d, c), lambda i: (0, 0)),
            pl.BlockSpec((d, c), lambda i: (0, 1)),
        ],
        out_specs=pl.BlockSpec((tm, c), lambda i: (i, 0)),
        out_shape=jax.ShapeDtypeStruct((n, c), F32),
        compiler_params=_params("parallel"),
        name="glu_proj",
    )(x, g, w_in, w_in)


def _ln_silu_out(y, lng, lnb, wout_ref, x):
    mu = jnp.mean(y, axis=-1, keepdims=True)
    yc = y - mu
    z = yc * lax.rsqrt(jnp.mean(yc * yc, axis=-1, keepdims=True) + LN_EPS) * lng + lnb
    z = z * _sigmoid(z)
    return x + jnp.dot(z.astype(BF16), wout_ref[...], preferred_element_type=F32)


HALO = 32
CONV_RC = 64
CONV_CC = 128


def _conv_prompt_kernel(ucur_ref, uprev_ref, x_ref, dww_ref, dwb_ref, lng_ref, lnb_ref,
                        wout_ref, o_ref, up_ref, y_ref):
    t = pl.program_id(1)
    tt, c = ucur_ref.shape
    keep = (t > 0).astype(F32)
    up_ref[0:HALO, :] = uprev_ref[...] * keep
    up_ref[HALO:HALO + tt, :] = ucur_ref[...]
    up_ref[HALO + tt:, :] = jnp.zeros((ROW_TILE, c), F32)
    off = HALO - PAST
    for r0 in range(0, tt, CONV_RC):
        for c0 in range(0, c, CONV_CC):
            y = jnp.zeros((CONV_RC, CONV_CC), F32) + dwb_ref[:, c0:c0 + CONV_CC]
            for s in range(ROW_TILE):
                v = None
                for q in range((off + CONV_WIDTH - 1) // ROW_TILE + 1):
                    k = ROW_TILE * q + s - off
                    if k < 0 or k >= CONV_WIDTH:
                        continue
                    lo = r0 + ROW_TILE * q
                    term = (up_ref[lo:lo + CONV_RC + ROW_TILE, c0:c0 + CONV_CC]
                            * dww_ref[k:k + 1, c0:c0 + CONV_CC])
                    v = term if v is None else v + term
                y = y + v[s:s + CONV_RC]
            y_ref[r0:r0 + CONV_RC, c0:c0 + CONV_CC] = y
    o_ref[...] = _ln_silu_out(y_ref[...], lng_ref[...], lnb_ref[...], wout_ref, x_ref[...])


def conv_prompt(u, x, dww, dwb, lng, lnb, wout, batch, seq, tt):
    n, c = u.shape
    d = x.shape[1]
    nt = seq // tt
    hb = tt // HALO
    return pl.pallas_call(
        _conv_prompt_kernel,
        grid=(batch, nt),
        in_specs=[
            pl.BlockSpec((tt, c), lambda b, t: (b * nt + t, 0)),
            pl.BlockSpec((HALO, c), lambda b, t: (jnp.maximum((b * nt + t) * hb - 1, 0), 0)),
            pl.BlockSpec((tt, d), lambda b, t: (b * nt + t, 0)),
            _full((HALO, c)), _full((1, c)), _full((1, c)), _full((1, c)),
            _full((c, d)),
        ],
        out_specs=pl.BlockSpec((tt, d), lambda b, t: (b * nt + t, 0)),
        out_shape=jax.ShapeDtypeStruct((n, d), F32),
        scratch_shapes=[pltpu.VMEM((tt + HALO + ROW_TILE, c), F32), pltpu.VMEM((tt, c), F32)],
        compiler_params=_params("parallel", "parallel"),
        name="conv_prompt",
    )(u, u, x, dww, dwb, lng, lnb, wout)


CONV_SB = 4


def _conv_sample_kernel(u_ref, st_ref, x_ref, dww_ref, dwb_ref, lng_ref, lnb_ref,
                        wout_ref, o_ref, up_ref, y_ref):
    bs, ts, c = u_ref.shape
    up_ref[:, 0:PAST, :] = st_ref[...]
    up_ref[:, PAST:PAST + ts, :] = u_ref[...]
    for b0 in range(0, bs, CONV_SB):
        acc = jnp.zeros((CONV_SB, ts, c), F32) + dwb_ref[...][None]
        for k in range(CONV_WIDTH):
            acc = acc + up_ref[b0:b0 + CONV_SB, k:k + ts, :] * dww_ref[k:k + 1, :][None]
        y_ref[b0 * ts:(b0 + CONV_SB) * ts, :] = acc.reshape(CONV_SB * ts, c)
    o_ref[...] = _ln_silu_out(y_ref[...], lng_ref[...], lnb_ref[...], wout_ref, x_ref[...])


def conv_sample(u3, state, x, dww, dwb, lng, lnb, wout, bs):
    nb, ts, c = u3.shape
    d = x.shape[1]
    return pl.pallas_call(
        _conv_sample_kernel,
        grid=(nb // bs,),
        in_specs=[
            pl.BlockSpec((bs, ts, c), lambda i: (i, 0, 0)),
            pl.BlockSpec((bs, PAST, c), lambda i: (i, 0, 0)),
            pl.BlockSpec((bs * ts, d), lambda i: (i, 0)),
            _full((HALO, c)), _full((1, c)), _full((1, c)), _full((1, c)),
            _full((c, d)),
        ],
        out_specs=pl.BlockSpec((bs * ts, d), lambda i: (i, 0)),
        out_shape=jax.ShapeDtypeStruct((nb * ts, d), F32),
        scratch_shapes=[pltpu.VMEM((bs, PAST + ts, c), F32), pltpu.VMEM((bs * ts, c), F32)],
        compiler_params=_params("parallel"),
        name="conv_sample",
    )(u3, state, x, dww, dwb, lng, lnb, wout)


def _token_specs(xs, tm):
    d = xs[0].shape[1]
    if len(xs) == 1:
        return [pl.BlockSpec((tm, d), lambda i: (i, 0))]
    first = xs[0].shape[0] // tm
    return [pl.BlockSpec((tm, d), lambda i: (jnp.minimum(i, first - 1), 0)),
            pl.BlockSpec((tm, d), lambda i: (jnp.maximum(i - first, 0), 0))]


def _read_tokens(x_refs, first_steps):
    if len(x_refs) == 1:
        return x_refs[0][...]
    return jnp.where(pl.program_id(0) < first_steps, x_refs[0][...], x_refs[1][...])


def _router_kernel(tiles_per_chunk, n_src, first_steps, *refs):
    x_refs = refs[:n_src]
    g_ref, wr_ref, br_ref, tri_ref, hs_ref, route_ref, route_t_ref, cnt_ref, carry_ref = refs[n_src:]
    h = _rms(_read_tokens(x_refs, first_steps), g_ref[...])
    tm, d = h.shape
    for j in range(PACK_ROWS):
        hs_ref[pl.ds(j, tm, stride=PACK_ROWS), :] = _pack_bf16_pair(
            h[:, 2 * j * LANES:(2 * j + 1) * LANES], h[:, (2 * j + 1) * LANES:(2 * j + 2) * LANES])
    h_hi = h.astype(BF16)
    h_lo = (h - h_hi.astype(F32)).astype(BF16)
    both = jnp.dot(h_hi, wr_ref[...], preferred_element_type=F32)
    cross = jnp.dot(h_lo, wr_ref[:, :LANES], preferred_element_type=F32)
    logits = both[:, :LANES] + both[:, LANES:] + cross + br_ref[...]
    lane = lax.broadcasted_iota(jnp.int32, logits.shape, 1).astype(F32)
    big = jnp.float32(LANES)
    is_g = (lane >= N_EXPERTS) & (lane < N_EXPERTS + N_GROUPS)
    gl = jnp.where(is_g, logits, NEG_INF)
    gm = jnp.max(gl, axis=-1, keepdims=True)
    g_sel = jnp.min(jnp.where(gl == gm, lane, big), axis=-1, keepdims=True) - N_EXPERTS
    gate_g = 1.0 / jnp.sum(jnp.where(is_g, jnp.exp(gl - gm), 0.0), axis=-1, keepdims=True)
    lo = g_sel * EPG
    in_grp = (lane >= lo) & (lane < lo + EPG)
    el = jnp.where(in_grp, logits, NEG_INF)
    v1 = jnp.max(el, axis=-1, keepdims=True)
    i1 = jnp.min(jnp.where(el == v1, lane, big), axis=-1, keepdims=True)
    el2 = jnp.where(lane == i1, NEG_INF, el)
    v2 = jnp.max(el2, axis=-1, keepdims=True)
    i2 = jnp.min(jnp.where(el2 == v2, lane, big), axis=-1, keepdims=True)
    e2 = jnp.exp(v2 - v1)
    w1 = gate_g / (1.0 + e2)
    w2 = gate_g * e2 / (1.0 + e2)
    @pl.when(pl.program_id(0) % tiles_per_chunk == 0)
    def _():
        carry_ref[...] = jnp.zeros_like(carry_ref)

    hit1 = lane == i1
    hit2 = lane == i2
    onehot = jnp.where(hit1 | hit2, 1.0, 0.0)
    before = carry_ref[...] + jnp.dot(tri_ref[...], onehot.astype(BF16), preferred_element_type=F32)
    r1 = jnp.sum(jnp.where(hit1, before, 0.0), axis=-1, keepdims=True)
    r2 = jnp.sum(jnp.where(hit2, before, 0.0), axis=-1, keepdims=True)
    carry_ref[...] += jnp.sum(onehot, axis=0, keepdims=True)
    cnt_ref[...] = jnp.broadcast_to(carry_ref[...], cnt_ref.shape)
    route = jnp.where(lane == 0.0, i1, jnp.where(lane == 1.0, i2, jnp.where(
        lane == 2.0, w1, jnp.where(lane == 3.0, w2, jnp.where(
            lane == 4.0, r1, jnp.where(lane == 5.0, r2, 0.0))))))
    route_ref[...] = route
    route_t_ref[...] = jnp.transpose(route)[0:ROW_TILE, :]


def router(xs, g, wr, br, tm, chunk):
    n = sum(x.shape[0] for x in xs)
    d = xs[0].shape[1]
    tpc = chunk // tm
    tri = jnp.asarray(np.tril(np.ones((tm, tm), np.float32), -1), BF16)
    return pl.pallas_call(
        functools.partial(_router_kernel, tpc, len(xs), xs[0].shape[0] // tm),
        grid=(n // tm,),
        in_specs=_token_specs(xs, tm) + [_full((1, d)),
                  _full((d, 2 * LANES)), _full((1, LANES)), _full((tm, tm))],
        out_specs=[pl.BlockSpec((tm * PACK_ROWS, LANES), lambda i: (i, 0)),
                   pl.BlockSpec((tm, LANES), lambda i: (i, 0)),
                   pl.BlockSpec((ROW_TILE, tm), lambda i: (i, 0)),
                   pl.BlockSpec((ROW_TILE, LANES), lambda i: (i // tpc, 0))],
        out_shape=[jax.ShapeDtypeStruct((n * PACK_ROWS, LANES), jnp.uint32),
                   jax.ShapeDtypeStruct((n, LANES), F32),
                   jax.ShapeDtypeStruct((n // tm * ROW_TILE, tm), F32),
                   jax.ShapeDtypeStruct((n // chunk * ROW_TILE, LANES), F32)],
        scratch_shapes=[pltpu.VMEM((1, LANES), F32)],
        compiler_params=_params("arbitrary"),
        name="router",
    )(*xs, g, wr, br, tri)


def _gather_rows(idx_ref, src_hbm, dst, sem, n_rows):
    unroll = 8

    def body(c, carry):
        for u in range(unroll):
            m = c * unroll + u
            tok = idx_ref[m]
            pltpu.async_copy(
                src_hbm.at[pl.ds(pl.multiple_of(tok * ROW_TILE, ROW_TILE), ROW_TILE), :],
                dst.at[pl.ds(pl.multiple_of(m * ROW_TILE, ROW_TILE), ROW_TILE), :],
                sem, priority=u % 2)
        return carry

    lax.fori_loop(0, n_rows // unroll, body, 0)


def _wait_rows(src_hbm, dst, sem):
    pltpu.make_async_copy(src_hbm.at[pl.ds(0, dst.shape[0]), :], dst, sem).wait()


def _from_token_tiles(buf, start, n_rows, stride):
    return jnp.concatenate(
        [buf[pl.ds(start + j, n_rows, stride=stride), :] for j in range(ROW_TILE)], axis=-1)


def _experts_by_expert_kernel(tiles_per_chunk, tmg, t0_ref, nt_ref, pos_ref, fill_hbm, hs_hbm,
                              win_ref, wout_ref, ys_hbm, hsv, gbuf_a, gbuf_b, ystage_a, ystage_b,
                              src_ref, sems, winb, woutb):
    c = pl.program_id(0)
    e = pl.program_id(1)
    chunk = pos_ref.shape[0] // 2
    tile_rows = tmg * ROW_TILE
    scatter_unroll = 32
    gather_unroll = 8

    def tile(r):
        return pl.ds(pl.multiple_of(r * PACK_ROWS, PACK_ROWS), PACK_ROWS)

    def ys_tile(t):
        return ys_hbm.at[pl.ds(pl.multiple_of((c * tiles_per_chunk + t) * tile_rows, tile_rows),
                               tile_rows), :]

    @pl.when(e == 0)
    def _():
        load = pltpu.make_async_copy(
            hs_hbm.at[pl.ds(pl.multiple_of(c * chunk * PACK_ROWS, PACK_ROWS), chunk * PACK_ROWS), :],
            hsv.at[pl.ds(0, chunk * PACK_ROWS), :], sems.at[0])
        load.start()
        fill = pltpu.make_async_copy(fill_hbm, src_ref, sems.at[1])
        fill.start()
        hsv[pl.ds(chunk * PACK_ROWS, PACK_ROWS), :] = jnp.zeros((PACK_ROWS, LANES), jnp.uint32)
        fill.wait()

        def scatter(i, carry):
            first = i * scatter_unroll
            tok0 = first // (2 * MOE_TM) * MOE_TM + first % MOE_TM
            for u in range(scatter_unroll):
                src_ref[pos_ref[first + u]] = tok0 + u
            return carry

        lax.fori_loop(0, 2 * chunk // scatter_unroll, scatter, 0)
        load.wait()

        def gather(i, carry):
            for u in range(gather_unroll):
                m = i * gather_unroll + u
                gbuf_a[tile(m), :] = hsv[tile(src_ref[m]), :]
            return carry

        lax.fori_loop(0, tmg // gather_unroll, gather, 0)

    t0 = t0_ref[c * N_EXPERTS + e]
    nt = nt_ref[c * N_EXPERTS + e]

    @pl.when(nt > 0)
    def _():
        winb[...] = win_ref[0].astype(BF16)
        woutb[...] = wout_ref[0].astype(BF16)

    def step(t, cur, nxt, ystage, sem):
        base = jnp.minimum(t + 1, tiles_per_chunk - 1) * tmg
        for m in range(tmg):
            nxt[pl.ds(m * PACK_ROWS, PACK_ROWS), :] = hsv[tile(src_ref[base + m]), :]
        halves = []
        for j in range(PACK_ROWS):
            halves.extend(_unpack_bf16_pair(cur[pl.ds(j, tmg, stride=PACK_ROWS), :]))
        x = jnp.concatenate(halves, axis=-1).astype(BF16)
        hid = jnp.dot(x, winb[...], preferred_element_type=F32)
        a = hid[:, :D_EXPERT]
        u = hid[:, D_EXPERT:]
        act = (a * _sigmoid(a) * u).astype(BF16)
        y = jnp.dot(act, woutb[...], preferred_element_type=F32)

        @pl.when(t >= 2)
        def _():
            pltpu.make_async_copy(ystage, ys_tile(t), sem).wait()

        for k in range(ROW_TILE):
            ystage[pl.ds(k, tmg, stride=ROW_TILE), :] = y[:, k * LANES:(k + 1) * LANES]
        pltpu.make_async_copy(ystage, ys_tile(t), sem).start()

    def one_tile(k, carry):
        t = t0 + k

        @pl.when(t % 2 == 0)
        def _():
            step(t, gbuf_a, gbuf_b, ystage_a, sems.at[2])

        @pl.when(t % 2 == 1)
        def _():
            step(t, gbuf_b, gbuf_a, ystage_b, sems.at[3])

        return carry

    lax.fori_loop(0, nt, one_tile, 0)

    @pl.when(e == pl.num_programs(1) - 1)
    def _():
        total = t0 + nt

        @pl.when(total >= 1)
        def _():
            pltpu.make_async_copy(ystage_a, ys_tile(0), sems.at[2]).wait()

        @pl.when(total >= 2)
        def _():
            pltpu.make_async_copy(ystage_b, ys_tile(0), sems.at[3]).wait()

        ystage_a[...] = jnp.zeros_like(ystage_a)

        def zero_tile(t, carry):
            cp = pltpu.make_async_copy(ystage_a, ys_tile(t), sems.at[2])
            cp.start()
            cp.wait()
            return carry

        lax.fori_loop(total, tiles_per_chunk, zero_tile, 0)


def experts_by_expert(hs, pos_local, tile_start, n_tiles, w_in, w_out, layer, tmg, chunk, tiles_per_chunk):
    _, d, f2 = w_in.shape
    nc = hs.shape[0] // (chunk * PACK_ROWS)
    rows_per_chunk = tiles_per_chunk * tmg
    fill = jnp.full((rows_per_chunk,), chunk, jnp.int32)
    off = layer * N_EXPERTS
    grid_spec = pltpu.PrefetchScalarGridSpec(
        num_scalar_prefetch=2,
        grid=(nc, N_EXPERTS),
        in_specs=[
            pl.BlockSpec((2 * chunk,), lambda c, e, t0, nt: (c,), memory_space=pltpu.SMEM),
            pl.BlockSpec(memory_space=pl.ANY),
            pl.BlockSpec(memory_space=pl.ANY),
            pl.BlockSpec((1, d, f2), lambda c, e, t0, nt: (off + e, 0, 0)),
            pl.BlockSpec((1, f2 // 2, d), lambda c, e, t0, nt: (off + e, 0, 0)),
        ],
        out_specs=pl.BlockSpec(memory_space=pl.ANY),
        scratch_shapes=[
            pltpu.VMEM(((chunk + 1) * PACK_ROWS, LANES), jnp.uint32),
            pltpu.VMEM((tmg * PACK_ROWS, LANES), jnp.uint32),
            pltpu.VMEM((tmg * PACK_ROWS, LANES), jnp.uint32),
            pltpu.VMEM((tmg * ROW_TILE, LANES), F32),
            pltpu.VMEM((tmg * ROW_TILE, LANES), F32),
            pltpu.SMEM((rows_per_chunk,), jnp.int32),
            pltpu.SemaphoreType.DMA((4,)),
            pltpu.VMEM((d, f2), BF16),
            pltpu.VMEM((f2 // 2, d), BF16),
        ],
    )
    return pl.pallas_call(
        functools.partial(_experts_by_expert_kernel, tiles_per_chunk, tmg),
        grid_spec=grid_spec,
        out_shape=jax.ShapeDtypeStruct((nc * rows_per_chunk * ROW_TILE, LANES), F32),
        compiler_params=pltpu.CompilerParams(dimension_semantics=("arbitrary", "arbitrary"),
                                             vmem_limit_bytes=MOE_VMEM_LIMIT),
        name="moe_experts",
    )(tile_start, n_tiles, pos_local, fill, hs, w_in, w_out)


def _combine_kernel(split_steps, n_src, first_steps, pos_cur_ref, pos_nxt_ref, ys_hbm, route_ref, *rest):
    x_refs, o_refs, (ybuf_a, ybuf_b, sems) = rest[:n_src], rest[n_src:-3], rest[-3:]
    i = pl.program_id(0)
    last = i == pl.num_programs(0) - 1
    tm = route_ref.shape[0]

    @pl.when(i == 0)
    def _():
        _gather_rows(pos_cur_ref, ys_hbm, ybuf_a, sems.at[0], 2 * tm)

    def step(cur, cur_sem, nxt, nxt_sem):
        _gather_rows(pos_nxt_ref, ys_hbm, nxt, nxt_sem, 2 * tm)
        _wait_rows(ys_hbm, cur, cur_sem)
        route = route_ref[...]
        y1 = _from_token_tiles(cur, 0, tm, ROW_TILE)
        y2 = _from_token_tiles(cur, tm * ROW_TILE, tm, ROW_TILE)
        val = _read_tokens(x_refs, first_steps) + route[:, 2:3] * y1 + route[:, 3:4] * y2
        if split_steps is None:
            o_refs[0][...] = val
        else:
            @pl.when(i < split_steps)
            def _():
                o_refs[0][...] = val

            @pl.when(i >= split_steps)
            def _():
                o_refs[1][...] = val

        @pl.when(last)
        def _():
            _wait_rows(ys_hbm, nxt, nxt_sem)

    @pl.when(i % 2 == 0)
    def _():
        step(ybuf_a, sems.at[0], ybuf_b, sems.at[1])

    @pl.when(i % 2 == 1)
    def _():
        step(ybuf_b, sems.at[1], ybuf_a, sems.at[0])


def combine(ys, pos, route, xs, tm, split_rows=None):
    n = sum(x.shape[0] for x in xs)
    d = xs[0].shape[1]
    nsteps = n // tm
    if split_rows is None:
        split_steps = None
        out_specs = pl.BlockSpec((tm, d), lambda i: (i, 0))
        out_shape = jax.ShapeDtypeStruct((n, d), F32)
    else:
        split_steps = split_rows // tm
        out_specs = [pl.BlockSpec((tm, d), lambda i: (jnp.minimum(i, split_steps - 1), 0)),
                     pl.BlockSpec((tm, d), lambda i: (jnp.maximum(i - split_steps, 0), 0))]
        out_shape = [jax.ShapeDtypeStruct((split_rows, d), F32),
                     jax.ShapeDtypeStruct((n - split_rows, d), F32)]
    return pl.pallas_call(
        functools.partial(_combine_kernel, split_steps, len(xs), xs[0].shape[0] // tm),
        grid=(nsteps,),
        in_specs=[
            pl.BlockSpec((2 * tm,), lambda i: (i,), memory_space=pltpu.SMEM),
            pl.BlockSpec((2 * tm,), lambda i: (jnp.minimum(i + 1, nsteps - 1),),
                         memory_space=pltpu.SMEM),
            pl.BlockSpec(memory_space=pl.ANY),
            pl.BlockSpec((tm, LANES), lambda i: (i, 0)),
        ] + _token_specs(xs, tm),
        out_specs=out_specs,
        out_shape=out_shape,
        scratch_shapes=[pltpu.VMEM((2 * tm * ROW_TILE, LANES), F32),
                        pltpu.VMEM((2 * tm * ROW_TILE, LANES), F32),
                        pltpu.SemaphoreType.DMA((2,))],
        compiler_params=_params("arbitrary"),
        name="moe_combine",
    )(pos, pos, ys, route, *xs)


def moe_layer(xs, g, wr, br, w_in, w_out, layer, split_rows=None):
    n = sum(x.shape[0] for x in xs)
    nc = -(-n // MOE_CHUNK)
    chunk = n // nc
    assert chunk * nc == n and chunk % MOE_TM == 0
    tpc = (2 * chunk + N_EXPERTS * (MOE_TMG - 1)) // MOE_TMG
    hs, route, route_t, counts = router(xs, g, wr, br, MOE_TM, chunk)
    rt = route_t.reshape(nc, chunk // MOE_TM, ROW_TILE, MOE_TM)
    ids = rt[:, :, 0:2, :].astype(jnp.int32)
    ranks = rt[:, :, 4:6, :].astype(jnp.int32)
    cnt = counts.reshape(nc, ROW_TILE, LANES)[:, 0, :N_EXPERTS].astype(jnp.int32)
    padded = (cnt + MOE_TMG - 1) // MOE_TMG * MOE_TMG
    ends = jnp.cumsum(padded, axis=1)
    starts = ends - padded
    start_of = jnp.zeros_like(ids)
    for e in range(N_EXPERTS):
        start_of = jnp.where(ids == e, starts[:, e][:, None, None, None], start_of)
    pos_local = start_of + ranks
    ys = experts_by_expert(hs, pos_local.reshape(-1), (starts // MOE_TMG).reshape(-1),
                           (padded // MOE_TMG).reshape(-1), w_in, w_out, layer, MOE_TMG, chunk, tpc)
    pos = pos_local + (jnp.arange(nc, dtype=jnp.int32) * (tpc * MOE_TMG))[:, None, None, None]
    return combine(ys, pos.reshape(-1), route, xs, MOE_TM, split_rows)


def _qkv_kernel(x_ref, g_ref, w_ref, qg_ref, kg_ref, seg_ref, q_ref, k_ref, v_ref):
    h = _rms(x_ref[...], g_ref[...]).astype(BF16)
    qkv = jnp.dot(h, w_ref[...], preferred_element_type=F32)
    nq = N_HEADS * HEAD_DIM
    nk = N_KV * HEAD_DIM
    q = qkv[:, :nq]
    k = qkv[:, nq:nq + nk]
    v_ref[...] = qkv[:, nq + nk:]

    def seg_mean_sq(z, seg, split):
        zz = z * z
        hi = zz.astype(BF16)
        ms = jnp.dot(hi, seg, preferred_element_type=F32)
        if split:
            lo = (zz - hi.astype(F32)).astype(BF16)
            ms = ms + jnp.dot(lo, seg, preferred_element_type=F32)
        return ms

    seg = seg_ref[...]
    qn = q * lax.rsqrt(seg_mean_sq(q, seg, False) + RMS_EPS) * qg_ref[...]
    q_ref[...] = (qn * (HEAD_DIM ** -0.5)).astype(BF16)
    kn = k * lax.rsqrt(seg_mean_sq(k, seg[:nk, :nk], True) + RMS_EPS) * kg_ref[...]
    k_ref[...] = kn


def qkv_proj(x, g, w, qg, kg, seg, tm):
    n, d = x.shape
    nq = N_HEADS * HEAD_DIM
    nk = N_KV * HEAD_DIM
    return pl.pallas_call(
        _qkv_kernel,
        grid=(n // tm,),
        in_specs=[pl.BlockSpec((tm, d), lambda i: (i, 0)), _full((1, d)),
                  _full((d, nq + 2 * nk)), _full((1, nq)), _full((1, nk)), _full((nq, nq))],
        out_specs=[pl.BlockSpec((tm, nq), lambda i: (i, 0)),
                   pl.BlockSpec((tm, nk), lambda i: (i, 0)),
                   pl.BlockSpec((tm, nk), lambda i: (i, 0))],
        out_shape=[jax.ShapeDtypeStruct((n, nq), BF16),
                   jax.ShapeDtypeStruct((n, nk), F32),
                   jax.ShapeDtypeStruct((n, nk), F32)],
        compiler_params=_params("parallel"),
        name="qkv_proj",
    )(x, g, w, qg, kg, seg)


def _attn_prompt_kernel(q_ref, kc_ref, kp_ref, vc_ref, vp_ref, tbl_ref, sink_ref, x_ref,
                        wo_ref, o_ref, att_ref, s_ref, e_ref, inv_ref):
    n = pl.program_id(1)
    first = jnp.where(n == 0, NEG_INF, 0.0).astype(F32)
    nt = (((1,), (1,)), ((), ()))
    lo_half = lax.broadcasted_iota(jnp.int32, (WINDOW, LANES), 1) < HEAD_DIM
    kp, kc = kp_ref[...], kc_ref[...]
    keys = (kp.astype(BF16), kc.astype(BF16))
    keys_swapped = (pltpu.roll(kp, HEAD_DIM, axis=1).astype(BF16),
                    pltpu.roll(kc, HEAD_DIM, axis=1).astype(BF16))
    vt_prev = jnp.transpose(vp_ref[...]).astype(BF16)
    vt_cur = jnp.transpose(vc_ref[...]).astype(BF16)
    for h in range(N_HEADS):
        kh = h // GROUP
        odd = h % 2
        pair = q_ref[:, (h // 2) * LANES:(h // 2 + 1) * LANES]
        qm = jnp.where(lo_half if odd == 0 else ~lo_half, pair, jnp.zeros_like(pair))
        k_prev, k_cur = keys_swapped if (kh == 0) == (odd == 1) else keys
        s_ref[h, 0:WINDOW, :] = (lax.dot_general(k_prev, qm, nt, preferred_element_type=F32)
                                 + tbl_ref[h, 0:WINDOW, :] + first)
        s_ref[h, WINDOW:, :] = (lax.dot_general(k_cur, qm, nt, preferred_element_type=F32)
                                + tbl_ref[h, WINDOW:, :])
    for h in range(N_HEADS):
        s = s_ref[h]
        sink = sink_ref[h]
        m = jnp.maximum(jnp.max(s, axis=0, keepdims=True), sink)
        e = jnp.exp(s - m)
        e_ref[h] = e.astype(BF16)
        inv_ref[h] = 1.0 / (jnp.sum(e, axis=0, keepdims=True) + jnp.exp(sink - m))
    for h in range(N_HEADS):
        kh = h // GROUP
        dims = slice(kh * HEAD_DIM, (kh + 1) * HEAD_DIM)
        out_t = (jnp.dot(vt_prev[dims, :], e_ref[h, 0:WINDOW, :], preferred_element_type=F32)
                 + jnp.dot(vt_cur[dims, :], e_ref[h, WINDOW:, :], preferred_element_type=F32))
        att_ref[h * HEAD_DIM:(h + 1) * HEAD_DIM, :] = out_t * inv_ref[h]
    att = jnp.transpose(att_ref[...]).astype(BF16)
    o_ref[...] = x_ref[...] + jnp.dot(att, wo_ref[...], preferred_element_type=F32)


def attn_prompt(q, k, v, tbl, sinks, x, wo, batch, seq):
    n, d = x.shape
    nb = seq // WINDOW
    nk = N_KV * HEAD_DIM
    cur = lambda b, i: (b * nb + i, 0)
    prev = lambda b, i: (b * nb + jnp.maximum(i - 1, 0), 0)
    return pl.pallas_call(
        _attn_prompt_kernel,
        grid=(batch, nb),
        in_specs=[
            pl.BlockSpec((WINDOW, d), cur),
            pl.BlockSpec((WINDOW, nk), cur), pl.BlockSpec((WINDOW, nk), prev),
            pl.BlockSpec((WINDOW, nk), cur), pl.BlockSpec((WINDOW, nk), prev),
            _full((N_HEADS, 2 * WINDOW, WINDOW)),
            pl.BlockSpec(memory_space=pltpu.SMEM),
            pl.BlockSpec((WINDOW, d), cur),
            _full((d, d)),
        ],
        out_specs=pl.BlockSpec((WINDOW, d), cur),
        out_shape=jax.ShapeDtypeStruct((n, d), F32),
        input_output_aliases={7: 0},
        scratch_shapes=[pltpu.VMEM((N_HEADS * HEAD_DIM, WINDOW), F32),
                        pltpu.VMEM((N_HEADS, 2 * WINDOW, WINDOW), F32),
                        pltpu.VMEM((N_HEADS, 2 * WINDOW, WINDOW), BF16),
                        pltpu.VMEM((N_HEADS, 1, WINDOW), F32)],
        compiler_params=_params("parallel", "parallel"),
        name="attn_prompt",
    )(q, k, k, v, v, tbl, sinks, x, wo)


def _attn_sample_kernel(q_ref, kn_ref, vn_ref, ck_ref, cv_ref, tblc_ref, tbln_ref, sink_ref, o_ref,
                        sc_ref, sn_ref, ec_ref, en_ref, inv_ref):
    bs = q_ref.shape[0]
    nt = (((1,), (1,)), ((), ()))
    for b in range(bs):
        ck = ck_ref[b].astype(BF16)
        kn = kn_ref[b].astype(BF16)
        for kh in range(N_KV):
            lanes = slice(kh * HEAD_DIM, (kh + 1) * HEAD_DIM)
            qt = q_ref[b, kh]
            sc_ref[b, kh] = lax.dot_general(qt, ck[:, lanes], nt, preferred_element_type=F32) + tblc_ref[kh]
            sn_ref[b, kh] = lax.dot_general(qt, kn[:, lanes], nt, preferred_element_type=F32) + tbln_ref[kh]
    for b in range(bs):
        for kh in range(N_KV):
            sc = sc_ref[b, kh]
            sn = sn_ref[b, kh]
            sink = sink_ref[kh]
            m = jnp.maximum(jnp.maximum(jnp.max(sc, axis=-1, keepdims=True),
                                        jnp.max(sn, axis=-1, keepdims=True)), sink)
            ec = jnp.exp(sc - m)
            en = jnp.exp(sn - m)
            ec_ref[b, kh] = ec.astype(BF16)
            en_ref[b, kh] = en.astype(BF16)
            inv_ref[b, kh] = 1.0 / (jnp.sum(ec, axis=-1, keepdims=True)
                                    + jnp.sum(en, axis=-1, keepdims=True) + jnp.exp(sink - m))
    for b in range(bs):
        cv = cv_ref[b].astype(BF16)
        vn = vn_ref[b].astype(BF16)
        for kh in range(N_KV):
            lanes = slice(kh * HEAD_DIM, (kh + 1) * HEAD_DIM)
            out = (jnp.dot(ec_ref[b, kh], cv[:, lanes], preferred_element_type=F32)
                   + jnp.dot(en_ref[b, kh], vn[:, lanes], preferred_element_type=F32))
            o_ref[b, kh] = (out * inv_ref[b, kh]).astype(BF16)


def attn_sample(q4, kn, vn, ck, cv, tbl, sink, bs):
    nb = q4.shape[0]
    ts = kn.shape[1]
    nk = N_KV * HEAD_DIM
    tg = ts * GROUP
    return pl.pallas_call(
        _attn_sample_kernel,
        grid=(nb // bs,),
        in_specs=[
            pl.BlockSpec((bs, N_KV, tg, HEAD_DIM), lambda i: (i, 0, 0, 0)),
            pl.BlockSpec((bs, ts, nk), lambda i: (i, 0, 0)),
            pl.BlockSpec((bs, ts, nk), lambda i: (i, 0, 0)),
            pl.BlockSpec((bs, WINDOW, nk), lambda i: (i, 0, 0)),
            pl.BlockSpec((bs, WINDOW, nk), lambda i: (i, 0, 0)),
            _full((N_KV, tg, WINDOW)),
            _full((N_KV, tg, ts)),
            _full((N_KV, tg, 1)),
        ],
        out_specs=pl.BlockSpec((bs, N_KV, tg, HEAD_DIM), lambda i: (i, 0, 0, 0)),
        out_shape=jax.ShapeDtypeStruct((nb, N_KV, tg, HEAD_DIM), BF16),
        scratch_shapes=[pltpu.VMEM((bs, N_KV, tg, WINDOW), F32),
                        pltpu.VMEM((bs, N_KV, tg, ts), F32),
                        pltpu.VMEM((bs, N_KV, tg, WINDOW), BF16),
                        pltpu.VMEM((bs, N_KV, tg, ts), BF16),
                        pltpu.VMEM((bs, N_KV, tg, 1), F32)],
        compiler_params=_params("parallel"),
        name="attn_sample",
    )(q4, kn, vn, ck, cv, tbl[:, :, :WINDOW], tbl[:, :, WINDOW:], sink)


def _proj_res_kernel(a_ref, w_ref, x_ref, o_ref):
    o_ref[...] = x_ref[...] + jnp.dot(a_ref[...], w_ref[...], preferred_element_type=F32)


def proj_residual(a, w, x, tm):
    n, d = x.shape
    na, kdim = a.shape
    off = (n - na) // tm
    return pl.pallas_call(
        _proj_res_kernel,
        grid=(na // tm,),
        in_specs=[pl.BlockSpec((tm, kdim), lambda i: (i, 0)), _full((kdim, d)),
                  pl.BlockSpec((tm, d), lambda i: (off + i, 0))],
        out_specs=pl.BlockSpec((tm, d), lambda i: (off + i, 0)),
        out_shape=jax.ShapeDtypeStruct((n, d), F32),
        input_output_aliases={2: 0},
        compiler_params=_params("parallel"),
        name="proj_residual",
    )(a, w, x)


def _t5_bucket_np(dist):
    n = np.maximum(dist, 0)
    max_exact = N_BUCKETS // 2
    large = max_exact + (np.log(np.maximum(n, 1).astype(np.float32) / max_exact)
                         / math.log(MAX_DISTANCE / max_exact) * (N_BUCKETS - max_exact)).astype(np.int32)
    large = np.minimum(large, N_BUCKETS - 1)
    return np.where(n < max_exact, n, large)


def _bias_table(rel_bias, dist):
    valid = (dist >= 0) & (dist <= WINDOW)
    onehot = (np.asarray(_t5_bucket_np(dist))[..., None] == np.arange(N_BUCKETS)).astype(np.float32)
    b = jnp.einsum("qkb,bh->hqk", jnp.asarray(onehot), rel_bias.astype(F32),
                   precision=lax.Precision.HIGHEST)
    return jnp.where(jnp.asarray(valid)[None], b, NEG_INF)


def kernel(x_prompt, x_sample, state_conv, cache_swa_k, cache_swa_v, rms_mix_g, rms_ffn_g, conv_w_in, conv_dw_w, conv_dw_b, conv_ln_g, conv_ln_b, conv_w_out, attn_w_qkv, attn_q_norm_g, attn_k_norm_g, attn_sinks, attn_w_o, rel_bias, router_group_w, router_group_b, router_expert_w, router_expert_b, expert_w_in, expert_w_out):
    batch, seq, d = x_prompt.shape
    nsb, ts, _ = x_sample.shape
    xp = x_prompt.reshape(batch * seq, d)
    xs = x_sample.reshape(nsb * ts, d)
    row = lambda a: a.reshape(1, -1).astype(F32)

    def router_w(i):
        we = jnp.transpose(router_expert_w[i], (1, 0, 2)).reshape(d, N_EXPERTS)
        wr = jnp.concatenate([we, router_group_w[i]], axis=1)
        wr = jnp.pad(wr, ((0, 0), (0, LANES - wr.shape[1])))
        br = jnp.concatenate([router_expert_b[i].reshape(-1), router_group_b[i]])
        br = jnp.pad(br, (0, LANES - br.shape[0])).reshape(1, LANES)
        wr = wr.astype(F32)
        w_hi = wr.astype(BF16)
        w_lo = (wr - w_hi.astype(F32)).astype(BF16)
        return jnp.concatenate([w_hi, w_lo], axis=1), br.astype(F32)

    g0 = row(rms_mix_g[0])
    w_in = conv_w_in[0].astype(BF16)
    dww = jnp.pad(conv_dw_w[0].astype(F32), ((0, HALO - CONV_WIDTH), (0, 0)))
    dwb, lng, lnb = row(conv_dw_b[0]), row(conv_ln_g[0]), row(conv_ln_b[0])
    w_out = conv_w_out[0].astype(BF16)
    up = glu_proj(xp, g0, w_in, TOKEN_TM)
    us = glu_proj(xs, g0, w_in, TOKEN_TM)
    n_p = batch * seq
    xp = conv_prompt(up, xp, dww, dwb, lng, lnb, w_out, batch, seq, CONV_TT)
    us3 = us.reshape(nsb, ts, -1)
    xs = conv_sample(us3, state_conv[0], xs, dww, dwb, lng, lnb, w_out, CONV_SAMPLE_BS)
    conv_p = up.reshape(batch, seq, -1)[:, seq - PAST:]
    conv_s = jnp.concatenate([state_conv[0], us3], axis=1)[:, ts:]

    wr0, br0 = router_w(0)
    ew_in = expert_w_in.reshape((DEPTH * N_EXPERTS,) + expert_w_in.shape[2:])
    ew_out = expert_w_out.reshape((DEPTH * N_EXPERTS,) + expert_w_out.shape[2:])
    x = moe_layer((xp, xs), row(rms_ffn_g[0]), wr0, br0, ew_in, ew_out, 0)

    g1 = row(rms_mix_g[1])
    w_qkv = attn_w_qkv[0].astype(BF16)
    qg = jnp.tile(attn_q_norm_g[0].astype(F32), N_HEADS).reshape(1, -1)
    kg = jnp.tile(attn_k_norm_g[0].astype(F32), N_KV).reshape(1, -1)
    nq = N_HEADS * HEAD_DIM
    seg = jnp.asarray(np.kron(np.eye(N_HEADS), np.ones((HEAD_DIM, HEAD_DIM))) / HEAD_DIM, BF16)
    w_o = attn_w_o[0].astype(BF16)
    sinks = attn_sinks[0].astype(F32)

    q, k, v = qkv_proj(x, g1, w_qkv, qg, kg, seg, TOKEN_TM)
    qs, ks, vs = q[n_p:], k[n_p:], v[n_p:]

    q_off = np.arange(WINDOW)[:, None]
    dist_p = q_off + WINDOW - np.arange(2 * WINDOW)[None, :]
    tbl_p = jnp.transpose(_bias_table(rel_bias, dist_p), (0, 2, 1))
    x_attn = attn_prompt(q, k, v, tbl_p, sinks, x, w_o, batch, seq)

    kt = WINDOW + ts
    dist_s = np.arange(ts)[:, None] + WINDOW - np.arange(kt)[None, :]
    tbl_s = _bias_table(rel_bias, dist_s)
    tbl_s = jnp.transpose(tbl_s.reshape(N_KV, GROUP, ts, kt), (0, 2, 1, 3)).reshape(N_KV, ts * GROUP, kt)
    sink_s = jnp.tile(sinks.reshape(N_KV, 1, GROUP), (1, ts, 1)).reshape(N_KV, ts * GROUP, 1)
    q4 = jnp.transpose(qs.reshape(nsb, ts, N_KV, GROUP, HEAD_DIM), (0, 2, 1, 3, 4))
    q4 = q4.reshape(nsb, N_KV, ts * GROUP, HEAD_DIM)
    nk = N_KV * HEAD_DIM
    ks3, vs3 = ks.reshape(nsb, ts, nk), vs.reshape(nsb, ts, nk)
    ck = cache_swa_k[0].reshape(nsb, WINDOW, nk)
    cv = cache_swa_v[0].reshape(nsb, WINDOW, nk)
    o4 = attn_sample(q4, ks3, vs3, ck, cv, tbl_s, sink_s, ATTN_SAMPLE_BS)
    os_ = jnp.transpose(o4.reshape(nsb, N_KV, ts, GROUP, HEAD_DIM), (0, 2, 1, 3, 4)).reshape(nsb * ts, nq)
    x = proj_residual(os_, w_o, x_attn, TOKEN_TM)

    last_window = lambda a: jnp.stack([a[(b + 1) * seq - WINDOW:(b + 1) * seq] for b in range(batch)]
                                      ).reshape(batch, WINDOW, N_KV, HEAD_DIM)
    k_p, v_p = last_window(k), last_window(v)
    k_s = jnp.concatenate([cache_swa_k[0], ks.reshape(nsb, ts, N_KV, HEAD_DIM)], axis=1)[:, ts:]
    v_s = jnp.concatenate([cache_swa_v[0], vs.reshape(nsb, ts, N_KV, HEAD_DIM)], axis=1)[:, ts:]

    wr1, br1 = router_w(1)
    xp, xs = moe_layer((x,), row(rms_ffn_g[1]), wr1, br1, ew_in, ew_out, 1, split_rows=n_p)

    return (xp.reshape(batch, seq, d), xs.reshape(nsb, ts, d),
            conv_p[None], conv_s[None], k_p[None], v_p[None], k_s[None], v_s[None])
```

```python
import functools
import math

import numpy as np
import jax
import jax.numpy as jnp
from jax import lax
from jax.experimental import pallas as pl
from jax.experimental.pallas import tpu as pltpu

D_MODEL = 1024
DEPTH = 2
CONV_WIDTH = 31
PAST = CONV_WIDTH - 1
HEAD_DIM = 64
N_HEADS = 16
N_KV = 2
GROUP = 8
WINDOW = 128
N_BUCKETS = 32
MAX_DISTANCE = 128
N_GROUPS = 4
EPG = 8
N_EXPERTS = 32
D_EXPERT = 256
RMS_EPS = 1e-6
LN_EPS = 1e-5
NEG_INF = -1e30

F32 = jnp.float32
BF16 = jnp.bfloat16
LANES = 128
ROW_TILE = 8
MOE_TM = 512
MOE_TMG = 256
PACK_ROWS = ROW_TILE // 2
MOE_CHUNK = 17408
TOKEN_TM = 512
CONV_TT = 256
CONV_SAMPLE_BS = 32
ATTN_SAMPLE_BS = 16
V7X_VMEM_BYTES = 64 * 1024 * 1024
MOE_VMEM_LIMIT = V7X_VMEM_BYTES - 8 * 1024 * 1024
VMEM_LIMIT = V7X_VMEM_BYTES - 16 * 1024 * 1024


def _params(*sem):
    return pltpu.CompilerParams(dimension_semantics=sem, vmem_limit_bytes=VMEM_LIMIT)


def _rms(x, g):
    return x * lax.rsqrt(jnp.mean(x * x, axis=-1, keepdims=True) + RMS_EPS) * g


def _sigmoid(x):
    return 1.0 / (1.0 + jnp.exp(-x))


def _full(shape):
    return pl.BlockSpec(shape, lambda *_: (0,) * len(shape))


def _pack_bf16_pair(lo, hi):
    lo_bits = lax.bitcast_convert_type(lo.astype(BF16).astype(F32), jnp.uint32) >> 16
    hi_bits = lax.bitcast_convert_type(hi.astype(BF16).astype(F32), jnp.uint32) & jnp.uint32(0xFFFF0000)
    return lo_bits | hi_bits


def _unpack_bf16_pair(words):
    lo = lax.bitcast_convert_type(words << 16, F32)
    hi = lax.bitcast_convert_type(words & jnp.uint32(0xFFFF0000), F32)
    return lo, hi


def _glu_kernel(x_ref, g_ref, wa_ref, wg_ref, u_ref):
    h = _rms(x_ref[...], g_ref[...]).astype(BF16)
    a = jnp.dot(h, wa_ref[...], preferred_element_type=F32)
    gate = jnp.dot(h, wg_ref[...], preferred_element_type=F32)
    u_ref[...] = a * _sigmoid(gate)


def glu_proj(x, g, w_in, tm):
    n, d = x.shape
    c = w_in.shape[1] // 2
    return pl.pallas_call(
        _glu_kernel,
        grid=(n // tm,),
        in_specs=[
            pl.BlockSpec((tm, d), lambda i: (i, 0)),
            _full((1, d)),
            pl.BlockSpec((d, c), lambda i: (0, 0)),
            pl.BlockSpec((d, c), lambda i: (0, 1)),
        ],
        out_specs=pl.BlockSpec((tm, c), lambda i: (i, 0)),
        out_shape=jax.ShapeDtypeStruct((n, c), F32),
        compiler_params=_params("parallel"),
        name="glu_proj",
    )(x, g, w_in, w_in)


def _ln_silu_out(y, lng, lnb, wout_ref, x):
    mu = jnp.mean(y, axis=-1, keepdims=True)
    yc = y - mu
    z = yc * lax.rsqrt(jnp.mean(yc * yc, axis=-1, keepdims=True) + LN_EPS) * lng + lnb
    z = z * _sigmoid(z)
    return x + jnp.dot(z.astype(BF16), wout_ref[...], preferred_element_type=F32)


HALO = 32
CONV_RC = 64
CONV_CC = 128


def _conv_prompt_kernel(ucur_ref, uprev_ref, x_ref, dww_ref, dwb_ref, lng_ref, lnb_ref,
                        wout_ref, o_ref, up_ref, y_ref):
    t = pl.program_id(1)
    tt, c = ucur_ref.shape
    keep = (t > 0).astype(F32)
    up_ref[0:HALO, :] = uprev_ref[...] * keep
    up_ref[HALO:HALO + tt, :] = ucur_ref[...]
    up_ref[HALO + tt:, :] = jnp.zeros((ROW_TILE, c), F32)
    off = HALO - PAST
    for r0 in range(0, tt, CONV_RC):
        for c0 in range(0, c, CONV_CC):
            y = jnp.zeros((CONV_RC, CONV_CC), F32) + dwb_ref[:, c0:c0 + CONV_CC]
            for s in range(ROW_TILE):
                v = None
                for q in range((off + CONV_WIDTH - 1) // ROW_TILE + 1):
                    k = ROW_TILE * q + s - off
                    if k < 0 or k >= CONV_WIDTH:
                        continue
                    lo = r0 + ROW_TILE * q
                    term = (up_ref[lo:lo + CONV_RC + ROW_TILE, c0:c0 + CONV_CC]
                            * dww_ref[k:k + 1, c0:c0 + CONV_CC])
                    v = term if v is None else v + term
                y = y + v[s:s + CONV_RC]
            y_ref[r0:r0 + CONV_RC, c0:c0 + CONV_CC] = y
    o_ref[...] = _ln_silu_out(y_ref[...], lng_ref[...], lnb_ref[...], wout_ref, x_ref[...])


def conv_prompt(u, x, dww, dwb, lng, lnb, wout, batch, seq, tt):
    n, c = u.shape
    d = x.shape[1]
    nt = seq // tt
    hb = tt // HALO
    return pl.pallas_call(
        _conv_prompt_kernel,
        grid=(batch, nt),
        in_specs=[
            pl.BlockSpec((tt, c), lambda b, t: (b * nt + t, 0)),
            pl.BlockSpec((HALO, c), lambda b, t: (jnp.maximum((b * nt + t) * hb - 1, 0), 0)),
            pl.BlockSpec((tt, d), lambda b, t: (b * nt + t, 0)),
            _full((HALO, c)), _full((1, c)), _full((1, c)), _full((1, c)),
            _full((c, d)),
        ],
        out_specs=pl.BlockSpec((tt, d), lambda b, t: (b * nt + t, 0)),
        out_shape=jax.ShapeDtypeStruct((n, d), F32),
        scratch_shapes=[pltpu.VMEM((tt + HALO + ROW_TILE, c), F32), pltpu.VMEM((tt, c), F32)],
        compiler_params=_params("parallel", "parallel"),
        name="conv_prompt",
    )(u, u, x, dww, dwb, lng, lnb, wout)


CONV_SB = 4


def _conv_sample_kernel(u_ref, st_ref, x_ref, dww_ref, dwb_ref, lng_ref, lnb_ref,
                        wout_ref, o_ref, up_ref, y_ref):
    bs, ts, c = u_ref.shape
    up_ref[:, 0:PAST, :] = st_ref[...]
    up_ref[:, PAST:PAST + ts, :] = u_ref[...]
    for b0 in range(0, bs, CONV_SB):
        acc = jnp.zeros((CONV_SB, ts, c), F32) + dwb_ref[...][None]
        for k in range(CONV_WIDTH):
            acc = acc + up_ref[b0:b0 + CONV_SB, k:k + ts, :] * dww_ref[k:k + 1, :][None]
        y_ref[b0 * ts:(b0 + CONV_SB) * ts, :] = acc.reshape(CONV_SB * ts, c)
    o_ref[...] = _ln_silu_out(y_ref[...], lng_ref[...], lnb_ref[...], wout_ref, x_ref[...])


def conv_sample(u3, state, x, dww, dwb, lng, lnb, wout, bs):
    nb, ts, c = u3.shape
    d = x.shape[1]
    return pl.pallas_call(
        _conv_sample_kernel,
        grid=(nb // bs,),
        in_specs=[
            pl.BlockSpec((bs, ts, c), lambda i: (i, 0, 0)),
            pl.BlockSpec((bs, PAST, c), lambda i: (i, 0, 0)),
            pl.BlockSpec((bs * ts, d), lambda i: (i, 0)),
            _full((HALO, c)), _full((1, c)), _full((1, c)), _full((1, c)),
            _full((c, d)),
        ],
        out_specs=pl.BlockSpec((bs * ts, d), lambda i: (i, 0)),
        out_shape=jax.ShapeDtypeStruct((nb * ts, d), F32),
        scratch_shapes=[pltpu.VMEM((bs, PAST + ts, c), F32), pltpu.VMEM((bs * ts, c), F32)],
        compiler_params=_params("parallel"),
        name="conv_sample",
    )(u3, state, x, dww, dwb, lng, lnb, wout)


def _token_specs(xs, tm):
    d = xs[0].shape[1]
    if len(xs) == 1:
        return [pl.BlockSpec((tm, d), lambda i: (i, 0))]
    first = xs[0].shape[0] // tm
    return [pl.BlockSpec((tm, d), lambda i: (jnp.minimum(i, first - 1), 0)),
            pl.BlockSpec((tm, d), lambda i: (jnp.maximum(i - first, 0), 0))]


def _read_tokens(x_refs, first_steps):
    if len(x_refs) == 1:
        return x_refs[0][...]
    return jnp.where(pl.program_id(0) < first_steps, x_refs[0][...], x_refs[1][...])


def _router_kernel(tiles_per_chunk, n_src, first_steps, *refs):
    x_refs = refs[:n_src]
    g_ref, wr_ref, br_ref, tri_ref, hs_ref, route_ref, route_t_ref, cnt_ref, carry_ref = refs[n_src:]
    h = _rms(_read_tokens(x_refs, first_steps), g_ref[...])
    tm, d = h.shape
    for j in range(PACK_ROWS):
        hs_ref[pl.ds(j, tm, stride=PACK_ROWS), :] = _pack_bf16_pair(
            h[:, 2 * j * LANES:(2 * j + 1) * LANES], h[:, (2 * j + 1) * LANES:(2 * j + 2) * LANES])
    h_hi = h.astype(BF16)
    h_lo = (h - h_hi.astype(F32)).astype(BF16)
    both = jnp.dot(h_hi, wr_ref[...], preferred_element_type=F32)
    cross = jnp.dot(h_lo, wr_ref[:, :LANES], preferred_element_type=F32)
    logits = both[:, :LANES] + both[:, LANES:] + cross + br_ref[...]
    lane = lax.broadcasted_iota(jnp.int32, logits.shape, 1).astype(F32)
    big = jnp.float32(LANES)
    is_g = (lane >= N_EXPERTS) & (lane < N_EXPERTS + N_GROUPS)
    gl = jnp.where(is_g, logits, NEG_INF)
    gm = jnp.max(gl, axis=-1, keepdims=True)
    g_sel = jnp.min(jnp.where(gl == gm, lane, big), axis=-1, keepdims=True) - N_EXPERTS
    gate_g = 1.0 / jnp.sum(jnp.where(is_g, jnp.exp(gl - gm), 0.0), axis=-1, keepdims=True)
    lo = g_sel * EPG
    in_grp = (lane >= lo) & (lane < lo + EPG)
    el = jnp.where(in_grp, logits, NEG_INF)
    v1 = jnp.max(el, axis=-1, keepdims=True)
    i1 = jnp.min(jnp.where(el == v1, lane, big), axis=-1, keepdims=True)
    el2 = jnp.where(lane == i1, NEG_INF, el)
    v2 = jnp.max(el2, axis=-1, keepdims=True)
    i2 = jnp.min(jnp.where(el2 == v2, lane, big), axis=-1, keepdims=True)
    e2 = jnp.exp(v2 - v1)
    w1 = gate_g / (1.0 + e2)
    w2 = gate_g * e2 / (1.0 + e2)
    @pl.when(pl.program_id(0) % tiles_per_chunk == 0)
    def _():
        carry_ref[...] = jnp.zeros_like(carry_ref)

    hit1 = lane == i1
    hit2 = lane == i2
    onehot = jnp.where(hit1 | hit2, 1.0, 0.0)
    before = carry_ref[...] + jnp.dot(tri_ref[...], onehot.astype(BF16), preferred_element_type=F32)
    r1 = jnp.sum(jnp.where(hit1, before, 0.0), axis=-1, keepdims=True)
    r2 = jnp.sum(jnp.where(hit2, before, 0.0), axis=-1, keepdims=True)
    carry_ref[...] += jnp.sum(onehot, axis=0, keepdims=True)
    cnt_ref[...] = jnp.broadcast_to(carry_ref[...], cnt_ref.shape)
    route = jnp.where(lane == 0.0, i1, jnp.where(lane == 1.0, i2, jnp.where(
        lane == 2.0, w1, jnp.where(lane == 3.0, w2, jnp.where(
            lane == 4.0, r1, jnp.where(lane == 5.0, r2, 0.0))))))
    route_ref[...] = route
    route_t_ref[...] = jnp.transpose(route)[0:ROW_TILE, :]


def router(xs, g, wr, br, tm, chunk):
    n = sum(x.shape[0] for x in xs)
    d = xs[0].shape[1]
    tpc = chunk // tm
    tri = jnp.asarray(np.tril(np.ones((tm, tm), np.float32), -1), BF16)
    return pl.pallas_call(
        functools.partial(_router_kernel, tpc, len(xs), xs[0].shape[0] // tm),
        grid=(n // tm,),
        in_specs=_token_specs(xs, tm) + [_full((1, d)),
                  _full((d, 2 * LANES)), _full((1, LANES)), _full((tm, tm))],
        out_specs=[pl.BlockSpec((tm * PACK_ROWS, LANES), lambda i: (i, 0)),
                   pl.BlockSpec((tm, LANES), lambda i: (i, 0)),
                   pl.BlockSpec((ROW_TILE, tm), lambda i: (i, 0)),
                   pl.BlockSpec((ROW_TILE, LANES), lambda i: (i // tpc, 0))],
        out_shape=[jax.ShapeDtypeStruct((n * PACK_ROWS, LANES), jnp.uint32),
                   jax.ShapeDtypeStruct((n, LANES), F32),
                   jax.ShapeDtypeStruct((n // tm * ROW_TILE, tm), F32),
                   jax.ShapeDtypeStruct((n // chunk * ROW_TILE, LANES), F32)],
        scratch_shapes=[pltpu.VMEM((1, LANES), F32)],
        compiler_params=_params("arbitrary"),
        name="router",
    )(*xs, g, wr, br, tri)


def _gather_rows(idx_ref, src_hbm, dst, sem, n_rows):
    unroll = 8

    def body(c, carry):
        for u in range(unroll):
            m = c * unroll + u
            tok = idx_ref[m]
            pltpu.async_copy(
                src_hbm.at[pl.ds(pl.multiple_of(tok * ROW_TILE, ROW_TILE), ROW_TILE), :],
                dst.at[pl.ds(pl.multiple_of(m * ROW_TILE, ROW_TILE), ROW_TILE), :],
                sem, priority=u % 2)
        return carry

    lax.fori_loop(0, n_rows // unroll, body, 0)


def _wait_rows(src_hbm, dst, sem):
    pltpu.make_async_copy(src_hbm.at[pl.ds(0, dst.shape[0]), :], dst, sem).wait()


def _from_token_tiles(buf, start, n_rows, stride):
    return jnp.concatenate(
        [buf[pl.ds(start + j, n_rows, stride=stride), :] for j in range(ROW_TILE)], axis=-1)


def _experts_by_expert_kernel(tiles_per_chunk, tmg, t0_ref, nt_ref, pos_ref, fill_hbm, hs_hbm,
                              win_ref, wout_ref, ys_hbm, hsv, gbuf_a, gbuf_b, ystage_a, ystage_b,
                              src_ref, sems, winb, woutb):
    c = pl.program_id(0)
    e = pl.program_id(1)
    chunk = pos_ref.shape[0] // 2
    tile_rows = tmg * ROW_TILE
    scatter_unroll = 32
    gather_unroll = 8

    def tile(r):
        return pl.ds(pl.multiple_of(r * PACK_ROWS, PACK_ROWS), PACK_ROWS)

    def ys_tile(t):
        return ys_hbm.at[pl.ds(pl.multiple_of((c * tiles_per_chunk + t) * tile_rows, tile_rows),
                               tile_rows), :]

    @pl.when(e == 0)
    def _():
        load = pltpu.make_async_copy(
            hs_hbm.at[pl.ds(pl.multiple_of(c * chunk * PACK_ROWS, PACK_ROWS), chunk * PACK_ROWS), :],
            hsv.at[pl.ds(0, chunk * PACK_ROWS), :], sems.at[0])
        load.start()
        fill = pltpu.make_async_copy(fill_hbm, src_ref, sems.at[1])
        fill.start()
        hsv[pl.ds(chunk * PACK_ROWS, PACK_ROWS), :] = jnp.zeros((PACK_ROWS, LANES), jnp.uint32)
        fill.wait()

        def scatter(i, carry):
            first = i * scatter_unroll
            tok0 = first // (2 * MOE_TM) * MOE_TM + first % MOE_TM
            for u in range(scatter_unroll):
                src_ref[pos_ref[first + u]] = tok0 + u
            return carry

        lax.fori_loop(0, 2 * chunk // scatter_unroll, scatter, 0)
        load.wait()

        def gather(i, carry):
            for u in range(gather_unroll):
                m = i * gather_unroll + u
                gbuf_a[tile(m), :] = hsv[tile(src_ref[m]), :]
            return carry

        lax.fori_loop(0, tmg // gather_unroll, gather, 0)

    t0 = t0_ref[c * N_EXPERTS + e]
    nt = nt_ref[c * N_EXPERTS + e]

    @pl.when(nt > 0)
    def _():
        winb[...] = win_ref[0].astype(BF16)
        woutb[...] = wout_ref[0].astype(BF16)

    def step(t, cur, nxt, ystage, sem):
        @pl.when(t >= 2)
        def _():
            pltpu.make_async_copy(ystage, ys_tile(t), sem).wait()

        base = jnp.minimum(t + 1, tiles_per_chunk - 1) * tmg
        for m in range(tmg):
            nxt[pl.ds(m * PACK_ROWS, PACK_ROWS), :] = hsv[tile(src_ref[base + m]), :]
        halves = []
        for j in range(PACK_ROWS):
            halves.extend(_unpack_bf16_pair(cur[pl.ds(j, tmg, stride=PACK_ROWS), :]))
        x = jnp.concatenate(halves, axis=-1).astype(BF16)
        hid = jnp.dot(x, winb[...], preferred_element_type=F32)
        a = hid[:, :D_EXPERT]
        u = hid[:, D_EXPERT:]
        act = (a * _sigmoid(a) * u).astype(BF16)
        y = jnp.dot(act, woutb[...], preferred_element_type=F32)
        for k in range(ROW_TILE):
            ystage[pl.ds(k, tmg, stride=ROW_TILE), :] = y[:, k * LANES:(k + 1) * LANES]
        pltpu.make_async_copy(ystage, ys_tile(t), sem).start()

    def one_tile(k, carry):
        t = t0 + k

        @pl.when(t % 2 == 0)
        def _():
            step(t, gbuf_a, gbuf_b, ystage_a, sems.at[2])

        @pl.when(t % 2 == 1)
        def _():
            step(t, gbuf_b, gbuf_a, ystage_b, sems.at[3])

        return carry

    lax.fori_loop(0, nt, one_tile, 0)

    @pl.when(e == pl.num_programs(1) - 1)
    def _():
        total = t0 + nt

        @pl.when(total >= 1)
        def _():
            pltpu.make_async_copy(ystage_a, ys_tile(0), sems.at[2]).wait()

        @pl.when(total >= 2)
        def _():
            pltpu.make_async_copy(ystage_b, ys_tile(0), sems.at[3]).wait()

        ystage_a[...] = jnp.zeros_like(ystage_a)

        def zero_tile(t, carry):
            cp = pltpu.make_async_copy(ystage_a, ys_tile(t), sems.at[2])
            cp.start()
            cp.wait()
            return carry

        lax.fori_loop(total, tiles_per_chunk, zero_tile, 0)


def experts_by_expert(hs, pos_local, tile_start, n_tiles, w_in, w_out, layer, tmg, chunk, tiles_per_chunk):
    _, d, f2 = w_in.shape
    nc = hs.shape[0] // (chunk * PACK_ROWS)
    rows_per_chunk = tiles_per_chunk * tmg
    fill = jnp.full((rows_per_chunk,), chunk, jnp.int32)
    off = layer * N_EXPERTS
    grid_spec = pltpu.PrefetchScalarGridSpec(
        num_scalar_prefetch=2,
        grid=(nc, N_EXPERTS),
        in_specs=[
            pl.BlockSpec((2 * chunk,), lambda c, e, t0, nt: (c,), memory_space=pltpu.SMEM),
            pl.BlockSpec(memory_space=pl.ANY),
            pl.BlockSpec(memory_space=pl.ANY),
            pl.BlockSpec((1, d, f2), lambda c, e, t0, nt: (off + e, 0, 0)),
            pl.BlockSpec((1, f2 // 2, d), lambda c, e, t0, nt: (off + e, 0, 0)),
        ],
        out_specs=pl.BlockSpec(memory_space=pl.ANY),
        scratch_shapes=[
            pltpu.VMEM(((chunk + 1) * PACK_ROWS, LANES), jnp.uint32),
            pltpu.VMEM((tmg * PACK_ROWS, LANES), jnp.uint32),
            pltpu.VMEM((tmg * PACK_ROWS, LANES), jnp.uint32),
            pltpu.VMEM((tmg * ROW_TILE, LANES), F32),
            pltpu.VMEM((tmg * ROW_TILE, LANES), F32),
            pltpu.SMEM((rows_per_chunk,), jnp.int32),
            pltpu.SemaphoreType.DMA((4,)),
            pltpu.VMEM((d, f2), BF16),
            pltpu.VMEM((f2 // 2, d), BF16),
        ],
    )
    return pl.pallas_call(
        functools.partial(_experts_by_expert_kernel, tiles_per_chunk, tmg),
        grid_spec=grid_spec,
        out_shape=jax.ShapeDtypeStruct((nc * rows_per_chunk * ROW_TILE, LANES), F32),
        compiler_params=pltpu.CompilerParams(dimension_semantics=("arbitrary", "arbitrary"),
                                             vmem_limit_bytes=MOE_VMEM_LIMIT),
        name="moe_experts",
    )(tile_start, n_tiles, pos_local, fill, hs, w_in, w_out)


def _combine_kernel(split_steps, n_src, first_steps, pos_cur_ref, pos_nxt_ref, ys_hbm, route_ref, *rest):
    x_refs, o_refs, (ybuf_a, ybuf_b, sems) = rest[:n_src], rest[n_src:-3], rest[-3:]
    i = pl.program_id(0)
    last = i == pl.num_programs(0) - 1
    tm = route_ref.shape[0]

    @pl.when(i == 0)
    def _():
        _gather_rows(pos_cur_ref, ys_hbm, ybuf_a, sems.at[0], 2 * tm)

    def step(cur, cur_sem, nxt, nxt_sem):
        _gather_rows(pos_nxt_ref, ys_hbm, nxt, nxt_sem, 2 * tm)
        _wait_rows(ys_hbm, cur, cur_sem)
        route = route_ref[...]
        y1 = _from_token_tiles(cur, 0, tm, ROW_TILE)
        y2 = _from_token_tiles(cur, tm * ROW_TILE, tm, ROW_TILE)
        val = _read_tokens(x_refs, first_steps) + route[:, 2:3] * y1 + route[:, 3:4] * y2
        if split_steps is None:
            o_refs[0][...] = val
        else:
            @pl.when(i < split_steps)
            def _():
                o_refs[0][...] = val

            @pl.when(i >= split_steps)
            def _():
                o_refs[1][...] = val

        @pl.when(last)
        def _():
            _wait_rows(ys_hbm, nxt, nxt_sem)

    @pl.when(i % 2 == 0)
    def _():
        step(ybuf_a, sems.at[0], ybuf_b, sems.at[1])

    @pl.when(i % 2 == 1)
    def _():
        step(ybuf_b, sems.at[1], ybuf_a, sems.at[0])


def combine(ys, pos, route, xs, tm, split_rows=None):
    n = sum(x.shape[0] for x in xs)
    d = xs[0].shape[1]
    nsteps = n // tm
    if split_rows is None:
        split_steps = None
        out_specs = pl.BlockSpec((tm, d), lambda i: (i, 0))
        out_shape = jax.ShapeDtypeStruct((n, d), F32)
    else:
        split_steps = split_rows // tm
        out_specs = [pl.BlockSpec((tm, d), lambda i: (jnp.minimum(i, split_steps - 1), 0)),
                     pl.BlockSpec((tm, d), lambda i: (jnp.maximum(i - split_steps, 0), 0))]
        out_shape = [jax.ShapeDtypeStruct((split_rows, d), F32),
                     jax.ShapeDtypeStruct((n - split_rows, d), F32)]
    return pl.pallas_call(
        functools.partial(_combine_kernel, split_steps, len(xs), xs[0].shape[0] // tm),
        grid=(nsteps,),
        in_specs=[
            pl.BlockSpec((2 * tm,), lambda i: (i,), memory_space=pltpu.SMEM),
            pl.BlockSpec((2 * tm,), lambda i: (jnp.minimum(i + 1, nsteps - 1),),
                         memory_space=pltpu.SMEM),
            pl.BlockSpec(memory_space=pl.ANY),
            pl.BlockSpec((tm, LANES), lambda i: (i, 0)),
        ] + _token_specs(xs, tm),
        out_specs=out_specs,
        out_shape=out_shape,
        scratch_shapes=[pltpu.VMEM((2 * tm * ROW_TILE, LANES), F32),
                        pltpu.VMEM((2 * tm * ROW_TILE, LANES), F32),
                        pltpu.SemaphoreType.DMA((2,))],
        compiler_params=_params("arbitrary"),
        name="moe_combine",
    )(pos, pos, ys, route, *xs)


def moe_layer(xs, g, wr, br, w_in, w_out, layer, split_rows=None):
    n = sum(x.shape[0] for x in xs)
    nc = -(-n // MOE_CHUNK)
    chunk = n // nc
    assert chunk * nc == n and chunk % MOE_TM == 0
    tpc = (2 * chunk + N_EXPERTS * (MOE_TMG - 1)) // MOE_TMG
    hs, route, route_t, counts = router(xs, g, wr, br, MOE_TM, chunk)
    rt = route_t.reshape(nc, chunk // MOE_TM, ROW_TILE, MOE_TM)
    ids = rt[:, :, 0:2, :].astype(jnp.int32)
    ranks = rt[:, :, 4:6, :].astype(jnp.int32)
    cnt = counts.reshape(nc, ROW_TILE, LANES)[:, 0, :N_EXPERTS].astype(jnp.int32)
    padded = (cnt + MOE_TMG - 1) // MOE_TMG * MOE_TMG
    ends = jnp.cumsum(padded, axis=1)
    starts = ends - padded
    start_of = jnp.zeros_like(ids)
    for e in range(N_EXPERTS):
        start_of = jnp.where(ids == e, starts[:, e][:, None, None, None], start_of)
    pos_local = start_of + ranks
    ys = experts_by_expert(hs, pos_local.reshape(-1), (starts // MOE_TMG).reshape(-1),
                           (padded // MOE_TMG).reshape(-1), w_in, w_out, layer, MOE_TMG, chunk, tpc)
    pos = pos_local + (jnp.arange(nc, dtype=jnp.int32) * (tpc * MOE_TMG))[:, None, None, None]
    return combine(ys, pos.reshape(-1), route, xs, MOE_TM, split_rows)


def _qkv_kernel(x_ref, g_ref, w_ref, qg_ref, kg_ref, seg_ref, q_ref, k_ref, v_ref):
    h = _rms(x_ref[...], g_ref[...]).astype(BF16)
    qkv = jnp.dot(h, w_ref[...], preferred_element_type=F32)
    nq = N_HEADS * HEAD_DIM
    nk = N_KV * HEAD_DIM
    q = qkv[:, :nq]
    k = qkv[:, nq:nq + nk]
    v_ref[...] = qkv[:, nq + nk:]

    def seg_mean_sq(z, seg, split):
        zz = z * z
        hi = zz.astype(BF16)
        ms = jnp.dot(hi, seg, preferred_element_type=F32)
        if split:
            lo = (zz - hi.astype(F32)).astype(BF16)
            ms = ms + jnp.dot(lo, seg, preferred_element_type=F32)
        return ms

    seg = seg_ref[...]
    qn = q * lax.rsqrt(seg_mean_sq(q, seg, False) + RMS_EPS) * qg_ref[...]
    q_ref[...] = (qn * (HEAD_DIM ** -0.5)).astype(BF16)
    kn = k * lax.rsqrt(seg_mean_sq(k, seg[:nk, :nk], True) + RMS_EPS) * kg_ref[...]
    k_ref[...] = kn


def qkv_proj(x, g, w, qg, kg, seg, tm):
    n, d = x.shape
    nq = N_HEADS * HEAD_DIM
    nk = N_KV * HEAD_DIM
    return pl.pallas_call(
        _qkv_kernel,
        grid=(n // tm,),
        in_specs=[pl.BlockSpec((tm, d), lambda i: (i, 0)), _full((1, d)),
                  _full((d, nq + 2 * nk)), _full((1, nq)), _full((1, nk)), _full((nq, nq))],
        out_specs=[pl.BlockSpec((tm, nq), lambda i: (i, 0)),
                   pl.BlockSpec((tm, nk), lambda i: (i, 0)),
                   pl.BlockSpec((tm, nk), lambda i: (i, 0))],
        out_shape=[jax.ShapeDtypeStruct((n, nq), BF16),
                   jax.ShapeDtypeStruct((n, nk), F32),
                   jax.ShapeDtypeStruct((n, nk), F32)],
        compiler_params=_params("parallel"),
        name="qkv_proj",
    )(x, g, w, qg, kg, seg)


def _attn_prompt_kernel(q_ref, kc_ref, kp_ref, vc_ref, vp_ref, tbl_ref, sink_ref, x_ref,
                        wo_ref, o_ref, att_ref, s_ref, e_ref, inv_ref):
    n = pl.program_id(1)
    first = jnp.where(n == 0, NEG_INF, 0.0).astype(F32)
    nt = (((1,), (1,)), ((), ()))
    lo_half = lax.broadcasted_iota(jnp.int32, (WINDOW, LANES), 1) < HEAD_DIM
    kp, kc = kp_ref[...], kc_ref[...]
    keys = (kp.astype(BF16), kc.astype(BF16))
    keys_swapped = (pltpu.roll(kp, HEAD_DIM, axis=1).astype(BF16),
                    pltpu.roll(kc, HEAD_DIM, axis=1).astype(BF16))
    vt_prev = jnp.transpose(vp_ref[...]).astype(BF16)
    vt_cur = jnp.transpose(vc_ref[...]).astype(BF16)
    for h in range(N_HEADS):
        kh = h // GROUP
        odd = h % 2
        pair = q_ref[:, (h // 2) * LANES:(h // 2 + 1) * LANES]
        qm = jnp.where(lo_half if odd == 0 else ~lo_half, pair, jnp.zeros_like(pair))
        k_prev, k_cur = keys_swapped if (kh == 0) == (odd == 1) else keys
        s_ref[h, 0:WINDOW, :] = (lax.dot_general(k_prev, qm, nt, preferred_element_type=F32)
                                 + tbl_ref[h, 0:WINDOW, :] + first)
        s_ref[h, WINDOW:, :] = (lax.dot_general(k_cur, qm, nt, preferred_element_type=F32)
                                + tbl_ref[h, WINDOW:, :])
    for h in range(N_HEADS):
        s = s_ref[h]
        sink = sink_ref[h]
        m = jnp.maximum(jnp.max(s, axis=0, keepdims=True), sink)
        e = jnp.exp(s - m)
        e_ref[h] = e.astype(BF16)
        inv_ref[h] = 1.0 / (jnp.sum(e, axis=0, keepdims=True) + jnp.exp(sink - m))
    for h in range(N_HEADS):
        kh = h // GROUP
        dims = slice(kh * HEAD_DIM, (kh + 1) * HEAD_DIM)
        out_t = (jnp.dot(vt_prev[dims, :], e_ref[h, 0:WINDOW, :], preferred_element_type=F32)
                 + jnp.dot(vt_cur[dims, :], e_ref[h, WINDOW:, :], preferred_element_type=F32))
        att_ref[h * HEAD_DIM:(h + 1) * HEAD_DIM, :] = out_t * inv_ref[h]
    att = jnp.transpose(att_ref[...]).astype(BF16)
    o_ref[...] = x_ref[...] + jnp.dot(att, wo_ref[...], preferred_element_type=F32)


def attn_prompt(q, k, v, tbl, sinks, x, wo, batch, seq):
    n, d = x.shape
    nb = seq // WINDOW
    nk = N_KV * HEAD_DIM
    cur = lambda b, i: (b * nb + i, 0)
    prev = lambda b, i: (b * nb + jnp.maximum(i - 1, 0), 0)
    return pl.pallas_call(
        _attn_prompt_kernel,
        grid=(batch, nb),
        in_specs=[
            pl.BlockSpec((WINDOW, d), cur),
            pl.BlockSpec((WINDOW, nk), cur), pl.BlockSpec((WINDOW, nk), prev),
            pl.BlockSpec((WINDOW, nk), cur), pl.BlockSpec((WINDOW, nk), prev),
            _full((N_HEADS, 2 * WINDOW, WINDOW)),
            pl.BlockSpec(memory_space=pltpu.SMEM),
            pl.BlockSpec((WINDOW, d), cur),
            _full((d, d)),
        ],
        out_specs=pl.BlockSpec((WINDOW, d), cur),
        out_shape=jax.ShapeDtypeStruct((n, d), F32),
        input_output_aliases={7: 0},
        scratch_shapes=[pltpu.VMEM((N_HEADS * HEAD_DIM, WINDOW), F32),
                        pltpu.VMEM((N_HEADS, 2 * WINDOW, WINDOW), F32),
                        pltpu.VMEM((N_HEADS, 2 * WINDOW, WINDOW), BF16),
                        pltpu.VMEM((N_HEADS, 1, WINDOW), F32)],
        compiler_params=_params("parallel", "parallel"),
        name="attn_prompt",
    )(q, k, k, v, v, tbl, sinks, x, wo)


def _attn_sample_kernel(q_ref, kn_ref, vn_ref, ck_ref, cv_ref, tblc_ref, tbln_ref, sink_ref, o_ref,
                        sc_ref, sn_ref, ec_ref, en_ref, inv_ref):
    bs = q_ref.shape[0]
    nt = (((1,), (1,)), ((), ()))
    for b in range(bs):
        ck = ck_ref[b].astype(BF16)
        kn = kn_ref[b].astype(BF16)
        for kh in range(N_KV):
            lanes = slice(kh * HEAD_DIM, (kh + 1) * HEAD_DIM)
            qt = q_ref[b, kh]
            sc_ref[b, kh] = lax.dot_general(qt, ck[:, lanes], nt, preferred_element_type=F32) + tblc_ref[kh]
            sn_ref[b, kh] = lax.dot_general(qt, kn[:, lanes], nt, preferred_element_type=F32) + tbln_ref[kh]
    for b in range(bs):
        for kh in range(N_KV):
            sc = sc_ref[b, kh]
            sn = sn_ref[b, kh]
            sink = sink_ref[kh]
            m = jnp.maximum(jnp.maximum(jnp.max(sc, axis=-1, keepdims=True),
                                        jnp.max(sn, axis=-1, keepdims=True)), sink)
            ec = jnp.exp(sc - m)
            en = jnp.exp(sn - m)
            ec_ref[b, kh] = ec.astype(BF16)
            en_ref[b, kh] = en.astype(BF16)
            inv_ref[b, kh] = 1.0 / (jnp.sum(ec, axis=-1, keepdims=True)
                                    + jnp.sum(en, axis=-1, keepdims=True) + jnp.exp(sink - m))
    for b in range(bs):
        cv = cv_ref[b].astype(BF16)
        vn = vn_ref[b].astype(BF16)
        for kh in range(N_KV):
            lanes = slice(kh * HEAD_DIM, (kh + 1) * HEAD_DIM)
            out = (jnp.dot(ec_ref[b, kh], cv[:, lanes], preferred_element_type=F32)
                   + jnp.dot(en_ref[b, kh], vn[:, lanes], preferred_element_type=F32))
            o_ref[b, kh] = (out * inv_ref[b, kh]).astype(BF16)


def attn_sample(q4, kn, vn, ck, cv, tbl, sink, bs):
    nb = q4.shape[0]
    ts = kn.shape[1]
    nk = N_KV * HEAD_DIM
    tg = ts * GROUP
    return pl.pallas_call(
        _attn_sample_kernel,
        grid=(nb // bs,),
        in_specs=[
            pl.BlockSpec((bs, N_KV, tg, HEAD_DIM), lambda i: (i, 0, 0, 0)),
            pl.BlockSpec((bs, ts, nk), lambda i: (i, 0, 0)),
            pl.BlockSpec((bs, ts, nk), lambda i: (i, 0, 0)),
            pl.BlockSpec((bs, WINDOW, nk), lambda i: (i, 0, 0)),
            pl.BlockSpec((bs, WINDOW, nk), lambda i: (i, 0, 0)),
            _full((N_KV, tg, WINDOW)),
            _full((N_KV, tg, ts)),
            _full((N_KV, tg, 1)),
        ],
        out_specs=pl.BlockSpec((bs, N_KV, tg, HEAD_DIM), lambda i: (i, 0, 0, 0)),
        out_shape=jax.ShapeDtypeStruct((nb, N_KV, tg, HEAD_DIM), BF16),
        scratch_shapes=[pltpu.VMEM((bs, N_KV, tg, WINDOW), F32),
                        pltpu.VMEM((bs, N_KV, tg, ts), F32),
                        pltpu.VMEM((bs, N_KV, tg, WINDOW), BF16),
                        pltpu.VMEM((bs, N_KV, tg, ts), BF16),
                        pltpu.VMEM((bs, N_KV, tg, 1), F32)],
        compiler_params=_params("parallel"),
        name="attn_sample",
    )(q4, kn, vn, ck, cv, tbl[:, :, :WINDOW], tbl[:, :, WINDOW:], sink)


def _proj_res_kernel(a_ref, w_ref, x_ref, o_ref):
    o_ref[...] = x_ref[...] + jnp.dot(a_ref[...], w_ref[...], preferred_element_type=F32)


def proj_residual(a, w, x, tm):
    n, d = x.shape
    na, kdim = a.shape
    off = (n - na) // tm
    return pl.pallas_call(
        _proj_res_kernel,
        grid=(na // tm,),
        in_specs=[pl.BlockSpec((tm, kdim), lambda i: (i, 0)), _full((kdim, d)),
                  pl.BlockSpec((tm, d), lambda i: (off + i, 0))],
        out_specs=pl.BlockSpec((tm, d), lambda i: (off + i, 0)),
        out_shape=jax.ShapeDtypeStruct((n, d), F32),
        input_output_aliases={2: 0},
        compiler_params=_params("parallel"),
        name="proj_residual",
    )(a, w, x)


def _t5_bucket_np(dist):
    n = np.maximum(dist, 0)
    max_exact = N_BUCKETS // 2
    large = max_exact + (np.log(np.maximum(n, 1).astype(np.float32) / max_exact)
                         / math.log(MAX_DISTANCE / max_exact) * (N_BUCKETS - max_exact)).astype(np.int32)
    large = np.minimum(large, N_BUCKETS - 1)
    return np.where(n < max_exact, n, large)


def _bias_table(rel_bias, dist):
    valid = (dist >= 0) & (dist <= WINDOW)
    onehot = (np.asarray(_t5_bucket_np(dist))[..., None] == np.arange(N_BUCKETS)).astype(np.float32)
    b = jnp.einsum("qkb,bh->hqk", jnp.asarray(onehot), rel_bias.astype(F32),
                   precision=lax.Precision.HIGHEST)
    return jnp.where(jnp.asarray(valid)[None], b, NEG_INF)


def kernel(x_prompt, x_sample, state_conv, cache_swa_k, cache_swa_v, rms_mix_g, rms_ffn_g, conv_w_in, conv_dw_w, conv_dw_b, conv_ln_g, conv_ln_b, conv_w_out, attn_w_qkv, attn_q_norm_g, attn_k_norm_g, attn_sinks, attn_w_o, rel_bias, router_group_w, router_group_b, router_expert_w, router_expert_b, expert_w_in, expert_w_out):
    batch, seq, d = x_prompt.shape
    nsb, ts, _ = x_sample.shape
    xp = x_prompt.reshape(batch * seq, d)
    xs = x_sample.reshape(nsb * ts, d)
    row = lambda a: a.reshape(1, -1).astype(F32)

    def router_w(i):
        we = jnp.transpose(router_expert_w[i], (1, 0, 2)).reshape(d, N_EXPERTS)
        wr = jnp.concatenate([we, router_group_w[i]], axis=1)
        wr = jnp.pad(wr, ((0, 0), (0, LANES - wr.shape[1])))
        br = jnp.concatenate([router_expert_b[i].reshape(-1), router_group_b[i]])
        br = jnp.pad(br, (0, LANES - br.shape[0])).reshape(1, LANES)
        wr = wr.astype(F32)
        w_hi = wr.astype(BF16)
        w_lo = (wr - w_hi.astype(F32)).astype(BF16)
        return jnp.concatenate([w_hi, w_lo], axis=1), br.astype(F32)

    g0 = row(rms_mix_g[0])
    w_in = conv_w_in[0].astype(BF16)
    dww = jnp.pad(conv_dw_w[0].astype(F32), ((0, HALO - CONV_WIDTH), (0, 0)))
    dwb, lng, lnb = row(conv_dw_b[0]), row(conv_ln_g[0]), row(conv_ln_b[0])
    w_out = conv_w_out[0].astype(BF16)
    up = glu_proj(xp, g0, w_in, TOKEN_TM)
    us = glu_proj(xs, g0, w_in, TOKEN_TM)
    n_p = batch * seq
    xp = conv_prompt(up, xp, dww, dwb, lng, lnb, w_out, batch, seq, CONV_TT)
    us3 = us.reshape(nsb, ts, -1)
    xs = conv_sample(us3, state_conv[0], xs, dww, dwb, lng, lnb, w_out, CONV_SAMPLE_BS)
    conv_p = up.reshape(batch, seq, -1)[:, seq - PAST:]
    conv_s = jnp.concatenate([state_conv[0], us3], axis=1)[:, ts:]

    wr0, br0 = router_w(0)
    ew_in = expert_w_in.reshape((DEPTH * N_EXPERTS,) + expert_w_in.shape[2:])
    ew_out = expert_w_out.reshape((DEPTH * N_EXPERTS,) + expert_w_out.shape[2:])
    x = moe_layer((xp, xs), row(rms_ffn_g[0]), wr0, br0, ew_in, ew_out, 0)

    g1 = row(rms_mix_g[1])
    w_qkv = attn_w_qkv[0].astype(BF16)
    qg = jnp.tile(attn_q_norm_g[0].astype(F32), N_HEADS).reshape(1, -1)
    kg = jnp.tile(attn_k_norm_g[0].astype(F32), N_KV).reshape(1, -1)
    nq = N_HEADS * HEAD_DIM
    seg = jnp.asarray(np.kron(np.eye(N_HEADS), np.ones((HEAD_DIM, HEAD_DIM))) / HEAD_DIM, BF16)
    w_o = attn_w_o[0].astype(BF16)
    sinks = attn_sinks[0].astype(F32)

    q, k, v = qkv_proj(x, g1, w_qkv, qg, kg, seg, TOKEN_TM)
    qs, ks, vs = q[n_p:], k[n_p:], v[n_p:]

    q_off = np.arange(WINDOW)[:, None]
    dist_p = q_off + WINDOW - np.arange(2 * WINDOW)[None, :]
    tbl_p = jnp.transpose(_bias_table(rel_bias, dist_p), (0, 2, 1))
    x_attn = attn_prompt(q, k, v, tbl_p, sinks, x, w_o, batch, seq)

    kt = WINDOW + ts
    dist_s = np.arange(ts)[:, None] + WINDOW - np.arange(kt)[None, :]
    tbl_s = _bias_table(rel_bias, dist_s)
    tbl_s = jnp.transpose(tbl_s.reshape(N_KV, GROUP, ts, kt), (0, 2, 1, 3)).reshape(N_KV, ts * GROUP, kt)
    sink_s = jnp.tile(sinks.reshape(N_KV, 1, GROUP), (1, ts, 1)).reshape(N_KV, ts * GROUP, 1)
    q4 = jnp.transpose(qs.reshape(nsb, ts, N_KV, GROUP, HEAD_DIM), (0, 2, 1, 3, 4))
    q4 = q4.reshape(nsb, N_KV, ts * GROUP, HEAD_DIM)
    nk = N_KV * HEAD_DIM
    ks3, vs3 = ks.reshape(nsb, ts, nk), vs.reshape(nsb, ts, nk)
    ck = cache_swa_k[0].reshape(nsb, WINDOW, nk)
    cv = cache_swa_v[0].reshape(nsb, WINDOW, nk)
    o4 = attn_sample(q4, ks3, vs3, ck, cv, tbl_s, sink_s, ATTN_SAMPLE_BS)
    os_ = jnp.transpose(o4.reshape(nsb, N_KV, ts, GROUP, HEAD_DIM), (0, 2, 1, 3, 4)).reshape(nsb * ts, nq)
    x = proj_residual(os_, w_o, x_attn, TOKEN_TM)

    last_window = lambda a: jnp.stack([a[(b + 1) * seq - WINDOW:(b + 1) * seq] for b in range(batch)]
                                      ).reshape(batch, WINDOW, N_KV, HEAD_DIM)
    k_p, v_p = last_window(k), last_window(v)
    k_s = jnp.concatenate([cache_swa_k[0], ks.reshape(nsb, ts, N_KV, HEAD_DIM)], axis=1)[:, ts:]
    v_s = jnp.concatenate([cache_swa_v[0], vs.reshape(nsb, ts, N_KV, HEAD_DIM)], axis=1)[:, ts:]

    wr1, br1 = router_w(1)
    xp, xs = moe_layer((x,), row(rms_ffn_g[1]), wr1, br1, ew_in, ew_out, 1, split_rows=n_p)

    return (xp.reshape(batch, seq, d), xs.reshape(nsb, ts, d),
            conv_p[None], conv_s[None], k_p[None], v_p[None], k_s[None], v_s[None])
```

```python
import functools
import math

import numpy as np
import jax
import jax.numpy as jnp
from jax import lax
from jax.experimental import pallas as pl
from jax.experimental.pallas import tpu as pltpu

D_MODEL = 1024
DEPTH = 2
CONV_WIDTH = 31
PAST = CONV_WIDTH - 1
HEAD_DIM = 64
N_HEADS = 16
N_KV = 2
GROUP = 8
WINDOW = 128
N_BUCKETS = 32
MAX_DISTANCE = 128
N_GROUPS = 4
EPG = 8
N_EXPERTS = 32
D_EXPERT = 256
RMS_EPS = 1e-6
LN_EPS = 1e-5
NEG_INF = -1e30

F32 = jnp.float32
BF16 = jnp.bfloat16
LANES = 128
ROW_TILE = 8
MOE_TM = 512
MOE_TMG = 256
PACK_ROWS = ROW_TILE // 2
MOE_CHUNK = 17408
TOKEN_TM = 512
CONV_TT = 256
CONV_SAMPLE_BS = 32
ATTN_SAMPLE_BS = 16
V7X_VMEM_BYTES = 64 * 1024 * 1024
MOE_VMEM_LIMIT = V7X_VMEM_BYTES - 8 * 1024 * 1024
VMEM_LIMIT = V7X_VMEM_BYTES - 16 * 1024 * 1024


def _params(*sem):
    return pltpu.CompilerParams(dimension_semantics=sem, vmem_limit_bytes=VMEM_LIMIT)


def _rms(x, g):
    return x * lax.rsqrt(jnp.mean(x * x, axis=-1, keepdims=True) + RMS_EPS) * g


def _sigmoid(x):
    return 1.0 / (1.0 + jnp.exp(-x))


def _full(shape):
    return pl.BlockSpec(shape, lambda *_: (0,) * len(shape))


def _pack_bf16_pair(lo, hi):
    lo_bits = lax.bitcast_convert_type(lo.astype(BF16).astype(F32), jnp.uint32) >> 16
    hi_bits = lax.bitcast_convert_type(hi.astype(BF16).astype(F32), jnp.uint32) & jnp.uint32(0xFFFF0000)
    return lo_bits | hi_bits


def _unpack_bf16_pair(words):
    lo = lax.bitcast_convert_type(words << 16, F32)
    hi = lax.bitcast_convert_type(words & jnp.uint32(0xFFFF0000), F32)
    return lo, hi


def _glu_kernel(x_ref, g_ref, wa_ref, wg_ref, u_ref):
    h = _rms(x_ref[...], g_ref[...]).astype(BF16)
    a = jnp.dot(h, wa_ref[...], preferred_element_type=F32)
    gate = jnp.dot(h, wg_ref[...], preferred_element_type=F32)
    u_ref[...] = a * _sigmoid(gate)


def glu_proj(x, g, w_in, tm):
    n, d = x.shape
    c = w_in.shape[1] // 2
    return pl.pallas_call(
        _glu_kernel,
        grid=(n // tm,),
        in_specs=[
            pl.BlockSpec((tm, d), lambda i: (i, 0)),
            _full((1, d)),
            pl.BlockSpec((d, c), lambda i: (0, 0)),
            pl.BlockSpec((d, c), lambda i: (0, 1)),
        ],
        out_specs=pl.BlockSpec((tm, c), lambda i: (i, 0)),
        out_shape=jax.ShapeDtypeStruct((n, c), F32),
        compiler_params=_params("parallel"),
        name="glu_proj",
    )(x, g, w_in, w_in)


def _ln_silu_out(y, lng, lnb, wout_ref, x):
    mu = jnp.mean(y, axis=-1, keepdims=True)
    yc = y - mu
    z = yc * lax.rsqrt(jnp.mean(yc * yc, axis=-1, keepdims=True) + LN_EPS) * lng + lnb
    z = z * _sigmoid(z)
    return x + jnp.dot(z.astype(BF16), wout_ref[...], preferred_element_type=F32)


HALO = 32
CONV_RC = 64
CONV_CC = 128


def _conv_prompt_kernel(ucur_ref, uprev_ref, x_ref, dww_ref, dwb_ref, lng_ref, lnb_ref,
                        wout_ref, o_ref, up_ref, y_ref):
    t = pl.program_id(1)
    tt, c = ucur_ref.shape
    keep = (t > 0).astype(F32)
    up_ref[0:HALO, :] = uprev_ref[...] * keep
    up_ref[HALO:HALO + tt, :] = ucur_ref[...]
    up_ref[HALO + tt:, :] = jnp.zeros((ROW_TILE, c), F32)
    off = HALO - PAST
    for r0 in range(0, tt, CONV_RC):
        for c0 in range(0, c, CONV_CC):
            y = jnp.zeros((CONV_RC, CONV_CC), F32) + dwb_ref[:, c0:c0 + CONV_CC]
            for s in range(ROW_TILE):
                v = None
                for q in range((off + CONV_WIDTH - 1) // ROW_TILE + 1):
                    k = ROW_TILE * q + s - off
                    if k < 0 or k >= CONV_WIDTH:
                        continue
                    lo = r0 + ROW_TILE * q
                    term = (up_ref[lo:lo + CONV_RC + ROW_TILE, c0:c0 + CONV_CC]
                            * dww_ref[k:k + 1, c0:c0 + CONV_CC])
                    v = term if v is None else v + term
                y = y + v[s:s + CONV_RC]
            y_ref[r0:r0 + CONV_RC, c0:c0 + CONV_CC] = y
    o_ref[...] = _ln_silu_out(y_ref[...], lng_ref[...], lnb_ref[...], wout_ref, x_ref[...])


def conv_prompt(u, x, dww, dwb, lng, lnb, wout, batch, seq, tt):
    n, c = u.shape
    d = x.shape[1]
    nt = seq // tt
    hb = tt // HALO
    return pl.pallas_call(
        _conv_prompt_kernel,
        grid=(batch, nt),
        in_specs=[
            pl.BlockSpec((tt, c), lambda b, t: (b * nt + t, 0)),
            pl.BlockSpec((HALO, c), lambda b, t: (jnp.maximum((b * nt + t) * hb - 1, 0), 0)),
            pl.BlockSpec((tt, d), lambda b, t: (b * nt + t, 0)),
            _full((HALO, c)), _full((1, c)), _full((1, c)), _full((1, c)),
            _full((c, d)),
        ],
        out_specs=pl.BlockSpec((tt, d), lambda b, t: (b * nt + t, 0)),
        out_shape=jax.ShapeDtypeStruct((n, d), F32),
        scratch_shapes=[pltpu.VMEM((tt + HALO + ROW_TILE, c), F32), pltpu.VMEM((tt, c), F32)],
        compiler_params=_params("parallel", "parallel"),
        name="conv_prompt",
    )(u, u, x, dww, dwb, lng, lnb, wout)


CONV_SB = 4


def _conv_sample_kernel(u_ref, st_ref, x_ref, dww_ref, dwb_ref, lng_ref, lnb_ref,
                        wout_ref, o_ref, up_ref, y_ref):
    bs, ts, c = u_ref.shape
    up_ref[:, 0:PAST, :] = st_ref[...]
    up_ref[:, PAST:PAST + ts, :] = u_ref[...]
    for b0 in range(0, bs, CONV_SB):
        acc = jnp.zeros((CONV_SB, ts, c), F32) + dwb_ref[...][None]
        for k in range(CONV_WIDTH):
            acc = acc + up_ref[b0:b0 + CONV_SB, k:k + ts, :] * dww_ref[k:k + 1, :][None]
        y_ref[b0 * ts:(b0 + CONV_SB) * ts, :] = acc.reshape(CONV_SB * ts, c)
    o_ref[...] = _ln_silu_out(y_ref[...], lng_ref[...], lnb_ref[...], wout_ref, x_ref[...])


def conv_sample(u3, state, x, dww, dwb, lng, lnb, wout, bs):
    nb, ts, c = u3.shape
    d = x.shape[1]
    return pl.pallas_call(
        _conv_sample_kernel,
        grid=(nb // bs,),
        in_specs=[
            pl.BlockSpec((bs, ts, c), lambda i: (i, 0, 0)),
            pl.BlockSpec((bs, PAST, c), lambda i: (i, 0, 0)),
            pl.BlockSpec((bs * ts, d), lambda i: (i, 0)),
            _full((HALO, c)), _full((1, c)), _full((1, c)), _full((1, c)),
            _full((c, d)),
        ],
        out_specs=pl.BlockSpec((bs * ts, d), lambda i: (i, 0)),
        out_shape=jax.ShapeDtypeStruct((nb * ts, d), F32),
        scratch_shapes=[pltpu.VMEM((bs, PAST + ts, c), F32), pltpu.VMEM((bs * ts, c), F32)],
        compiler_params=_params("parallel"),
        name="conv_sample",
    )(u3, state, x, dww, dwb, lng, lnb, wout)


def _token_specs(xs, tm):
    d = xs[0].shape[1]
    if len(xs) == 1:
        return [pl.BlockSpec((tm, d), lambda i: (i, 0))]
    first = xs[0].shape[0] // tm
    return [pl.BlockSpec((tm, d), lambda i: (jnp.minimum(i, first - 1), 0)),
            pl.BlockSpec((tm, d), lambda i: (jnp.maximum(i - first, 0), 0))]


def _read_tokens(x_refs, first_steps):
    if len(x_refs) == 1:
        return x_refs[0][...]
    return jnp.where(pl.program_id(0) < first_steps, x_refs[0][...], x_refs[1][...])


def _router_kernel(tiles_per_chunk, n_src, first_steps, *refs):
    x_refs = refs[:n_src]
    g_ref, wr_ref, br_ref, tri_ref, hs_ref, route_ref, route_t_ref, cnt_ref, carry_ref = refs[n_src:]
    h = _rms(_read_tokens(x_refs, first_steps), g_ref[...])
    tm, d = h.shape
    for j in range(PACK_ROWS):
        hs_ref[pl.ds(j, tm, stride=PACK_ROWS), :] = _pack_bf16_pair(
            h[:, 2 * j * LANES:(2 * j + 1) * LANES], h[:, (2 * j + 1) * LANES:(2 * j + 2) * LANES])
    h_hi = h.astype(BF16)
    h_lo = (h - h_hi.astype(F32)).astype(BF16)
    both = jnp.dot(h_hi, wr_ref[...], preferred_element_type=F32)
    cross = jnp.dot(h_lo, wr_ref[:, :LANES], preferred_element_type=F32)
    logits = both[:, :LANES] + both[:, LANES:] + cross + br_ref[...]
    lane = lax.broadcasted_iota(jnp.int32, logits.shape, 1).astype(F32)
    big = jnp.float32(LANES)
    is_g = (lane >= N_EXPERTS) & (lane < N_EXPERTS + N_GROUPS)
    gl = jnp.where(is_g, logits, NEG_INF)
    gm = jnp.max(gl, axis=-1, keepdims=True)
    g_sel = jnp.min(jnp.where(gl == gm, lane, big), axis=-1, keepdims=True) - N_EXPERTS
    gate_g = 1.0 / jnp.sum(jnp.where(is_g, jnp.exp(gl - gm), 0.0), axis=-1, keepdims=True)
    lo = g_sel * EPG
    in_grp = (lane >= lo) & (lane < lo + EPG)
    el = jnp.where(in_grp, logits, NEG_INF)
    v1 = jnp.max(el, axis=-1, keepdims=True)
    i1 = jnp.min(jnp.where(el == v1, lane, big), axis=-1, keepdims=True)
    el2 = jnp.where(lane == i1, NEG_INF, el)
    v2 = jnp.max(el2, axis=-1, keepdims=True)
    i2 = jnp.min(jnp.where(el2 == v2, lane, big), axis=-1, keepdims=True)
    e2 = jnp.exp(v2 - v1)
    w1 = gate_g / (1.0 + e2)
    w2 = gate_g * e2 / (1.0 + e2)
    @pl.when(pl.program_id(0) % tiles_per_chunk == 0)
    def _():
        carry_ref[...] = jnp.zeros_like(carry_ref)

    hit1 = lane == i1
    hit2 = lane == i2
    onehot = jnp.where(hit1 | hit2, 1.0, 0.0)
    before = carry_ref[...] + jnp.dot(tri_ref[...], onehot.astype(BF16), preferred_element_type=F32)
    r1 = jnp.sum(jnp.where(hit1, before, 0.0), axis=-1, keepdims=True)
    r2 = jnp.sum(jnp.where(hit2, before, 0.0), axis=-1, keepdims=True)
    carry_ref[...] += jnp.sum(onehot, axis=0, keepdims=True)
    cnt_ref[...] = jnp.broadcast_to(carry_ref[...], cnt_ref.shape)
    route = jnp.where(lane == 0.0, i1, jnp.where(lane == 1.0, i2, jnp.where(
        lane == 2.0, w1, jnp.where(lane == 3.0, w2, jnp.where(
            lane == 4.0, r1, jnp.where(lane == 5.0, r2, 0.0))))))
    route_ref[...] = route
    route_t_ref[...] = jnp.transpose(route)[0:ROW_TILE, :]


def router(xs, g, wr, br, tm, chunk):
    n = sum(x.shape[0] for x in xs)
    d = xs[0].shape[1]
    tpc = chunk // tm
    tri = jnp.asarray(np.tril(np.ones((tm, tm), np.float32), -1), BF16)
    return pl.pallas_call(
        functools.partial(_router_kernel, tpc, len(xs), xs[0].shape[0] // tm),
        grid=(n // tm,),
        in_specs=_token_specs(xs, tm) + [_full((1, d)),
                  _full((d, 2 * LANES)), _full((1, LANES)), _full((tm, tm))],
        out_specs=[pl.BlockSpec((tm * PACK_ROWS, LANES), lambda i: (i, 0)),
                   pl.BlockSpec((tm, LANES), lambda i: (i, 0)),
                   pl.BlockSpec((ROW_TILE, tm), lambda i: (i, 0)),
                   pl.BlockSpec((ROW_TILE, LANES), lambda i: (i // tpc, 0))],
        out_shape=[jax.ShapeDtypeStruct((n * PACK_ROWS, LANES), jnp.uint32),
                   jax.ShapeDtypeStruct((n, LANES), F32),
                   jax.ShapeDtypeStruct((n // tm * ROW_TILE, tm), F32),
                   jax.ShapeDtypeStruct((n // chunk * ROW_TILE, LANES), F32)],
        scratch_shapes=[pltpu.VMEM((1, LANES), F32)],
        compiler_params=_params("arbitrary"),
        name="router",
    )(*xs, g, wr, br, tri)


def _gather_rows(idx_ref, src_hbm, dst, sem, n_rows):
    unroll = 8

    def body(c, carry):
        for u in range(unroll):
            m = c * unroll + u
            tok = idx_ref[m]
            pltpu.async_copy(
                src_hbm.at[pl.ds(pl.multiple_of(tok * ROW_TILE, ROW_TILE), ROW_TILE), :],
                dst.at[pl.ds(pl.multiple_of(m * ROW_TILE, ROW_TILE), ROW_TILE), :],
                sem, priority=u % 2)
        return carry

    lax.fori_loop(0, n_rows // unroll, body, 0)


def _wait_rows(src_hbm, dst, sem):
    pltpu.make_async_copy(src_hbm.at[pl.ds(0, dst.shape[0]), :], dst, sem).wait()


def _from_token_tiles(buf, start, n_rows, stride):
    return jnp.concatenate(
        [buf[pl.ds(start + j, n_rows, stride=stride), :] for j in range(ROW_TILE)], axis=-1)


def _experts_by_expert_kernel(tiles_per_chunk, tmg, t0_ref, nt_ref, pos_ref, fill_hbm, hs_hbm,
                              win_ref, wout_ref, ys_hbm, hsv, gbuf_a, gbuf_b, ystage_a, ystage_b,
                              src_ref, sems, winb, woutb):
    c = pl.program_id(0)
    e = pl.program_id(1)
    chunk = pos_ref.shape[0] // 2
    tile_rows = tmg * ROW_TILE
    scatter_unroll = 32
    gather_unroll = 8

    def tile(r):
        return pl.ds(pl.multiple_of(r * PACK_ROWS, PACK_ROWS), PACK_ROWS)

    def ys_tile(t):
        return ys_hbm.at[pl.ds(pl.multiple_of((c * tiles_per_chunk + t) * tile_rows, tile_rows),
                               tile_rows), :]

    @pl.when(e == 0)
    def _():
        load = pltpu.make_async_copy(
            hs_hbm.at[pl.ds(pl.multiple_of(c * chunk * PACK_ROWS, PACK_ROWS), chunk * PACK_ROWS), :],
            hsv.at[pl.ds(0, chunk * PACK_ROWS), :], sems.at[0])
        load.start()
        fill = pltpu.make_async_copy(fill_hbm, src_ref, sems.at[1])
        fill.start()
        hsv[pl.ds(chunk * PACK_ROWS, PACK_ROWS), :] = jnp.zeros((PACK_ROWS, LANES), jnp.uint32)
        fill.wait()

        def scatter(i, carry):
            first = i * scatter_unroll
            tok0 = first // (2 * MOE_TM) * MOE_TM + first % MOE_TM
            for u in range(scatter_unroll):
                src_ref[pos_ref[first + u]] = tok0 + u
            return carry

        lax.fori_loop(0, 2 * chunk // scatter_unroll, scatter, 0)
        load.wait()

        def gather(i, carry):
            for u in range(gather_unroll):
                m = i * gather_unroll + u
                gbuf_a[tile(m), :] = hsv[tile(src_ref[m]), :]
            return carry

        lax.fori_loop(0, tmg // gather_unroll, gather, 0)

    t0 = t0_ref[c * N_EXPERTS + e]
    nt = nt_ref[c * N_EXPERTS + e]

    @pl.when(nt > 0)
    def _():
        winb[...] = win_ref[0].astype(BF16)
        woutb[...] = wout_ref[0].astype(BF16)

    def step(t, cur, nxt, ystage, sem):
        @pl.when(t >= 2)
        def _():
            pltpu.make_async_copy(ystage, ys_tile(t), sem).wait()

        base = jnp.minimum(t + 1, tiles_per_chunk - 1) * tmg
        for m in range(tmg):
            nxt[pl.ds(m * PACK_ROWS, PACK_ROWS), :] = hsv[tile(src_ref[base + m]), :]
        halves = []
        for j in range(PACK_ROWS):
            halves.extend(_unpack_bf16_pair(cur[pl.ds(j, tmg, stride=PACK_ROWS), :]))
        x = jnp.concatenate(halves, axis=-1).astype(BF16)
        hid = jnp.dot(x, winb[...], preferred_element_type=F32)
        a = hid[:, :D_EXPERT]
        u = hid[:, D_EXPERT:]
        act = (a * _sigmoid(a) * u).astype(BF16)
        y = jnp.dot(act, woutb[...], preferred_element_type=F32)
        for k in range(ROW_TILE):
            ystage[pl.ds(k, tmg, stride=ROW_TILE), :] = y[:, k * LANES:(k + 1) * LANES]
        pltpu.make_async_copy(ystage, ys_tile(t), sem).start()

    def one_tile(k, carry):
        t = t0 + k

        @pl.when(t % 2 == 0)
        def _():
            step(t, gbuf_a, gbuf_b, ystage_a, sems.at[2])

        @pl.when(t % 2 == 1)
        def _():
            step(t, gbuf_b, gbuf_a, ystage_b, sems.at[3])

        return carry

    lax.fori_loop(0, nt, one_tile, 0)

    @pl.when(e == pl.num_programs(1) - 1)
    def _():
        total = t0 + nt

        @pl.when(total >= 1)
        def _():
            pltpu.make_async_copy(ystage_a, ys_tile(0), sems.at[2]).wait()

        @pl.when(total >= 2)
        def _():
            pltpu.make_async_copy(ystage_b, ys_tile(0), sems.at[3]).wait()

        ystage_a[...] = jnp.zeros_like(ystage_a)

        def zero_tile(t, carry):
            cp = pltpu.make_async_copy(ystage_a, ys_tile(t), sems.at[2])
            cp.start()
            cp.wait()
            return carry

        lax.fori_loop(total, tiles_per_chunk, zero_tile, 0)


def experts_by_expert(hs, pos_local, tile_start, n_tiles, w_in, w_out, layer, tmg, chunk, tiles_per_chunk):
    _, d, f2 = w_in.shape
    nc = hs.shape[0] // (chunk * PACK_ROWS)
    rows_per_chunk = tiles_per_chunk * tmg
    fill = jnp.full((rows_per_chunk,), chunk, jnp.int32)
    off = layer * N_EXPERTS
    grid_spec = pltpu.PrefetchScalarGridSpec(
        num_scalar_prefetch=2,
        grid=(nc, N_EXPERTS),
        in_specs=[
            pl.BlockSpec((2 * chunk,), lambda c, e, t0, nt: (c,), memory_space=pltpu.SMEM),
            pl.BlockSpec(memory_space=pl.ANY),
            pl.BlockSpec(memory_space=pl.ANY),
            pl.BlockSpec((1, d, f2), lambda c, e, t0, nt: (off + e, 0, 0)),
            pl.BlockSpec((1, f2 // 2, d), lambda c, e, t0, nt: (off + e, 0, 0)),
        ],
        out_specs=pl.BlockSpec(memory_space=pl.ANY),
        scratch_shapes=[
            pltpu.VMEM(((chunk + 1) * PACK_ROWS, LANES), jnp.uint32),
            pltpu.VMEM((tmg * PACK_ROWS, LANES), jnp.uint32),
            pltpu.VMEM((tmg * PACK_ROWS, LANES), jnp.uint32),
            pltpu.VMEM((tmg * ROW_TILE, LANES), F32),
            pltpu.VMEM((tmg * ROW_TILE, LANES), F32),
            pltpu.SMEM((rows_per_chunk,), jnp.int32),
            pltpu.SemaphoreType.DMA((4,)),
            pltpu.VMEM((d, f2), BF16),
            pltpu.VMEM((f2 // 2, d), BF16),
        ],
    )
    return pl.pallas_call(
        functools.partial(_experts_by_expert_kernel, tiles_per_chunk, tmg),
        grid_spec=grid_spec,
        out_shape=jax.ShapeDtypeStruct((nc * rows_per_chunk * ROW_TILE, LANES), F32),
        compiler_params=pltpu.CompilerParams(dimension_semantics=("arbitrary", "arbitrary"),
                                             vmem_limit_bytes=MOE_VMEM_LIMIT),
        name="moe_experts",
    )(tile_start, n_tiles, pos_local, fill, hs, w_in, w_out)


def _combine_kernel(split_steps, n_src, first_steps, pos0_ref, pos1_ref, pos2_ref, ys_hbm, route_ref, *rest):
    x_refs, o_refs, (ybuf_a, ybuf_b, ybuf_c, sems) = rest[:n_src], rest[n_src:-4], rest[-4:]
    bufs = (ybuf_a, ybuf_b, ybuf_c)
    i = pl.program_id(0)
    last = i == pl.num_programs(0) - 1
    tm = route_ref.shape[0]

    @pl.when(i == 0)
    def _():
        _gather_rows(pos0_ref, ys_hbm, ybuf_a, sems.at[0], 2 * tm)
        _gather_rows(pos1_ref, ys_hbm, ybuf_b, sems.at[1], 2 * tm)

    def step(b):
        cur, cur_sem = bufs[b], sems.at[b]
        ahead, ahead_sem = bufs[(b + 2) % 3], sems.at[(b + 2) % 3]
        _wait_rows(ys_hbm, cur, cur_sem)
        for m in range(2 * tm):
            pltpu.async_copy(
                ys_hbm.at[pl.ds(pl.multiple_of(pos2_ref[m] * ROW_TILE, ROW_TILE), ROW_TILE), :],
                ahead.at[pl.ds(m * ROW_TILE, ROW_TILE), :], ahead_sem, priority=m % 2)
        route = route_ref[...]
        y1 = _from_token_tiles(cur, 0, tm, ROW_TILE)
        y2 = _from_token_tiles(cur, tm * ROW_TILE, tm, ROW_TILE)
        val = _read_tokens(x_refs, first_steps) + route[:, 2:3] * y1 + route[:, 3:4] * y2
        if split_steps is None:
            o_refs[0][...] = val
        else:
            @pl.when(i < split_steps)
            def _():
                o_refs[0][...] = val

            @pl.when(i >= split_steps)
            def _():
                o_refs[1][...] = val

        @pl.when(last)
        def _():
            _wait_rows(ys_hbm, bufs[(b + 1) % 3], sems.at[(b + 1) % 3])
            _wait_rows(ys_hbm, ahead, ahead_sem)

    for b in range(3):
        @pl.when(i % 3 == b)
        def _(b=b):
            step(b)


def combine(ys, pos, route, xs, tm, split_rows=None):
    n = sum(x.shape[0] for x in xs)
    d = xs[0].shape[1]
    nsteps = n // tm
    if split_rows is None:
        split_steps = None
        out_specs = pl.BlockSpec((tm, d), lambda i: (i, 0))
        out_shape = jax.ShapeDtypeStruct((n, d), F32)
    else:
        split_steps = split_rows // tm
        out_specs = [pl.BlockSpec((tm, d), lambda i: (jnp.minimum(i, split_steps - 1), 0)),
                     pl.BlockSpec((tm, d), lambda i: (jnp.maximum(i - split_steps, 0), 0))]
        out_shape = [jax.ShapeDtypeStruct((split_rows, d), F32),
                     jax.ShapeDtypeStruct((n - split_rows, d), F32)]
    return pl.pallas_call(
        functools.partial(_combine_kernel, split_steps, len(xs), xs[0].shape[0] // tm),
        grid=(nsteps,),
        in_specs=[
            pl.BlockSpec((2 * tm,), lambda i: (i,), memory_space=pltpu.SMEM),
            pl.BlockSpec((2 * tm,), lambda i: (jnp.minimum(i + 1, nsteps - 1),),
                         memory_space=pltpu.SMEM),
            pl.BlockSpec((2 * tm,), lambda i: (jnp.minimum(i + 2, nsteps - 1),),
                         memory_space=pltpu.SMEM),
            pl.BlockSpec(memory_space=pl.ANY),
            pl.BlockSpec((tm, LANES), lambda i: (i, 0)),
        ] + _token_specs(xs, tm),
        out_specs=out_specs,
        out_shape=out_shape,
        scratch_shapes=[pltpu.VMEM((2 * tm * ROW_TILE, LANES), F32),
                        pltpu.VMEM((2 * tm * ROW_TILE, LANES), F32),
                        pltpu.VMEM((2 * tm * ROW_TILE, LANES), F32),
                        pltpu.SemaphoreType.DMA((3,))],
        compiler_params=_params("arbitrary"),
        name="moe_combine",
    )(pos, pos, pos, ys, route, *xs)


def moe_layer(xs, g, wr, br, w_in, w_out, layer, split_rows=None):
    n = sum(x.shape[0] for x in xs)
    nc = -(-n // MOE_CHUNK)
    chunk = n // nc
    assert chunk * nc == n and chunk % MOE_TM == 0
    tpc = (2 * chunk + N_EXPERTS * (MOE_TMG - 1)) // MOE_TMG
    hs, route, route_t, counts = router(xs, g, wr, br, MOE_TM, chunk)
    rt = route_t.reshape(nc, chunk // MOE_TM, ROW_TILE, MOE_TM)
    ids = rt[:, :, 0:2, :].astype(jnp.int32)
    ranks = rt[:, :, 4:6, :].astype(jnp.int32)
    cnt = counts.reshape(nc, ROW_TILE, LANES)[:, 0, :N_EXPERTS].astype(jnp.int32)
    padded = (cnt + MOE_TMG - 1) // MOE_TMG * MOE_TMG
    ends = jnp.cumsum(padded, axis=1)
    starts = ends - padded
    start_of = jnp.zeros_like(ids)
    for e in range(N_EXPERTS):
        start_of = jnp.where(ids == e, starts[:, e][:, None, None, None], start_of)
    pos_local = start_of + ranks
    ys = experts_by_expert(hs, pos_local.reshape(-1), (starts // MOE_TMG).reshape(-1),
                           (padded // MOE_TMG).reshape(-1), w_in, w_out, layer, MOE_TMG, chunk, tpc)
    pos = pos_local + (jnp.arange(nc, dtype=jnp.int32) * (tpc * MOE_TMG))[:, None, None, None]
    return combine(ys, pos.reshape(-1), route, xs, MOE_TM, split_rows)


def _qkv_kernel(x_ref, g_ref, w_ref, qg_ref, kg_ref, seg_ref, q_ref, k_ref, v_ref):
    h = _rms(x_ref[...], g_ref[...]).astype(BF16)
    qkv = jnp.dot(h, w_ref[...], preferred_element_type=F32)
    nq = N_HEADS * HEAD_DIM
    nk = N_KV * HEAD_DIM
    q = qkv[:, :nq]
    k = qkv[:, nq:nq + nk]
    v_ref[...] = qkv[:, nq + nk:]

    def seg_mean_sq(z, seg, split):
        zz = z * z
        hi = zz.astype(BF16)
        ms = jnp.dot(hi, seg, preferred_element_type=F32)
        if split:
            lo = (zz - hi.astype(F32)).astype(BF16)
            ms = ms + jnp.dot(lo, seg, preferred_element_type=F32)
        return ms

    seg = seg_ref[...]
    qn = q * lax.rsqrt(seg_mean_sq(q, seg, False) + RMS_EPS) * qg_ref[...]
    q_ref[...] = (qn * (HEAD_DIM ** -0.5)).astype(BF16)
    kn = k * lax.rsqrt(seg_mean_sq(k, seg[:nk, :nk], True) + RMS_EPS) * kg_ref[...]
    k_ref[...] = kn


def qkv_proj(x, g, w, qg, kg, seg, tm):
    n, d = x.shape
    nq = N_HEADS * HEAD_DIM
    nk = N_KV * HEAD_DIM
    return pl.pallas_call(
        _qkv_kernel,
        grid=(n // tm,),
        in_specs=[pl.BlockSpec((tm, d), lambda i: (i, 0)), _full((1, d)),
                  _full((d, nq + 2 * nk)), _full((1, nq)), _full((1, nk)), _full((nq, nq))],
        out_specs=[pl.BlockSpec((tm, nq), lambda i: (i, 0)),
                   pl.BlockSpec((tm, nk), lambda i: (i, 0)),
                   pl.BlockSpec((tm, nk), lambda i: (i, 0))],
        out_shape=[jax.ShapeDtypeStruct((n, nq), BF16),
                   jax.ShapeDtypeStruct((n, nk), F32),
                   jax.ShapeDtypeStruct((n, nk), F32)],
        compiler_params=_params("parallel"),
        name="qkv_proj",
    )(x, g, w, qg, kg, seg)


def _attn_prompt_kernel(q_ref, kc_ref, kp_ref, vc_ref, vp_ref, tbl_ref, sink_ref, x_ref,
                        wo_ref, o_ref, att_ref, s_ref, e_ref, inv_ref):
    n = pl.program_id(1)
    first = jnp.where(n == 0, NEG_INF, 0.0).astype(F32)
    nt = (((1,), (1,)), ((), ()))
    lo_half = lax.broadcasted_iota(jnp.int32, (WINDOW, LANES), 1) < HEAD_DIM
    kp, kc = kp_ref[...], kc_ref[...]
    keys = (kp.astype(BF16), kc.astype(BF16))
    keys_swapped = (pltpu.roll(kp, HEAD_DIM, axis=1).astype(BF16),
                    pltpu.roll(kc, HEAD_DIM, axis=1).astype(BF16))
    vt_prev = jnp.transpose(vp_ref[...]).astype(BF16)
    vt_cur = jnp.transpose(vc_ref[...]).astype(BF16)
    for h in range(N_HEADS):
        kh = h // GROUP
        odd = h % 2
        pair = q_ref[:, (h // 2) * LANES:(h // 2 + 1) * LANES]
        qm = jnp.where(lo_half if odd == 0 else ~lo_half, pair, jnp.zeros_like(pair))
        k_prev, k_cur = keys_swapped if (kh == 0) == (odd == 1) else keys
        s_ref[h, 0:WINDOW, :] = (lax.dot_general(k_prev, qm, nt, preferred_element_type=F32)
                                 + tbl_ref[h, 0:WINDOW, :] + first)
        s_ref[h, WINDOW:, :] = (lax.dot_general(k_cur, qm, nt, preferred_element_type=F32)
                                + tbl_ref[h, WINDOW:, :])
    for h in range(N_HEADS):
        s = s_ref[h]
        sink = sink_ref[h]
        m = jnp.maximum(jnp.max(s, axis=0, keepdims=True), sink)
        e = jnp.exp(s - m)
        e_ref[h] = e.astype(BF16)
        inv_ref[h] = 1.0 / (jnp.sum(e, axis=0, keepdims=True) + jnp.exp(sink - m))
    for h in range(N_HEADS):
        kh = h // GROUP
        dims = slice(kh * HEAD_DIM, (kh + 1) * HEAD_DIM)
        out_t = (jnp.dot(vt_prev[dims, :], e_ref[h, 0:WINDOW, :], preferred_element_type=F32)
                 + jnp.dot(vt_cur[dims, :], e_ref[h, WINDOW:, :], preferred_element_type=F32))
        att_ref[h * HEAD_DIM:(h + 1) * HEAD_DIM, :] = out_t * inv_ref[h]
    att = jnp.transpose(att_ref[...]).astype(BF16)
    o_ref[...] = x_ref[...] + jnp.dot(att, wo_ref[...], preferred_element_type=F32)


def attn_prompt(q, k, v, tbl, sinks, x, wo, batch, seq):
    n, d = x.shape
    nb = seq // WINDOW
    nk = N_KV * HEAD_DIM
    cur = lambda b, i: (b * nb + i, 0)
    prev = lambda b, i: (b * nb + jnp.maximum(i - 1, 0), 0)
    return pl.pallas_call(
        _attn_prompt_kernel,
        grid=(batch, nb),
        in_specs=[
            pl.BlockSpec((WINDOW, d), cur),
            pl.BlockSpec((WINDOW, nk), cur), pl.BlockSpec((WINDOW, nk), prev),
            pl.BlockSpec((WINDOW, nk), cur), pl.BlockSpec((WINDOW, nk), prev),
            _full((N_HEADS, 2 * WINDOW, WINDOW)),
            pl.BlockSpec(memory_space=pltpu.SMEM),
            pl.BlockSpec((WINDOW, d), cur),
            _full((d, d)),
        ],
        out_specs=pl.BlockSpec((WINDOW, d), cur),
        out_shape=jax.ShapeDtypeStruct((n, d), F32),
        input_output_aliases={7: 0},
        scratch_shapes=[pltpu.VMEM((N_HEADS * HEAD_DIM, WINDOW), F32),
                        pltpu.VMEM((N_HEADS, 2 * WINDOW, WINDOW), F32),
                        pltpu.VMEM((N_HEADS, 2 * WINDOW, WINDOW), BF16),
                        pltpu.VMEM((N_HEADS, 1, WINDOW), F32)],
        compiler_params=_params("parallel", "parallel"),
        name="attn_prompt",
    )(q, k, k, v, v, tbl, sinks, x, wo)


def _attn_sample_kernel(q_ref, kn_ref, vn_ref, ck_ref, cv_ref, tblc_ref, tbln_ref, sink_ref, o_ref,
                        sc_ref, sn_ref, ec_ref, en_ref, inv_ref):
    bs = q_ref.shape[0]
    nt = (((1,), (1,)), ((), ()))
    for b in range(bs):
        ck = ck_ref[b].astype(BF16)
        kn = kn_ref[b].astype(BF16)
        for kh in range(N_KV):
            lanes = slice(kh * HEAD_DIM, (kh + 1) * HEAD_DIM)
            qt = q_ref[b, kh]
            sc_ref[b, kh] = lax.dot_general(qt, ck[:, lanes], nt, preferred_element_type=F32) + tblc_ref[kh]
            sn_ref[b, kh] = lax.dot_general(qt, kn[:, lanes], nt, preferred_element_type=F32) + tbln_ref[kh]
    for b in range(bs):
        for kh in range(N_KV):
            sc = sc_ref[b, kh]
            sn = sn_ref[b, kh]
            sink = sink_ref[kh]
            m = jnp.maximum(jnp.maximum(jnp.max(sc, axis=-1, keepdims=True),
                                        jnp.max(sn, axis=-1, keepdims=True)), sink)
            ec = jnp.exp(sc - m)
            en = jnp.exp(sn - m)
            ec_ref[b, kh] = ec.astype(BF16)
            en_ref[b, kh] = en.astype(BF16)
            inv_ref[b, kh] = 1.0 / (jnp.sum(ec, axis=-1, keepdims=True)
                                    + jnp.sum(en, axis=-1, keepdims=True) + jnp.exp(sink - m))
    for b in range(bs):
        cv = cv_ref[b].astype(BF16)
        vn = vn_ref[b].astype(BF16)
        for kh in range(N_KV):
            lanes = slice(kh * HEAD_DIM, (kh + 1) * HEAD_DIM)
            out = (jnp.dot(ec_ref[b, kh], cv[:, lanes], preferred_element_type=F32)
                   + jnp.dot(en_ref[b, kh], vn[:, lanes], preferred_element_type=F32))
            o_ref[b, kh] = (out * inv_ref[b, kh]).astype(BF16)


def attn_sample(q4, kn, vn, ck, cv, tbl, sink, bs):
    nb = q4.shape[0]
    ts = kn.shape[1]
    nk = N_KV * HEAD_DIM
    tg = ts * GROUP
    return pl.pallas_call(
        _attn_sample_kernel,
        grid=(nb // bs,),
        in_specs=[
            pl.BlockSpec((bs, N_KV, tg, HEAD_DIM), lambda i: (i, 0, 0, 0)),
            pl.BlockSpec((bs, ts, nk), lambda i: (i, 0, 0)),
            pl.BlockSpec((bs, ts, nk), lambda i: (i, 0, 0)),
            pl.BlockSpec((bs, WINDOW, nk), lambda i: (i, 0, 0)),
            pl.BlockSpec((bs, WINDOW, nk), lambda i: (i, 0, 0)),
            _full((N_KV, tg, WINDOW)),
            _full((N_KV, tg, ts)),
            _full((N_KV, tg, 1)),
        ],
        out_specs=pl.BlockSpec((bs, N_KV, tg, HEAD_DIM), lambda i: (i, 0, 0, 0)),
        out_shape=jax.ShapeDtypeStruct((nb, N_KV, tg, HEAD_DIM), BF16),
        scratch_shapes=[pltpu.VMEM((bs, N_KV, tg, WINDOW), F32),
                        pltpu.VMEM((bs, N_KV, tg, ts), F32),
                        pltpu.VMEM((bs, N_KV, tg, WINDOW), BF16),
                        pltpu.VMEM((bs, N_KV, tg, ts), BF16),
                        pltpu.VMEM((bs, N_KV, tg, 1), F32)],
        compiler_params=_params("parallel"),
        name="attn_sample",
    )(q4, kn, vn, ck, cv, tbl[:, :, :WINDOW], tbl[:, :, WINDOW:], sink)


def _proj_res_kernel(a_ref, w_ref, x_ref, o_ref):
    o_ref[...] = x_ref[...] + jnp.dot(a_ref[...], w_ref[...], preferred_element_type=F32)


def proj_residual(a, w, x, tm):
    n, d = x.shape
    na, kdim = a.shape
    off = (n - na) // tm
    return pl.pallas_call(
        _proj_res_kernel,
        grid=(na // tm,),
        in_specs=[pl.BlockSpec((tm, kdim), lambda i: (i, 0)), _full((kdim, d)),
                  pl.BlockSpec((tm, d), lambda i: (off + i, 0))],
        out_specs=pl.BlockSpec((tm, d), lambda i: (off + i, 0)),
        out_shape=jax.ShapeDtypeStruct((n, d), F32),
        input_output_aliases={2: 0},
        compiler_params=_params("parallel"),
        name="proj_residual",
    )(a, w, x)


def _t5_bucket_np(dist):
    n = np.maximum(dist, 0)
    max_exact = N_BUCKETS // 2
    large = max_exact + (np.log(np.maximum(n, 1).astype(np.float32) / max_exact)
                         / math.log(MAX_DISTANCE / max_exact) * (N_BUCKETS - max_exact)).astype(np.int32)
    large = np.minimum(large, N_BUCKETS - 1)
    return np.where(n < max_exact, n, large)


def _bias_table(rel_bias, dist):
    valid = (dist >= 0) & (dist <= WINDOW)
    onehot = (np.asarray(_t5_bucket_np(dist))[..., None] == np.arange(N_BUCKETS)).astype(np.float32)
    b = jnp.einsum("qkb,bh->hqk", jnp.asarray(onehot), rel_bias.astype(F32),
                   precision=lax.Precision.HIGHEST)
    return jnp.where(jnp.asarray(valid)[None], b, NEG_INF)


def kernel(x_prompt, x_sample, state_conv, cache_swa_k, cache_swa_v, rms_mix_g, rms_ffn_g, conv_w_in, conv_dw_w, conv_dw_b, conv_ln_g, conv_ln_b, conv_w_out, attn_w_qkv, attn_q_norm_g, attn_k_norm_g, attn_sinks, attn_w_o, rel_bias, router_group_w, router_group_b, router_expert_w, router_expert_b, expert_w_in, expert_w_out):
    batch, seq, d = x_prompt.shape
    nsb, ts, _ = x_sample.shape
    xp = x_prompt.reshape(batch * seq, d)
    xs = x_sample.reshape(nsb * ts, d)
    row = lambda a: a.reshape(1, -1).astype(F32)

    def router_w(i):
        we = jnp.transpose(router_expert_w[i], (1, 0, 2)).reshape(d, N_EXPERTS)
        wr = jnp.concatenate([we, router_group_w[i]], axis=1)
        wr = jnp.pad(wr, ((0, 0), (0, LANES - wr.shape[1])))
        br = jnp.concatenate([router_expert_b[i].reshape(-1), router_group_b[i]])
        br = jnp.pad(br, (0, LANES - br.shape[0])).reshape(1, LANES)
        wr = wr.astype(F32)
        w_hi = wr.astype(BF16)
        w_lo = (wr - w_hi.astype(F32)).astype(BF16)
        return jnp.concatenate([w_hi, w_lo], axis=1), br.astype(F32)

    g0 = row(rms_mix_g[0])
    w_in = conv_w_in[0].astype(BF16)
    dww = jnp.pad(conv_dw_w[0].astype(F32), ((0, HALO - CONV_WIDTH), (0, 0)))
    dwb, lng, lnb = row(conv_dw_b[0]), row(conv_ln_g[0]), row(conv_ln_b[0])
    w_out = conv_w_out[0].astype(BF16)
    up = glu_proj(xp, g0, w_in, TOKEN_TM)
    us = glu_proj(xs, g0, w_in, TOKEN_TM)
    n_p = batch * seq
    xp = conv_prompt(up, xp, dww, dwb, lng, lnb, w_out, batch, seq, CONV_TT)
    us3 = us.reshape(nsb, ts, -1)
    xs = conv_sample(us3, state_conv[0], xs, dww, dwb, lng, lnb, w_out, CONV_SAMPLE_BS)
    conv_p = up.reshape(batch, seq, -1)[:, seq - PAST:]
    conv_s = jnp.concatenate([state_conv[0], us3], axis=1)[:, ts:]

    wr0, br0 = router_w(0)
    ew_in = expert_w_in.reshape((DEPTH * N_EXPERTS,) + expert_w_in.shape[2:])
    ew_out = expert_w_out.reshape((DEPTH * N_EXPERTS,) + expert_w_out.shape[2:])
    x = moe_layer((xp, xs), row(rms_ffn_g[0]), wr0, br0, ew_in, ew_out, 0)

    g1 = row(rms_mix_g[1])
    w_qkv = attn_w_qkv[0].astype(BF16)
    qg = jnp.tile(attn_q_norm_g[0].astype(F32), N_HEADS).reshape(1, -1)
    kg = jnp.tile(attn_k_norm_g[0].astype(F32), N_KV).reshape(1, -1)
    nq = N_HEADS * HEAD_DIM
    seg = jnp.asarray(np.kron(np.eye(N_HEADS), np.ones((HEAD_DIM, HEAD_DIM))) / HEAD_DIM, BF16)
    w_o = attn_w_o[0].astype(BF16)
    sinks = attn_sinks[0].astype(F32)

    q, k, v = qkv_proj(x, g1, w_qkv, qg, kg, seg, TOKEN_TM)
    qs, ks, vs = q[n_p:], k[n_p:], v[n_p:]

    q_off = np.arange(WINDOW)[:, None]
    dist_p = q_off + WINDOW - np.arange(2 * WINDOW)[None, :]
    tbl_p = jnp.transpose(_bias_table(rel_bias, dist_p), (0, 2, 1))
    x_attn = attn_prompt(q, k, v, tbl_p, sinks, x, w_o, batch, seq)

    kt = WINDOW + ts
    dist_s = np.arange(ts)[:, None] + WINDOW - np.arange(kt)[None, :]
    tbl_s = _bias_table(rel_bias, dist_s)
    tbl_s = jnp.transpose(tbl_s.reshape(N_KV, GROUP, ts, kt), (0, 2, 1, 3)).reshape(N_KV, ts * GROUP, kt)
    sink_s = jnp.tile(sinks.reshape(N_KV, 1, GROUP), (1, ts, 1)).reshape(N_KV, ts * GROUP, 1)
    q4 = jnp.transpose(qs.reshape(nsb, ts, N_KV, GROUP, HEAD_DIM), (0, 2, 1, 3, 4))
    q4 = q4.reshape(nsb, N_KV, ts * GROUP, HEAD_DIM)
    nk = N_KV * HEAD_DIM
    ks3, vs3 = ks.reshape(nsb, ts, nk), vs.reshape(nsb, ts, nk)
    ck = cache_swa_k[0].reshape(nsb, WINDOW, nk)
    cv = cache_swa_v[0].reshape(nsb, WINDOW, nk)
    o4 = attn_sample(q4, ks3, vs3, ck, cv, tbl_s, sink_s, ATTN_SAMPLE_BS)
    os_ = jnp.transpose(o4.reshape(nsb, N_KV, ts, GROUP, HEAD_DIM), (0, 2, 1, 3, 4)).reshape(nsb * ts, nq)
    x = proj_residual(os_, w_o, x_attn, TOKEN_TM)

    last_window = lambda a: jnp.stack([a[(b + 1) * seq - WINDOW:(b + 1) * seq] for b in range(batch)]
                                      ).reshape(batch, WINDOW, N_KV, HEAD_DIM)
    k_p, v_p = last_window(k), last_window(v)
    k_s = jnp.concatenate([cache_swa_k[0], ks.reshape(nsb, ts, N_KV, HEAD_DIM)], axis=1)[:, ts:]
    v_s = jnp.concatenate([cache_swa_v[0], vs.reshape(nsb, ts, N_KV, HEAD_DIM)], axis=1)[:, ts:]

    wr1, br1 = router_w(1)
    xp, xs = moe_layer((x,), row(rms_ffn_g[1]), wr1, br1, ew_in, ew_out, 1, split_rows=n_p)

    return (xp.reshape(batch, seq, d), xs.reshape(nsb, ts, d),
            conv_p[None], conv_s[None], k_p[None], v_p[None], k_s[None], v_s[None])
```
